```python
import jax, jax.numpy as jnp
from jax import lax
import numpy as np

D_MODEL = 1024
BATCH = 8
SEQ = 2048
DEPTH = 2
DEC_BATCH = 128
DEC_SEQ = 1
PAST_LEN = 8192
PAGE_SIZE = 128

D_POOL = D_MODEL // 4
POOL_WINDOWS = (2, 4, 8, 16)
N_POOL_GROUPS = 4
POOL_GC = D_POOL // N_POOL_GROUPS
POOL_BUF = 15
D_SGU = D_MODEL // 4
CHUNK = 128
N_SGU_GROUPS = 4
SGU_GC = D_SGU // N_SGU_GROUPS
HEAD_DIM = 64
N_HEADS = (D_MODEL // 2) // HEAD_DIM
N_KV_HEADS = 2
Q_PER_KV = N_HEADS // N_KV_HEADS
D_ATTN = N_HEADS * HEAD_DIM
D_KV = N_KV_HEADS * HEAD_DIM
WINDOW = 128
BLOCK = 128
ROPE_THETA = 10000.0
N_BRANCHES = 3
SPLIT_SIZES = (D_POOL, D_POOL, D_SGU, D_SGU, D_SGU, D_ATTN, D_KV, D_KV, D_ATTN, N_BRANCHES * D_MODEL)
D_IN = 2 * D_POOL + 3 * D_SGU + 2 * D_ATTN + 2 * D_KV + N_BRANCHES * D_MODEL
V_OFF = 2 * D_POOL + 3 * D_SGU + D_ATTN + D_KV
ALPHA = (2.0 * DEPTH) ** 0.25
BETA = (8.0 * DEPTH) ** -0.25
LN_EPS = 1e-5
NEG_INF = -1e30

kernel_name = "hybrid_pool_sgu_swa_decoder_step"


def layer_norm(x, g, b):
    xf = x.astype(jnp.float32)
    mu = xf.mean(-1, keepdims=True)
    var = jnp.square(xf - mu).mean(-1, keepdims=True)
    y = (xf - mu) * lax.rsqrt(var + LN_EPS)
    return (y * g.astype(jnp.float32) + b.astype(jnp.float32)).astype(x.dtype)


def split_in(h):
    offs = np.cumsum(SPLIT_SIZES)[:-1].tolist()
    return jnp.split(h, offs, axis=-1)


def rope(x, pos):
    half = HEAD_DIM // 2
    inv = ROPE_THETA ** (-jnp.arange(half, dtype=jnp.float32) / half)
    ang = pos.astype(jnp.float32)[:, None] * inv[None, :]
    cos = jnp.cos(ang)[None, :, None, :]
    sin = jnp.sin(ang)[None, :, None, :]
    xf = x.astype(jnp.float32)
    x1, x2 = xf[..., :half], xf[..., half:]
    return jnp.concatenate([x1 * cos - x2 * sin, x2 * cos + x1 * sin], axis=-1).astype(x.dtype)


def pool_mix(xa_ext, n_hist, pos, pool_w, pool_scale):
    B, L, _ = xa_ext.shape
    T = L - n_hist
    xg = xa_ext.astype(jnp.float32).reshape(B, L, N_POOL_GROUPS, POOL_GC)
    cs = jnp.cumsum(xg, axis=1)
    outs = []
    for g, w in enumerate(POOL_WINDOWS):
        csg = cs[:, :, g]
        shifted = jnp.pad(csg, ((0, 0), (w, 0), (0, 0)))[:, :L]
        win_sum = (csg - shifted)[:, n_hist:]
        cnt = jnp.minimum(pos + 1, w).astype(jnp.float32)
        outs.append(win_sum / cnt[None, :, None] - xg[:, n_hist:, g])
    pooled = jnp.stack(outs, axis=2)
    mixed = jnp.einsum('btgc,gcd->btgd', pooled, pool_w.astype(jnp.float32)).reshape(B, T, D_POOL)
    return (mixed * pool_scale.astype(jnp.float32)).astype(xa_ext.dtype)


def sgu_spatial(v_chunks, sgu_w, sgu_b):
    tc = v_chunks.shape[2]
    mask = jnp.tril(jnp.ones((tc, tc), dtype=bool))
    w = jnp.where(mask[None], sgu_w[:, :tc, :tc], 0.0).astype(v_chunks.dtype)
    s = jnp.einsum('gts,bnsgc->bntgc', w, v_chunks)
    return s + jnp.transpose(sgu_b[:, :tc])[None, None, :, :, None].astype(v_chunks.dtype)


def sink_softmax(scores, allowed, sinks):
    scores = jnp.where(allowed, scores, NEG_INF)
    sink = sinks.astype(jnp.float32).reshape(N_KV_HEADS, Q_PER_KV, 1, 1)
    m = jnp.maximum(scores.max(-1, keepdims=True), sink)
    p = jnp.exp(scores - m)
    return p / (p.sum(-1, keepdims=True) + jnp.exp(sink - m))


def banded_window_attention(q, k, v, sinks):
    B, L = q.shape[:2]
    NB = L // BLOCK
    qb = q.reshape(B, NB, BLOCK, N_KV_HEADS, Q_PER_KV, HEAD_DIM)
    kb = k.reshape(B, NB, BLOCK, N_KV_HEADS, HEAD_DIM)
    vb = v.reshape(B, NB, BLOCK, N_KV_HEADS, HEAD_DIM)
    pad = ((0, 0), (1, 0), (0, 0), (0, 0), (0, 0))
    keys = jnp.concatenate([jnp.pad(kb, pad)[:, :NB], kb], axis=2)
    vals = jnp.concatenate([jnp.pad(vb, pad)[:, :NB], vb], axis=2)
    scores = jnp.einsum('bnqkgd,bnskd->bnkgqs', qb, keys).astype(jnp.float32) * (HEAD_DIM ** -0.5)
    blk = jnp.arange(NB, dtype=jnp.int32)[:, None] * BLOCK
    q_pos = blk + jnp.arange(BLOCK, dtype=jnp.int32)[None, :]
    k_pos = blk - BLOCK + jnp.arange(2 * BLOCK, dtype=jnp.int32)[None, :]
    diff = q_pos[:, :, None] - k_pos[:, None, :]
    allowed = (diff >= 0) & (diff <= WINDOW) & (k_pos[:, None, :] >= 0)
    probs = sink_softmax(scores, allowed[None, :, None, None], sinks)
    out = jnp.einsum('bnkgqs,bnskd->bnqkgd', probs.astype(vals.dtype), vals)
    return out.reshape(B, L, D_ATTN)


def window_decode_attention(q, keys, vals, pos, sinks):
    Bd, T = q.shape[:2]
    qg = q.reshape(Bd, T, N_KV_HEADS, Q_PER_KV, HEAD_DIM)
    scores = jnp.einsum('btkgd,bskd->bkgts', qg, keys).astype(jnp.float32) * (HEAD_DIM ** -0.5)
    k_pos = jnp.concatenate([PAST_LEN - WINDOW + jnp.arange(WINDOW, dtype=jnp.int32), pos])
    diff = pos[:, None] - k_pos[None, :]
    allowed = (diff >= 0) & (diff <= WINDOW)
    probs = sink_softmax(scores, allowed, sinks)
    out = jnp.einsum('bkgts,bskd->btkgd', probs.astype(vals.dtype), vals)
    return out.reshape(Bd, T, D_ATTN)


def merge_and_norm(x, ya, za, yb, zb, yc, zc, gates, b_gate, w_pa, w_pb, w_pc, w_out, ln_g, ln_b):
    lead = gates.shape[:-1]
    g = jax.nn.sigmoid((gates.reshape(lead + (N_BRANCHES, D_MODEL)) + b_gate).astype(jnp.float32)).astype(x.dtype)
    oa = jnp.einsum('blc,cd->bld', ya * jax.nn.silu(za), w_pa)
    ob = jnp.einsum('blc,cd->bld', yb * jax.nn.silu(zb), w_pb)
    oc = jnp.einsum('blc,cd->bld', yc * jax.nn.silu(zc), w_pc)
    merged = g[..., 0, :] * oa + g[..., 1, :] * ob + g[..., 2, :] * oc
    out = jnp.einsum('bld,de->ble', merged, w_out)
    return layer_norm(ALPHA * x + out, ln_g, ln_b)


def layer_prompt(x, w_in, b_gate, pool_w, pool_scale, sgu_ln_g, sgu_ln_b, sgu_w, sgu_b, attn_sinks,
                 w_pa, w_pb, w_pc, w_out, ln_g, ln_b):
    B, L, _ = x.shape
    pos = jnp.arange(L, dtype=jnp.int32)
    xa, za, u, v, zb, q, k, vv, zc, gates = split_in(jnp.einsum('bld,de->ble', x, w_in))
    ya = pool_mix(xa, 0, pos, pool_w, pool_scale)
    vn = layer_norm(v, sgu_ln_g, sgu_ln_b)
    yb = u * sgu_spatial(vn.reshape(B, L // CHUNK, CHUNK, N_SGU_GROUPS, SGU_GC), sgu_w, sgu_b).reshape(B, L, D_SGU)
    qr = rope(q.reshape(B, L, N_HEADS, HEAD_DIM), pos)
    kr = rope(k.reshape(B, L, N_KV_HEADS, HEAD_DIM), pos)
    vr = vv.reshape(B, L, N_KV_HEADS, HEAD_DIM)
    yc = banded_window_attention(qr, kr, vr, attn_sinks)
    y = merge_and_norm(x, ya, za, yb, zb, yc, zc, gates, b_gate, w_pa, w_pb, w_pc, w_out, ln_g, ln_b)
    return y, xa[:, L - POOL_BUF:], kr[:, L - WINDOW:], vr[:, L - WINDOW:]


def layer_sample(x, pool_buf, k_buf, v_buf, w_in, b_gate, pool_w, pool_scale, sgu_ln_g, sgu_ln_b, sgu_w, sgu_b,
                 attn_sinks, w_pa, w_pb, w_pc, w_out, ln_g, ln_b):
    Bd, T, _ = x.shape
    pos = PAST_LEN + jnp.arange(T, dtype=jnp.int32)
    xa, za, u, v, zb, q, k, vv, zc, gates = split_in(jnp.einsum('bld,de->ble', x, w_in))
    xa_ext = jnp.concatenate([pool_buf.astype(xa.dtype), xa], axis=1)
    ya = pool_mix(xa_ext, POOL_BUF, pos, pool_w, pool_scale)
    vn = layer_norm(v, sgu_ln_g, sgu_ln_b)
    yb = u * sgu_spatial(vn.reshape(Bd, 1, T, N_SGU_GROUPS, SGU_GC), sgu_w, sgu_b).reshape(Bd, T, D_SGU)
    qr = rope(q.reshape(Bd, T, N_HEADS, HEAD_DIM), pos)
    kr = rope(k.reshape(Bd, T, N_KV_HEADS, HEAD_DIM), pos)
    vr = vv.reshape(Bd, T, N_KV_HEADS, HEAD_DIM)
    keys = jnp.concatenate([k_buf.astype(kr.dtype), kr], axis=1)
    vals = jnp.concatenate([v_buf.astype(vr.dtype), vr], axis=1)
    yc = window_decode_attention(qr, keys, vals, pos, attn_sinks)
    y = merge_and_norm(x, ya, za, yb, zb, yc, zc, gates, b_gate, w_pa, w_pb, w_pc, w_out, ln_g, ln_b)
    return y, xa_ext[:, T:], keys[:, T:], vals[:, T:], vn


def setup_inputs(seed: int = 0) -> dict:
    key = jax.random.key(seed)
    ks = jax.random.split(key, 20)
    f32 = jnp.float32
    nrm = lambda k, s: jax.random.normal(k, s, dtype=f32)
    w_in = nrm(ks[5], (DEPTH, D_MODEL, D_IN)) * D_MODEL ** -0.5
    w_in = w_in.at[:, :, V_OFF:V_OFF + D_KV].multiply(BETA)
    return {
        "x_prompt": nrm(ks[0], (BATCH, SEQ, D_MODEL)),
        "x_sample": nrm(ks[1], (DEC_BATCH, DEC_SEQ, D_MODEL)),
        "state_pool": nrm(ks[2], (DEPTH, DEC_BATCH, POOL_BUF, D_POOL)),
        "cache_k_win": nrm(ks[3], (DEPTH, DEC_BATCH, WINDOW, N_KV_HEADS, HEAD_DIM)),
        "cache_v_win": nrm(ks[4], (DEPTH, DEC_BATCH, WINDOW, N_KV_HEADS, HEAD_DIM)),
        "w_in": w_in,
        "b_gate": 0.02 * nrm(ks[6], (DEPTH, N_BRANCHES, D_MODEL)),
        "pool_w": nrm(ks[7], (DEPTH, N_POOL_GROUPS, POOL_GC, POOL_GC)) * POOL_GC ** -0.5,
        "pool_scale": 1.0 + 0.1 * nrm(ks[8], (DEPTH, D_POOL)),
        "sgu_ln_g": 1.0 + 0.1 * nrm(ks[9], (DEPTH, D_SGU)),
        "sgu_ln_b": 0.02 * nrm(ks[10], (DEPTH, D_SGU)),
        "sgu_w": nrm(ks[11], (DEPTH, N_SGU_GROUPS, CHUNK, CHUNK)) * CHUNK ** -0.5,
        "sgu_b": 1.0 + 0.1 * nrm(ks[12], (DEPTH, N_SGU_GROUPS, CHUNK)),
        "attn_sinks": 0.5 * nrm(ks[13], (DEPTH, N_HEADS)),
        "w_proj_a": nrm(ks[14], (DEPTH, D_POOL, D_MODEL)) * D_POOL ** -0.5 * BETA,
        "w_proj_b": nrm(ks[15], (DEPTH, D_SGU, D_MODEL)) * D_SGU ** -0.5 * BETA,
        "w_proj_c": nrm(ks[16], (DEPTH, D_ATTN, D_MODEL)) * D_ATTN ** -0.5 * BETA,
        "w_out": nrm(ks[17], (DEPTH, D_MODEL, D_MODEL)) * D_MODEL ** -0.5 * BETA,
        "ln_g": 1.0 + 0.1 * nrm(ks[18], (DEPTH, D_MODEL)),
        "ln_b": 0.02 * nrm(ks[19], (DEPTH, D_MODEL)),
    }


def reference(x_prompt, x_sample, state_pool, cache_k_win, cache_v_win, w_in, b_gate, pool_w, pool_scale,
              sgu_ln_g, sgu_ln_b, sgu_w, sgu_b, attn_sinks, w_proj_a, w_proj_b, w_proj_c, w_out, ln_g, ln_b):
    y_p, y_s = x_prompt, x_sample
    pool_p, kp, vp, pool_s, ksm, vsm, chunk_v = [], [], [], [], [], [], []
    for l in range(DEPTH):
        lw = (w_in[l], b_gate[l], pool_w[l], pool_scale[l], sgu_ln_g[l], sgu_ln_b[l], sgu_w[l], sgu_b[l],
              attn_sinks[l], w_proj_a[l], w_proj_b[l], w_proj_c[l], w_out[l], ln_g[l], ln_b[l])
        y_p, sp, kpl, vpl = layer_prompt(y_p, *lw)
        y_s, ss, ksl, vsl, cvl = layer_sample(y_s, state_pool[l], cache_k_win[l], cache_v_win[l], *lw)
        pool_p.append(sp); kp.append(kpl); vp.append(vpl)
        pool_s.append(ss); ksm.append(ksl); vsm.append(vsl); chunk_v.append(cvl)
    new_state_pool_prompt = jnp.stack(pool_p)
    new_cache_k_win_prompt = jnp.stack(kp)
    new_cache_v_win_prompt = jnp.stack(vp)
    new_state_pool_sample = jnp.stack(pool_s)
    new_cache_k_win_sample = jnp.stack(ksm)
    new_cache_v_win_sample = jnp.stack(vsm)
    new_state_chunk_v_sample = jnp.stack(chunk_v)
    return (y_p, y_s, new_state_pool_prompt, new_cache_k_win_prompt, new_cache_v_win_prompt,
            new_state_pool_sample, new_cache_k_win_sample, new_cache_v_win_sample, new_state_chunk_v_sample)
```

```python
import functools

import numpy as np
import jax
import jax.numpy as jnp
from jax import lax
from jax.experimental import pallas as pl
from jax.experimental.pallas import tpu as pltpu

D_MODEL = 1024
DEPTH = 2
PAST_LEN = 8192
D_POOL = 256
POOL_WINDOWS = (2, 4, 8, 16)
POOL_GC = 64
POOL_BUF = 15
D_SGU = 256
CHUNK = 128
N_SGU_GROUPS = 4
HEAD_DIM = 64
N_HEADS = 8
N_KV_HEADS = 2
Q_PER_KV = 4
D_ATTN = 512
D_KV = 128
WINDOW = 128
BLOCK = 128
ROPE_THETA = 10000.0
N_BRANCHES = 3
D_IN = 2 * D_POOL + 3 * D_SGU + 2 * D_ATTN + 2 * D_KV + N_BRANCHES * D_MODEL
ALPHA = (2.0 * DEPTH) ** 0.25
LN_EPS = 1e-5
NEG_INF = -1e30
SCALE = HEAD_DIM ** -0.5

OFF_XA, OFF_ZA, OFF_U, OFF_V, OFF_ZB = 0, 256, 512, 768, 1024
OFF_Q, OFF_K, OFF_VV, OFF_ZC, OFF_G = 1280, 1792, 1920, 2048, 2560

LANES = 128
TM = 512
NBLK = TM // BLOCK
HIST = 32
NCHUNK = 256
VMEM_LIMIT = 56 * 1024 * 1024
SB = 16

bf16 = jnp.bfloat16
f32 = jnp.float32


def _dot(a, b):
    return jnp.dot(a, b, preferred_element_type=f32)


def _dot_nt(a, b):
    return lax.dot_general(a, b, (((1,), (1,)), ((), ())), preferred_element_type=f32)


def _silu(z):
    return z * jax.nn.sigmoid(z)


def _layer_norm(x, g, b):
    mu = jnp.mean(x, axis=-1, keepdims=True)
    xc = x - mu
    var = jnp.mean(xc * xc, axis=-1, keepdims=True)
    return xc * lax.rsqrt(var + LN_EPS) * g + b


def _rope128(x, cos, sin_signed):
    lane = lax.broadcasted_iota(jnp.int32, x.shape, 1)
    first_half = (lane % HEAD_DIM) < (HEAD_DIM // 2)
    partner = jnp.where(first_half,
                        pltpu.roll(x, LANES - HEAD_DIM // 2, 1),
                        pltpu.roll(x, HEAD_DIM // 2, 1))
    return x * cos + partner * sin_signed


def _group_select(lane, a0, a1, a2, a3):
    return jnp.where(lane < 64, a0, jnp.where(lane < 128, a1, jnp.where(lane < 192, a2, a3)))


def _prompt_kernel(sinks_ref, x_ref, cos_ref, sin_ref, w_in_ref, bg_ref, poolw_ref, pscale_ref,
                   slng_ref, slnb_ref, sguw_ref, sgub_ref, wpa_ref, wpb_ref, wpc_ref, wout_ref,
                   lng_ref, lnb_ref,
                   y_ref, pool_out_ref, k_out_ref, v_out_ref,
                   xb_ref, h_ref, ext_ref, s2_ref, s4_ref, s8_ref, qs_ref,
                   ka_ref, kb_ref, kc_ref, kd_ref, va_ref, vb_ref, vc_ref, vd_ref,
                   ain_ref, bin_ref, cin_ref, mg_ref):
    i = pl.program_id(1)
    last = pl.num_programs(1) - 1
    kv_refs = (ka_ref, kb_ref, kc_ref, kd_ref, va_ref, vb_ref, vc_ref, vd_ref)

    @pl.when(i == 0)
    def _():
        ext_ref[0:HIST, :] = jnp.zeros((HIST, D_POOL), f32)
        for r in kv_refs:
            r[0:BLOCK, :] = jnp.zeros((BLOCK, LANES), bf16)

    xb = x_ref[...].astype(bf16)
    xb_ref[...] = xb
    half = OFF_G // 2
    h_ref[:, 0:half] = _dot(xb, w_in_ref[:, 0:half])
    h_ref[:, half:OFF_G] = _dot(xb, w_in_ref[:, half:OFF_G])

    xa = h_ref[:, OFF_XA:OFF_XA + D_POOL]
    ext_ref[HIST:HIST + TM, :] = xa
    n = HIST + TM
    s2_ref[8:n, :] = ext_ref[8:n, :] + ext_ref[7:n - 1, :]
    s4_ref[16:n, :] = s2_ref[16:n, :] + s2_ref[14:n - 2, :]
    s8_ref[24:n, :] = s4_ref[24:n, :] + s4_ref[20:n - 4, :]
    w16 = s8_ref[HIST:n, :] + s8_ref[HIST - 8:n - 8, :]
    lane_p = lax.broadcasted_iota(jnp.int32, (TM, D_POOL), 1)
    row_p = lax.broadcasted_iota(jnp.int32, (TM, D_POOL), 0)
    win = _group_select(lane_p, s2_ref[HIST:n, :], s4_ref[HIST:n, :], s8_ref[HIST:n, :], w16)
    width = _group_select(lane_p, POOL_WINDOWS[0], POOL_WINDOWS[1], POOL_WINDOWS[2], POOL_WINDOWS[3])
    cnt = jnp.minimum(row_p + (i * TM + 1), width).astype(f32)
    pooled = win / cnt - xa
    ya = _dot(pooled.astype(bf16), poolw_ref[...]) * pscale_ref[...]
    za = h_ref[:, OFF_ZA:OFF_ZA + D_POOL]
    ain_ref[...] = (ya * _silu(za)).astype(bf16)
    ext_ref[HIST - 16:HIST, :] = ext_ref[n - 16:n, :]

    vn = _layer_norm(h_ref[:, OFF_V:OFF_V + D_SGU], slng_ref[...], slnb_ref[...]).astype(bf16)
    wr = lax.broadcasted_iota(jnp.int32, (N_SGU_GROUPS * CHUNK, CHUNK), 0) % CHUNK
    wc = lax.broadcasted_iota(jnp.int32, (N_SGU_GROUPS * CHUNK, CHUNK), 1)
    w_s = jnp.where(wc <= wr, sguw_ref[...], 0.0).astype(bf16)
    lane_c = lax.broadcasted_iota(jnp.int32, (CHUNK, D_SGU), 1)
    for j in range(NBLK):
        rows = slice(j * CHUNK, (j + 1) * CHUNK)
        r = _dot(w_s, vn[rows, :])
        s = _group_select(lane_c, r[0:CHUNK], r[CHUNK:2 * CHUNK], r[2 * CHUNK:3 * CHUNK],
                          r[3 * CHUNK:4 * CHUNK]) + sgub_ref[...]
        yb = h_ref[rows, OFF_U:OFF_U + D_SGU] * s
        bin_ref[rows, :] = (yb * _silu(h_ref[rows, OFF_ZB:OFF_ZB + D_SGU])).astype(bf16)

    cos = cos_ref[...]
    sin = sin_ref[...]
    for c in range(D_ATTN // LANES):
        qc = _rope128(h_ref[:, OFF_Q + c * LANES:OFF_Q + (c + 1) * LANES], cos, sin)
        qs_ref[:, c * LANES:(c + 1) * LANES] = (qc * SCALE).astype(bf16)
    kr = _rope128(h_ref[:, OFF_K:OFF_K + D_KV], cos, sin)
    vv = h_ref[:, OFF_VV:OFF_VV + D_KV]
    lane_k = lax.broadcasted_iota(jnp.int32, (TM, LANES), 1)
    lo = lane_k < HEAD_DIM
    for t, (a_ref, b_ref, c_ref, d_ref) in ((kr, kv_refs[0:4]), (vv, kv_refs[4:8])):
        sw = pltpu.roll(t, HEAD_DIM, 1)
        a_ref[BLOCK:BLOCK + TM, :] = jnp.where(lo, t, 0.0).astype(bf16)
        b_ref[BLOCK:BLOCK + TM, :] = jnp.where(lo, 0.0, t).astype(bf16)
        c_ref[BLOCK:BLOCK + TM, :] = jnp.where(lo, sw, 0.0).astype(bf16)
        d_ref[BLOCK:BLOCK + TM, :] = jnp.where(lo, 0.0, sw).astype(bf16)

    qrow = lax.broadcasted_iota(jnp.int32, (2 * BLOCK, 2 * BLOCK), 0) % BLOCK
    kcol = lax.broadcasted_iota(jnp.int32, (2 * BLOCK, 2 * BLOCK), 1)
    band = (kcol >= qrow) & (kcol <= qrow + WINDOW)
    band_first = band & (kcol >= jnp.where(i > 0, 0, BLOCK))
    top = lax.broadcasted_iota(jnp.int32, (2 * BLOCK, 1), 0) < BLOCK
    for j in range(NBLK):
        rows = slice(j * BLOCK, (j + 1) * BLOCK)
        keys = slice(j * BLOCK, j * BLOCK + 2 * BLOCK)
        allowed = band_first if j == 0 else band
        for kv in range(N_KV_HEADS):
            c0 = kv * Q_PER_KV * HEAD_DIM
            qst = jnp.concatenate([qs_ref[rows, c0:c0 + LANES], qs_ref[rows, c0 + LANES:c0 + 2 * LANES]], axis=0)
            k_even, k_odd = (ka_ref, kd_ref) if kv == 0 else (kc_ref, kb_ref)
            v_even, v_odd = (va_ref, vd_ref) if kv == 0 else (vc_ref, vb_ref)
            kcat = jnp.concatenate([k_even[keys, :], k_odd[keys, :]], axis=0)
            sc = _dot_nt(qst, kcat)
            probs = []
            for par in range(2):
                sink = jnp.where(top, sinks_ref[kv * Q_PER_KV + par], sinks_ref[kv * Q_PER_KV + 2 + par])
                sm = jnp.where(allowed, sc[:, par * 2 * BLOCK:(par + 1) * 2 * BLOCK], NEG_INF)
                m = jnp.maximum(jnp.max(sm, axis=-1, keepdims=True), sink)
                p = jnp.exp(sm - m)
                den = jnp.sum(p, axis=-1, keepdims=True) + jnp.exp(sink - m)
                probs.append((p / den).astype(bf16))
            pcat = jnp.concatenate(probs, axis=1)
            vcat = jnp.concatenate([v_even[keys, :], v_odd[keys, :]], axis=0)
            o = _dot(pcat, vcat)
            for pr in range(2):
                cols = slice(c0 + pr * LANES, c0 + (pr + 1) * LANES)
                zc = h_ref[rows, OFF_ZC + c0 + pr * LANES:OFF_ZC + c0 + (pr + 1) * LANES]
                cin_ref[rows, cols] = (o[pr * BLOCK:(pr + 1) * BLOCK] * _silu(zc)).astype(bf16)

    @pl.when(i == last)
    def _():
        pool_out_ref[...] = ext_ref[n - POOL_BUF:n, :]
        k_out_ref[...] = kr[TM - WINDOW:TM, :]
        v_out_ref[...] = vv[TM - WINDOW:TM, :]

    for r in kv_refs:
        r[0:BLOCK, :] = r[TM:TM + BLOCK, :]

    xb = xb_ref[...]
    for c in range(D_MODEL // NCHUNK):
        cols = slice(c * NCHUNK, (c + 1) * NCHUNK)
        acc = None
        for br, (in_ref, wp_ref) in enumerate(((ain_ref, wpa_ref), (bin_ref, wpb_ref), (cin_ref, wpc_ref))):
            g0 = OFF_G + br * D_MODEL + c * NCHUNK
            gate = jax.nn.sigmoid(_dot(xb, w_in_ref[:, g0:g0 + NCHUNK]) + bg_ref[br:br + 1, cols])
            term = gate * _dot(in_ref[...], wp_ref[:, cols])
            acc = term if acc is None else acc + term
        mg_ref[:, cols] = acc.astype(bf16)

    out = _dot(mg_ref[...], wout_ref[...])
    y_ref[...] = _layer_norm(ALPHA * x_ref[...] + out, lng_ref[...], lnb_ref[...])


def _const_spec(shape):
    nd = len(shape)
    return pl.BlockSpec(shape, lambda b, i: (0,) * nd, pipeline_mode=pl.Buffered(1))


def _prompt_layer(x, cos, sin, sinks, w_in, b_gate, poolw, pscale, slng, slnb, sguw, sgub,
                  wpa, wpb, wpc, wout, lng, lnb):
    B, L, _ = x.shape
    grid = (B, L // TM)
    row_spec = lambda w: pl.BlockSpec((TM, w), lambda b, i: (i, 0))
    in_specs = [
        pl.BlockSpec(memory_space=pltpu.SMEM),
        pl.BlockSpec((None, TM, D_MODEL), lambda b, i: (b, i, 0)),
        row_spec(LANES), row_spec(LANES),
        _const_spec(w_in.shape), _const_spec(b_gate.shape), _const_spec(poolw.shape),
        _const_spec(pscale.shape), _const_spec(slng.shape), _const_spec(slnb.shape),
        _const_spec(sguw.shape), _const_spec(sgub.shape), _const_spec(wpa.shape),
        _const_spec(wpb.shape), _const_spec(wpc.shape), _const_spec(wout.shape),
        _const_spec(lng.shape), _const_spec(lnb.shape),
    ]
    out_shape = (
        jax.ShapeDtypeStruct((B, L, D_MODEL), f32),
        jax.ShapeDtypeStruct((B, POOL_BUF, D_POOL), f32),
        jax.ShapeDtypeStruct((B, WINDOW, D_KV), f32),
        jax.ShapeDtypeStruct((B, WINDOW, D_KV), f32),
    )
    out_specs = (
        pl.BlockSpec((None, TM, D_MODEL), lambda b, i: (b, i, 0)),
        pl.BlockSpec((None, POOL_BUF, D_POOL), lambda b, i: (b, 0, 0)),
        pl.BlockSpec((None, WINDOW, D_KV), lambda b, i: (b, 0, 0)),
        pl.BlockSpec((None, WINDOW, D_KV), lambda b, i: (b, 0, 0)),
    )
    kv_scratch = [pltpu.VMEM((BLOCK + TM, LANES), bf16) for _ in range(8)]
    scratch = [
        pltpu.VMEM((TM, D_MODEL), bf16),
        pltpu.VMEM((TM, OFF_G), f32),
        pltpu.VMEM((HIST + TM, D_POOL), f32),
        pltpu.VMEM((HIST + TM, D_POOL), f32),
        pltpu.VMEM((HIST + TM, D_POOL), f32),
        pltpu.VMEM((HIST + TM, D_POOL), f32),
        pltpu.VMEM((TM, D_ATTN), bf16),
        *kv_scratch,
        pltpu.VMEM((TM, D_POOL), bf16),
        pltpu.VMEM((TM, D_SGU), bf16),
        pltpu.VMEM((TM, D_ATTN), bf16),
        pltpu.VMEM((TM, D_MODEL), bf16),
    ]
    return pl.pallas_call(
        _prompt_kernel,
        out_shape=out_shape,
        grid=grid,
        in_specs=in_specs,
        out_specs=out_specs,
        scratch_shapes=scratch,
        compiler_params=pltpu.CompilerParams(
            dimension_semantics=("arbitrary", "arbitrary"),
            vmem_limit_bytes=VMEM_LIMIT),
        name="prompt_layer",
    )(sinks, x, cos, sin, w_in, b_gate, poolw, pscale, slng, slnb, sguw, sgub,
      wpa, wpb, wpc, wout, lng, lnb)


def _sample_proj_kernel(x_ref, cos_ref, sin_ref, w_in_ref, bg_ref, pb_ref, poolw_ref, pscale_ref,
                        slng_ref, slnb_ref, sw0_ref, sb0_ref,
                        q_ref, k_ref, v_ref, pool_ref, vn_ref, ain_ref, bin_ref, szc_ref, g_ref):
    xb = x_ref[...].astype(bf16)
    h = _dot(xb, w_in_ref[:, 0:OFF_G])
    xa = h[:, OFF_XA:OFF_XA + D_POOL]
    lane = lax.broadcasted_iota(jnp.int32, xa.shape, 1)
    first_row = _group_select(lane, *(POOL_BUF - (w - 1) for w in POOL_WINDOWS))
    win = xa
    for r in range(POOL_BUF):
        win = win + jnp.where(first_row <= r, pb_ref[:, r * D_POOL:(r + 1) * D_POOL], 0.0)
    width = _group_select(lane, *POOL_WINDOWS).astype(f32)
    pooled = win / width - xa
    ya = _dot(pooled.astype(bf16), poolw_ref[...]) * pscale_ref[...]
    ain_ref[...] = (ya * _silu(h[:, OFF_ZA:OFF_ZA + D_POOL])).astype(bf16)
    pool_ref[:, 0:(POOL_BUF - 1) * D_POOL] = pb_ref[:, D_POOL:POOL_BUF * D_POOL]
    pool_ref[:, (POOL_BUF - 1) * D_POOL:POOL_BUF * D_POOL] = xa
    vn = _layer_norm(h[:, OFF_V:OFF_V + D_SGU], slng_ref[...], slnb_ref[...])
    vn_ref[...] = vn
    yb = h[:, OFF_U:OFF_U + D_SGU] * (sw0_ref[...] * vn + sb0_ref[...])
    bin_ref[...] = (yb * _silu(h[:, OFF_ZB:OFF_ZB + D_SGU])).astype(bf16)
    cos = cos_ref[...]
    sin = sin_ref[...]
    for c in range(D_ATTN // LANES):
        q_ref[:, c * LANES:(c + 1) * LANES] = _rope128(
            h[:, OFF_Q + c * LANES:OFF_Q + (c + 1) * LANES], cos, sin) * SCALE
    k_ref[...] = _rope128(h[:, OFF_K:OFF_K + D_KV], cos, sin)
    v_ref[...] = h[:, OFF_VV:OFF_VV + D_KV]
    szc_ref[...] = _silu(h[:, OFF_ZC:OFF_ZC + D_ATTN])
    for br in range(N_BRANCHES):
        cols = slice(br * D_MODEL, (br + 1) * D_MODEL)
        g_ref[:, cols] = jax.nn.sigmoid(
            _dot(xb, w_in_ref[:, OFF_G + br * D_MODEL:OFF_G + (br + 1) * D_MODEL]) + bg_ref[br:br + 1, :])


def _sample_proj(x, cos, sin, w_in, b_gate, pool_buf, poolw, pscale, slng, slnb, sw0, sb0):
    nb = x.shape[0]
    out_shape = (
        jax.ShapeDtypeStruct((nb, D_ATTN), f32),
        jax.ShapeDtypeStruct((nb, D_KV), f32),
        jax.ShapeDtypeStruct((nb, D_KV), f32),
        jax.ShapeDtypeStruct((nb, POOL_BUF * D_POOL), f32),
        jax.ShapeDtypeStruct((nb, D_SGU), f32),
        jax.ShapeDtypeStruct((nb, D_POOL), bf16),
        jax.ShapeDtypeStruct((nb, D_SGU), bf16),
        jax.ShapeDtypeStruct((nb, D_ATTN), f32),
        jax.ShapeDtypeStruct((nb, N_BRANCHES * D_MODEL), f32),
    )
    return pl.pallas_call(
        _sample_proj_kernel,
        out_shape=out_shape,
        compiler_params=pltpu.CompilerParams(vmem_limit_bytes=VMEM_LIMIT),
        name="sample_proj",
    )(x, cos, sin, w_in, b_gate, pool_buf, poolw, pscale, slng, slnb, sw0, sb0)


def _sample_attn_kernel(sink_ref, q_ref, kc_ref, vc_ref, kn_ref, vn_ref, o_ref, ko_ref, vo_ref):
    q = q_ref[...]
    qb = q.astype(bf16)
    kn = kn_ref[...]
    vn = vn_ref[...]
    sink = sink_ref[...]
    s = jnp.einsum('bhc,bsc->bhs', qb, kc_ref[...].astype(bf16), preferred_element_type=f32)
    s_new = jnp.sum(qb.astype(f32) * kn.astype(bf16).astype(f32), axis=-1, keepdims=True)
    m = jnp.maximum(jnp.maximum(jnp.max(s, axis=-1, keepdims=True), s_new), sink)
    p = jnp.exp(s - m)
    p_new = jnp.exp(s_new - m)
    den = jnp.sum(p, axis=-1, keepdims=True) + p_new + jnp.exp(sink - m)
    o = jnp.einsum('bhs,bsc->bhc', (p / den).astype(bf16), vc_ref[...].astype(bf16),
                   preferred_element_type=f32)
    o_ref[...] = o + (p_new / den) * vn
    ko_ref[:, 0:WINDOW - 1, :] = kc_ref[:, 1:WINDOW, :]
    ko_ref[:, WINDOW - 1:WINDOW, :] = kn
    vo_ref[:, 0:WINDOW - 1, :] = vc_ref[:, 1:WINDOW, :]
    vo_ref[:, WINDOW - 1:WINDOW, :] = vn


def _sample_attn(sink, qbd, kc, vc, kn, vn):
    nb = qbd.shape[0]
    blk = lambda s1: pl.BlockSpec((SB, s1, LANES), lambda b: (b, 0, 0))
    return pl.pallas_call(
        _sample_attn_kernel,
        out_shape=(jax.ShapeDtypeStruct((nb, N_HEADS, D_KV), f32),
                   jax.ShapeDtypeStruct(kc.shape, f32),
                   jax.ShapeDtypeStruct(vc.shape, f32)),
        grid=(nb // SB,),
        in_specs=[pl.BlockSpec((1, N_HEADS, 1), lambda b: (0, 0, 0)),
                  blk(N_HEADS), blk(WINDOW), blk(WINDOW), blk(1), blk(1)],
        out_specs=(blk(N_HEADS), blk(WINDOW), blk(WINDOW)),
        compiler_params=pltpu.CompilerParams(dimension_semantics=("arbitrary",),
                                             vmem_limit_bytes=VMEM_LIMIT),
        name="sample_attn",
    )(sink, qbd, kc, vc, kn, vn)


def _sample_merge_kernel(x_ref, ain_ref, bin_ref, yc_ref, szc_ref, g_ref, wpa_ref, wpb_ref, wpc_ref,
                         wout_ref, lng_ref, lnb_ref, y_ref):
    cin = (yc_ref[...] * szc_ref[...]).astype(bf16)
    merged = (g_ref[:, 0:D_MODEL] * _dot(ain_ref[...], wpa_ref[...])
              + g_ref[:, D_MODEL:2 * D_MODEL] * _dot(bin_ref[...], wpb_ref[...])
              + g_ref[:, 2 * D_MODEL:3 * D_MODEL] * _dot(cin, wpc_ref[...]))
    out = _dot(merged.astype(bf16), wout_ref[...])
    y_ref[...] = _layer_norm(ALPHA * x_ref[...] + out, lng_ref[...], lnb_ref[...])


def _sample_merge(x, ain, bin_, yc, szc, g, wpa, wpb, wpc, wout, lng, lnb):
    return pl.pallas_call(
        _sample_merge_kernel,
        out_shape=jax.ShapeDtypeStruct(x.shape, f32),
        compiler_params=pltpu.CompilerParams(vmem_limit_bytes=VMEM_LIMIT),
        name="sample_merge",
    )(x, ain, bin_, yc, szc, g, wpa, wpb, wpc, wout, lng, lnb)


def _rope_tables(positions):
    halfd = HEAD_DIM // 2
    inv = ROPE_THETA ** (-np.arange(halfd, dtype=np.float64) / halfd)
    ang = np.asarray(positions, dtype=np.float64)[:, None] * inv[None, :]
    cos = np.tile(np.cos(ang), (1, LANES // halfd))
    sin = np.tile(np.concatenate([-np.sin(ang), np.sin(ang)], axis=1), (1, LANES // HEAD_DIM))
    return jnp.asarray(cos, f32), jnp.asarray(sin, f32)


def _block_diag(w):
    g, c, _ = w.shape
    eye = jnp.eye(g, dtype=w.dtype)
    return (eye[:, None, :, None] * w[:, :, None, :]).reshape(g * c, g * c)


def kernel(x_prompt, x_sample, state_pool, cache_k_win, cache_v_win, w_in, b_gate, pool_w, pool_scale, sgu_ln_g, sgu_ln_b, sgu_w, sgu_b, attn_sinks, w_proj_a, w_proj_b, w_proj_c, w_out, ln_g, ln_b):
    B, L, _ = x_prompt.shape
    nb = x_sample.shape[0]
    cos_p, sin_p = _rope_tables(np.arange(L))
    cos_s, sin_s = _rope_tables(np.array([PAST_LEN]))
    kv_half = (jnp.arange(LANES)[None, :] // HEAD_DIM) == (jnp.arange(N_HEADS)[:, None] // Q_PER_KV)

    y_p = x_prompt
    y_s = x_sample.reshape(nb, D_MODEL)
    outs = [[] for _ in range(7)]
    for l in range(DEPTH):
        w_in_l = w_in[l].astype(bf16)
        wpa, wpb, wpc, wout = (w[l].astype(bf16) for w in (w_proj_a, w_proj_b, w_proj_c, w_out))
        poolw = _block_diag(pool_w[l]).astype(bf16)
        pscale = pool_scale[l][None, :]
        slng, slnb = sgu_ln_g[l][None, :], sgu_ln_b[l][None, :]
        lng, lnb = ln_g[l][None, :], ln_b[l][None, :]
        sguw = sgu_w[l].reshape(-1, CHUNK)
        sgub = jnp.repeat(sgu_b[l].T, POOL_GC, axis=1)

        y_p, pool_p, k_p, v_p = _prompt_layer(
            y_p, cos_p, sin_p, attn_sinks[l], w_in_l, b_gate[l], poolw, pscale, slng, slnb, sguw, sgub,
            wpa, wpb, wpc, wout, lng, lnb)

        sw0 = jnp.repeat(sgu_w[l][:, 0, 0], POOL_GC)[None, :]
        sb0 = jnp.repeat(sgu_b[l][:, 0], POOL_GC)[None, :]
        q, kn, vn, pool_s, chunk_v, ain, bin_, szc, g = _sample_proj(
            y_s, cos_s, sin_s, w_in_l, b_gate[l], state_pool[l].reshape(nb, POOL_BUF * D_POOL),
            poolw, pscale, slng, slnb, sw0, sb0)
        q3 = q.reshape(nb, N_HEADS, HEAD_DIM)
        qbd = jnp.where(kv_half[None], jnp.concatenate([q3, q3], axis=-1), 0.0)
        o, k_s, v_s = _sample_attn(
            attn_sinks[l].reshape(1, N_HEADS, 1), qbd,
            cache_k_win[l].reshape(nb, WINDOW, D_KV), cache_v_win[l].reshape(nb, WINDOW, D_KV),
            kn.reshape(nb, 1, D_KV), vn.reshape(nb, 1, D_KV))
        o4 = o.reshape(nb, N_KV_HEADS, Q_PER_KV, N_KV_HEADS, HEAD_DIM)
        yc = jnp.stack([o4[:, kv, :, kv, :] for kv in range(N_KV_HEADS)], axis=1).reshape(nb, D_ATTN)
        y_s = _sample_merge(y_s, ain, bin_, yc, szc, g, wpa, wpb, wpc, wout, lng, lnb)

        for lst, val in zip(outs, (
                pool_p, k_p.reshape(B, WINDOW, N_KV_HEADS, HEAD_DIM), v_p.reshape(B, WINDOW, N_KV_HEADS, HEAD_DIM),
                pool_s.reshape(nb, POOL_BUF, D_POOL), k_s.reshape(nb, WINDOW, N_KV_HEADS, HEAD_DIM),
                v_s.reshape(nb, WINDOW, N_KV_HEADS, HEAD_DIM), chunk_v.reshape(nb, 1, D_SGU))):
            lst.append(val)

    return (y_p, y_s.reshape(nb, 1, D_MODEL), *(jnp.stack(o) for o in outs))
```

```python
import functools

import numpy as np
import jax
import jax.numpy as jnp
from jax import lax
from jax.experimental import pallas as pl
from jax.experimental.pallas import tpu as pltpu

D_MODEL = 1024
DEPTH = 2
PAST_LEN = 8192
D_POOL = 256
POOL_WINDOWS = (2, 4, 8, 16)
POOL_GC = 64
POOL_BUF = 15
D_SGU = 256
CHUNK = 128
N_SGU_GROUPS = 4
HEAD_DIM = 64
N_HEADS = 8
N_KV_HEADS = 2
Q_PER_KV = 4
D_ATTN = 512
D_KV = 128
WINDOW = 128
BLOCK = 128
ROPE_THETA = 10000.0
N_BRANCHES = 3
D_IN = 2 * D_POOL + 3 * D_SGU + 2 * D_ATTN + 2 * D_KV + N_BRANCHES * D_MODEL
ALPHA = (2.0 * DEPTH) ** 0.25
LN_EPS = 1e-5
NEG_INF = -1e30
SCALE = HEAD_DIM ** -0.5

OFF_XA, OFF_ZA, OFF_U, OFF_V, OFF_ZB = 0, 256, 512, 768, 1024
OFF_Q, OFF_K, OFF_VV, OFF_ZC, OFF_G = 1280, 1792, 1920, 2048, 2560

LANES = 128
TM = 512
NBLK = TM // BLOCK
HIST = 32
NCHUNK = 256
VMEM_LIMIT = 56 * 1024 * 1024
SB = 16

bf16 = jnp.bfloat16
f32 = jnp.float32


def _dot(a, b):
    return jnp.dot(a, b, preferred_element_type=f32)


def _dot_nt(a, b):
    return lax.dot_general(a, b, (((1,), (1,)), ((), ())), preferred_element_type=f32)


def _silu(z):
    return z * jax.nn.sigmoid(z)


def _layer_norm(x, g, b):
    mu = jnp.mean(x, axis=-1, keepdims=True)
    xc = x - mu
    var = jnp.mean(xc * xc, axis=-1, keepdims=True)
    return xc * lax.rsqrt(var + LN_EPS) * g + b


def _rope128(x, cos, sin_signed):
    lane = lax.broadcasted_iota(jnp.int32, x.shape, 1)
    first_half = (lane % HEAD_DIM) < (HEAD_DIM // 2)
    partner = jnp.where(first_half,
                        pltpu.roll(x, LANES - HEAD_DIM // 2, 1),
                        pltpu.roll(x, HEAD_DIM // 2, 1))
    return x * cos + partner * sin_signed


def _group_select(lane, a0, a1, a2, a3):
    return jnp.where(lane < 64, a0, jnp.where(lane < 128, a1, jnp.where(lane < 192, a2, a3)))


def _layer_spec(arr, layer, grid_rank, single_buffer=False):
    block = (None,) + arr.shape[1:]
    zeros = (0,) * (arr.ndim - 1)
    index_map = lambda *_: (layer,) + zeros
    del grid_rank
    if single_buffer:
        return pl.BlockSpec(block, index_map, pipeline_mode=pl.Buffered(1))
    return pl.BlockSpec(block, index_map)


def _full_spec(arr):
    zeros = (0,) * arr.ndim
    return pl.BlockSpec(arr.shape, lambda *_: zeros)


def _prompt_kernel(layer, sinks_ref, x_ref, cos_ref, sin_ref, w_in_ref, bg_ref, poolw_ref, pscale_ref,
                   slng_ref, slnb_ref, sguw_ref, sgub_ref, wpa_ref, wpb_ref, wpc_ref, wout_ref,
                   lng_ref, lnb_ref,
                   y_ref, pool_out_ref, k_out_ref, v_out_ref,
                   xb_ref, h_ref, ext_ref, s2_ref, s4_ref, s8_ref, qs_ref,
                   ka_ref, kb_ref, kc_ref, kd_ref, va_ref, vb_ref, vc_ref, vd_ref,
                   ain_ref, bin_ref, cin_ref, mg_ref):
    i = pl.program_id(1)
    last = pl.num_programs(1) - 1
    kv_refs = (ka_ref, kb_ref, kc_ref, kd_ref, va_ref, vb_ref, vc_ref, vd_ref)
    row = lambda ref: ref[layer:layer + 1, :]

    @pl.when(i == 0)
    def _():
        ext_ref[0:HIST, :] = jnp.zeros((HIST, D_POOL), f32)
        for r in kv_refs:
            r[0:BLOCK, :] = jnp.zeros((BLOCK, LANES), bf16)

    xb = x_ref[...].astype(bf16)
    xb_ref[...] = xb
    half = OFF_G // 2
    h_ref[:, 0:half] = _dot(xb, w_in_ref[:, 0:half])
    h_ref[:, half:OFF_G] = _dot(xb, w_in_ref[:, half:OFF_G])

    xa = h_ref[:, OFF_XA:OFF_XA + D_POOL]
    ext_ref[HIST:HIST + TM, :] = xa
    n = HIST + TM
    s2_ref[8:n, :] = ext_ref[8:n, :] + ext_ref[7:n - 1, :]
    s4_ref[16:n, :] = s2_ref[16:n, :] + s2_ref[14:n - 2, :]
    s8_ref[24:n, :] = s4_ref[24:n, :] + s4_ref[20:n - 4, :]
    w16 = s8_ref[HIST:n, :] + s8_ref[HIST - 8:n - 8, :]
    lane_p = lax.broadcasted_iota(jnp.int32, (TM, D_POOL), 1)
    row_p = lax.broadcasted_iota(jnp.int32, (TM, D_POOL), 0)
    win = _group_select(lane_p, s2_ref[HIST:n, :], s4_ref[HIST:n, :], s8_ref[HIST:n, :], w16)
    width = _group_select(lane_p, POOL_WINDOWS[0], POOL_WINDOWS[1], POOL_WINDOWS[2], POOL_WINDOWS[3])
    cnt = jnp.minimum(row_p + (i * TM + 1), width).astype(f32)
    pooled = win / cnt - xa
    ya = _dot(pooled.astype(bf16), poolw_ref[...]) * row(pscale_ref)
    za = h_ref[:, OFF_ZA:OFF_ZA + D_POOL]
    ain_ref[...] = (ya * _silu(za)).astype(bf16)
    ext_ref[HIST - 16:HIST, :] = ext_ref[n - 16:n, :]

    vn = _layer_norm(h_ref[:, OFF_V:OFF_V + D_SGU], row(slng_ref), row(slnb_ref)).astype(bf16)
    wr = lax.broadcasted_iota(jnp.int32, (N_SGU_GROUPS * CHUNK, CHUNK), 0) % CHUNK
    wc = lax.broadcasted_iota(jnp.int32, (N_SGU_GROUPS * CHUNK, CHUNK), 1)
    w_s = jnp.where(wc <= wr, sguw_ref[...], 0.0).astype(bf16)
    lane_c = lax.broadcasted_iota(jnp.int32, (CHUNK, D_SGU), 1)
    for j in range(NBLK):
        rows = slice(j * CHUNK, (j + 1) * CHUNK)
        r = _dot(w_s, vn[rows, :])
        s = _group_select(lane_c, r[0:CHUNK], r[CHUNK:2 * CHUNK], r[2 * CHUNK:3 * CHUNK],
                          r[3 * CHUNK:4 * CHUNK]) + sgub_ref[...]
        yb = h_ref[rows, OFF_U:OFF_U + D_SGU] * s
        bin_ref[rows, :] = (yb * _silu(h_ref[rows, OFF_ZB:OFF_ZB + D_SGU])).astype(bf16)

    cos = cos_ref[...]
    sin = sin_ref[...]
    for c in range(D_ATTN // LANES):
        qc = _rope128(h_ref[:, OFF_Q + c * LANES:OFF_Q + (c + 1) * LANES], cos, sin)
        qs_ref[:, c * LANES:(c + 1) * LANES] = (qc * SCALE).astype(bf16)
    kr = _rope128(h_ref[:, OFF_K:OFF_K + D_KV], cos, sin)
    vv = h_ref[:, OFF_VV:OFF_VV + D_KV]
    lane_k = lax.broadcasted_iota(jnp.int32, (TM, LANES), 1)
    lo = lane_k < HEAD_DIM
    for t, (a_ref, b_ref, c_ref, d_ref) in ((kr, kv_refs[0:4]), (vv, kv_refs[4:8])):
        sw = pltpu.roll(t, HEAD_DIM, 1)
        a_ref[BLOCK:BLOCK + TM, :] = jnp.where(lo, t, 0.0).astype(bf16)
        b_ref[BLOCK:BLOCK + TM, :] = jnp.where(lo, 0.0, t).astype(bf16)
        c_ref[BLOCK:BLOCK + TM, :] = jnp.where(lo, sw, 0.0).astype(bf16)
        d_ref[BLOCK:BLOCK + TM, :] = jnp.where(lo, 0.0, sw).astype(bf16)

    qrow = lax.broadcasted_iota(jnp.int32, (2 * BLOCK, 2 * BLOCK), 0) % BLOCK
    kcol = lax.broadcasted_iota(jnp.int32, (2 * BLOCK, 2 * BLOCK), 1)
    band = (kcol >= qrow) & (kcol <= qrow + WINDOW)
    band_first = band & (kcol >= jnp.where(i > 0, 0, BLOCK))
    top = lax.broadcasted_iota(jnp.int32, (2 * BLOCK, 1), 0) < BLOCK
    for j in range(NBLK):
        rows = slice(j * BLOCK, (j + 1) * BLOCK)
        keys = slice(j * BLOCK, j * BLOCK + 2 * BLOCK)
        allowed = band_first if j == 0 else band
        for kv in range(N_KV_HEADS):
            c0 = kv * Q_PER_KV * HEAD_DIM
            h0 = kv * Q_PER_KV
            qst = jnp.concatenate([qs_ref[rows, c0:c0 + LANES], qs_ref[rows, c0 + LANES:c0 + 2 * LANES]], axis=0)
            k_even, k_odd = (ka_ref, kd_ref) if kv == 0 else (kc_ref, kb_ref)
            v_even, v_odd = (va_ref, vd_ref) if kv == 0 else (vc_ref, vb_ref)
            kcat = jnp.concatenate([k_even[keys, :], k_odd[keys, :]], axis=0)
            sc = _dot_nt(qst, kcat)
            probs = []
            for par in range(2):
                sink = jnp.where(top, sinks_ref[layer, h0 + par], sinks_ref[layer, h0 + 2 + par])
                sm = jnp.where(allowed, sc[:, par * 2 * BLOCK:(par + 1) * 2 * BLOCK], NEG_INF)
                m = jnp.maximum(jnp.max(sm, axis=-1, keepdims=True), sink)
                p = jnp.exp(sm - m)
                den = jnp.sum(p, axis=-1, keepdims=True) + jnp.exp(sink - m)
                probs.append((p / den).astype(bf16))
            pcat = jnp.concatenate(probs, axis=1)
            vcat = jnp.concatenate([v_even[keys, :], v_odd[keys, :]], axis=0)
            o = _dot(pcat, vcat)
            for pr in range(2):
                cols = slice(c0 + pr * LANES, c0 + (pr + 1) * LANES)
                zc = h_ref[rows, OFF_ZC + c0 + pr * LANES:OFF_ZC + c0 + (pr + 1) * LANES]
                cin_ref[rows, cols] = (o[pr * BLOCK:(pr + 1) * BLOCK] * _silu(zc)).astype(bf16)

    @pl.when(i == last)
    def _():
        pool_out_ref[...] = ext_ref[n - POOL_BUF:n, :]
        k_out_ref[...] = kr[TM - WINDOW:TM, :].T
        v_out_ref[...] = vv[TM - WINDOW:TM, :].T

    for r in kv_refs:
        r[0:BLOCK, :] = r[TM:TM + BLOCK, :]

    xb = xb_ref[...]
    for c in range(D_MODEL // NCHUNK):
        cols = slice(c * NCHUNK, (c + 1) * NCHUNK)
        acc = None
        for br, (in_ref, wp_ref) in enumerate(((ain_ref, wpa_ref), (bin_ref, wpb_ref), (cin_ref, wpc_ref))):
            g0 = OFF_G + br * D_MODEL + c * NCHUNK
            gate = jax.nn.sigmoid(_dot(xb, w_in_ref[:, g0:g0 + NCHUNK]) + bg_ref[br:br + 1, cols])
            term = gate * _dot(in_ref[...], wp_ref[:, cols])
            acc = term if acc is None else acc + term
        mg_ref[:, cols] = acc.astype(bf16)

    out = _dot(mg_ref[...], wout_ref[...])
    y_ref[...] = _layer_norm(ALPHA * x_ref[...] + out, row(lng_ref), row(lnb_ref))


def _prompt_layer(layer, x, cos, sin, sinks, w_in, b_gate, poolw, pscale, slng, slnb, sguw, sgub,
                  wpa, wpb, wpc, wout, lng, lnb):
    B, L, _ = x.shape
    grid = (B, L // TM)
    row_spec = lambda w: pl.BlockSpec((TM, w), lambda b, i: (i, 0))
    lspec = lambda a: _layer_spec(a, layer, 2, single_buffer=True)
    in_specs = [
        pl.BlockSpec(memory_space=pltpu.SMEM),
        pl.BlockSpec((None, TM, D_MODEL), lambda b, i: (b, i, 0)),
        row_spec(LANES), row_spec(LANES),
        lspec(w_in), lspec(b_gate), lspec(poolw), _full_spec(pscale), _full_spec(slng), _full_spec(slnb),
        lspec(sguw), lspec(sgub), lspec(wpa), lspec(wpb), lspec(wpc), lspec(wout),
        _full_spec(lng), _full_spec(lnb),
    ]
    out_shape = (
        jax.ShapeDtypeStruct((B, L, D_MODEL), f32),
        jax.ShapeDtypeStruct((B, POOL_BUF, D_POOL), f32),
        jax.ShapeDtypeStruct((B, D_KV, WINDOW), f32),
        jax.ShapeDtypeStruct((B, D_KV, WINDOW), f32),
    )
    out_specs = (
        pl.BlockSpec((None, TM, D_MODEL), lambda b, i: (b, i, 0)),
        pl.BlockSpec((None, POOL_BUF, D_POOL), lambda b, i: (b, 0, 0)),
        pl.BlockSpec((None, D_KV, WINDOW), lambda b, i: (b, 0, 0)),
        pl.BlockSpec((None, D_KV, WINDOW), lambda b, i: (b, 0, 0)),
    )
    kv_scratch = [pltpu.VMEM((BLOCK + TM, LANES), bf16) for _ in range(8)]
    scratch = [
        pltpu.VMEM((TM, D_MODEL), bf16),
        pltpu.VMEM((TM, OFF_G), f32),
        pltpu.VMEM((HIST + TM, D_POOL), f32),
        pltpu.VMEM((HIST + TM, D_POOL), f32),
        pltpu.VMEM((HIST + TM, D_POOL), f32),
        pltpu.VMEM((HIST + TM, D_POOL), f32),
        pltpu.VMEM((TM, D_ATTN), bf16),
        *kv_scratch,
        pltpu.VMEM((TM, D_POOL), bf16),
        pltpu.VMEM((TM, D_SGU), bf16),
        pltpu.VMEM((TM, D_ATTN), bf16),
        pltpu.VMEM((TM, D_MODEL), bf16),
    ]
    return pl.pallas_call(
        functools.partial(_prompt_kernel, layer),
        out_shape=out_shape,
        grid=grid,
        in_specs=in_specs,
        out_specs=out_specs,
        scratch_shapes=scratch,
        compiler_params=pltpu.CompilerParams(
            dimension_semantics=("arbitrary", "arbitrary"),
            vmem_limit_bytes=VMEM_LIMIT),
        name="prompt_layer",
    )(sinks, x, cos, sin, w_in, b_gate, poolw, pscale, slng, slnb, sguw, sgub,
      wpa, wpb, wpc, wout, lng, lnb)


def _sample_proj_kernel(layer, x_ref, cos_ref, sin_ref, w_in_ref, bg_ref, pb_ref, poolw_ref, pscale_ref,
                        slng_ref, slnb_ref, sw0_ref, sb0_ref,
                        q_ref, k_ref, v_ref, kt_ref, vt_ref, pool_ref, vn_ref, ain_ref, bin_ref, szc_ref,
                        g_ref):
    row = lambda ref: ref[layer:layer + 1, :]
    xb = x_ref[...].astype(bf16)
    h = _dot(xb, w_in_ref[:, 0:OFF_G])
    xa = h[:, OFF_XA:OFF_XA + D_POOL]
    lane = lax.broadcasted_iota(jnp.int32, xa.shape, 1)
    first_row = _group_select(lane, *(POOL_BUF - (w - 1) for w in POOL_WINDOWS))
    win = xa
    for r in range(POOL_BUF):
        win = win + jnp.where(first_row <= r, pb_ref[r], 0.0)
    width = _group_select(lane, *POOL_WINDOWS).astype(f32)
    pooled = win / width - xa
    ya = _dot(pooled.astype(bf16), poolw_ref[...]) * row(pscale_ref)
    ain_ref[...] = (ya * _silu(h[:, OFF_ZA:OFF_ZA + D_POOL])).astype(bf16)
    for r in range(POOL_BUF - 1):
        pool_ref[r] = pb_ref[r + 1]
    pool_ref[POOL_BUF - 1] = xa
    vn = _layer_norm(h[:, OFF_V:OFF_V + D_SGU], row(slng_ref), row(slnb_ref))
    vn_ref[...] = vn
    yb = h[:, OFF_U:OFF_U + D_SGU] * (row(sw0_ref) * vn + row(sb0_ref))
    bin_ref[...] = (yb * _silu(h[:, OFF_ZB:OFF_ZB + D_SGU])).astype(bf16)
    cos = cos_ref[...]
    sin = sin_ref[...]
    for c in range(D_ATTN // LANES):
        q_ref[:, c * LANES:(c + 1) * LANES] = _rope128(
            h[:, OFF_Q + c * LANES:OFF_Q + (c + 1) * LANES], cos, sin) * SCALE
    kr = _rope128(h[:, OFF_K:OFF_K + D_KV], cos, sin)
    vv = h[:, OFF_VV:OFF_VV + D_KV]
    k_ref[...] = kr
    v_ref[...] = vv
    kt_ref[...] = kr.T
    vt_ref[...] = vv.T
    szc_ref[...] = _silu(h[:, OFF_ZC:OFF_ZC + D_ATTN])
    for br in range(N_BRANCHES):
        cols = slice(br * D_MODEL, (br + 1) * D_MODEL)
        g_ref[:, cols] = jax.nn.sigmoid(
            _dot(xb, w_in_ref[:, OFF_G + br * D_MODEL:OFF_G + (br + 1) * D_MODEL]) + bg_ref[br:br + 1, :])


def _sample_proj(layer, x, cos, sin, w_in, b_gate, pool_t, poolw, pscale, slng, slnb, sw0, sb0):
    nb = x.shape[0]
    out_shape = (
        jax.ShapeDtypeStruct((nb, D_ATTN), f32),
        jax.ShapeDtypeStruct((nb, D_KV), f32),
        jax.ShapeDtypeStruct((nb, D_KV), f32),
        jax.ShapeDtypeStruct((D_KV, nb), f32),
        jax.ShapeDtypeStruct((D_KV, nb), f32),
        jax.ShapeDtypeStruct((POOL_BUF, nb, D_POOL), f32),
        jax.ShapeDtypeStruct((nb, D_SGU), f32),
        jax.ShapeDtypeStruct((nb, D_POOL), bf16),
        jax.ShapeDtypeStruct((nb, D_SGU), bf16),
        jax.ShapeDtypeStruct((nb, D_ATTN), f32),
        jax.ShapeDtypeStruct((nb, N_BRANCHES * D_MODEL), f32),
    )
    lspec = lambda a: _layer_spec(a, layer, 1)
    args = (x, cos, sin, w_in, b_gate, pool_t, poolw, pscale, slng, slnb, sw0, sb0)
    in_specs = [_full_spec(x), _full_spec(cos), _full_spec(sin), lspec(w_in), lspec(b_gate), lspec(pool_t),
                lspec(poolw), _full_spec(pscale), _full_spec(slng), _full_spec(slnb), _full_spec(sw0),
                _full_spec(sb0)]
    out_specs = tuple(pl.BlockSpec(s.shape, functools.partial(lambda nd, i: (0,) * nd, len(s.shape)))
                      for s in out_shape)
    return pl.pallas_call(
        functools.partial(_sample_proj_kernel, layer),
        out_shape=out_shape,
        grid=(1,),
        in_specs=in_specs,
        out_specs=out_specs,
        compiler_params=pltpu.CompilerParams(dimension_semantics=("arbitrary",),
                                             vmem_limit_bytes=VMEM_LIMIT),
        name="sample_proj",
    )(*args)


def _sample_attn_kernel(sink_ref, q_ref, kc_ref, vc_ref, kn_ref, vn_ref, o_ref):
    for kv in range(N_KV_HEADS):
        qb = q_ref[:, kv].astype(bf16)
        kn = kn_ref[:, kv]
        vn = vn_ref[:, kv]
        sink = sink_ref[kv][None]
        s = jnp.einsum('bgd,bdw->bgw', qb, kc_ref[:, kv].astype(bf16), preferred_element_type=f32)
        s_new = jnp.sum(qb.astype(f32) * kn.astype(bf16).astype(f32), axis=-1, keepdims=True)
        m = jnp.maximum(jnp.maximum(jnp.max(s, axis=-1, keepdims=True), s_new), sink)
        p = jnp.exp(s - m)
        p_new = jnp.exp(s_new - m)
        den = jnp.sum(p, axis=-1, keepdims=True) + p_new + jnp.exp(sink - m)
        o = jnp.einsum('bgw,bdw->bgd', (p / den).astype(bf16), vc_ref[:, kv].astype(bf16),
                       preferred_element_type=f32)
        o_ref[:, kv] = o + (p_new / den) * vn


def _sample_attn(layer, sink, q4, kc, vc, kn, vn):
    nb = q4.shape[0]
    cache_spec = pl.BlockSpec((None, SB, N_KV_HEADS, HEAD_DIM, WINDOW), lambda b: (layer, b, 0, 0, 0))
    blk = lambda s2, s3: pl.BlockSpec((SB, N_KV_HEADS, s2, s3), lambda b: (b, 0, 0, 0))
    return pl.pallas_call(
        _sample_attn_kernel,
        out_shape=jax.ShapeDtypeStruct((nb, N_KV_HEADS, Q_PER_KV, HEAD_DIM), f32),
        grid=(nb // SB,),
        in_specs=[_layer_spec(sink, layer, 1), blk(Q_PER_KV, HEAD_DIM), cache_spec, cache_spec,
                  blk(1, HEAD_DIM), blk(1, HEAD_DIM)],
        out_specs=blk(Q_PER_KV, HEAD_DIM),
        compiler_params=pltpu.CompilerParams(dimension_semantics=("arbitrary",),
                                             vmem_limit_bytes=VMEM_LIMIT),
        name="sample_attn",
    )(sink, q4, kc, vc, kn, vn)


def _cache_slide_kernel(kt_ref, vt_ref, kc_ref, vc_ref, ko_ref, vo_ref):
    j = pl.program_id(1)
    lane = lax.broadcasted_iota(jnp.int32, (D_KV, WINDOW), 1)
    shift = jnp.where(j == 0, 0, LANES - j * SB)
    for t_ref, c_ref, o_ref in ((kt_ref, kc_ref, ko_ref), (vt_ref, vc_ref, vo_ref)):
        new_cols = pltpu.roll(t_ref[...], shift, 1)
        for b in range(SB):
            old = c_ref[b].reshape(D_KV, WINDOW)
            slid = pltpu.roll(old, WINDOW - 1, 1)
            newest = jnp.broadcast_to(new_cols[:, b:b + 1], (D_KV, WINDOW))
            o_ref[b] = jnp.where(lane == WINDOW - 1, newest, slid).reshape(N_KV_HEADS, HEAD_DIM, WINDOW)


def _cache_slide(kt, vt, kc, vc):
    depth, nb = kc.shape[0], kc.shape[1]
    cache_spec = pl.BlockSpec((None, SB, N_KV_HEADS, HEAD_DIM, WINDOW), lambda l, b: (l, b, 0, 0, 0))
    new_spec = pl.BlockSpec((None, D_KV, nb), lambda l, b: (l, 0, 0))
    return pl.pallas_call(
        _cache_slide_kernel,
        out_shape=(jax.ShapeDtypeStruct(kc.shape, f32), jax.ShapeDtypeStruct(vc.shape, f32)),
        grid=(depth, nb // SB),
        in_specs=[new_spec, new_spec, cache_spec, cache_spec],
        out_specs=(cache_spec, cache_spec),
        compiler_params=pltpu.CompilerParams(dimension_semantics=("arbitrary", "arbitrary"),
                                             vmem_limit_bytes=VMEM_LIMIT),
        name="cache_slide",
    )(kt, vt, kc, vc)


def _sample_merge_kernel(layer, x_ref, ain_ref, bin_ref, yc_ref, szc_ref, g_ref, wpa_ref, wpb_ref, wpc_ref,
                         wout_ref, lng_ref, lnb_ref, y_ref):
    row = lambda ref: ref[layer:layer + 1, :]
    cin = (yc_ref[...] * szc_ref[...]).astype(bf16)
    merged = (g_ref[:, 0:D_MODEL] * _dot(ain_ref[...], wpa_ref[...])
              + g_ref[:, D_MODEL:2 * D_MODEL] * _dot(bin_ref[...], wpb_ref[...])
              + g_ref[:, 2 * D_MODEL:3 * D_MODEL] * _dot(cin, wpc_ref[...]))
    out = _dot(merged.astype(bf16), wout_ref[...])
    y_ref[...] = _layer_norm(ALPHA * x_ref[...] + out, row(lng_ref), row(lnb_ref))


def _sample_merge(layer, x, ain, bin_, yc, szc, g, wpa, wpb, wpc, wout, lng, lnb):
    lspec = lambda a: _layer_spec(a, layer, 1)
    return pl.pallas_call(
        functools.partial(_sample_merge_kernel, layer),
        out_shape=jax.ShapeDtypeStruct(x.shape, f32),
        grid=(1,),
        in_specs=[_full_spec(x), _full_spec(ain), _full_spec(bin_), _full_spec(yc), _full_spec(szc),
                  _full_spec(g), lspec(wpa), lspec(wpb), lspec(wpc), lspec(wout), _full_spec(lng),
                  _full_spec(lnb)],
        out_specs=_full_spec(x),
        compiler_params=pltpu.CompilerParams(dimension_semantics=("arbitrary",),
                                             vmem_limit_bytes=VMEM_LIMIT),
        name="sample_merge",
    )(x, ain, bin_, yc, szc, g, wpa, wpb, wpc, wout, lng, lnb)


def _rope_tables(positions):
    halfd = HEAD_DIM // 2
    inv = ROPE_THETA ** (-np.arange(halfd, dtype=np.float64) / halfd)
    ang = np.asarray(positions, dtype=np.float64)[:, None] * inv[None, :]
    cos = np.tile(np.cos(ang), (1, LANES // halfd))
    sin = np.tile(np.concatenate([-np.sin(ang), np.sin(ang)], axis=1), (1, LANES // HEAD_DIM))
    return jnp.asarray(cos, f32), jnp.asarray(sin, f32)


def _block_diag(w):
    nl, g, c, _ = w.shape
    eye = jnp.eye(g, dtype=w.dtype)
    return (eye[None, :, None, :, None] * w[:, :, :, None, :]).reshape(nl, g * c, g * c)


def kernel(x_prompt, x_sample, state_pool, cache_k_win, cache_v_win, w_in, b_gate, pool_w, pool_scale, sgu_ln_g, sgu_ln_b, sgu_w, sgu_b, attn_sinks, w_proj_a, w_proj_b, w_proj_c, w_out, ln_g, ln_b):
    B, L, _ = x_prompt.shape
    nb = x_sample.shape[0]
    cos_p, sin_p = _rope_tables(np.arange(L))
    cos_s, sin_s = _rope_tables(np.array([PAST_LEN]))

    w_in_b, wpa, wpb, wpc, wout = (w.astype(bf16) for w in (w_in, w_proj_a, w_proj_b, w_proj_c, w_out))
    poolw = _block_diag(pool_w).astype(bf16)
    sguw = sgu_w.reshape(DEPTH, N_SGU_GROUPS * CHUNK, CHUNK)
    sgub = jnp.repeat(jnp.swapaxes(sgu_b, 1, 2), POOL_GC, axis=2)
    sw0 = jnp.repeat(sgu_w[:, :, 0, 0], POOL_GC, axis=1)
    sb0 = jnp.repeat(sgu_b[:, :, 0], POOL_GC, axis=1)
    sink4 = attn_sinks.reshape(DEPTH, N_KV_HEADS, Q_PER_KV, 1)
    kc_t = jnp.transpose(cache_k_win, (0, 1, 3, 4, 2))
    vc_t = jnp.transpose(cache_v_win, (0, 1, 3, 4, 2))
    pool_t = jnp.transpose(state_pool, (0, 2, 1, 3))

    y_p = x_prompt
    y_s = x_sample.reshape(nb, D_MODEL)
    pool_p, k_p, v_p, pool_s, chunk_v, kts, vts = ([] for _ in range(7))
    for l in range(DEPTH):
        y_p, pp, kp, vp = _prompt_layer(
            l, y_p, cos_p, sin_p, attn_sinks, w_in_b, b_gate, poolw, pool_scale, sgu_ln_g, sgu_ln_b, sguw,
            sgub, wpa, wpb, wpc, wout, ln_g, ln_b)
        pool_p.append(pp); k_p.append(kp); v_p.append(vp)

        q, kn, vn, kt, vt, ps, cv, ain, bin_, szc, g = _sample_proj(
            l, y_s, cos_s, sin_s, w_in_b, b_gate, pool_t, poolw, pool_scale, sgu_ln_g, sgu_ln_b, sw0, sb0)
        o = _sample_attn(
            l, sink4, q.reshape(nb, N_KV_HEADS, Q_PER_KV, HEAD_DIM), kc_t, vc_t,
            kn.reshape(nb, N_KV_HEADS, 1, HEAD_DIM), vn.reshape(nb, N_KV_HEADS, 1, HEAD_DIM))
        y_s = _sample_merge(l, y_s, ain, bin_, o.reshape(nb, D_ATTN), szc, g, wpa, wpb, wpc, wout, ln_g, ln_b)
        pool_s.append(ps); chunk_v.append(cv); kts.append(kt); vts.append(vt)

    k_s, v_s = _cache_slide(jnp.stack(kts), jnp.stack(vts), kc_t, vc_t)

    to_cache = lambda a: jnp.transpose(a, (0, 1, 4, 2, 3))
    prompt_cache = lambda lst: to_cache(jnp.stack(lst).reshape(DEPTH, B, N_KV_HEADS, HEAD_DIM, WINDOW))
    return (y_p, y_s.reshape(nb, 1, D_MODEL),
            jnp.stack(pool_p), prompt_cache(k_p), prompt_cache(v_p),
            jnp.transpose(jnp.stack(pool_s), (0, 2, 1, 3)), to_cache(k_s), to_cache(v_s),
            jnp.stack(chunk_v).reshape(DEPTH, nb, 1, D_SGU))
```

```python
import functools

import numpy as np
import jax
import jax.numpy as jnp
from jax import lax
from jax.experimental import pallas as pl
from jax.experimental.pallas import tpu as pltpu

D_MODEL = 1024
DEPTH = 2
PAST_LEN = 8192
D_POOL = 256
POOL_WINDOWS = (2, 4, 8, 16)
POOL_GC = 64
POOL_BUF = 15
D_SGU = 256
CHUNK = 128
N_SGU_GROUPS = 4
HEAD_DIM = 64
N_HEADS = 8
N_KV_HEADS = 2
Q_PER_KV = 4
D_ATTN = 512
D_KV = 128
WINDOW = 128
BLOCK = 128
ROPE_THETA = 10000.0
N_BRANCHES = 3
D_IN = 2 * D_POOL + 3 * D_SGU + 2 * D_ATTN + 2 * D_KV + N_BRANCHES * D_MODEL
ALPHA = (2.0 * DEPTH) ** 0.25
LN_EPS = 1e-5
NEG_INF = -1e30
SCALE = HEAD_DIM ** -0.5

OFF_XA, OFF_ZA, OFF_U, OFF_V, OFF_ZB = 0, 256, 512, 768, 1024
OFF_Q, OFF_K, OFF_VV, OFF_ZC, OFF_G = 1280, 1792, 1920, 2048, 2560

LANES = 128
TM = 512
NBLK = TM // BLOCK
HIST = 32
NCHUNK = 256
VMEM_LIMIT = 56 * 1024 * 1024
SB = 16

bf16 = jnp.bfloat16
f32 = jnp.float32


def _dot(a, b):
    return jnp.dot(a, b, preferred_element_type=f32)


def _dot_nt(a, b):
    return lax.dot_general(a, b, (((1,), (1,)), ((), ())), preferred_element_type=f32)


def _silu(z):
    return z * jax.nn.sigmoid(z)


def _layer_norm(x, g, b):
    mu = jnp.mean(x, axis=-1, keepdims=True)
    xc = x - mu
    var = jnp.mean(xc * xc, axis=-1, keepdims=True)
    return xc * lax.rsqrt(var + LN_EPS) * g + b


def _rope128(x, cos, sin_signed):
    lane = lax.broadcasted_iota(jnp.int32, x.shape, 1)
    first_half = (lane % HEAD_DIM) < (HEAD_DIM // 2)
    partner = jnp.where(first_half,
                        pltpu.roll(x, LANES - HEAD_DIM // 2, 1),
                        pltpu.roll(x, HEAD_DIM // 2, 1))
    return x * cos + partner * sin_signed


def _group_select(lane, a0, a1, a2, a3):
    return jnp.where(lane < 64, a0, jnp.where(lane < 128, a1, jnp.where(lane < 192, a2, a3)))


def _layer_spec(arr, layer, grid_rank, single_buffer=False):
    block = (None,) + arr.shape[1:]
    zeros = (0,) * (arr.ndim - 1)
    index_map = lambda *_: (layer,) + zeros
    del grid_rank
    if single_buffer:
        return pl.BlockSpec(block, index_map, pipeline_mode=pl.Buffered(1))
    return pl.BlockSpec(block, index_map)


def _full_spec(arr):
    zeros = (0,) * arr.ndim
    return pl.BlockSpec(arr.shape, lambda *_: zeros)


def _prompt_kernel(layer, sinks_ref, x_ref, cos_ref, sin_ref, w_in_ref, bg_ref, poolw_ref, pscale_ref,
                   slng_ref, slnb_ref, sguw_ref, sgub_ref, wpa_ref, wpb_ref, wpc_ref, wout_ref,
                   lng_ref, lnb_ref,
                   y_ref, pool_out_ref, k_out_ref, v_out_ref,
                   xb_ref, h_ref, ext_ref, s2_ref, s4_ref, s8_ref, qs_ref,
                   ka_ref, kb_ref, kc_ref, kd_ref, va_ref, vb_ref, vc_ref, vd_ref,
                   ain_ref, bin_ref, cin_ref, mg_ref, gate_ref):
    i = pl.program_id(1)
    last = pl.num_programs(1) - 1
    kv_refs = (ka_ref, kb_ref, kc_ref, kd_ref, va_ref, vb_ref, vc_ref, vd_ref)
    row = lambda ref: ref[layer:layer + 1, :]

    @pl.when(i == 0)
    def _():
        ext_ref[0:HIST, :] = jnp.zeros((HIST, D_POOL), f32)
        for r in kv_refs:
            r[0:BLOCK, :] = jnp.zeros((BLOCK, LANES), bf16)

    xb = x_ref[...].astype(bf16)
    xb_ref[...] = xb
    half = OFF_G // 2
    h_ref[:, 0:half] = _dot(xb, w_in_ref[:, 0:half])
    h_ref[:, half:OFF_G] = _dot(xb, w_in_ref[:, half:OFF_G])

    xa = h_ref[:, OFF_XA:OFF_XA + D_POOL]
    ext_ref[HIST:HIST + TM, :] = xa
    n = HIST + TM
    s2_ref[8:n, :] = ext_ref[8:n, :] + ext_ref[7:n - 1, :]
    s4_ref[16:n, :] = s2_ref[16:n, :] + s2_ref[14:n - 2, :]
    s8_ref[24:n, :] = s4_ref[24:n, :] + s4_ref[20:n - 4, :]
    w16 = s8_ref[HIST:n, :] + s8_ref[HIST - 8:n - 8, :]
    lane_p = lax.broadcasted_iota(jnp.int32, (TM, D_POOL), 1)
    row_p = lax.broadcasted_iota(jnp.int32, (TM, D_POOL), 0)
    win = _group_select(lane_p, s2_ref[HIST:n, :], s4_ref[HIST:n, :], s8_ref[HIST:n, :], w16)
    width = _group_select(lane_p, POOL_WINDOWS[0], POOL_WINDOWS[1], POOL_WINDOWS[2], POOL_WINDOWS[3])
    cnt = jnp.minimum(row_p + (i * TM + 1), width).astype(f32)
    pooled = win / cnt - xa
    ya = _dot(pooled.astype(bf16), poolw_ref[...]) * row(pscale_ref)
    za = h_ref[:, OFF_ZA:OFF_ZA + D_POOL]
    ain_ref[...] = (ya * _silu(za)).astype(bf16)
    ext_ref[HIST - 16:HIST, :] = ext_ref[n - 16:n, :]

    vn = _layer_norm(h_ref[:, OFF_V:OFF_V + D_SGU], row(slng_ref), row(slnb_ref)).astype(bf16)
    wr = lax.broadcasted_iota(jnp.int32, (N_SGU_GROUPS * CHUNK, CHUNK), 0) % CHUNK
    wc = lax.broadcasted_iota(jnp.int32, (N_SGU_GROUPS * CHUNK, CHUNK), 1)
    w_s = jnp.where(wc <= wr, sguw_ref[...], 0.0).astype(bf16)
    lane_c = lax.broadcasted_iota(jnp.int32, (CHUNK, D_SGU), 1)
    for j in range(NBLK):
        rows = slice(j * CHUNK, (j + 1) * CHUNK)
        r = _dot(w_s, vn[rows, :])
        s = _group_select(lane_c, r[0:CHUNK], r[CHUNK:2 * CHUNK], r[2 * CHUNK:3 * CHUNK],
                          r[3 * CHUNK:4 * CHUNK]) + sgub_ref[...]
        yb = h_ref[rows, OFF_U:OFF_U + D_SGU] * s
        bin_ref[rows, :] = (yb * _silu(h_ref[rows, OFF_ZB:OFF_ZB + D_SGU])).astype(bf16)

    cos = cos_ref[...]
    sin = sin_ref[...]
    for c in range(D_ATTN // LANES):
        qc = _rope128(h_ref[:, OFF_Q + c * LANES:OFF_Q + (c + 1) * LANES], cos, sin)
        qs_ref[:, c * LANES:(c + 1) * LANES] = (qc * SCALE).astype(bf16)
    kr = _rope128(h_ref[:, OFF_K:OFF_K + D_KV], cos, sin)
    vv = h_ref[:, OFF_VV:OFF_VV + D_KV]
    lane_k = lax.broadcasted_iota(jnp.int32, (TM, LANES), 1)
    lo = lane_k < HEAD_DIM
    for t, (a_ref, b_ref, c_ref, d_ref) in ((kr, kv_refs[0:4]), (vv, kv_refs[4:8])):
        sw = pltpu.roll(t, HEAD_DIM, 1)
        a_ref[BLOCK:BLOCK + TM, :] = jnp.where(lo, t, 0.0).astype(bf16)
        b_ref[BLOCK:BLOCK + TM, :] = jnp.where(lo, 0.0, t).astype(bf16)
        c_ref[BLOCK:BLOCK + TM, :] = jnp.where(lo, sw, 0.0).astype(bf16)
        d_ref[BLOCK:BLOCK + TM, :] = jnp.where(lo, 0.0, sw).astype(bf16)

    qrow = lax.broadcasted_iota(jnp.int32, (2 * BLOCK, 2 * BLOCK), 0) % BLOCK
    kcol = lax.broadcasted_iota(jnp.int32, (2 * BLOCK, 2 * BLOCK), 1)
    band = (kcol >= qrow) & (kcol <= qrow + WINDOW)
    band_first = band & (kcol >= jnp.where(i > 0, 0, BLOCK))
    top = lax.broadcasted_iota(jnp.int32, (2 * BLOCK, 1), 0) < BLOCK
    n_gate = N_BRANCHES * D_MODEL // NCHUNK
    n_unit = NBLK * N_KV_HEADS
    gate_sched = [range(u * n_gate // n_unit, (u + 1) * n_gate // n_unit) for u in range(n_unit)]

    def scores(u):
        j, kv = divmod(u, N_KV_HEADS)
        rows = slice(j * BLOCK, (j + 1) * BLOCK)
        keys = slice(j * BLOCK, j * BLOCK + 2 * BLOCK)
        c0 = kv * Q_PER_KV * HEAD_DIM
        qst = jnp.concatenate([qs_ref[rows, c0:c0 + LANES], qs_ref[rows, c0 + LANES:c0 + 2 * LANES]], axis=0)
        k_even, k_odd = (ka_ref, kd_ref) if kv == 0 else (kc_ref, kb_ref)
        kcat = jnp.concatenate([k_even[keys, :], k_odd[keys, :]], axis=0)
        return _dot_nt(qst, kcat)

    def attend(u, sc):
        j, kv = divmod(u, N_KV_HEADS)
        rows = slice(j * BLOCK, (j + 1) * BLOCK)
        keys = slice(j * BLOCK, j * BLOCK + 2 * BLOCK)
        allowed = band_first if j == 0 else band
        c0 = kv * Q_PER_KV * HEAD_DIM
        h0 = kv * Q_PER_KV
        v_even, v_odd = (va_ref, vd_ref) if kv == 0 else (vc_ref, vb_ref)
        probs = []
        for par in range(2):
            sink = jnp.where(top, sinks_ref[layer, h0 + par], sinks_ref[layer, h0 + 2 + par])
            sm = jnp.where(allowed, sc[:, par * 2 * BLOCK:(par + 1) * 2 * BLOCK], NEG_INF)
            m = jnp.maximum(jnp.max(sm, axis=-1, keepdims=True), sink)
            p = jnp.exp(sm - m)
            den = jnp.sum(p, axis=-1, keepdims=True) + jnp.exp(sink - m)
            probs.append((p / den).astype(bf16))
        pcat = jnp.concatenate(probs, axis=1)
        vcat = jnp.concatenate([v_even[keys, :], v_odd[keys, :]], axis=0)
        o = _dot(pcat, vcat)
        for pr in range(2):
            cols = slice(c0 + pr * LANES, c0 + (pr + 1) * LANES)
            zc = h_ref[rows, OFF_ZC + c0 + pr * LANES:OFF_ZC + c0 + (pr + 1) * LANES]
            cin_ref[rows, cols] = (o[pr * BLOCK:(pr + 1) * BLOCK] * _silu(zc)).astype(bf16)

    sc_next = scores(0)
    for u in range(n_unit):
        sc = sc_next
        for gc in gate_sched[u]:
            gcols = slice(gc * NCHUNK, (gc + 1) * NCHUNK)
            gate_ref[:, gcols] = _dot(xb, w_in_ref[:, OFF_G + gc * NCHUNK:OFF_G + (gc + 1) * NCHUNK])
        if u + 1 < n_unit:
            sc_next = scores(u + 1)
        attend(u, sc)

    @pl.when(i == last)
    def _():
        pool_out_ref[...] = ext_ref[n - POOL_BUF:n, :]
        k_out_ref[...] = kr[TM - WINDOW:TM, :].T
        v_out_ref[...] = vv[TM - WINDOW:TM, :].T

    for r in kv_refs:
        r[0:BLOCK, :] = r[TM:TM + BLOCK, :]

    for c in range(D_MODEL // NCHUNK):
        cols = slice(c * NCHUNK, (c + 1) * NCHUNK)
        acc = None
        for br, (in_ref, wp_ref) in enumerate(((ain_ref, wpa_ref), (bin_ref, wpb_ref), (cin_ref, wpc_ref))):
            g0 = br * D_MODEL + c * NCHUNK
            gate = jax.nn.sigmoid(gate_ref[:, g0:g0 + NCHUNK] + bg_ref[br:br + 1, cols])
            term = gate * _dot(in_ref[...], wp_ref[:, cols])
            acc = term if acc is None else acc + term
        mg_ref[:, cols] = acc.astype(bf16)

    out = _dot(mg_ref[...], wout_ref[...])
    y_ref[...] = _layer_norm(ALPHA * x_ref[...] + out, row(lng_ref), row(lnb_ref))


def _prompt_layer(layer, x, cos, sin, sinks, w_in, b_gate, poolw, pscale, slng, slnb, sguw, sgub,
                  wpa, wpb, wpc, wout, lng, lnb):
    B, L, _ = x.shape
    grid = (B, L // TM)
    row_spec = lambda w: pl.BlockSpec((TM, w), lambda b, i: (i, 0))
    lspec = lambda a: _layer_spec(a, layer, 2, single_buffer=True)
    in_specs = [
        pl.BlockSpec(memory_space=pltpu.SMEM),
        pl.BlockSpec((None, TM, D_MODEL), lambda b, i: (b, i, 0)),
        row_spec(LANES), row_spec(LANES),
        lspec(w_in), lspec(b_gate), lspec(poolw), _full_spec(pscale), _full_spec(slng), _full_spec(slnb),
        lspec(sguw), lspec(sgub), lspec(wpa), lspec(wpb), lspec(wpc), lspec(wout),
        _full_spec(lng), _full_spec(lnb),
    ]
    out_shape = (
        jax.ShapeDtypeStruct((B, L, D_MODEL), f32),
        jax.ShapeDtypeStruct((B, POOL_BUF, D_POOL), f32),
        jax.ShapeDtypeStruct((B, D_KV, WINDOW), f32),
        jax.ShapeDtypeStruct((B, D_KV, WINDOW), f32),
    )
    out_specs = (
        pl.BlockSpec((None, TM, D_MODEL), lambda b, i: (b, i, 0)),
        pl.BlockSpec((None, POOL_BUF, D_POOL), lambda b, i: (b, 0, 0)),
        pl.BlockSpec((None, D_KV, WINDOW), lambda b, i: (b, 0, 0)),
        pl.BlockSpec((None, D_KV, WINDOW), lambda b, i: (b, 0, 0)),
    )
    kv_scratch = [pltpu.VMEM((BLOCK + TM, LANES), bf16) for _ in range(8)]
    scratch = [
        pltpu.VMEM((TM, D_MODEL), bf16),
        pltpu.VMEM((TM, OFF_G), f32),
        pltpu.VMEM((HIST + TM, D_POOL), f32),
        pltpu.VMEM((HIST + TM, D_POOL), f32),
        pltpu.VMEM((HIST + TM, D_POOL), f32),
        pltpu.VMEM((HIST + TM, D_POOL), f32),
        pltpu.VMEM((TM, D_ATTN), bf16),
        *kv_scratch,
        pltpu.VMEM((TM, D_POOL), bf16),
        pltpu.VMEM((TM, D_SGU), bf16),
        pltpu.VMEM((TM, D_ATTN), bf16),
        pltpu.VMEM((TM, D_MODEL), bf16),
        pltpu.VMEM((TM, N_BRANCHES * D_MODEL), f32),
    ]
    return pl.pallas_call(
        functools.partial(_prompt_kernel, layer),
        out_shape=out_shape,
        grid=grid,
        in_specs=in_specs,
        out_specs=out_specs,
        scratch_shapes=scratch,
        compiler_params=pltpu.CompilerParams(
            dimension_semantics=("arbitrary", "arbitrary"),
            vmem_limit_bytes=VMEM_LIMIT),
        name="prompt_layer",
    )(sinks, x, cos, sin, w_in, b_gate, poolw, pscale, slng, slnb, sguw, sgub,
      wpa, wpb, wpc, wout, lng, lnb)


def _sample_proj_kernel(layer, x_ref, cos_ref, sin_ref, w_in_ref, bg_ref, pb_ref, poolw_ref, pscale_ref,
                        slng_ref, slnb_ref, sw0_ref, sb0_ref,
                        q_ref, k_ref, v_ref, kt_ref, vt_ref, pool_ref, vn_ref, ain_ref, bin_ref, szc_ref,
                        g_ref):
    row = lambda ref: ref[layer:layer + 1, :]
    xb = x_ref[...].astype(bf16)
    h = _dot(xb, w_in_ref[:, 0:OFF_G])
    xa = h[:, OFF_XA:OFF_XA + D_POOL]
    lane = lax.broadcasted_iota(jnp.int32, xa.shape, 1)
    first_row = _group_select(lane, *(POOL_BUF - (w - 1) for w in POOL_WINDOWS))
    win = xa
    for r in range(POOL_BUF):
        win = win + jnp.where(first_row <= r, pb_ref[r], 0.0)
    width = _group_select(lane, *POOL_WINDOWS).astype(f32)
    pooled = win / width - xa
    ya = _dot(pooled.astype(bf16), poolw_ref[...]) * row(pscale_ref)
    ain_ref[...] = (ya * _silu(h[:, OFF_ZA:OFF_ZA + D_POOL])).astype(bf16)
    for r in range(POOL_BUF - 1):
        pool_ref[r] = pb_ref[r + 1]
    pool_ref[POOL_BUF - 1] = xa
    vn = _layer_norm(h[:, OFF_V:OFF_V + D_SGU], row(slng_ref), row(slnb_ref))
    vn_ref[...] = vn
    yb = h[:, OFF_U:OFF_U + D_SGU] * (row(sw0_ref) * vn + row(sb0_ref))
    bin_ref[...] = (yb * _silu(h[:, OFF_ZB:OFF_ZB + D_SGU])).astype(bf16)
    cos = cos_ref[...]
    sin = sin_ref[...]
    for c in range(D_ATTN // LANES):
        q_ref[:, c * LANES:(c + 1) * LANES] = _rope128(
            h[:, OFF_Q + c * LANES:OFF_Q + (c + 1) * LANES], cos, sin) * SCALE
    kr = _rope128(h[:, OFF_K:OFF_K + D_KV], cos, sin)
    vv = h[:, OFF_VV:OFF_VV + D_KV]
    k_ref[...] = kr
    v_ref[...] = vv
    kt_ref[...] = kr.T
    vt_ref[...] = vv.T
    szc_ref[...] = _silu(h[:, OFF_ZC:OFF_ZC + D_ATTN])
    for br in range(N_BRANCHES):
        cols = slice(br * D_MODEL, (br + 1) * D_MODEL)
        g_ref[:, cols] = jax.nn.sigmoid(
            _dot(xb, w_in_ref[:, OFF_G + br * D_MODEL:OFF_G + (br + 1) * D_MODEL]) + bg_ref[br:br + 1, :])


def _sample_proj(layer, x, cos, sin, w_in, b_gate, pool_t, poolw, pscale, slng, slnb, sw0, sb0):
    nb = x.shape[0]
    out_shape = (
        jax.ShapeDtypeStruct((nb, D_ATTN), f32),
        jax.ShapeDtypeStruct((nb, D_KV), f32),
        jax.ShapeDtypeStruct((nb, D_KV), f32),
        jax.ShapeDtypeStruct((D_KV, nb), f32),
        jax.ShapeDtypeStruct((D_KV, nb), f32),
        jax.ShapeDtypeStruct((POOL_BUF, nb, D_POOL), f32),
        jax.ShapeDtypeStruct((nb, D_SGU), f32),
        jax.ShapeDtypeStruct((nb, D_POOL), bf16),
        jax.ShapeDtypeStruct((nb, D_SGU), bf16),
        jax.ShapeDtypeStruct((nb, D_ATTN), f32),
        jax.ShapeDtypeStruct((nb, N_BRANCHES * D_MODEL), f32),
    )
    lspec = lambda a: _layer_spec(a, layer, 1)
    args = (x, cos, sin, w_in, b_gate, pool_t, poolw, pscale, slng, slnb, sw0, sb0)
    in_specs = [_full_spec(x), _full_spec(cos), _full_spec(sin), lspec(w_in), lspec(b_gate), lspec(pool_t),
                lspec(poolw), _full_spec(pscale), _full_spec(slng), _full_spec(slnb), _full_spec(sw0),
                _full_spec(sb0)]
    out_specs = tuple(pl.BlockSpec(s.shape, functools.partial(lambda nd, i: (0,) * nd, len(s.shape)))
                      for s in out_shape)
    return pl.pallas_call(
        functools.partial(_sample_proj_kernel, layer),
        out_shape=out_shape,
        grid=(1,),
        in_specs=in_specs,
        out_specs=out_specs,
        compiler_params=pltpu.CompilerParams(dimension_semantics=("arbitrary",),
                                             vmem_limit_bytes=VMEM_LIMIT),
        name="sample_proj",
    )(*args)


def _sample_attn_kernel(sink_ref, q_ref, kc_ref, vc_ref, kn_ref, vn_ref, o_ref):
    for kv in range(N_KV_HEADS):
        qb = q_ref[:, kv].astype(bf16)
        kn = kn_ref[:, kv]
        vn = vn_ref[:, kv]
        sink = sink_ref[kv][None]
        s = jnp.einsum('bgd,bdw->bgw', qb, kc_ref[:, kv].astype(bf16), preferred_element_type=f32)
        s_new = jnp.sum(qb.astype(f32) * kn.astype(bf16).astype(f32), axis=-1, keepdims=True)
        m = jnp.maximum(jnp.maximum(jnp.max(s, axis=-1, keepdims=True), s_new), sink)
        p = jnp.exp(s - m)
        p_new = jnp.exp(s_new - m)
        den = jnp.sum(p, axis=-1, keepdims=True) + p_new + jnp.exp(sink - m)
        o = jnp.einsum('bgw,bdw->bgd', (p / den).astype(bf16), vc_ref[:, kv].astype(bf16),
                       preferred_element_type=f32)
        o_ref[:, kv] = o + (p_new / den) * vn


def _sample_attn(layer, sink, q4, kc, vc, kn, vn):
    nb = q4.shape[0]
    cache_spec = pl.BlockSpec((None, SB, N_KV_HEADS, HEAD_DIM, WINDOW), lambda b: (layer, b, 0, 0, 0))
    blk = lambda s2, s3: pl.BlockSpec((SB, N_KV_HEADS, s2, s3), lambda b: (b, 0, 0, 0))
    return pl.pallas_call(
        _sample_attn_kernel,
        out_shape=jax.ShapeDtypeStruct((nb, N_KV_HEADS, Q_PER_KV, HEAD_DIM), f32),
        grid=(nb // SB,),
        in_specs=[_layer_spec(sink, layer, 1), blk(Q_PER_KV, HEAD_DIM), cache_spec, cache_spec,
                  blk(1, HEAD_DIM), blk(1, HEAD_DIM)],
        out_specs=blk(Q_PER_KV, HEAD_DIM),
        compiler_params=pltpu.CompilerParams(dimension_semantics=("arbitrary",),
                                             vmem_limit_bytes=VMEM_LIMIT),
        name="sample_attn",
    )(sink, q4, kc, vc, kn, vn)


def _cache_slide_kernel(kt_ref, vt_ref, kc_ref, vc_ref, ko_ref, vo_ref):
    j = pl.program_id(1)
    lane = lax.broadcasted_iota(jnp.int32, (D_KV, WINDOW), 1)
    shift = jnp.where(j == 0, 0, LANES - j * SB)
    for t_ref, c_ref, o_ref in ((kt_ref, kc_ref, ko_ref), (vt_ref, vc_ref, vo_ref)):
        new_cols = pltpu.roll(t_ref[...], shift, 1)
        for b in range(SB):
            old = c_ref[b].reshape(D_KV, WINDOW)
            slid = pltpu.roll(old, WINDOW - 1, 1)
            newest = jnp.broadcast_to(new_cols[:, b:b + 1], (D_KV, WINDOW))
            o_ref[b] = jnp.where(lane == WINDOW - 1, newest, slid).reshape(N_KV_HEADS, HEAD_DIM, WINDOW)


def _cache_slide(kt, vt, kc, vc):
    depth, nb = kc.shape[0], kc.shape[1]
    cache_spec = pl.BlockSpec((None, SB, N_KV_HEADS, HEAD_DIM, WINDOW), lambda l, b: (l, b, 0, 0, 0))
    new_spec = pl.BlockSpec((None, D_KV, nb), lambda l, b: (l, 0, 0))
    return pl.pallas_call(
        _cache_slide_kernel,
        out_shape=(jax.ShapeDtypeStruct(kc.shape, f32), jax.ShapeDtypeStruct(vc.shape, f32)),
        grid=(depth, nb // SB),
        in_specs=[new_spec, new_spec, cache_spec, cache_spec],
        out_specs=(cache_spec, cache_spec),
        compiler_params=pltpu.CompilerParams(dimension_semantics=("arbitrary", "arbitrary"),
                                             vmem_limit_bytes=VMEM_LIMIT),
        name="cache_slide",
    )(kt, vt, kc, vc)


def _sample_merge_kernel(layer, x_ref, ain_ref, bin_ref, yc_ref, szc_ref, g_ref, wpa_ref, wpb_ref, wpc_ref,
                         wout_ref, lng_ref, lnb_ref, y_ref):
    row = lambda ref: ref[layer:layer + 1, :]
    cin = (yc_ref[...] * szc_ref[...]).astype(bf16)
    merged = (g_ref[:, 0:D_MODEL] * _dot(ain_ref[...], wpa_ref[...])
              + g_ref[:, D_MODEL:2 * D_MODEL] * _dot(bin_ref[...], wpb_ref[...])
              + g_ref[:, 2 * D_MODEL:3 * D_MODEL] * _dot(cin, wpc_ref[...]))
    out = _dot(merged.astype(bf16), wout_ref[...])
    y_ref[...] = _layer_norm(ALPHA * x_ref[...] + out, row(lng_ref), row(lnb_ref))


def _sample_merge(layer, x, ain, bin_, yc, szc, g, wpa, wpb, wpc, wout, lng, lnb):
    lspec = lambda a: _layer_spec(a, layer, 1)
    return pl.pallas_call(
        functools.partial(_sample_merge_kernel, layer),
        out_shape=jax.ShapeDtypeStruct(x.shape, f32),
        grid=(1,),
        in_specs=[_full_spec(x), _full_spec(ain), _full_spec(bin_), _full_spec(yc), _full_spec(szc),
                  _full_spec(g), lspec(wpa), lspec(wpb), lspec(wpc), lspec(wout), _full_spec(lng),
                  _full_spec(lnb)],
        out_specs=_full_spec(x),
        compiler_params=pltpu.CompilerParams(dimension_semantics=("arbitrary",),
                                             vmem_limit_bytes=VMEM_LIMIT),
        name="sample_merge",
    )(x, ain, bin_, yc, szc, g, wpa, wpb, wpc, wout, lng, lnb)


def _rope_tables(positions):
    halfd = HEAD_DIM // 2
    inv = ROPE_THETA ** (-np.arange(halfd, dtype=np.float64) / halfd)
    ang = np.asarray(positions, dtype=np.float64)[:, None] * inv[None, :]
    cos = np.tile(np.cos(ang), (1, LANES // halfd))
    sin = np.tile(np.concatenate([-np.sin(ang), np.sin(ang)], axis=1), (1, LANES // HEAD_DIM))
    return jnp.asarray(cos, f32), jnp.asarray(sin, f32)


def _block_diag(w):
    nl, g, c, _ = w.shape
    eye = jnp.eye(g, dtype=w.dtype)
    return (eye[None, :, None, :, None] * w[:, :, :, None, :]).reshape(nl, g * c, g * c)


def kernel(x_prompt, x_sample, state_pool, cache_k_win, cache_v_win, w_in, b_gate, pool_w, pool_scale, sgu_ln_g, sgu_ln_b, sgu_w, sgu_b, attn_sinks, w_proj_a, w_proj_b, w_proj_c, w_out, ln_g, ln_b):
    B, L, _ = x_prompt.shape
    nb = x_sample.shape[0]
    cos_p, sin_p = _rope_tables(np.arange(L))
    cos_s, sin_s = _rope_tables(np.array([PAST_LEN]))

    w_in_b, wpa, wpb, wpc, wout = (w.astype(bf16) for w in (w_in, w_proj_a, w_proj_b, w_proj_c, w_out))
    poolw = _block_diag(pool_w).astype(bf16)
    sguw = sgu_w.reshape(DEPTH, N_SGU_GROUPS * CHUNK, CHUNK)
    sgub = jnp.repeat(jnp.swapaxes(sgu_b, 1, 2), POOL_GC, axis=2)
    sw0 = jnp.repeat(sgu_w[:, :, 0, 0], POOL_GC, axis=1)
    sb0 = jnp.repeat(sgu_b[:, :, 0], POOL_GC, axis=1)
    sink4 = attn_sinks.reshape(DEPTH, N_KV_HEADS, Q_PER_KV, 1)
    kc_t = jnp.transpose(cache_k_win, (0, 1, 3, 4, 2))
    vc_t = jnp.transpose(cache_v_win, (0, 1, 3, 4, 2))
    pool_t = jnp.transpose(state_pool, (0, 2, 1, 3))

    y_p = x_prompt
    y_s = x_sample.reshape(nb, D_MODEL)
    pool_p, k_p, v_p, pool_s, chunk_v, kts, vts = ([] for _ in range(7))
    for l in range(DEPTH):
        y_p, pp, kp, vp = _prompt_layer(
            l, y_p, cos_p, sin_p, attn_sinks, w_in_b, b_gate, poolw, pool_scale, sgu_ln_g, sgu_ln_b, sguw,
            sgub, wpa, wpb, wpc, wout, ln_g, ln_b)
        pool_p.append(pp); k_p.append(kp); v_p.append(vp)

        q, kn, vn, kt, vt, ps, cv, ain, bin_, szc, g = _sample_proj(
            l, y_s, cos_s, sin_s, w_in_b, b_gate, pool_t, poolw, pool_scale, sgu_ln_g, sgu_ln_b, sw0, sb0)
        o = _sample_attn(
            l, sink4, q.reshape(nb, N_KV_HEADS, Q_PER_KV, HEAD_DIM), kc_t, vc_t,
            kn.reshape(nb, N_KV_HEADS, 1, HEAD_DIM), vn.reshape(nb, N_KV_HEADS, 1, HEAD_DIM))
        y_s = _sample_merge(l, y_s, ain, bin_, o.reshape(nb, D_ATTN), szc, g, wpa, wpb, wpc, wout, ln_g, ln_b)
        pool_s.append(ps); chunk_v.append(cv); kts.append(kt); vts.append(vt)

    k_s, v_s = _cache_slide(jnp.stack(kts), jnp.stack(vts), kc_t, vc_t)

    to_cache = lambda a: jnp.transpose(a, (0, 1, 4, 2, 3))
    prompt_cache = lambda lst: to_cache(jnp.stack(lst).reshape(DEPTH, B, N_KV_HEADS, HEAD_DIM, WINDOW))
    return (y_p, y_s.reshape(nb, 1, D_MODEL),
            jnp.stack(pool_p), prompt_cache(k_p), prompt_cache(v_p),
            jnp.transpose(jnp.stack(pool_s), (0, 2, 1, 3)), to_cache(k_s), to_cache(v_s),
            jnp.stack(chunk_v).reshape(DEPTH, nb, 1, D_SGU))
```

```python
import functools

import numpy as np
import jax
import jax.numpy as jnp
from jax import lax
from jax.experimental import pallas as pl
from jax.experimental.pallas import tpu as pltpu

D_MODEL = 1024
DEPTH = 2
PAST_LEN = 8192
D_POOL = 256
POOL_WINDOWS = (2, 4, 8, 16)
POOL_GC = 64
POOL_BUF = 15
D_SGU = 256
CHUNK = 128
N_SGU_GROUPS = 4
HEAD_DIM = 64
N_HEADS = 8
N_KV_HEADS = 2
Q_PER_KV = 4
D_ATTN = 512
D_KV = 128
WINDOW = 128
BLOCK = 128
ROPE_THETA = 10000.0
N_BRANCHES = 3
D_IN = 2 * D_POOL + 3 * D_SGU + 2 * D_ATTN + 2 * D_KV + N_BRANCHES * D_MODEL
ALPHA = (2.0 * DEPTH) ** 0.25
LN_EPS = 1e-5
NEG_INF = -1e30
SCALE = HEAD_DIM ** -0.5

OFF_XA, OFF_ZA, OFF_U, OFF_V, OFF_ZB = 0, 256, 512, 768, 1024
OFF_Q, OFF_K, OFF_VV, OFF_ZC, OFF_G = 1280, 1792, 1920, 2048, 2560

LANES = 128
TM = 512
NBLK = TM // BLOCK
HIST = 32
NCHUNK = 256
OUT_ROWS = 256
VMEM_LIMIT = 56 * 1024 * 1024
SB = 16

bf16 = jnp.bfloat16
f32 = jnp.float32


def _dot(a, b):
    return jnp.dot(a, b, preferred_element_type=f32)


def _dot_nt(a, b):
    return lax.dot_general(a, b, (((1,), (1,)), ((), ())), preferred_element_type=f32)


def _sigmoid(z):
    return 0.5 * jnp.tanh(0.5 * z) + 0.5


def _silu(z):
    return z * _sigmoid(z)


def _gate2(half_pre, bias):
    return jnp.tanh(half_pre + 0.5 * bias) + 1.0


def _layer_norm(x, g, b):
    mu = jnp.mean(x, axis=-1, keepdims=True)
    xc = x - mu
    var = jnp.mean(xc * xc, axis=-1, keepdims=True)
    return xc * lax.rsqrt(var + LN_EPS) * g + b


def _rope128(x, cos, sin_signed):
    lane = lax.broadcasted_iota(jnp.int32, x.shape, 1)
    first_half = (lane % HEAD_DIM) < (HEAD_DIM // 2)
    partner = jnp.where(first_half,
                        pltpu.roll(x, LANES - HEAD_DIM // 2, 1),
                        pltpu.roll(x, HEAD_DIM // 2, 1))
    return x * cos + partner * sin_signed


def _group_select(lane, a0, a1, a2, a3):
    return jnp.where(lane < 64, a0, jnp.where(lane < 128, a1, jnp.where(lane < 192, a2, a3)))


def _layer_spec(arr, layer, grid_rank, single_buffer=False):
    block = (None,) + arr.shape[1:]
    zeros = (0,) * (arr.ndim - 1)
    index_map = lambda *_: (layer,) + zeros
    del grid_rank
    if single_buffer:
        return pl.BlockSpec(block, index_map, pipeline_mode=pl.Buffered(1))
    return pl.BlockSpec(block, index_map)


def _full_spec(arr):
    zeros = (0,) * arr.ndim
    return pl.BlockSpec(arr.shape, lambda *_: zeros)


def _prompt_kernel(layer, sinks_ref, x_ref, cos_ref, sin_ref, w_in_ref, bg_ref, poolw_ref, pscale_ref,
                   slng_ref, slnb_ref, sguw_ref, sgub_ref, wpa_ref, wpb_ref, wpc_ref, wout_ref,
                   lng_ref, lnb_ref,
                   y_ref, pool_out_ref, k_out_ref, v_out_ref,
                   h_ref, ext_ref, s2_ref, s4_ref, s8_ref, qs_ref,
                   ka_ref, kb_ref, kc_ref, kd_ref, va_ref, vb_ref, vc_ref, vd_ref,
                   ain_ref, bin_ref, cin_ref, mg_ref, gate_ref, klast_ref, vlast_ref):
    i = pl.program_id(1)
    last = pl.num_programs(1) - 1
    kv_refs = (ka_ref, kb_ref, kc_ref, kd_ref, va_ref, vb_ref, vc_ref, vd_ref)
    row = lambda ref: ref[layer:layer + 1, :]

    @pl.when(i == 0)
    def _():
        ext_ref[0:HIST, :] = jnp.zeros((HIST, D_POOL), f32)
        for r in kv_refs:
            r[0:BLOCK, :] = jnp.zeros((BLOCK, LANES), bf16)

    xb = x_ref[...].astype(bf16)
    half = OFF_G // 2
    h_ref[:, 0:half] = _dot(xb, w_in_ref[:, 0:half])
    h_ref[:, half:OFF_G] = _dot(xb, w_in_ref[:, half:OFF_G])

    xa = h_ref[:, OFF_XA:OFF_XA + D_POOL]
    ext_ref[HIST:HIST + TM, :] = xa
    n = HIST + TM
    s2_ref[8:n, :] = ext_ref[8:n, :] + ext_ref[7:n - 1, :]
    s4_ref[16:n, :] = s2_ref[16:n, :] + s2_ref[14:n - 2, :]
    s8_ref[24:n, :] = s4_ref[24:n, :] + s4_ref[20:n - 4, :]
    w16 = s8_ref[HIST:n, :] + s8_ref[HIST - 8:n - 8, :]
    lane_p = lax.broadcasted_iota(jnp.int32, (TM, D_POOL), 1)
    row_p = lax.broadcasted_iota(jnp.int32, (TM, D_POOL), 0)
    win = _group_select(lane_p, s2_ref[HIST:n, :], s4_ref[HIST:n, :], s8_ref[HIST:n, :], w16)
    width = _group_select(lane_p, POOL_WINDOWS[0], POOL_WINDOWS[1], POOL_WINDOWS[2], POOL_WINDOWS[3])
    cnt = jnp.minimum(row_p + (i * TM + 1), width).astype(f32)
    pooled = win / cnt - xa
    ya = _dot(pooled.astype(bf16), poolw_ref[...]) * row(pscale_ref)
    za = h_ref[:, OFF_ZA:OFF_ZA + D_POOL]
    ain_ref[...] = (ya * _silu(za)).astype(bf16)
    ext_ref[HIST - 16:HIST, :] = ext_ref[n - 16:n, :]

    vn = _layer_norm(h_ref[:, OFF_V:OFF_V + D_SGU], row(slng_ref), row(slnb_ref)).astype(bf16)
    wr = lax.broadcasted_iota(jnp.int32, (N_SGU_GROUPS * CHUNK, CHUNK), 0) % CHUNK
    wc = lax.broadcasted_iota(jnp.int32, (N_SGU_GROUPS * CHUNK, CHUNK), 1)
    w_s = jnp.where(wc <= wr, sguw_ref[...], 0.0).astype(bf16)
    lane_c = lax.broadcasted_iota(jnp.int32, (CHUNK, D_SGU), 1)
    for j in range(NBLK):
        rows = slice(j * CHUNK, (j + 1) * CHUNK)
        r = _dot(w_s, vn[rows, :])
        s = _group_select(lane_c, r[0:CHUNK], r[CHUNK:2 * CHUNK], r[2 * CHUNK:3 * CHUNK],
                          r[3 * CHUNK:4 * CHUNK]) + sgub_ref[...]
        yb = h_ref[rows, OFF_U:OFF_U + D_SGU] * s
        bin_ref[rows, :] = (yb * _silu(h_ref[rows, OFF_ZB:OFF_ZB + D_SGU])).astype(bf16)

    cos = cos_ref[...]
    sin = sin_ref[...]
    for c in range(D_ATTN // LANES):
        qc = _rope128(h_ref[:, OFF_Q + c * LANES:OFF_Q + (c + 1) * LANES], cos, sin)
        qs_ref[:, c * LANES:(c + 1) * LANES] = (qc * SCALE).astype(bf16)
    kr = _rope128(h_ref[:, OFF_K:OFF_K + D_KV], cos, sin)
    vv = h_ref[:, OFF_VV:OFF_VV + D_KV]
    klast_ref[...] = kr[TM - WINDOW:TM, :]
    vlast_ref[...] = vv[TM - WINDOW:TM, :]
    lane_k = lax.broadcasted_iota(jnp.int32, (TM, LANES), 1)
    lo = lane_k < HEAD_DIM
    for t, (a_ref, b_ref, c_ref, d_ref) in ((kr, kv_refs[0:4]), (vv, kv_refs[4:8])):
        sw = pltpu.roll(t, HEAD_DIM, 1)
        a_ref[BLOCK:BLOCK + TM, :] = jnp.where(lo, t, 0.0).astype(bf16)
        b_ref[BLOCK:BLOCK + TM, :] = jnp.where(lo, 0.0, t).astype(bf16)
        c_ref[BLOCK:BLOCK + TM, :] = jnp.where(lo, sw, 0.0).astype(bf16)
        d_ref[BLOCK:BLOCK + TM, :] = jnp.where(lo, 0.0, sw).astype(bf16)

    qrow = lax.broadcasted_iota(jnp.int32, (2 * BLOCK, 2 * BLOCK), 0) % BLOCK
    kcol = lax.broadcasted_iota(jnp.int32, (2 * BLOCK, 2 * BLOCK), 1)
    band = (kcol >= qrow) & (kcol <= qrow + WINDOW)
    band_first = band & (kcol >= jnp.where(i > 0, 0, BLOCK))
    top = lax.broadcasted_iota(jnp.int32, (2 * BLOCK, 1), 0) < BLOCK
    n_gate = N_BRANCHES * D_MODEL // NCHUNK
    n_unit = NBLK * N_KV_HEADS
    gate_sched = [range(u * n_gate // n_unit, (u + 1) * n_gate // n_unit) for u in range(n_unit)]

    def scores(u):
        j, kv = divmod(u, N_KV_HEADS)
        rows = slice(j * BLOCK, (j + 1) * BLOCK)
        keys = slice(j * BLOCK, j * BLOCK + 2 * BLOCK)
        c0 = kv * Q_PER_KV * HEAD_DIM
        qst = jnp.concatenate([qs_ref[rows, c0:c0 + LANES], qs_ref[rows, c0 + LANES:c0 + 2 * LANES]], axis=0)
        k_even, k_odd = (ka_ref, kd_ref) if kv == 0 else (kc_ref, kb_ref)
        kcat = jnp.concatenate([k_even[keys, :], k_odd[keys, :]], axis=0)
        return _dot_nt(qst, kcat)

    def attend(u, sc):
        j, kv = divmod(u, N_KV_HEADS)
        rows = slice(j * BLOCK, (j + 1) * BLOCK)
        keys = slice(j * BLOCK, j * BLOCK + 2 * BLOCK)
        allowed = band_first if j == 0 else band
        c0 = kv * Q_PER_KV * HEAD_DIM
        h0 = kv * Q_PER_KV
        v_even, v_odd = (va_ref, vd_ref) if kv == 0 else (vc_ref, vb_ref)
        probs = []
        for par in range(2):
            sink = jnp.where(top, sinks_ref[layer, h0 + par], sinks_ref[layer, h0 + 2 + par])
            sm = jnp.where(allowed, sc[:, par * 2 * BLOCK:(par + 1) * 2 * BLOCK], NEG_INF)
            m = jnp.maximum(jnp.max(sm, axis=-1, keepdims=True), sink)
            p = jnp.exp(sm - m)
            den = jnp.sum(p, axis=-1, keepdims=True) + jnp.exp(sink - m)
            probs.append((p / den).astype(bf16))
        pcat = jnp.concatenate(probs, axis=1)
        vcat = jnp.concatenate([v_even[keys, :], v_odd[keys, :]], axis=0)
        o = _dot(pcat, vcat)
        for pr in range(2):
            cols = slice(c0 + pr * LANES, c0 + (pr + 1) * LANES)
            zc = h_ref[rows, OFF_ZC + c0 + pr * LANES:OFF_ZC + c0 + (pr + 1) * LANES]
            cin_ref[rows, cols] = (o[pr * BLOCK:(pr + 1) * BLOCK] * _silu(zc)).astype(bf16)

    sc_next = scores(0)
    for u in range(n_unit):
        sc = sc_next
        for gc in gate_sched[u]:
            gcols = slice(gc * NCHUNK, (gc + 1) * NCHUNK)
            gate_ref[:, gcols] = _dot(xb, w_in_ref[:, OFF_G + gc * NCHUNK:OFF_G + (gc + 1) * NCHUNK])
        if u + 1 < n_unit:
            sc_next = scores(u + 1)
        attend(u, sc)

    for r in kv_refs:
        r[0:BLOCK, :] = r[TM:TM + BLOCK, :]

    for c in range(D_MODEL // NCHUNK):
        cols = slice(c * NCHUNK, (c + 1) * NCHUNK)
        acc = None
        for br, (in_ref, wp_ref) in enumerate(((ain_ref, wpa_ref), (bin_ref, wpb_ref), (cin_ref, wpc_ref))):
            g0 = br * D_MODEL + c * NCHUNK
            term = _gate2(gate_ref[:, g0:g0 + NCHUNK], bg_ref[br:br + 1, cols]) * _dot(in_ref[...], wp_ref[:, cols])
            acc = term if acc is None else acc + term
        mg_ref[:, cols] = acc.astype(bf16)

    for r0 in range(0, TM, OUT_ROWS):
        rows = slice(r0, r0 + OUT_ROWS)
        out = _dot(mg_ref[rows, :], wout_ref[...])
        y_ref[rows, :] = _layer_norm(ALPHA * x_ref[rows, :] + out, row(lng_ref), row(lnb_ref))

    @pl.when(i == last)
    def _():
        pool_out_ref[...] = ext_ref[n - POOL_BUF:n, :]
        k_out_ref[...] = klast_ref[...].T
        v_out_ref[...] = vlast_ref[...].T


def _prompt_layer(layer, x, cos, sin, sinks, w_in, b_gate, poolw, pscale, slng, slnb, sguw, sgub,
                  wpa, wpb, wpc, wout, lng, lnb):
    B, L, _ = x.shape
    grid = (B, L // TM)
    row_spec = lambda w: pl.BlockSpec((TM, w), lambda b, i: (i, 0))
    lspec = lambda a: _layer_spec(a, layer, 2, single_buffer=True)
    in_specs = [
        pl.BlockSpec(memory_space=pltpu.SMEM),
        pl.BlockSpec((None, TM, D_MODEL), lambda b, i: (b, i, 0)),
        row_spec(LANES), row_spec(LANES),
        lspec(w_in), lspec(b_gate), lspec(poolw), _full_spec(pscale), _full_spec(slng), _full_spec(slnb),
        lspec(sguw), lspec(sgub), lspec(wpa), lspec(wpb), lspec(wpc), lspec(wout),
        _full_spec(lng), _full_spec(lnb),
    ]
    out_shape = (
        jax.ShapeDtypeStruct((B, L, D_MODEL), f32),
        jax.ShapeDtypeStruct((B, POOL_BUF, D_POOL), f32),
        jax.ShapeDtypeStruct((B, D_KV, WINDOW), f32),
        jax.ShapeDtypeStruct((B, D_KV, WINDOW), f32),
    )
    out_specs = (
        pl.BlockSpec((None, TM, D_MODEL), lambda b, i: (b, i, 0)),
        pl.BlockSpec((None, POOL_BUF, D_POOL), lambda b, i: (b, 0, 0)),
        pl.BlockSpec((None, D_KV, WINDOW), lambda b, i: (b, 0, 0)),
        pl.BlockSpec((None, D_KV, WINDOW), lambda b, i: (b, 0, 0)),
    )
    kv_scratch = [pltpu.VMEM((BLOCK + TM, LANES), bf16) for _ in range(8)]
    scratch = [
        pltpu.VMEM((TM, OFF_G), f32),
        pltpu.VMEM((HIST + TM, D_POOL), f32),
        pltpu.VMEM((HIST + TM, D_POOL), f32),
        pltpu.VMEM((HIST + TM, D_POOL), f32),
        pltpu.VMEM((HIST + TM, D_POOL), f32),
        pltpu.VMEM((TM, D_ATTN), bf16),
        *kv_scratch,
        pltpu.VMEM((TM, D_POOL), bf16),
        pltpu.VMEM((TM, D_SGU), bf16),
        pltpu.VMEM((TM, D_ATTN), bf16),
        pltpu.VMEM((TM, D_MODEL), bf16),
        pltpu.VMEM((TM, N_BRANCHES * D_MODEL), f32),
        pltpu.VMEM((WINDOW, D_KV), f32),
        pltpu.VMEM((WINDOW, D_KV), f32),
    ]
    return pl.pallas_call(
        functools.partial(_prompt_kernel, layer),
        out_shape=out_shape,
        grid=grid,
        in_specs=in_specs,
        out_specs=out_specs,
        scratch_shapes=scratch,
        compiler_params=pltpu.CompilerParams(
            dimension_semantics=("arbitrary", "arbitrary"),
            vmem_limit_bytes=VMEM_LIMIT),
        name="prompt_layer",
    )(sinks, x, cos, sin, w_in, b_gate, poolw, pscale, slng, slnb, sguw, sgub,
      wpa, wpb, wpc, wout, lng, lnb)


def _sample_proj_kernel(layer, x_ref, cos_ref, sin_ref, w_in_ref, bg_ref, pb_ref, poolw_ref, pscale_ref,
                        slng_ref, slnb_ref, sw0_ref, sb0_ref,
                        q_ref, k_ref, v_ref, kt_ref, vt_ref, pool_ref, vn_ref, ain_ref, bin_ref, szc_ref,
                        g_ref):
    row = lambda ref: ref[layer:layer + 1, :]
    xb = x_ref[...].astype(bf16)
    h = _dot(xb, w_in_ref[:, 0:OFF_G])
    xa = h[:, OFF_XA:OFF_XA + D_POOL]
    lane = lax.broadcasted_iota(jnp.int32, xa.shape, 1)
    first_row = _group_select(lane, *(POOL_BUF - (w - 1) for w in POOL_WINDOWS))
    win = xa
    for r in range(POOL_BUF):
        win = win + jnp.where(first_row <= r, pb_ref[r], 0.0)
    width = _group_select(lane, *POOL_WINDOWS).astype(f32)
    pooled = win / width - xa
    ya = _dot(pooled.astype(bf16), poolw_ref[...]) * row(pscale_ref)
    ain_ref[...] = (ya * _silu(h[:, OFF_ZA:OFF_ZA + D_POOL])).astype(bf16)
    for r in range(POOL_BUF - 1):
        pool_ref[r] = pb_ref[r + 1]
    pool_ref[POOL_BUF - 1] = xa
    vn = _layer_norm(h[:, OFF_V:OFF_V + D_SGU], row(slng_ref), row(slnb_ref))
    vn_ref[...] = vn
    yb = h[:, OFF_U:OFF_U + D_SGU] * (row(sw0_ref) * vn + row(sb0_ref))
    bin_ref[...] = (yb * _silu(h[:, OFF_ZB:OFF_ZB + D_SGU])).astype(bf16)
    cos = cos_ref[...]
    sin = sin_ref[...]
    for c in range(D_ATTN // LANES):
        q_ref[:, c * LANES:(c + 1) * LANES] = _rope128(
            h[:, OFF_Q + c * LANES:OFF_Q + (c + 1) * LANES], cos, sin) * SCALE
    kr = _rope128(h[:, OFF_K:OFF_K + D_KV], cos, sin)
    vv = h[:, OFF_VV:OFF_VV + D_KV]
    k_ref[...] = kr
    v_ref[...] = vv
    kt_ref[...] = kr.T
    vt_ref[...] = vv.T
    szc_ref[...] = _silu(h[:, OFF_ZC:OFF_ZC + D_ATTN])
    for br in range(N_BRANCHES):
        cols = slice(br * D_MODEL, (br + 1) * D_MODEL)
        g_ref[:, cols] = _gate2(
            _dot(xb, w_in_ref[:, OFF_G + br * D_MODEL:OFF_G + (br + 1) * D_MODEL]), bg_ref[br:br + 1, :])


def _sample_proj(layer, x, cos, sin, w_in, b_gate, pool_t, poolw, pscale, slng, slnb, sw0, sb0):
    nb = x.shape[0]
    out_shape = (
        jax.ShapeDtypeStruct((nb, D_ATTN), f32),
        jax.ShapeDtypeStruct((nb, D_KV), f32),
        jax.ShapeDtypeStruct((nb, D_KV), f32),
        jax.ShapeDtypeStruct((D_KV, nb), f32),
        jax.ShapeDtypeStruct((D_KV, nb), f32),
        jax.ShapeDtypeStruct((POOL_BUF, nb, D_POOL), f32),
        jax.ShapeDtypeStruct((nb, D_SGU), f32),
        jax.ShapeDtypeStruct((nb, D_POOL), bf16),
        jax.ShapeDtypeStruct((nb, D_SGU), bf16),
        jax.ShapeDtypeStruct((nb, D_ATTN), f32),
        jax.ShapeDtypeStruct((nb, N_BRANCHES * D_MODEL), f32),
    )
    lspec = lambda a: _layer_spec(a, layer, 1)
    args = (x, cos, sin, w_in, b_gate, pool_t, poolw, pscale, slng, slnb, sw0, sb0)
    in_specs = [_full_spec(x), _full_spec(cos), _full_spec(sin), lspec(w_in), lspec(b_gate), lspec(pool_t),
                lspec(poolw), _full_spec(pscale), _full_spec(slng), _full_spec(slnb), _full_spec(sw0),
                _full_spec(sb0)]
    out_specs = tuple(pl.BlockSpec(s.shape, functools.partial(lambda nd, i: (0,) * nd, len(s.shape)))
                      for s in out_shape)
    return pl.pallas_call(
        functools.partial(_sample_proj_kernel, layer),
        out_shape=out_shape,
        grid=(1,),
        in_specs=in_specs,
        out_specs=out_specs,
        compiler_params=pltpu.CompilerParams(dimension_semantics=("arbitrary",),
                                             vmem_limit_bytes=VMEM_LIMIT),
        name="sample_proj",
    )(*args)


def _sample_attn_kernel(sink_ref, q_ref, kc_ref, vc_ref, kn_ref, vn_ref, o_ref):
    for kv in range(N_KV_HEADS):
        qb = q_ref[:, kv].astype(bf16)
        kn = kn_ref[:, kv]
        vn = vn_ref[:, kv]
        sink = sink_ref[kv][None]
        s = jnp.einsum('bgd,bdw->bgw', qb, kc_ref[:, kv].astype(bf16), preferred_element_type=f32)
        s_new = jnp.sum(qb.astype(f32) * kn.astype(bf16).astype(f32), axis=-1, keepdims=True)
        m = jnp.maximum(jnp.maximum(jnp.max(s, axis=-1, keepdims=True), s_new), sink)
        p = jnp.exp(s - m)
        p_new = jnp.exp(s_new - m)
        den = jnp.sum(p, axis=-1, keepdims=True) + p_new + jnp.exp(sink - m)
        o = jnp.einsum('bgw,bdw->bgd', (p / den).astype(bf16), vc_ref[:, kv].astype(bf16),
                       preferred_element_type=f32)
        o_ref[:, kv] = o + (p_new / den) * vn


def _sample_attn(layer, sink, q4, kc, vc, kn, vn):
    nb = q4.shape[0]
    cache_spec = pl.BlockSpec((None, SB, N_KV_HEADS, HEAD_DIM, WINDOW), lambda b: (layer, b, 0, 0, 0))
    blk = lambda s2, s3: pl.BlockSpec((SB, N_KV_HEADS, s2, s3), lambda b: (b, 0, 0, 0))
    return pl.pallas_call(
        _sample_attn_kernel,
        out_shape=jax.ShapeDtypeStruct((nb, N_KV_HEADS, Q_PER_KV, HEAD_DIM), f32),
        grid=(nb // SB,),
        in_specs=[_layer_spec(sink, layer, 1), blk(Q_PER_KV, HEAD_DIM), cache_spec, cache_spec,
                  blk(1, HEAD_DIM), blk(1, HEAD_DIM)],
        out_specs=blk(Q_PER_KV, HEAD_DIM),
        compiler_params=pltpu.CompilerParams(dimension_semantics=("arbitrary",),
                                             vmem_limit_bytes=VMEM_LIMIT),
        name="sample_attn",
    )(sink, q4, kc, vc, kn, vn)


def _cache_slide_kernel(kt_ref, vt_ref, kc_ref, vc_ref, ko_ref, vo_ref):
    j = pl.program_id(1)
    lane = lax.broadcasted_iota(jnp.int32, (D_KV, WINDOW), 1)
    shift = jnp.where(j == 0, 0, LANES - j * SB)
    for t_ref, c_ref, o_ref in ((kt_ref, kc_ref, ko_ref), (vt_ref, vc_ref, vo_ref)):
        new_cols = pltpu.roll(t_ref[...], shift, 1)
        for b in range(SB):
            old = c_ref[b].reshape(D_KV, WINDOW)
            slid = pltpu.roll(old, WINDOW - 1, 1)
            newest = jnp.broadcast_to(new_cols[:, b:b + 1], (D_KV, WINDOW))
            o_ref[b] = jnp.where(lane == WINDOW - 1, newest, slid).reshape(N_KV_HEADS, HEAD_DIM, WINDOW)


def _cache_slide(kt, vt, kc, vc):
    depth, nb = kc.shape[0], kc.shape[1]
    cache_spec = pl.BlockSpec((None, SB, N_KV_HEADS, HEAD_DIM, WINDOW), lambda l, b: (l, b, 0, 0, 0))
    new_spec = pl.BlockSpec((None, D_KV, nb), lambda l, b: (l, 0, 0))
    return pl.pallas_call(
        _cache_slide_kernel,
        out_shape=(jax.ShapeDtypeStruct(kc.shape, f32), jax.ShapeDtypeStruct(vc.shape, f32)),
        grid=(depth, nb // SB),
        in_specs=[new_spec, new_spec, cache_spec, cache_spec],
        out_specs=(cache_spec, cache_spec),
        compiler_params=pltpu.CompilerParams(dimension_semantics=("arbitrary", "arbitrary"),
                                             vmem_limit_bytes=VMEM_LIMIT),
        name="cache_slide",
    )(kt, vt, kc, vc)


def _sample_merge_kernel(layer, x_ref, ain_ref, bin_ref, yc_ref, szc_ref, g_ref, wpa_ref, wpb_ref, wpc_ref,
                         wout_ref, lng_ref, lnb_ref, y_ref):
    row = lambda ref: ref[layer:layer + 1, :]
    cin = (yc_ref[...] * szc_ref[...]).astype(bf16)
    merged = (g_ref[:, 0:D_MODEL] * _dot(ain_ref[...], wpa_ref[...])
              + g_ref[:, D_MODEL:2 * D_MODEL] * _dot(bin_ref[...], wpb_ref[...])
              + g_ref[:, 2 * D_MODEL:3 * D_MODEL] * _dot(cin, wpc_ref[...]))
    out = _dot(merged.astype(bf16), wout_ref[...])
    y_ref[...] = _layer_norm(ALPHA * x_ref[...] + out, row(lng_ref), row(lnb_ref))


def _sample_merge(layer, x, ain, bin_, yc, szc, g, wpa, wpb, wpc, wout, lng, lnb):
    lspec = lambda a: _layer_spec(a, layer, 1)
    return pl.pallas_call(
        functools.partial(_sample_merge_kernel, layer),
        out_shape=jax.ShapeDtypeStruct(x.shape, f32),
        grid=(1,),
        in_specs=[_full_spec(x), _full_spec(ain), _full_spec(bin_), _full_spec(yc), _full_spec(szc),
                  _full_spec(g), lspec(wpa), lspec(wpb), lspec(wpc), lspec(wout), _full_spec(lng),
                  _full_spec(lnb)],
        out_specs=_full_spec(x),
        compiler_params=pltpu.CompilerParams(dimension_semantics=("arbitrary",),
                                             vmem_limit_bytes=VMEM_LIMIT),
        name="sample_merge",
    )(x, ain, bin_, yc, szc, g, wpa, wpb, wpc, wout, lng, lnb)


def _rope_tables(positions):
    halfd = HEAD_DIM // 2
    inv = ROPE_THETA ** (-np.arange(halfd, dtype=np.float64) / halfd)
    ang = np.asarray(positions, dtype=np.float64)[:, None] * inv[None, :]
    cos = np.tile(np.cos(ang), (1, LANES // halfd))
    sin = np.tile(np.concatenate([-np.sin(ang), np.sin(ang)], axis=1), (1, LANES // HEAD_DIM))
    return jnp.asarray(cos, f32), jnp.asarray(sin, f32)


def _block_diag(w):
    nl, g, c, _ = w.shape
    eye = jnp.eye(g, dtype=w.dtype)
    return (eye[None, :, None, :, None] * w[:, :, :, None, :]).reshape(nl, g * c, g * c)


def kernel(x_prompt, x_sample, state_pool, cache_k_win, cache_v_win, w_in, b_gate, pool_w, pool_scale, sgu_ln_g, sgu_ln_b, sgu_w, sgu_b, attn_sinks, w_proj_a, w_proj_b, w_proj_c, w_out, ln_g, ln_b):
    B, L, _ = x_prompt.shape
    nb = x_sample.shape[0]
    cos_p, sin_p = _rope_tables(np.arange(L))
    cos_s, sin_s = _rope_tables(np.array([PAST_LEN]))

    gate_cols = jnp.arange(D_IN) >= OFF_G
    w_in_b = (w_in * jnp.where(gate_cols, 0.5, 1.0).astype(f32)).astype(bf16)
    wpa, wpb, wpc = ((0.5 * w).astype(bf16) for w in (w_proj_a, w_proj_b, w_proj_c))
    wout = w_out.astype(bf16)
    poolw = _block_diag(pool_w).astype(bf16)
    sguw = sgu_w.reshape(DEPTH, N_SGU_GROUPS * CHUNK, CHUNK)
    sgub = jnp.repeat(jnp.swapaxes(sgu_b, 1, 2), POOL_GC, axis=2)
    sw0 = jnp.repeat(sgu_w[:, :, 0, 0], POOL_GC, axis=1)
    sb0 = jnp.repeat(sgu_b[:, :, 0], POOL_GC, axis=1)
    sink4 = attn_sinks.reshape(DEPTH, N_KV_HEADS, Q_PER_KV, 1)
    kc_t = jnp.transpose(cache_k_win, (0, 1, 3, 4, 2))
    vc_t = jnp.transpose(cache_v_win, (0, 1, 3, 4, 2))
    pool_t = jnp.transpose(state_pool, (0, 2, 1, 3))

    y_p = x_prompt
    y_s = x_sample.reshape(nb, D_MODEL)
    pool_p, k_p, v_p, pool_s, chunk_v, kts, vts = ([] for _ in range(7))
    for l in range(DEPTH):
        y_p, pp, kp, vp = _prompt_layer(
            l, y_p, cos_p, sin_p, attn_sinks, w_in_b, b_gate, poolw, pool_scale, sgu_ln_g, sgu_ln_b, sguw,
            sgub, wpa, wpb, wpc, wout, ln_g, ln_b)
        pool_p.append(pp); k_p.append(kp); v_p.append(vp)

        q, kn, vn, kt, vt, ps, cv, ain, bin_, szc, g = _sample_proj(
            l, y_s, cos_s, sin_s, w_in_b, b_gate, pool_t, poolw, pool_scale, sgu_ln_g, sgu_ln_b, sw0, sb0)
        o = _sample_attn(
            l, sink4, q.reshape(nb, N_KV_HEADS, Q_PER_KV, HEAD_DIM), kc_t, vc_t,
            kn.reshape(nb, N_KV_HEADS, 1, HEAD_DIM), vn.reshape(nb, N_KV_HEADS, 1, HEAD_DIM))
        y_s = _sample_merge(l, y_s, ain, bin_, o.reshape(nb, D_ATTN), szc, g, wpa, wpb, wpc, wout, ln_g, ln_b)
        pool_s.append(ps); chunk_v.append(cv); kts.append(kt); vts.append(vt)

    k_s, v_s = _cache_slide(jnp.stack(kts), jnp.stack(vts), kc_t, vc_t)

    to_cache = lambda a: jnp.transpose(a, (0, 1, 4, 2, 3))
    prompt_cache = lambda lst: to_cache(jnp.stack(lst).reshape(DEPTH, B, N_KV_HEADS, HEAD_DIM, WINDOW))
    return (y_p, y_s.reshape(nb, 1, D_MODEL),
            jnp.stack(pool_p), prompt_cache(k_p), prompt_cache(v_p),
            jnp.transpose(jnp.stack(pool_s), (0, 2, 1, 3)), to_cache(k_s), to_cache(v_s),
            jnp.stack(chunk_v).reshape(DEPTH, nb, 1, D_SGU))
```

```python
import functools

import numpy as np
import jax
import jax.numpy as jnp
from jax import lax
from jax.experimental import pallas as pl
from jax.experimental.pallas import tpu as pltpu

D_MODEL = 1024
DEPTH = 2
PAST_LEN = 8192
D_POOL = 256
POOL_WINDOWS = (2, 4, 8, 16)
POOL_GC = 64
POOL_BUF = 15
D_SGU = 256
CHUNK = 128
N_SGU_GROUPS = 4
HEAD_DIM = 64
N_HEADS = 8
N_KV_HEADS = 2
Q_PER_KV = 4
D_ATTN = 512
D_KV = 128
WINDOW = 128
BLOCK = 128
ROPE_THETA = 10000.0
N_BRANCHES = 3
D_IN = 2 * D_POOL + 3 * D_SGU + 2 * D_ATTN + 2 * D_KV + N_BRANCHES * D_MODEL
ALPHA = (2.0 * DEPTH) ** 0.25
LN_EPS = 1e-5
NEG_INF = -1e30
SCALE = HEAD_DIM ** -0.5

OFF_XA, OFF_ZA, OFF_U, OFF_V, OFF_ZB = 0, 256, 512, 768, 1024
OFF_Q, OFF_K, OFF_VV, OFF_ZC, OFF_G = 1280, 1792, 1920, 2048, 2560

LANES = 128
TM = 512
NBLK = TM // BLOCK
HIST = 32
NCHUNK = 256
OUT_ROWS = 256
VMEM_LIMIT = 56 * 1024 * 1024
SB = 16

bf16 = jnp.bfloat16
f32 = jnp.float32


def _dot(a, b):
    return jnp.dot(a, b, preferred_element_type=f32)


def _dot_nt(a, b):
    return lax.dot_general(a, b, (((1,), (1,)), ((), ())), preferred_element_type=f32)


def _sigmoid(z):
    return 0.5 * jnp.tanh(0.5 * z) + 0.5


def _silu(z):
    return z * _sigmoid(z)


def _gate2(half_pre, bias):
    return jnp.tanh(half_pre + 0.5 * bias) + 1.0


def _layer_norm(x, g, b):
    mu = jnp.mean(x, axis=-1, keepdims=True)
    xc = x - mu
    var = jnp.mean(xc * xc, axis=-1, keepdims=True)
    return xc * lax.rsqrt(var + LN_EPS) * g + b


def _rope128(x, cos, sin_signed):
    lane = lax.broadcasted_iota(jnp.int32, x.shape, 1)
    first_half = (lane % HEAD_DIM) < (HEAD_DIM // 2)
    partner = jnp.where(first_half,
                        pltpu.roll(x, LANES - HEAD_DIM // 2, 1),
                        pltpu.roll(x, HEAD_DIM // 2, 1))
    return x * cos + partner * sin_signed


def _group_select(lane, a0, a1, a2, a3):
    return jnp.where(lane < 64, a0, jnp.where(lane < 128, a1, jnp.where(lane < 192, a2, a3)))


def _layer_spec(arr, layer, single_buffer=False):
    block = (None,) + arr.shape[1:]
    zeros = (0,) * (arr.ndim - 1)
    index_map = lambda *_: (layer,) + zeros
    if single_buffer:
        return pl.BlockSpec(block, index_map, pipeline_mode=pl.Buffered(1))
    return pl.BlockSpec(block, index_map)


def _full_spec(arr):
    zeros = (0,) * arr.ndim
    return pl.BlockSpec(arr.shape, lambda *_: zeros)


def _slide_windows(step, per_step, new_refs, old_refs, out_refs):
    lane = lax.broadcasted_iota(jnp.int32, (D_KV, WINDOW), 1)
    shift = jnp.where(step == 0, 0, LANES - step * per_step)
    for new_ref, old_ref, out_ref in zip(new_refs, old_refs, out_refs):
        for l in range(DEPTH):
            new_cols = pltpu.roll(new_ref[l], shift, 1)
            for s in range(per_step):
                slid = pltpu.roll(old_ref[l, s].reshape(D_KV, WINDOW), WINDOW - 1, 1)
                newest = jnp.broadcast_to(new_cols[:, s:s + 1], (D_KV, WINDOW))
                out_ref[l, s] = jnp.where(lane == WINDOW - 1, newest, slid).reshape(
                    N_KV_HEADS, HEAD_DIM, WINDOW)


def _prompt_kernel(layer, slide, *refs):
    n_in = 18 + (4 if slide else 0)
    n_out = 4 + (2 if slide else 0)
    (sinks_ref, x_ref, cos_ref, sin_ref, w_in_ref, bg_ref, poolw_ref, pscale_ref,
     slng_ref, slnb_ref, sguw_ref, sgub_ref, wpa_ref, wpb_ref, wpc_ref, wout_ref,
     lng_ref, lnb_ref) = refs[:18]
    y_ref, pool_out_ref, k_out_ref, v_out_ref = refs[n_in:n_in + 4]
    (h_ref, ext_ref, s2_ref, s4_ref, s8_ref, qs_ref,
     ka_ref, kb_ref, kc_ref, kd_ref, va_ref, vb_ref, vc_ref, vd_ref,
     ain_ref, bin_ref, cin_ref, mg_ref, gate_ref, klast_ref, vlast_ref) = refs[n_in + n_out:]
    i = pl.program_id(1)
    last = pl.num_programs(1) - 1
    kv_refs = (ka_ref, kb_ref, kc_ref, kd_ref, va_ref, vb_ref, vc_ref, vd_ref)
    row = lambda ref: ref[layer:layer + 1, :]

    @pl.when(i == 0)
    def _():
        ext_ref[0:HIST, :] = jnp.zeros((HIST, D_POOL), f32)
        for r in kv_refs:
            r[0:BLOCK, :] = jnp.zeros((BLOCK, LANES), bf16)

    xb = x_ref[...].astype(bf16)
    half = OFF_G // 2
    h_ref[:, 0:half] = _dot(xb, w_in_ref[:, 0:half])
    h_ref[:, half:OFF_G] = _dot(xb, w_in_ref[:, half:OFF_G])

    if slide:
        step = pl.program_id(0) * pl.num_programs(1) + i
        _slide_windows(step, refs[20].shape[1], refs[18:20], refs[20:22], refs[n_in + 4:n_in + 6])

    xa = h_ref[:, OFF_XA:OFF_XA + D_POOL]
    ext_ref[HIST:HIST + TM, :] = xa
    n = HIST + TM
    s2_ref[8:n, :] = ext_ref[8:n, :] + ext_ref[7:n - 1, :]
    s4_ref[16:n, :] = s2_ref[16:n, :] + s2_ref[14:n - 2, :]
    s8_ref[24:n, :] = s4_ref[24:n, :] + s4_ref[20:n - 4, :]
    w16 = s8_ref[HIST:n, :] + s8_ref[HIST - 8:n - 8, :]
    lane_p = lax.broadcasted_iota(jnp.int32, (TM, D_POOL), 1)
    row_p = lax.broadcasted_iota(jnp.int32, (TM, D_POOL), 0)
    win = _group_select(lane_p, s2_ref[HIST:n, :], s4_ref[HIST:n, :], s8_ref[HIST:n, :], w16)
    width = _group_select(lane_p, POOL_WINDOWS[0], POOL_WINDOWS[1], POOL_WINDOWS[2], POOL_WINDOWS[3])
    cnt = jnp.minimum(row_p + (i * TM + 1), width).astype(f32)
    pooled = win / cnt - xa
    ya = _dot(pooled.astype(bf16), poolw_ref[...]) * row(pscale_ref)
    za = h_ref[:, OFF_ZA:OFF_ZA + D_POOL]
    ain_ref[...] = (ya * _silu(za)).astype(bf16)
    ext_ref[HIST - 16:HIST, :] = ext_ref[n - 16:n, :]

    vn = _layer_norm(h_ref[:, OFF_V:OFF_V + D_SGU], row(slng_ref), row(slnb_ref)).astype(bf16)
    wr = lax.broadcasted_iota(jnp.int32, (N_SGU_GROUPS * CHUNK, CHUNK), 0) % CHUNK
    wc = lax.broadcasted_iota(jnp.int32, (N_SGU_GROUPS * CHUNK, CHUNK), 1)
    w_s = jnp.where(wc <= wr, sguw_ref[...], 0.0).astype(bf16)
    lane_c = lax.broadcasted_iota(jnp.int32, (CHUNK, D_SGU), 1)
    for j in range(NBLK):
        rows = slice(j * CHUNK, (j + 1) * CHUNK)
        r = _dot(w_s, vn[rows, :])
        s = _group_select(lane_c, r[0:CHUNK], r[CHUNK:2 * CHUNK], r[2 * CHUNK:3 * CHUNK],
                          r[3 * CHUNK:4 * CHUNK]) + sgub_ref[...]
        yb = h_ref[rows, OFF_U:OFF_U + D_SGU] * s
        bin_ref[rows, :] = (yb * _silu(h_ref[rows, OFF_ZB:OFF_ZB + D_SGU])).astype(bf16)

    cos = cos_ref[...]
    sin = sin_ref[...]
    for c in range(D_ATTN // LANES):
        qc = _rope128(h_ref[:, OFF_Q + c * LANES:OFF_Q + (c + 1) * LANES], cos, sin)
        qs_ref[:, c * LANES:(c + 1) * LANES] = (qc * SCALE).astype(bf16)
    kr = _rope128(h_ref[:, OFF_K:OFF_K + D_KV], cos, sin)
    vv = h_ref[:, OFF_VV:OFF_VV + D_KV]
    klast_ref[...] = kr[TM - WINDOW:TM, :]
    vlast_ref[...] = vv[TM - WINDOW:TM, :]
    lane_k = lax.broadcasted_iota(jnp.int32, (TM, LANES), 1)
    lo = lane_k < HEAD_DIM
    for t, (a_ref, b_ref, c_ref, d_ref) in ((kr, kv_refs[0:4]), (vv, kv_refs[4:8])):
        sw = pltpu.roll(t, HEAD_DIM, 1)
        a_ref[BLOCK:BLOCK + TM, :] = jnp.where(lo, t, 0.0).astype(bf16)
        b_ref[BLOCK:BLOCK + TM, :] = jnp.where(lo, 0.0, t).astype(bf16)
        c_ref[BLOCK:BLOCK + TM, :] = jnp.where(lo, sw, 0.0).astype(bf16)
        d_ref[BLOCK:BLOCK + TM, :] = jnp.where(lo, 0.0, sw).astype(bf16)

    qrow = lax.broadcasted_iota(jnp.int32, (2 * BLOCK, 2 * BLOCK), 0) % BLOCK
    kcol = lax.broadcasted_iota(jnp.int32, (2 * BLOCK, 2 * BLOCK), 1)
    band = (kcol >= qrow) & (kcol <= qrow + WINDOW)
    band_first = band & (kcol >= jnp.where(i > 0, 0, BLOCK))
    top = lax.broadcasted_iota(jnp.int32, (2 * BLOCK, 1), 0) < BLOCK
    n_gate = N_BRANCHES * D_MODEL // NCHUNK
    n_unit = NBLK * N_KV_HEADS
    gate_sched = [range(u * n_gate // n_unit, (u + 1) * n_gate // n_unit) for u in range(n_unit)]

    def scores(u):
        j, kv = divmod(u, N_KV_HEADS)
        rows = slice(j * BLOCK, (j + 1) * BLOCK)
        keys = slice(j * BLOCK, j * BLOCK + 2 * BLOCK)
        c0 = kv * Q_PER_KV * HEAD_DIM
        qst = jnp.concatenate([qs_ref[rows, c0:c0 + LANES], qs_ref[rows, c0 + LANES:c0 + 2 * LANES]], axis=0)
        k_even, k_odd = (ka_ref, kd_ref) if kv == 0 else (kc_ref, kb_ref)
        kcat = jnp.concatenate([k_even[keys, :], k_odd[keys, :]], axis=0)
        return _dot_nt(qst, kcat)

    def attend(u, sc):
        j, kv = divmod(u, N_KV_HEADS)
        rows = slice(j * BLOCK, (j + 1) * BLOCK)
        keys = slice(j * BLOCK, j * BLOCK + 2 * BLOCK)
        allowed = band_first if j == 0 else band
        c0 = kv * Q_PER_KV * HEAD_DIM
        h0 = kv * Q_PER_KV
        v_even, v_odd = (va_ref, vd_ref) if kv == 0 else (vc_ref, vb_ref)
        probs = []
        for par in range(2):
            sink = jnp.where(top, sinks_ref[layer, h0 + par], sinks_ref[layer, h0 + 2 + par])
            sm = jnp.where(allowed, sc[:, par * 2 * BLOCK:(par + 1) * 2 * BLOCK], NEG_INF)
            m = jnp.maximum(jnp.max(sm, axis=-1, keepdims=True), sink)
            p = jnp.exp(sm - m)
            den = jnp.sum(p, axis=-1, keepdims=True) + jnp.exp(sink - m)
            probs.append((p / den).astype(bf16))
        pcat = jnp.concatenate(probs, axis=1)
        vcat = jnp.concatenate([v_even[keys, :], v_odd[keys, :]], axis=0)
        o = _dot(pcat, vcat)
        for pr in range(2):
            cols = slice(c0 + pr * LANES, c0 + (pr + 1) * LANES)
            zc = h_ref[rows, OFF_ZC + c0 + pr * LANES:OFF_ZC + c0 + (pr + 1) * LANES]
            cin_ref[rows, cols] = (o[pr * BLOCK:(pr + 1) * BLOCK] * _silu(zc)).astype(bf16)

    sc_next = scores(0)
    for u in range(n_unit):
        sc = sc_next
        for gc in gate_sched[u]:
            gcols = slice(gc * NCHUNK, (gc + 1) * NCHUNK)
            gate_ref[:, gcols] = _dot(xb, w_in_ref[:, OFF_G + gc * NCHUNK:OFF_G + (gc + 1) * NCHUNK])
        if u + 1 < n_unit:
            sc_next = scores(u + 1)
        attend(u, sc)

    for r in kv_refs:
        r[0:BLOCK, :] = r[TM:TM + BLOCK, :]

    for c in range(D_MODEL // NCHUNK):
        cols = slice(c * NCHUNK, (c + 1) * NCHUNK)
        acc = None
        for br, (in_ref, wp_ref) in enumerate(((ain_ref, wpa_ref), (bin_ref, wpb_ref), (cin_ref, wpc_ref))):
            g0 = br * D_MODEL + c * NCHUNK
            term = _gate2(gate_ref[:, g0:g0 + NCHUNK], bg_ref[br:br + 1, cols]) * _dot(in_ref[...], wp_ref[:, cols])
            acc = term if acc is None else acc + term
        mg_ref[:, cols] = acc.astype(bf16)

    for r0 in range(0, TM, OUT_ROWS):
        rows = slice(r0, r0 + OUT_ROWS)
        out = _dot(mg_ref[rows, :], wout_ref[...])
        y_ref[rows, :] = _layer_norm(ALPHA * x_ref[rows, :] + out, row(lng_ref), row(lnb_ref))

    @pl.when(i == last)
    def _():
        pool_out_ref[...] = ext_ref[n - POOL_BUF:n, :]
        k_out_ref[...] = klast_ref[...].T
        v_out_ref[...] = vlast_ref[...].T


def _prompt_layer(layer, x, cos, sin, sinks, w_in, b_gate, poolw, pscale, slng, slnb, sguw, sgub,
                  wpa, wpb, wpc, wout, lng, lnb, slide_args=None):
    B, L, _ = x.shape
    nt = L // TM
    grid = (B, nt)
    row_spec = lambda w: pl.BlockSpec((TM, w), lambda b, i: (i, 0))
    lspec = lambda a: _layer_spec(a, layer, single_buffer=True)
    in_specs = [
        pl.BlockSpec(memory_space=pltpu.SMEM),
        pl.BlockSpec((None, TM, D_MODEL), lambda b, i: (b, i, 0)),
        row_spec(LANES), row_spec(LANES),
        lspec(w_in), lspec(b_gate), lspec(poolw), _full_spec(pscale), _full_spec(slng), _full_spec(slnb),
        lspec(sguw), lspec(sgub), lspec(wpa), lspec(wpb), lspec(wpc), lspec(wout),
        _full_spec(lng), _full_spec(lnb),
    ]
    out_shape = (
        jax.ShapeDtypeStruct((B, L, D_MODEL), f32),
        jax.ShapeDtypeStruct((B, POOL_BUF, D_POOL), f32),
        jax.ShapeDtypeStruct((B, D_KV, WINDOW), f32),
        jax.ShapeDtypeStruct((B, D_KV, WINDOW), f32),
    )
    out_specs = (
        pl.BlockSpec((None, TM, D_MODEL), lambda b, i: (b, i, 0)),
        pl.BlockSpec((None, POOL_BUF, D_POOL), lambda b, i: (b, 0, 0)),
        pl.BlockSpec((None, D_KV, WINDOW), lambda b, i: (b, 0, 0)),
        pl.BlockSpec((None, D_KV, WINDOW), lambda b, i: (b, 0, 0)),
    )
    args = [sinks, x, cos, sin, w_in, b_gate, poolw, pscale, slng, slnb, sguw, sgub,
            wpa, wpb, wpc, wout, lng, lnb]
    if slide_args is not None:
        new_k, new_v, cache_k, cache_v = slide_args
        depth, nb = cache_k.shape[:2]
        per_step = nb // (B * nt)
        assert per_step * B * nt == nb and depth == DEPTH
        cache_spec = pl.BlockSpec((depth, per_step) + cache_k.shape[2:], lambda b, i: (0, b * nt + i, 0, 0, 0))
        in_specs += [_full_spec(new_k), _full_spec(new_v), cache_spec, cache_spec]
        args += [new_k, new_v, cache_k, cache_v]
        out_shape += (jax.ShapeDtypeStruct(cache_k.shape, f32), jax.ShapeDtypeStruct(cache_v.shape, f32))
        out_specs += (cache_spec, cache_spec)
    kv_scratch = [pltpu.VMEM((BLOCK + TM, LANES), bf16) for _ in range(8)]
    scratch = [
        pltpu.VMEM((TM, OFF_G), f32),
        pltpu.VMEM((HIST + TM, D_POOL), f32),
        pltpu.VMEM((HIST + TM, D_POOL), f32),
        pltpu.VMEM((HIST + TM, D_POOL), f32),
        pltpu.VMEM((HIST + TM, D_POOL), f32),
        pltpu.VMEM((TM, D_ATTN), bf16),
        *kv_scratch,
        pltpu.VMEM((TM, D_POOL), bf16),
        pltpu.VMEM((TM, D_SGU), bf16),
        pltpu.VMEM((TM, D_ATTN), bf16),
        pltpu.VMEM((TM, D_MODEL), bf16),
        pltpu.VMEM((TM, N_BRANCHES * D_MODEL), f32),
        pltpu.VMEM((WINDOW, D_KV), f32),
        pltpu.VMEM((WINDOW, D_KV), f32),
    ]
    return pl.pallas_call(
        functools.partial(_prompt_kernel, layer, slide_args is not None),
        out_shape=out_shape,
        grid=grid,
        in_specs=in_specs,
        out_specs=out_specs,
        scratch_shapes=scratch,
        compiler_params=pltpu.CompilerParams(
            dimension_semantics=("arbitrary", "arbitrary"),
            vmem_limit_bytes=VMEM_LIMIT),
        name="prompt_layer",
    )(*args)


def _sample_proj_kernel(layer, x_ref, cos_ref, sin_ref, w_in_ref, bg_ref, pb_ref, poolw_ref, pscale_ref,
                        slng_ref, slnb_ref, sw0_ref, sb0_ref,
                        q_ref, k_ref, v_ref, kt_ref, vt_ref, pool_ref, vn_ref, ain_ref, bin_ref, szc_ref,
                        g_ref):
    row = lambda ref: ref[layer:layer + 1, :]
    xb = x_ref[...].astype(bf16)
    h = _dot(xb, w_in_ref[:, 0:OFF_G])
    xa = h[:, OFF_XA:OFF_XA + D_POOL]
    lane = lax.broadcasted_iota(jnp.int32, xa.shape, 1)
    first_row = _group_select(lane, *(POOL_BUF - (w - 1) for w in POOL_WINDOWS))
    win = xa
    for r in range(POOL_BUF):
        win = win + jnp.where(first_row <= r, pb_ref[r], 0.0)
    width = _group_select(lane, *POOL_WINDOWS).astype(f32)
    pooled = win / width - xa
    ya = _dot(pooled.astype(bf16), poolw_ref[...]) * row(pscale_ref)
    ain_ref[...] = (ya * _silu(h[:, OFF_ZA:OFF_ZA + D_POOL])).astype(bf16)
    for r in range(POOL_BUF - 1):
        pool_ref[r] = pb_ref[r + 1]
    pool_ref[POOL_BUF - 1] = xa
    vn = _layer_norm(h[:, OFF_V:OFF_V + D_SGU], row(slng_ref), row(slnb_ref))
    vn_ref[...] = vn
    yb = h[:, OFF_U:OFF_U + D_SGU] * (row(sw0_ref) * vn + row(sb0_ref))
    bin_ref[...] = (yb * _silu(h[:, OFF_ZB:OFF_ZB + D_SGU])).astype(bf16)
    cos = cos_ref[...]
    sin = sin_ref[...]
    for c in range(D_ATTN // LANES):
        q_ref[:, c * LANES:(c + 1) * LANES] = _rope128(
            h[:, OFF_Q + c * LANES:OFF_Q + (c + 1) * LANES], cos, sin) * SCALE
    kr = _rope128(h[:, OFF_K:OFF_K + D_KV], cos, sin)
    vv = h[:, OFF_VV:OFF_VV + D_KV]
    k_ref[...] = kr
    v_ref[...] = vv
    kt_ref[...] = kr.T
    vt_ref[...] = vv.T
    szc_ref[...] = _silu(h[:, OFF_ZC:OFF_ZC + D_ATTN])
    for br in range(N_BRANCHES):
        cols = slice(br * D_MODEL, (br + 1) * D_MODEL)
        g_ref[:, cols] = _gate2(
            _dot(xb, w_in_ref[:, OFF_G + br * D_MODEL:OFF_G + (br + 1) * D_MODEL]), bg_ref[br:br + 1, :])


def _sample_proj(layer, x, cos, sin, w_in, b_gate, pool_t, poolw, pscale, slng, slnb, sw0, sb0):
    nb = x.shape[0]
    out_shape = (
        jax.ShapeDtypeStruct((nb, D_ATTN), f32),
        jax.ShapeDtypeStruct((nb, D_KV), f32),
        jax.ShapeDtypeStruct((nb, D_KV), f32),
        jax.ShapeDtypeStruct((D_KV, nb), f32),
        jax.ShapeDtypeStruct((D_KV, nb), f32),
        jax.ShapeDtypeStruct((POOL_BUF, nb, D_POOL), f32),
        jax.ShapeDtypeStruct((nb, D_SGU), f32),
        jax.ShapeDtypeStruct((nb, D_POOL), bf16),
        jax.ShapeDtypeStruct((nb, D_SGU), bf16),
        jax.ShapeDtypeStruct((nb, D_ATTN), f32),
        jax.ShapeDtypeStruct((nb, N_BRANCHES * D_MODEL), f32),
    )
    lspec = lambda a: _layer_spec(a, layer)
    args = (x, cos, sin, w_in, b_gate, pool_t, poolw, pscale, slng, slnb, sw0, sb0)
    in_specs = [_full_spec(x), _full_spec(cos), _full_spec(sin), lspec(w_in), lspec(b_gate), lspec(pool_t),
                lspec(poolw), _full_spec(pscale), _full_spec(slng), _full_spec(slnb), _full_spec(sw0),
                _full_spec(sb0)]
    out_specs = tuple(pl.BlockSpec(s.shape, functools.partial(lambda nd, i: (0,) * nd, len(s.shape)))
                      for s in out_shape)
    return pl.pallas_call(
        functools.partial(_sample_proj_kernel, layer),
        out_shape=out_shape,
        grid=(1,),
        in_specs=in_specs,
        out_specs=out_specs,
        compiler_params=pltpu.CompilerParams(dimension_semantics=("arbitrary",),
                                             vmem_limit_bytes=VMEM_LIMIT),
        name="sample_proj",
    )(*args)


def _sample_attn_kernel(sink_ref, q_ref, kc_ref, vc_ref, kn_ref, vn_ref, o_ref):
    for kv in range(N_KV_HEADS):
        qb = q_ref[:, kv].astype(bf16)
        kn = kn_ref[:, kv]
        vn = vn_ref[:, kv]
        sink = sink_ref[kv][None]
        s = jnp.einsum('bgd,bdw->bgw', qb, kc_ref[:, kv].astype(bf16), preferred_element_type=f32)
        s_new = jnp.sum(qb.astype(f32) * kn.astype(bf16).astype(f32), axis=-1, keepdims=True)
        m = jnp.maximum(jnp.maximum(jnp.max(s, axis=-1, keepdims=True), s_new), sink)
        p = jnp.exp(s - m)
        p_new = jnp.exp(s_new - m)
        den = jnp.sum(p, axis=-1, keepdims=True) + p_new + jnp.exp(sink - m)
        o = jnp.einsum('bgw,bdw->bgd', (p / den).astype(bf16), vc_ref[:, kv].astype(bf16),
                       preferred_element_type=f32)
        o_ref[:, kv] = o + (p_new / den) * vn


def _sample_attn(layer, sink, q4, kc, vc, kn, vn):
    nb = q4.shape[0]
    cache_spec = pl.BlockSpec((None, SB, N_KV_HEADS, HEAD_DIM, WINDOW), lambda b: (layer, b, 0, 0, 0))
    blk = lambda s2, s3: pl.BlockSpec((SB, N_KV_HEADS, s2, s3), lambda b: (b, 0, 0, 0))
    return pl.pallas_call(
        _sample_attn_kernel,
        out_shape=jax.ShapeDtypeStruct((nb, N_KV_HEADS, Q_PER_KV, HEAD_DIM), f32),
        grid=(nb // SB,),
        in_specs=[_layer_spec(sink, layer), blk(Q_PER_KV, HEAD_DIM), cache_spec, cache_spec,
                  blk(1, HEAD_DIM), blk(1, HEAD_DIM)],
        out_specs=blk(Q_PER_KV, HEAD_DIM),
        compiler_params=pltpu.CompilerParams(dimension_semantics=("arbitrary",),
                                             vmem_limit_bytes=VMEM_LIMIT),
        name="sample_attn",
    )(sink, q4, kc, vc, kn, vn)


def _sample_merge_kernel(layer, x_ref, ain_ref, bin_ref, yc_ref, szc_ref, g_ref, wpa_ref, wpb_ref, wpc_ref,
                         wout_ref, lng_ref, lnb_ref, y_ref):
    row = lambda ref: ref[layer:layer + 1, :]
    cin = (yc_ref[...] * szc_ref[...]).astype(bf16)
    merged = (g_ref[:, 0:D_MODEL] * _dot(ain_ref[...], wpa_ref[...])
              + g_ref[:, D_MODEL:2 * D_MODEL] * _dot(bin_ref[...], wpb_ref[...])
              + g_ref[:, 2 * D_MODEL:3 * D_MODEL] * _dot(cin, wpc_ref[...]))
    out = _dot(merged.astype(bf16), wout_ref[...])
    y_ref[...] = _layer_norm(ALPHA * x_ref[...] + out, row(lng_ref), row(lnb_ref))


def _sample_merge(layer, x, ain, bin_, yc, szc, g, wpa, wpb, wpc, wout, lng, lnb):
    lspec = lambda a: _layer_spec(a, layer)
    return pl.pallas_call(
        functools.partial(_sample_merge_kernel, layer),
        out_shape=jax.ShapeDtypeStruct(x.shape, f32),
        grid=(1,),
        in_specs=[_full_spec(x), _full_spec(ain), _full_spec(bin_), _full_spec(yc), _full_spec(szc),
                  _full_spec(g), lspec(wpa), lspec(wpb), lspec(wpc), lspec(wout), _full_spec(lng),
                  _full_spec(lnb)],
        out_specs=_full_spec(x),
        compiler_params=pltpu.CompilerParams(dimension_semantics=("arbitrary",),
                                             vmem_limit_bytes=VMEM_LIMIT),
        name="sample_merge",
    )(x, ain, bin_, yc, szc, g, wpa, wpb, wpc, wout, lng, lnb)


def _rope_tables(positions):
    halfd = HEAD_DIM // 2
    inv = ROPE_THETA ** (-np.arange(halfd, dtype=np.float64) / halfd)
    ang = np.asarray(positions, dtype=np.float64)[:, None] * inv[None, :]
    cos = np.tile(np.cos(ang), (1, LANES // halfd))
    sin = np.tile(np.concatenate([-np.sin(ang), np.sin(ang)], axis=1), (1, LANES // HEAD_DIM))
    return jnp.asarray(cos, f32), jnp.asarray(sin, f32)


def _block_diag(w):
    nl, g, c, _ = w.shape
    eye = jnp.eye(g, dtype=w.dtype)
    return (eye[None, :, None, :, None] * w[:, :, :, None, :]).reshape(nl, g * c, g * c)


def kernel(x_prompt, x_sample, state_pool, cache_k_win, cache_v_win, w_in, b_gate, pool_w, pool_scale, sgu_ln_g, sgu_ln_b, sgu_w, sgu_b, attn_sinks, w_proj_a, w_proj_b, w_proj_c, w_out, ln_g, ln_b):
    B, L, _ = x_prompt.shape
    nb = x_sample.shape[0]
    cos_p, sin_p = _rope_tables(np.arange(L))
    cos_s, sin_s = _rope_tables(np.array([PAST_LEN]))

    gate_cols = jnp.arange(D_IN) >= OFF_G
    w_in_b = (w_in * jnp.where(gate_cols, 0.5, 1.0).astype(f32)).astype(bf16)
    wpa, wpb, wpc = ((0.5 * w).astype(bf16) for w in (w_proj_a, w_proj_b, w_proj_c))
    wout = w_out.astype(bf16)
    poolw = _block_diag(pool_w).astype(bf16)
    sguw = sgu_w.reshape(DEPTH, N_SGU_GROUPS * CHUNK, CHUNK)
    sgub = jnp.repeat(jnp.swapaxes(sgu_b, 1, 2), POOL_GC, axis=2)
    sw0 = jnp.repeat(sgu_w[:, :, 0, 0], POOL_GC, axis=1)
    sb0 = jnp.repeat(sgu_b[:, :, 0], POOL_GC, axis=1)
    sink4 = attn_sinks.reshape(DEPTH, N_KV_HEADS, Q_PER_KV, 1)
    kc_t = jnp.transpose(cache_k_win, (0, 1, 3, 4, 2))
    vc_t = jnp.transpose(cache_v_win, (0, 1, 3, 4, 2))
    pool_t = jnp.transpose(state_pool, (0, 2, 1, 3))

    y_s = x_sample.reshape(nb, D_MODEL)
    pool_s, chunk_v, kts, vts = ([] for _ in range(4))
    for l in range(DEPTH):
        q, kn, vn, kt, vt, ps, cv, ain, bin_, szc, g = _sample_proj(
            l, y_s, cos_s, sin_s, w_in_b, b_gate, pool_t, poolw, pool_scale, sgu_ln_g, sgu_ln_b, sw0, sb0)
        o = _sample_attn(
            l, sink4, q.reshape(nb, N_KV_HEADS, Q_PER_KV, HEAD_DIM), kc_t, vc_t,
            kn.reshape(nb, N_KV_HEADS, 1, HEAD_DIM), vn.reshape(nb, N_KV_HEADS, 1, HEAD_DIM))
        y_s = _sample_merge(l, y_s, ain, bin_, o.reshape(nb, D_ATTN), szc, g, wpa, wpb, wpc, wout, ln_g, ln_b)
        pool_s.append(ps); chunk_v.append(cv); kts.append(kt); vts.append(vt)

    y_p = x_prompt
    pool_p, k_p, v_p = ([] for _ in range(3))
    for l in range(DEPTH):
        slide_args = (jnp.stack(kts), jnp.stack(vts), kc_t, vc_t) if l == DEPTH - 1 else None
        y_p, pp, kp, vp, *slid = _prompt_layer(
            l, y_p, cos_p, sin_p, attn_sinks, w_in_b, b_gate, poolw, pool_scale, sgu_ln_g, sgu_ln_b, sguw,
            sgub, wpa, wpb, wpc, wout, ln_g, ln_b, slide_args=slide_args)
        pool_p.append(pp); k_p.append(kp); v_p.append(vp)
    k_s, v_s = slid

    to_cache = lambda a: jnp.transpose(a, (0, 1, 4, 2, 3))
    prompt_cache = lambda lst: to_cache(jnp.stack(lst).reshape(DEPTH, B, N_KV_HEADS, HEAD_DIM, WINDOW))
    return (y_p, y_s.reshape(nb, 1, D_MODEL),
            jnp.stack(pool_p), prompt_cache(k_p), prompt_cache(v_p),
            jnp.transpose(jnp.stack(pool_s), (0, 2, 1, 3)), to_cache(k_s), to_cache(v_s),
            jnp.stack(chunk_v).reshape(DEPTH, nb, 1, D_SGU))
```

```python
import functools

import numpy as np
import jax
import jax.numpy as jnp
from jax import lax
from jax.experimental import pallas as pl
from jax.experimental.pallas import tpu as pltpu

D_MODEL = 1024
DEPTH = 2
PAST_LEN = 8192
D_POOL = 256
POOL_WINDOWS = (2, 4, 8, 16)
POOL_GC = 64
POOL_BUF = 15
D_SGU = 256
CHUNK = 128
N_SGU_GROUPS = 4
HEAD_DIM = 64
N_HEADS = 8
N_KV_HEADS = 2
Q_PER_KV = 4
D_ATTN = 512
D_KV = 128
WINDOW = 128
BLOCK = 128
ROPE_THETA = 10000.0
N_BRANCHES = 3
D_IN = 2 * D_POOL + 3 * D_SGU + 2 * D_ATTN + 2 * D_KV + N_BRANCHES * D_MODEL
ALPHA = (2.0 * DEPTH) ** 0.25
LN_EPS = 1e-5
NEG_INF = -1e30
SCALE = HEAD_DIM ** -0.5

OFF_XA, OFF_ZA, OFF_U, OFF_V, OFF_ZB = 0, 256, 512, 768, 1024
OFF_Q, OFF_K, OFF_VV, OFF_ZC, OFF_G = 1280, 1792, 1920, 2048, 2560

LANES = 128
TM = 512
NBLK = TM // BLOCK
HIST = 32
NCHUNK = 256
OUT_ROWS = 256
VMEM_LIMIT = 56 * 1024 * 1024
SB = 16
CW = 512
NCW = D_IN // CW

bf16 = jnp.bfloat16
f32 = jnp.float32


def _dot(a, b):
    return jnp.dot(a, b, preferred_element_type=f32)


def _dot_nt(a, b):
    return lax.dot_general(a, b, (((1,), (1,)), ((), ())), preferred_element_type=f32)


def _sigmoid(z):
    return 0.5 * jnp.tanh(0.5 * z) + 0.5


def _silu(z):
    return z * _sigmoid(z)


def _gate2(half_pre, bias):
    return jnp.tanh(half_pre + 0.5 * bias) + 1.0


def _layer_norm(x, g, b):
    mu = jnp.mean(x, axis=-1, keepdims=True)
    xc = x - mu
    var = jnp.mean(xc * xc, axis=-1, keepdims=True)
    return xc * lax.rsqrt(var + LN_EPS) * g + b


def _rope128(x, cos, sin_signed):
    lane = lax.broadcasted_iota(jnp.int32, x.shape, 1)
    first_half = (lane % HEAD_DIM) < (HEAD_DIM // 2)
    partner = jnp.where(first_half,
                        pltpu.roll(x, LANES - HEAD_DIM // 2, 1),
                        pltpu.roll(x, HEAD_DIM // 2, 1))
    return x * cos + partner * sin_signed


def _group_select(lane, a0, a1, a2, a3):
    return jnp.where(lane < 64, a0, jnp.where(lane < 128, a1, jnp.where(lane < 192, a2, a3)))


def _layer_spec(arr, layer, single_buffer=False):
    block = (None,) + arr.shape[1:]
    zeros = (0,) * (arr.ndim - 1)
    index_map = lambda *_: (layer,) + zeros
    if single_buffer:
        return pl.BlockSpec(block, index_map, pipeline_mode=pl.Buffered(1))
    return pl.BlockSpec(block, index_map)


def _full_spec(arr, single_buffer=False):
    zeros = (0,) * len(arr.shape)
    if single_buffer:
        return pl.BlockSpec(arr.shape, lambda *_: zeros, pipeline_mode=pl.Buffered(1))
    return pl.BlockSpec(arr.shape, lambda *_: zeros)


def _slide_windows(step, per_step, new_refs, old_refs, out_refs):
    lane = lax.broadcasted_iota(jnp.int32, (D_KV, WINDOW), 1)
    shift = jnp.where(step == 0, 0, LANES - step * per_step)
    for new_ref, old_ref, out_ref in zip(new_refs, old_refs, out_refs):
        for l in range(DEPTH):
            new_cols = pltpu.roll(new_ref[l], shift, 1)
            for s in range(per_step):
                slid = pltpu.roll(old_ref[l, s].reshape(D_KV, WINDOW), WINDOW - 1, 1)
                newest = jnp.broadcast_to(new_cols[:, s:s + 1], (D_KV, WINDOW))
                out_ref[l, s] = jnp.where(lane == WINDOW - 1, newest, slid).reshape(
                    N_KV_HEADS, HEAD_DIM, WINDOW)


def _prompt_kernel(layer, slide, *refs):
    n_in = 18 + (4 if slide else 0)
    n_out = 4 + (2 if slide else 0)
    (sinks_ref, x_ref, cos_ref, sin_ref, w_in_ref, bg_ref, poolw_ref, pscale_ref,
     slng_ref, slnb_ref, sguw_ref, sgub_ref, wpa_ref, wpb_ref, wpc_ref, wout_ref,
     lng_ref, lnb_ref) = refs[:18]
    y_ref, pool_out_ref, k_out_ref, v_out_ref = refs[n_in:n_in + 4]
    (h_ref, ext_ref, s2_ref, s4_ref, s8_ref, qs_ref,
     ka_ref, kb_ref, kc_ref, kd_ref, va_ref, vb_ref, vc_ref, vd_ref,
     ain_ref, bin_ref, cin_ref, mg_ref, gate_ref, klast_ref, vlast_ref) = refs[n_in + n_out:]
    i = pl.program_id(1)
    last = pl.num_programs(1) - 1
    kv_refs = (ka_ref, kb_ref, kc_ref, kd_ref, va_ref, vb_ref, vc_ref, vd_ref)
    row = lambda ref: ref[layer:layer + 1, :]

    @pl.when(i == 0)
    def _():
        ext_ref[0:HIST, :] = jnp.zeros((HIST, D_POOL), f32)
        for r in kv_refs:
            r[0:BLOCK, :] = jnp.zeros((BLOCK, LANES), bf16)

    xb = x_ref[...].astype(bf16)
    half = OFF_G // 2
    h_ref[:, 0:half] = _dot(xb, w_in_ref[:, 0:half])
    h_ref[:, half:OFF_G] = _dot(xb, w_in_ref[:, half:OFF_G])

    if slide:
        step = pl.program_id(0) * pl.num_programs(1) + i
        _slide_windows(step, refs[20].shape[1], refs[18:20], refs[20:22], refs[n_in + 4:n_in + 6])

    xa = h_ref[:, OFF_XA:OFF_XA + D_POOL]
    ext_ref[HIST:HIST + TM, :] = xa
    n = HIST + TM
    s2_ref[8:n, :] = ext_ref[8:n, :] + ext_ref[7:n - 1, :]
    s4_ref[16:n, :] = s2_ref[16:n, :] + s2_ref[14:n - 2, :]
    s8_ref[24:n, :] = s4_ref[24:n, :] + s4_ref[20:n - 4, :]
    w16 = s8_ref[HIST:n, :] + s8_ref[HIST - 8:n - 8, :]
    lane_p = lax.broadcasted_iota(jnp.int32, (TM, D_POOL), 1)
    row_p = lax.broadcasted_iota(jnp.int32, (TM, D_POOL), 0)
    win = _group_select(lane_p, s2_ref[HIST:n, :], s4_ref[HIST:n, :], s8_ref[HIST:n, :], w16)
    width = _group_select(lane_p, POOL_WINDOWS[0], POOL_WINDOWS[1], POOL_WINDOWS[2], POOL_WINDOWS[3])
    cnt = jnp.minimum(row_p + (i * TM + 1), width).astype(f32)
    pooled = win / cnt - xa
    ya = _dot(pooled.astype(bf16), poolw_ref[...]) * row(pscale_ref)
    za = h_ref[:, OFF_ZA:OFF_ZA + D_POOL]
    ain_ref[...] = (ya * _silu(za)).astype(bf16)
    ext_ref[HIST - 16:HIST, :] = ext_ref[n - 16:n, :]

    vn = _layer_norm(h_ref[:, OFF_V:OFF_V + D_SGU], row(slng_ref), row(slnb_ref)).astype(bf16)
    wr = lax.broadcasted_iota(jnp.int32, (N_SGU_GROUPS * CHUNK, CHUNK), 0) % CHUNK
    wc = lax.broadcasted_iota(jnp.int32, (N_SGU_GROUPS * CHUNK, CHUNK), 1)
    w_s = jnp.where(wc <= wr, sguw_ref[...], 0.0).astype(bf16)
    lane_c = lax.broadcasted_iota(jnp.int32, (CHUNK, D_SGU), 1)
    for j in range(NBLK):
        rows = slice(j * CHUNK, (j + 1) * CHUNK)
        r = _dot(w_s, vn[rows, :])
        s = _group_select(lane_c, r[0:CHUNK], r[CHUNK:2 * CHUNK], r[2 * CHUNK:3 * CHUNK],
                          r[3 * CHUNK:4 * CHUNK]) + sgub_ref[...]
        yb = h_ref[rows, OFF_U:OFF_U + D_SGU] * s
        bin_ref[rows, :] = (yb * _silu(h_ref[rows, OFF_ZB:OFF_ZB + D_SGU])).astype(bf16)

    cos = cos_ref[...]
    sin = sin_ref[...]
    for c in range(D_ATTN // LANES):
        qc = _rope128(h_ref[:, OFF_Q + c * LANES:OFF_Q + (c + 1) * LANES], cos, sin)
        qs_ref[:, c * LANES:(c + 1) * LANES] = (qc * SCALE).astype(bf16)
    kr = _rope128(h_ref[:, OFF_K:OFF_K + D_KV], cos, sin)
    vv = h_ref[:, OFF_VV:OFF_VV + D_KV]
    klast_ref[...] = kr[TM - WINDOW:TM, :]
    vlast_ref[...] = vv[TM - WINDOW:TM, :]
    lane_k = lax.broadcasted_iota(jnp.int32, (TM, LANES), 1)
    lo = lane_k < HEAD_DIM
    for src, (a_ref, b_ref, c_ref, d_ref) in ((kr, kv_refs[0:4]), (vv, kv_refs[4:8])):
        sw = pltpu.roll(src, HEAD_DIM, 1)
        a_ref[BLOCK:BLOCK + TM, :] = jnp.where(lo, src, 0.0).astype(bf16)
        b_ref[BLOCK:BLOCK + TM, :] = jnp.where(lo, 0.0, src).astype(bf16)
        c_ref[BLOCK:BLOCK + TM, :] = jnp.where(lo, sw, 0.0).astype(bf16)
        d_ref[BLOCK:BLOCK + TM, :] = jnp.where(lo, 0.0, sw).astype(bf16)

    qrow = lax.broadcasted_iota(jnp.int32, (2 * BLOCK, 2 * BLOCK), 0) % BLOCK
    kcol = lax.broadcasted_iota(jnp.int32, (2 * BLOCK, 2 * BLOCK), 1)
    band = (kcol >= qrow) & (kcol <= qrow + WINDOW)
    band_first = band & (kcol >= jnp.where(i > 0, 0, BLOCK))
    top = lax.broadcasted_iota(jnp.int32, (2 * BLOCK, 1), 0) < BLOCK
    n_gate = N_BRANCHES * D_MODEL // NCHUNK
    n_unit = NBLK * N_KV_HEADS
    gate_sched = [range(u * n_gate // n_unit, (u + 1) * n_gate // n_unit) for u in range(n_unit)]

    def scores(u):
        j, kv = divmod(u, N_KV_HEADS)
        rows = slice(j * BLOCK, (j + 1) * BLOCK)
        keys = slice(j * BLOCK, j * BLOCK + 2 * BLOCK)
        c0 = kv * Q_PER_KV * HEAD_DIM
        qst = jnp.concatenate([qs_ref[rows, c0:c0 + LANES], qs_ref[rows, c0 + LANES:c0 + 2 * LANES]], axis=0)
        k_even, k_odd = (ka_ref, kd_ref) if kv == 0 else (kc_ref, kb_ref)
        kcat = jnp.concatenate([k_even[keys, :], k_odd[keys, :]], axis=0)
        return _dot_nt(qst, kcat)

    def attend(u, sc):
        j, kv = divmod(u, N_KV_HEADS)
        rows = slice(j * BLOCK, (j + 1) * BLOCK)
        keys = slice(j * BLOCK, j * BLOCK + 2 * BLOCK)
        allowed = band_first if j == 0 else band
        c0 = kv * Q_PER_KV * HEAD_DIM
        h0 = kv * Q_PER_KV
        v_even, v_odd = (va_ref, vd_ref) if kv == 0 else (vc_ref, vb_ref)
        probs = []
        for par in range(2):
            sink = jnp.where(top, sinks_ref[layer, h0 + par], sinks_ref[layer, h0 + 2 + par])
            sm = jnp.where(allowed, sc[:, par * 2 * BLOCK:(par + 1) * 2 * BLOCK], NEG_INF)
            m = jnp.maximum(jnp.max(sm, axis=-1, keepdims=True), sink)
            p = jnp.exp(sm - m)
            den = jnp.sum(p, axis=-1, keepdims=True) + jnp.exp(sink - m)
            probs.append((p / den).astype(bf16))
        pcat = jnp.concatenate(probs, axis=1)
        vcat = jnp.concatenate([v_even[keys, :], v_odd[keys, :]], axis=0)
        o = _dot(pcat, vcat)
        for pr in range(2):
            cols = slice(c0 + pr * LANES, c0 + (pr + 1) * LANES)
            zc = h_ref[rows, OFF_ZC + c0 + pr * LANES:OFF_ZC + c0 + (pr + 1) * LANES]
            cin_ref[rows, cols] = (o[pr * BLOCK:(pr + 1) * BLOCK] * _silu(zc)).astype(bf16)

    sc_next = scores(0)
    for u in range(n_unit):
        sc = sc_next
        for gc in gate_sched[u]:
            gcols = slice(gc * NCHUNK, (gc + 1) * NCHUNK)
            gate_ref[:, gcols] = _dot(xb, w_in_ref[:, OFF_G + gc * NCHUNK:OFF_G + (gc + 1) * NCHUNK])
        if u + 1 < n_unit:
            sc_next = scores(u + 1)
        attend(u, sc)

    for r in kv_refs:
        r[0:BLOCK, :] = r[TM:TM + BLOCK, :]

    for c in range(D_MODEL // NCHUNK):
        cols = slice(c * NCHUNK, (c + 1) * NCHUNK)
        acc = None
        for br, (in_ref, wp_ref) in enumerate(((ain_ref, wpa_ref), (bin_ref, wpb_ref), (cin_ref, wpc_ref))):
            g0 = br * D_MODEL + c * NCHUNK
            term = _gate2(gate_ref[:, g0:g0 + NCHUNK], bg_ref[br:br + 1, cols]) * _dot(in_ref[...], wp_ref[:, cols])
            acc = term if acc is None else acc + term
        mg_ref[:, cols] = acc.astype(bf16)

    for r0 in range(0, TM, OUT_ROWS):
        rows = slice(r0, r0 + OUT_ROWS)
        out = _dot(mg_ref[rows, :], wout_ref[...])
        y_ref[rows, :] = _layer_norm(ALPHA * x_ref[rows, :] + out, row(lng_ref), row(lnb_ref))

    @pl.when(i == last)
    def _():
        pool_out_ref[...] = ext_ref[n - POOL_BUF:n, :]
        k_out_ref[...] = klast_ref[...].T
        v_out_ref[...] = vlast_ref[...].T


def _prompt_layer(layer, x, cos, sin, sinks, w_in, b_gate, poolw, pscale, slng, slnb, sguw, sgub,
                  wpa, wpb, wpc, wout, lng, lnb, slide_args=None):
    B, L, _ = x.shape
    nt = L // TM
    grid = (B, nt)
    row_spec = lambda w: pl.BlockSpec((TM, w), lambda b, i: (i, 0))
    lspec = lambda a: _layer_spec(a, layer, single_buffer=True)
    wspec = lambda a: _full_spec(a, single_buffer=True)
    in_specs = [
        pl.BlockSpec(memory_space=pltpu.SMEM),
        pl.BlockSpec((None, TM, D_MODEL), lambda b, i: (b, i, 0)),
        row_spec(LANES), row_spec(LANES),
        wspec(w_in), lspec(b_gate), lspec(poolw), _full_spec(pscale), _full_spec(slng), _full_spec(slnb),
        lspec(sguw), lspec(sgub), wspec(wpa), wspec(wpb), wspec(wpc), wspec(wout),
        _full_spec(lng), _full_spec(lnb),
    ]
    out_shape = (
        jax.ShapeDtypeStruct((B, L, D_MODEL), f32),
        jax.ShapeDtypeStruct((B, POOL_BUF, D_POOL), f32),
        jax.ShapeDtypeStruct((B, D_KV, WINDOW), f32),
        jax.ShapeDtypeStruct((B, D_KV, WINDOW), f32),
    )
    out_specs = (
        pl.BlockSpec((None, TM, D_MODEL), lambda b, i: (b, i, 0)),
        pl.BlockSpec((None, POOL_BUF, D_POOL), lambda b, i: (b, 0, 0)),
        pl.BlockSpec((None, D_KV, WINDOW), lambda b, i: (b, 0, 0)),
        pl.BlockSpec((None, D_KV, WINDOW), lambda b, i: (b, 0, 0)),
    )
    args = [sinks, x, cos, sin, w_in, b_gate, poolw, pscale, slng, slnb, sguw, sgub,
            wpa, wpb, wpc, wout, lng, lnb]
    if slide_args is not None:
        new_k, new_v, cache_k, cache_v = slide_args
        depth, nb = cache_k.shape[:2]
        per_step = nb // (B * nt)
        assert per_step * B * nt == nb and depth == DEPTH
        cache_spec = pl.BlockSpec((depth, per_step) + cache_k.shape[2:], lambda b, i: (0, b * nt + i, 0, 0, 0))
        in_specs += [_full_spec(new_k), _full_spec(new_v), cache_spec, cache_spec]
        args += [new_k, new_v, cache_k, cache_v]
        out_shape += (jax.ShapeDtypeStruct(cache_k.shape, f32), jax.ShapeDtypeStruct(cache_v.shape, f32))
        out_specs += (cache_spec, cache_spec)
    kv_scratch = [pltpu.VMEM((BLOCK + TM, LANES), bf16) for _ in range(8)]
    scratch = [
        pltpu.VMEM((TM, OFF_G), f32),
        pltpu.VMEM((HIST + TM, D_POOL), f32),
        pltpu.VMEM((HIST + TM, D_POOL), f32),
        pltpu.VMEM((HIST + TM, D_POOL), f32),
        pltpu.VMEM((HIST + TM, D_POOL), f32),
        pltpu.VMEM((TM, D_ATTN), bf16),
        *kv_scratch,
        pltpu.VMEM((TM, D_POOL), bf16),
        pltpu.VMEM((TM, D_SGU), bf16),
        pltpu.VMEM((TM, D_ATTN), bf16),
        pltpu.VMEM((TM, D_MODEL), bf16),
        pltpu.VMEM((TM, N_BRANCHES * D_MODEL), f32),
        pltpu.VMEM((WINDOW, D_KV), f32),
        pltpu.VMEM((WINDOW, D_KV), f32),
    ]
    return pl.pallas_call(
        functools.partial(_prompt_kernel, layer, slide_args is not None),
        out_shape=out_shape,
        grid=grid,
        in_specs=in_specs,
        out_specs=out_specs,
        scratch_shapes=scratch,
        compiler_params=pltpu.CompilerParams(
            dimension_semantics=("arbitrary", "arbitrary"),
            vmem_limit_bytes=VMEM_LIMIT),
        name="prompt_layer",
    )(*args)


def _sample_proj_kernel(layer, x_ref, cos_ref, sin_ref, w_in_ref, bg_ref, pb_ref, poolw_ref, pscale_ref,
                        slng_ref, slnb_ref, sw0_ref, sb0_ref,
                        wb_ref, q_ref, k_ref, v_ref, kt_ref, vt_ref, pool_ref, vn_ref, ain_ref, bin_ref,
                        szc_ref, g_ref, h_ref):
    j = pl.program_id(0)
    row = lambda ref: ref[layer:layer + 1, :]
    xb = x_ref[...].astype(bf16)
    gate_chunk = j * CW >= OFF_G
    wb = (w_in_ref[...] * jnp.where(gate_chunk, 0.5, 1.0)).astype(bf16)
    wb_ref[...] = wb
    h_ref[j] = _dot(xb, wb)

    def hcols(off, width):
        parts = []
        while width > 0:
            c, o = divmod(off, CW)
            w = min(width, CW - o)
            parts.append(h_ref[c][:, o:o + w])
            off, width = off + w, width - w
        return parts[0] if len(parts) == 1 else jnp.concatenate(parts, axis=1)

    @pl.when(j == NCW - 1)
    def _():
        xa = hcols(OFF_XA, D_POOL)
        lane = lax.broadcasted_iota(jnp.int32, xa.shape, 1)
        first_row = _group_select(lane, *(POOL_BUF - (w - 1) for w in POOL_WINDOWS))
        win = xa
        for r in range(POOL_BUF):
            win = win + jnp.where(first_row <= r, pb_ref[r], 0.0)
        width = _group_select(lane, *POOL_WINDOWS).astype(f32)
        pooled = win / width - xa
        ya = _dot(pooled.astype(bf16), poolw_ref[...]) * row(pscale_ref)
        ain_ref[...] = (ya * _silu(hcols(OFF_ZA, D_POOL))).astype(bf16)
        for r in range(POOL_BUF - 1):
            pool_ref[r] = pb_ref[r + 1]
        pool_ref[POOL_BUF - 1] = xa
        vn = _layer_norm(hcols(OFF_V, D_SGU), row(slng_ref), row(slnb_ref))
        vn_ref[...] = vn
        yb = hcols(OFF_U, D_SGU) * (row(sw0_ref) * vn + row(sb0_ref))
        bin_ref[...] = (yb * _silu(hcols(OFF_ZB, D_SGU))).astype(bf16)
        cos = cos_ref[...]
        sin = sin_ref[...]
        for c in range(D_ATTN // LANES):
            q_ref[:, c * LANES:(c + 1) * LANES] = _rope128(hcols(OFF_Q + c * LANES, LANES), cos, sin) * SCALE
        kr = _rope128(hcols(OFF_K, D_KV), cos, sin)
        vv = hcols(OFF_VV, D_KV)
        k_ref[...] = kr
        v_ref[...] = vv
        kt_ref[...] = kr.T
        vt_ref[...] = vv.T
        szc_ref[...] = _silu(hcols(OFF_ZC, D_ATTN))
        for br in range(N_BRANCHES):
            cols = slice(br * D_MODEL, (br + 1) * D_MODEL)
            g_ref[:, cols] = _gate2(hcols(OFF_G + br * D_MODEL, D_MODEL), bg_ref[br:br + 1, :])


def _sample_proj(layer, x, cos, sin, w_in, b_gate, pool_t, poolw, pscale, slng, slnb, sw0, sb0):
    nb = x.shape[0]
    out_shape = (
        jax.ShapeDtypeStruct((D_MODEL, D_IN), bf16),
        jax.ShapeDtypeStruct((nb, D_ATTN), f32),
        jax.ShapeDtypeStruct((nb, D_KV), f32),
        jax.ShapeDtypeStruct((nb, D_KV), f32),
        jax.ShapeDtypeStruct((D_KV, nb), f32),
        jax.ShapeDtypeStruct((D_KV, nb), f32),
        jax.ShapeDtypeStruct((POOL_BUF, nb, D_POOL), f32),
        jax.ShapeDtypeStruct((nb, D_SGU), f32),
        jax.ShapeDtypeStruct((nb, D_POOL), bf16),
        jax.ShapeDtypeStruct((nb, D_SGU), bf16),
        jax.ShapeDtypeStruct((nb, D_ATTN), f32),
        jax.ShapeDtypeStruct((nb, N_BRANCHES * D_MODEL), f32),
    )
    lspec = lambda a: _layer_spec(a, layer)
    args = (x, cos, sin, w_in, b_gate, pool_t, poolw, pscale, slng, slnb, sw0, sb0)
    in_specs = [_full_spec(x), _full_spec(cos), _full_spec(sin),
                pl.BlockSpec((None, D_MODEL, CW), lambda j: (layer, 0, j)),
                lspec(b_gate), lspec(pool_t), lspec(poolw), _full_spec(pscale), _full_spec(slng),
                _full_spec(slnb), _full_spec(sw0), _full_spec(sb0)]
    out_specs = (pl.BlockSpec((D_MODEL, CW), lambda j: (0, j)),) + tuple(_full_spec(s) for s in out_shape[1:])
    return pl.pallas_call(
        functools.partial(_sample_proj_kernel, layer),
        out_shape=out_shape,
        grid=(NCW,),
        in_specs=in_specs,
        out_specs=out_specs,
        scratch_shapes=[pltpu.VMEM((NCW, nb, CW), f32)],
        compiler_params=pltpu.CompilerParams(dimension_semantics=("arbitrary",),
                                             vmem_limit_bytes=VMEM_LIMIT),
        name="sample_proj",
    )(*args)


def _sample_attn_kernel(sink_ref, q_ref, kc_ref, vc_ref, kn_ref, vn_ref, o_ref):
    for kv in range(N_KV_HEADS):
        qb = q_ref[:, kv].astype(bf16)
        kn = kn_ref[:, kv]
        vn = vn_ref[:, kv]
        sink = sink_ref[kv][None]
        s = jnp.einsum('bgd,bdw->bgw', qb, kc_ref[:, kv].astype(bf16), preferred_element_type=f32)
        s_new = jnp.sum(qb.astype(f32) * kn.astype(bf16).astype(f32), axis=-1, keepdims=True)
        m = jnp.maximum(jnp.maximum(jnp.max(s, axis=-1, keepdims=True), s_new), sink)
        p = jnp.exp(s - m)
        p_new = jnp.exp(s_new - m)
        den = jnp.sum(p, axis=-1, keepdims=True) + p_new + jnp.exp(sink - m)
        o = jnp.einsum('bgw,bdw->bgd', (p / den).astype(bf16), vc_ref[:, kv].astype(bf16),
                       preferred_element_type=f32)
        o_ref[:, kv] = o + (p_new / den) * vn


def _sample_attn(layer, sink, q4, kc, vc, kn, vn):
    nb = q4.shape[0]
    cache_spec = pl.BlockSpec((None, SB, N_KV_HEADS, HEAD_DIM, WINDOW), lambda b: (layer, b, 0, 0, 0))
    blk = lambda s2, s3: pl.BlockSpec((SB, N_KV_HEADS, s2, s3), lambda b: (b, 0, 0, 0))
    return pl.pallas_call(
        _sample_attn_kernel,
        out_shape=jax.ShapeDtypeStruct((nb, N_KV_HEADS, Q_PER_KV, HEAD_DIM), f32),
        grid=(nb // SB,),
        in_specs=[_layer_spec(sink, layer), blk(Q_PER_KV, HEAD_DIM), cache_spec, cache_spec,
                  blk(1, HEAD_DIM), blk(1, HEAD_DIM)],
        out_specs=blk(Q_PER_KV, HEAD_DIM),
        compiler_params=pltpu.CompilerParams(dimension_semantics=("arbitrary",),
                                             vmem_limit_bytes=VMEM_LIMIT),
        name="sample_attn",
    )(sink, q4, kc, vc, kn, vn)


def _sample_merge_kernel(layer, x_ref, ain_ref, bin_ref, yc_ref, szc_ref, g_ref, wpa_ref, wpb_ref, wpc_ref,
                         wout_ref, lng_ref, lnb_ref, y_ref, wpa_b_ref, wpb_b_ref, wpc_b_ref, wout_b_ref):
    row = lambda ref: ref[layer:layer + 1, :]
    wpa = (0.5 * wpa_ref[...]).astype(bf16)
    wpb = (0.5 * wpb_ref[...]).astype(bf16)
    wpc = (0.5 * wpc_ref[...]).astype(bf16)
    wout = wout_ref[...].astype(bf16)
    wpa_b_ref[...] = wpa
    wpb_b_ref[...] = wpb
    wpc_b_ref[...] = wpc
    wout_b_ref[...] = wout
    cin = (yc_ref[...] * szc_ref[...]).astype(bf16)
    merged = (g_ref[:, 0:D_MODEL] * _dot(ain_ref[...], wpa)
              + g_ref[:, D_MODEL:2 * D_MODEL] * _dot(bin_ref[...], wpb)
              + g_ref[:, 2 * D_MODEL:3 * D_MODEL] * _dot(cin, wpc))
    out = _dot(merged.astype(bf16), wout)
    y_ref[...] = _layer_norm(ALPHA * x_ref[...] + out, row(lng_ref), row(lnb_ref))


def _sample_merge(layer, x, ain, bin_, yc, szc, g, wpa, wpb, wpc, wout, lng, lnb):
    lspec = lambda a: _layer_spec(a, layer)
    out_shape = (jax.ShapeDtypeStruct(x.shape, f32),) + tuple(
        jax.ShapeDtypeStruct(w.shape[1:], bf16) for w in (wpa, wpb, wpc, wout))
    return pl.pallas_call(
        functools.partial(_sample_merge_kernel, layer),
        out_shape=out_shape,
        grid=(1,),
        in_specs=[_full_spec(x), _full_spec(ain), _full_spec(bin_), _full_spec(yc), _full_spec(szc),
                  _full_spec(g), lspec(wpa), lspec(wpb), lspec(wpc), lspec(wout), _full_spec(lng),
                  _full_spec(lnb)],
        out_specs=tuple(_full_spec(s) for s in out_shape),
        compiler_params=pltpu.CompilerParams(dimension_semantics=("arbitrary",),
                                             vmem_limit_bytes=VMEM_LIMIT),
        name="sample_merge",
    )(x, ain, bin_, yc, szc, g, wpa, wpb, wpc, wout, lng, lnb)


def _rope_tables(positions):
    halfd = HEAD_DIM // 2
    inv = ROPE_THETA ** (-np.arange(halfd, dtype=np.float64) / halfd)
    ang = np.asarray(positions, dtype=np.float64)[:, None] * inv[None, :]
    cos = np.tile(np.cos(ang), (1, LANES // halfd))
    sin = np.tile(np.concatenate([-np.sin(ang), np.sin(ang)], axis=1), (1, LANES // HEAD_DIM))
    return jnp.asarray(cos, f32), jnp.asarray(sin, f32)


def _block_diag(w):
    nl, g, c, _ = w.shape
    eye = jnp.eye(g, dtype=w.dtype)
    return (eye[None, :, None, :, None] * w[:, :, :, None, :]).reshape(nl, g * c, g * c)


def kernel(x_prompt, x_sample, state_pool, cache_k_win, cache_v_win, w_in, b_gate, pool_w, pool_scale, sgu_ln_g, sgu_ln_b, sgu_w, sgu_b, attn_sinks, w_proj_a, w_proj_b, w_proj_c, w_out, ln_g, ln_b):
    B, L, _ = x_prompt.shape
    nb = x_sample.shape[0]
    cos_p, sin_p = _rope_tables(np.arange(L))
    cos_s, sin_s = _rope_tables(np.array([PAST_LEN]))

    poolw = _block_diag(pool_w).astype(bf16)
    sguw = sgu_w.reshape(DEPTH, N_SGU_GROUPS * CHUNK, CHUNK)
    sgub = jnp.repeat(jnp.swapaxes(sgu_b, 1, 2), POOL_GC, axis=2)
    sw0 = jnp.repeat(sgu_w[:, :, 0, 0], POOL_GC, axis=1)
    sb0 = jnp.repeat(sgu_b[:, :, 0], POOL_GC, axis=1)
    sink4 = attn_sinks.reshape(DEPTH, N_KV_HEADS, Q_PER_KV, 1)
    kc_t = jnp.transpose(cache_k_win, (0, 1, 3, 4, 2))
    vc_t = jnp.transpose(cache_v_win, (0, 1, 3, 4, 2))
    pool_t = jnp.transpose(state_pool, (0, 2, 1, 3))

    y_s = x_sample.reshape(nb, D_MODEL)
    pool_s, chunk_v, kts, vts, weights = ([] for _ in range(5))
    for l in range(DEPTH):
        w_in_b, q, kn, vn, kt, vt, ps, cv, ain, bin_, szc, g = _sample_proj(
            l, y_s, cos_s, sin_s, w_in, b_gate, pool_t, poolw, pool_scale, sgu_ln_g, sgu_ln_b, sw0, sb0)
        o = _sample_attn(
            l, sink4, q.reshape(nb, N_KV_HEADS, Q_PER_KV, HEAD_DIM), kc_t, vc_t,
            kn.reshape(nb, N_KV_HEADS, 1, HEAD_DIM), vn.reshape(nb, N_KV_HEADS, 1, HEAD_DIM))
        y_s, wpa, wpb, wpc, wout = _sample_merge(
            l, y_s, ain, bin_, o.reshape(nb, D_ATTN), szc, g, w_proj_a, w_proj_b, w_proj_c, w_out, ln_g, ln_b)
        pool_s.append(ps); chunk_v.append(cv); kts.append(kt); vts.append(vt)
        weights.append((w_in_b, wpa, wpb, wpc, wout))

    y_p = x_prompt
    pool_p, k_p, v_p = ([] for _ in range(3))
    for l in range(DEPTH):
        w_in_b, wpa, wpb, wpc, wout = weights[l]
        slide_args = (jnp.stack(kts), jnp.stack(vts), kc_t, vc_t) if l == DEPTH - 1 else None
        y_p, pp, kp, vp, *slid = _prompt_layer(
            l, y_p, cos_p, sin_p, attn_sinks, w_in_b, b_gate, poolw, pool_scale, sgu_ln_g, sgu_ln_b, sguw,
            sgub, wpa, wpb, wpc, wout, ln_g, ln_b, slide_args=slide_args)
        pool_p.append(pp); k_p.append(kp); v_p.append(vp)
    k_s, v_s = slid

    to_cache = lambda a: jnp.transpose(a, (0, 1, 4, 2, 3))
    prompt_cache = lambda lst: to_cache(jnp.stack(lst).reshape(DEPTH, B, N_KV_HEADS, HEAD_DIM, WINDOW))
    return (y_p, y_s.reshape(nb, 1, D_MODEL),
            jnp.stack(pool_p), prompt_cache(k_p), prompt_cache(v_p),
            jnp.transpose(jnp.stack(pool_s), (0, 2, 1, 3)), to_cache(k_s), to_cache(v_s),
            jnp.stack(chunk_v).reshape(DEPTH, nb, 1, D_SGU))
```

```python
import functools

import numpy as np
import jax
import jax.numpy as jnp
from jax import lax
from jax.experimental import pallas as pl
from jax.experimental.pallas import tpu as pltpu

D_MODEL = 1024
DEPTH = 2
PAST_LEN = 8192
D_POOL = 256
POOL_WINDOWS = (2, 4, 8, 16)
POOL_GC = 64
POOL_BUF = 15
D_SGU = 256
CHUNK = 128
N_SGU_GROUPS = 4
HEAD_DIM = 64
N_HEADS = 8
N_KV_HEADS = 2
Q_PER_KV = 4
D_ATTN = 512
D_KV = 128
WINDOW = 128
BLOCK = 128
ROPE_THETA = 10000.0
N_BRANCHES = 3
D_IN = 2 * D_POOL + 3 * D_SGU + 2 * D_ATTN + 2 * D_KV + N_BRANCHES * D_MODEL
ALPHA = (2.0 * DEPTH) ** 0.25
LN_EPS = 1e-5
NEG_INF = -1e30
SCALE = HEAD_DIM ** -0.5

OFF_XA, OFF_ZA, OFF_U, OFF_V, OFF_ZB = 0, 256, 512, 768, 1024
OFF_Q, OFF_K, OFF_VV, OFF_ZC, OFF_G = 1280, 1792, 1920, 2048, 2560

LANES = 128
TM = 512
NBLK = TM // BLOCK
HIST = 32
NCHUNK = 256
OUT_ROWS = 256
VMEM_LIMIT = 56 * 1024 * 1024
SB = 16
RK = 128
RK_STEPS = D_MODEL // RK

bf16 = jnp.bfloat16
f32 = jnp.float32


def _dot(a, b):
    return jnp.dot(a, b, preferred_element_type=f32)


def _dot_nt(a, b):
    return lax.dot_general(a, b, (((1,), (1,)), ((), ())), preferred_element_type=f32)


def _sigmoid(z):
    return 0.5 * jnp.tanh(0.5 * z) + 0.5


def _silu(z):
    return z * _sigmoid(z)


def _gate2(half_pre, bias):
    return jnp.tanh(half_pre + 0.5 * bias) + 1.0


def _layer_norm(x, g, b):
    mu = jnp.mean(x, axis=-1, keepdims=True)
    xc = x - mu
    var = jnp.mean(xc * xc, axis=-1, keepdims=True)
    return xc * lax.rsqrt(var + LN_EPS) * g + b


def _rope128(x, cos, sin_signed):
    lane = lax.broadcasted_iota(jnp.int32, x.shape, 1)
    first_half = (lane % HEAD_DIM) < (HEAD_DIM // 2)
    partner = jnp.where(first_half,
                        pltpu.roll(x, LANES - HEAD_DIM // 2, 1),
                        pltpu.roll(x, HEAD_DIM // 2, 1))
    return x * cos + partner * sin_signed


def _group_select(lane, a0, a1, a2, a3):
    return jnp.where(lane < 64, a0, jnp.where(lane < 128, a1, jnp.where(lane < 192, a2, a3)))


def _layer_spec(arr, layer, single_buffer=False):
    block = (None,) + arr.shape[1:]
    zeros = (0,) * (arr.ndim - 1)
    index_map = lambda *_: (layer,) + zeros
    if single_buffer:
        return pl.BlockSpec(block, index_map, pipeline_mode=pl.Buffered(1))
    return pl.BlockSpec(block, index_map)


def _full_spec(arr, single_buffer=False):
    zeros = (0,) * len(arr.shape)
    if single_buffer:
        return pl.BlockSpec(arr.shape, lambda *_: zeros, pipeline_mode=pl.Buffered(1))
    return pl.BlockSpec(arr.shape, lambda *_: zeros)


def _slide_windows(step, per_step, new_refs, old_refs, out_refs):
    lane = lax.broadcasted_iota(jnp.int32, (D_KV, WINDOW), 1)
    shift = jnp.where(step == 0, 0, LANES - step * per_step)
    for new_ref, old_ref, out_ref in zip(new_refs, old_refs, out_refs):
        for l in range(DEPTH):
            new_cols = pltpu.roll(new_ref[l], shift, 1)
            for s in range(per_step):
                slid = pltpu.roll(old_ref[l, s].reshape(D_KV, WINDOW), WINDOW - 1, 1)
                newest = jnp.broadcast_to(new_cols[:, s:s + 1], (D_KV, WINDOW))
                out_ref[l, s] = jnp.where(lane == WINDOW - 1, newest, slid).reshape(
                    N_KV_HEADS, HEAD_DIM, WINDOW)


def _prompt_kernel(layer, slide, *refs):
    n_in = 18 + (4 if slide else 0)
    n_out = 4 + (2 if slide else 0)
    (sinks_ref, x_ref, cos_ref, sin_ref, w_in_ref, bg_ref, poolw_ref, pscale_ref,
     slng_ref, slnb_ref, sguw_ref, sgub_ref, wpa_ref, wpb_ref, wpc_ref, wout_ref,
     lng_ref, lnb_ref) = refs[:18]
    y_ref, pool_out_ref, k_out_ref, v_out_ref = refs[n_in:n_in + 4]
    (h_ref, ext_ref, s2_ref, s4_ref, s8_ref, qs_ref,
     ka_ref, kb_ref, kc_ref, kd_ref, va_ref, vb_ref, vc_ref, vd_ref,
     ain_ref, bin_ref, cin_ref, mg_ref, gate_ref, klast_ref, vlast_ref) = refs[n_in + n_out:]
    i = pl.program_id(1)
    last = pl.num_programs(1) - 1
    kv_refs = (ka_ref, kb_ref, kc_ref, kd_ref, va_ref, vb_ref, vc_ref, vd_ref)
    row = lambda ref: ref[layer:layer + 1, :]

    @pl.when(i == 0)
    def _():
        ext_ref[0:HIST, :] = jnp.zeros((HIST, D_POOL), f32)
        for r in kv_refs:
            r[0:BLOCK, :] = jnp.zeros((BLOCK, LANES), bf16)

    xb = x_ref[...].astype(bf16)
    half = OFF_G // 2
    h_ref[:, 0:half] = _dot(xb, w_in_ref[:, 0:half])
    h_ref[:, half:OFF_G] = _dot(xb, w_in_ref[:, half:OFF_G])

    if slide:
        step = pl.program_id(0) * pl.num_programs(1) + i
        _slide_windows(step, refs[20].shape[1], refs[18:20], refs[20:22], refs[n_in + 4:n_in + 6])

    xa = h_ref[:, OFF_XA:OFF_XA + D_POOL]
    ext_ref[HIST:HIST + TM, :] = xa
    n = HIST + TM
    s2_ref[8:n, :] = ext_ref[8:n, :] + ext_ref[7:n - 1, :]
    s4_ref[16:n, :] = s2_ref[16:n, :] + s2_ref[14:n - 2, :]
    s8_ref[24:n, :] = s4_ref[24:n, :] + s4_ref[20:n - 4, :]
    w16 = s8_ref[HIST:n, :] + s8_ref[HIST - 8:n - 8, :]
    lane_p = lax.broadcasted_iota(jnp.int32, (TM, D_POOL), 1)
    row_p = lax.broadcasted_iota(jnp.int32, (TM, D_POOL), 0)
    win = _group_select(lane_p, s2_ref[HIST:n, :], s4_ref[HIST:n, :], s8_ref[HIST:n, :], w16)
    width = _group_select(lane_p, POOL_WINDOWS[0], POOL_WINDOWS[1], POOL_WINDOWS[2], POOL_WINDOWS[3])
    cnt = jnp.minimum(row_p + (i * TM + 1), width).astype(f32)
    pooled = win / cnt - xa
    ya = _dot(pooled.astype(bf16), poolw_ref[...]) * row(pscale_ref)
    za = h_ref[:, OFF_ZA:OFF_ZA + D_POOL]
    ain_ref[...] = (ya * _silu(za)).astype(bf16)
    ext_ref[HIST - 16:HIST, :] = ext_ref[n - 16:n, :]

    vn = _layer_norm(h_ref[:, OFF_V:OFF_V + D_SGU], row(slng_ref), row(slnb_ref)).astype(bf16)
    wr = lax.broadcasted_iota(jnp.int32, (N_SGU_GROUPS * CHUNK, CHUNK), 0) % CHUNK
    wc = lax.broadcasted_iota(jnp.int32, (N_SGU_GROUPS * CHUNK, CHUNK), 1)
    w_s = jnp.where(wc <= wr, sguw_ref[...], 0.0).astype(bf16)
    lane_c = lax.broadcasted_iota(jnp.int32, (CHUNK, D_SGU), 1)
    for j in range(NBLK):
        rows = slice(j * CHUNK, (j + 1) * CHUNK)
        r = _dot(w_s, vn[rows, :])
        s = _group_select(lane_c, r[0:CHUNK], r[CHUNK:2 * CHUNK], r[2 * CHUNK:3 * CHUNK],
                          r[3 * CHUNK:4 * CHUNK]) + sgub_ref[...]
        yb = h_ref[rows, OFF_U:OFF_U + D_SGU] * s
        bin_ref[rows, :] = (yb * _silu(h_ref[rows, OFF_ZB:OFF_ZB + D_SGU])).astype(bf16)

    cos = cos_ref[...]
    sin = sin_ref[...]
    for c in range(D_ATTN // LANES):
        qc = _rope128(h_ref[:, OFF_Q + c * LANES:OFF_Q + (c + 1) * LANES], cos, sin)
        qs_ref[:, c * LANES:(c + 1) * LANES] = (qc * SCALE).astype(bf16)
    kr = _rope128(h_ref[:, OFF_K:OFF_K + D_KV], cos, sin)
    vv = h_ref[:, OFF_VV:OFF_VV + D_KV]
    klast_ref[...] = kr[TM - WINDOW:TM, :]
    vlast_ref[...] = vv[TM - WINDOW:TM, :]
    lane_k = lax.broadcasted_iota(jnp.int32, (TM, LANES), 1)
    lo = lane_k < HEAD_DIM
    for src, (a_ref, b_ref, c_ref, d_ref) in ((kr, kv_refs[0:4]), (vv, kv_refs[4:8])):
        sw = pltpu.roll(src, HEAD_DIM, 1)
        a_ref[BLOCK:BLOCK + TM, :] = jnp.where(lo, src, 0.0).astype(bf16)
        b_ref[BLOCK:BLOCK + TM, :] = jnp.where(lo, 0.0, src).astype(bf16)
        c_ref[BLOCK:BLOCK + TM, :] = jnp.where(lo, sw, 0.0).astype(bf16)
        d_ref[BLOCK:BLOCK + TM, :] = jnp.where(lo, 0.0, sw).astype(bf16)

    qrow = lax.broadcasted_iota(jnp.int32, (2 * BLOCK, 2 * BLOCK), 0) % BLOCK
    kcol = lax.broadcasted_iota(jnp.int32, (2 * BLOCK, 2 * BLOCK), 1)
    band = (kcol >= qrow) & (kcol <= qrow + WINDOW)
    band_first = band & (kcol >= jnp.where(i > 0, 0, BLOCK))
    top = lax.broadcasted_iota(jnp.int32, (2 * BLOCK, 1), 0) < BLOCK
    n_gate = N_BRANCHES * D_MODEL // NCHUNK
    n_unit = NBLK * N_KV_HEADS
    gate_sched = [range(u * n_gate // n_unit, (u + 1) * n_gate // n_unit) for u in range(n_unit)]

    def scores(u):
        j, kv = divmod(u, N_KV_HEADS)
        rows = slice(j * BLOCK, (j + 1) * BLOCK)
        keys = slice(j * BLOCK, j * BLOCK + 2 * BLOCK)
        c0 = kv * Q_PER_KV * HEAD_DIM
        qst = jnp.concatenate([qs_ref[rows, c0:c0 + LANES], qs_ref[rows, c0 + LANES:c0 + 2 * LANES]], axis=0)
        k_even, k_odd = (ka_ref, kd_ref) if kv == 0 else (kc_ref, kb_ref)
        kcat = jnp.concatenate([k_even[keys, :], k_odd[keys, :]], axis=0)
        return _dot_nt(qst, kcat)

    def attend(u, sc):
        j, kv = divmod(u, N_KV_HEADS)
        rows = slice(j * BLOCK, (j + 1) * BLOCK)
        keys = slice(j * BLOCK, j * BLOCK + 2 * BLOCK)
        allowed = band_first if j == 0 else band
        c0 = kv * Q_PER_KV * HEAD_DIM
        h0 = kv * Q_PER_KV
        v_even, v_odd = (va_ref, vd_ref) if kv == 0 else (vc_ref, vb_ref)
        probs = []
        for par in range(2):
            sink = jnp.where(top, sinks_ref[layer, h0 + par], sinks_ref[layer, h0 + 2 + par])
            sm = jnp.where(allowed, sc[:, par * 2 * BLOCK:(par + 1) * 2 * BLOCK], NEG_INF)
            m = jnp.maximum(jnp.max(sm, axis=-1, keepdims=True), sink)
            p = jnp.exp(sm - m)
            den = jnp.sum(p, axis=-1, keepdims=True) + jnp.exp(sink - m)
            probs.append((p / den).astype(bf16))
        pcat = jnp.concatenate(probs, axis=1)
        vcat = jnp.concatenate([v_even[keys, :], v_odd[keys, :]], axis=0)
        o = _dot(pcat, vcat)
        for pr in range(2):
            cols = slice(c0 + pr * LANES, c0 + (pr + 1) * LANES)
            zc = h_ref[rows, OFF_ZC + c0 + pr * LANES:OFF_ZC + c0 + (pr + 1) * LANES]
            cin_ref[rows, cols] = (o[pr * BLOCK:(pr + 1) * BLOCK] * _silu(zc)).astype(bf16)

    sc_next = scores(0)
    for u in range(n_unit):
        sc = sc_next
        for gc in gate_sched[u]:
            gcols = slice(gc * NCHUNK, (gc + 1) * NCHUNK)
            gate_ref[:, gcols] = _dot(xb, w_in_ref[:, OFF_G + gc * NCHUNK:OFF_G + (gc + 1) * NCHUNK])
        if u + 1 < n_unit:
            sc_next = scores(u + 1)
        attend(u, sc)

    for r in kv_refs:
        r[0:BLOCK, :] = r[TM:TM + BLOCK, :]

    for c in range(D_MODEL // NCHUNK):
        cols = slice(c * NCHUNK, (c + 1) * NCHUNK)
        acc = None
        for br, (in_ref, wp_ref) in enumerate(((ain_ref, wpa_ref), (bin_ref, wpb_ref), (cin_ref, wpc_ref))):
            g0 = br * D_MODEL + c * NCHUNK
            term = _gate2(gate_ref[:, g0:g0 + NCHUNK], bg_ref[br:br + 1, cols]) * _dot(in_ref[...], wp_ref[:, cols])
            acc = term if acc is None else acc + term
        mg_ref[:, cols] = acc.astype(bf16)

    for r0 in range(0, TM, OUT_ROWS):
        rows = slice(r0, r0 + OUT_ROWS)
        out = _dot(mg_ref[rows, :], wout_ref[...])
        y_ref[rows, :] = _layer_norm(ALPHA * x_ref[rows, :] + out, row(lng_ref), row(lnb_ref))

    @pl.when(i == last)
    def _():
        pool_out_ref[...] = ext_ref[n - POOL_BUF:n, :]
        k_out_ref[...] = klast_ref[...].T
        v_out_ref[...] = vlast_ref[...].T


def _prompt_layer(layer, x, cos, sin, sinks, w_in, b_gate, poolw, pscale, slng, slnb, sguw, sgub,
                  wpa, wpb, wpc, wout, lng, lnb, slide_args=None):
    B, L, _ = x.shape
    nt = L // TM
    grid = (B, nt)
    row_spec = lambda w: pl.BlockSpec((TM, w), lambda b, i: (i, 0))
    lspec = lambda a: _layer_spec(a, layer, single_buffer=True)
    wspec = lambda a: _full_spec(a, single_buffer=True)
    in_specs = [
        pl.BlockSpec(memory_space=pltpu.SMEM),
        pl.BlockSpec((None, TM, D_MODEL), lambda b, i: (b, i, 0)),
        row_spec(LANES), row_spec(LANES),
        wspec(w_in), lspec(b_gate), lspec(poolw), _full_spec(pscale), _full_spec(slng), _full_spec(slnb),
        lspec(sguw), lspec(sgub), wspec(wpa), wspec(wpb), wspec(wpc), wspec(wout),
        _full_spec(lng), _full_spec(lnb),
    ]
    out_shape = (
        jax.ShapeDtypeStruct((B, L, D_MODEL), f32),
        jax.ShapeDtypeStruct((B, POOL_BUF, D_POOL), f32),
        jax.ShapeDtypeStruct((B, D_KV, WINDOW), f32),
        jax.ShapeDtypeStruct((B, D_KV, WINDOW), f32),
    )
    out_specs = (
        pl.BlockSpec((None, TM, D_MODEL), lambda b, i: (b, i, 0)),
        pl.BlockSpec((None, POOL_BUF, D_POOL), lambda b, i: (b, 0, 0)),
        pl.BlockSpec((None, D_KV, WINDOW), lambda b, i: (b, 0, 0)),
        pl.BlockSpec((None, D_KV, WINDOW), lambda b, i: (b, 0, 0)),
    )
    args = [sinks, x, cos, sin, w_in, b_gate, poolw, pscale, slng, slnb, sguw, sgub,
            wpa, wpb, wpc, wout, lng, lnb]
    if slide_args is not None:
        new_k, new_v, cache_k, cache_v = slide_args
        depth, nb = cache_k.shape[:2]
        per_step = nb // (B * nt)
        assert per_step * B * nt == nb and depth == DEPTH
        cache_spec = pl.BlockSpec((depth, per_step) + cache_k.shape[2:], lambda b, i: (0, b * nt + i, 0, 0, 0))
        in_specs += [_full_spec(new_k), _full_spec(new_v), cache_spec, cache_spec]
        args += [new_k, new_v, cache_k, cache_v]
        out_shape += (jax.ShapeDtypeStruct(cache_k.shape, f32), jax.ShapeDtypeStruct(cache_v.shape, f32))
        out_specs += (cache_spec, cache_spec)
    kv_scratch = [pltpu.VMEM((BLOCK + TM, LANES), bf16) for _ in range(8)]
    scratch = [
        pltpu.VMEM((TM, OFF_G), f32),
        pltpu.VMEM((HIST + TM, D_POOL), f32),
        pltpu.VMEM((HIST + TM, D_POOL), f32),
        pltpu.VMEM((HIST + TM, D_POOL), f32),
        pltpu.VMEM((HIST + TM, D_POOL), f32),
        pltpu.VMEM((TM, D_ATTN), bf16),
        *kv_scratch,
        pltpu.VMEM((TM, D_POOL), bf16),
        pltpu.VMEM((TM, D_SGU), bf16),
        pltpu.VMEM((TM, D_ATTN), bf16),
        pltpu.VMEM((TM, D_MODEL), bf16),
        pltpu.VMEM((TM, N_BRANCHES * D_MODEL), f32),
        pltpu.VMEM((WINDOW, D_KV), f32),
        pltpu.VMEM((WINDOW, D_KV), f32),
    ]
    return pl.pallas_call(
        functools.partial(_prompt_kernel, layer, slide_args is not None),
        out_shape=out_shape,
        grid=grid,
        in_specs=in_specs,
        out_specs=out_specs,
        scratch_shapes=scratch,
        compiler_params=pltpu.CompilerParams(
            dimension_semantics=("arbitrary", "arbitrary"),
            vmem_limit_bytes=VMEM_LIMIT),
        name="prompt_layer",
    )(*args)


def _sample_proj_kernel(layer, x_ref, cos_ref, sin_ref, w_in_ref, bg_ref, pb_ref, poolw_ref, pscale_ref,
                        slng_ref, slnb_ref, sw0_ref, sb0_ref,
                        wb_ref, q_ref, k_ref, v_ref, kt_ref, vt_ref, pool_ref, vn_ref, ain_ref, bin_ref,
                        szc_ref, g_ref, h_ref):
    j = pl.program_id(0)
    row = lambda ref: ref[layer:layer + 1, :]
    wb_ref[:, 0:OFF_G] = w_in_ref[:, 0:OFF_G].astype(bf16)
    wb_ref[:, OFF_G:D_IN] = (0.5 * w_in_ref[:, OFF_G:D_IN]).astype(bf16)

    @pl.when(j == 0)
    def _():
        h_ref[...] = jnp.zeros(h_ref.shape, f32)

    h_ref[...] += _dot(x_ref[...].astype(bf16), wb_ref[...])

    def hcols(off, width):
        return h_ref[:, off:off + width]

    @pl.when(j == RK_STEPS - 1)
    def _():
        xa = hcols(OFF_XA, D_POOL)
        lane = lax.broadcasted_iota(jnp.int32, xa.shape, 1)
        first_row = _group_select(lane, *(POOL_BUF - (w - 1) for w in POOL_WINDOWS))
        win = xa
        for r in range(POOL_BUF):
            win = win + jnp.where(first_row <= r, pb_ref[r], 0.0)
        width = _group_select(lane, *POOL_WINDOWS).astype(f32)
        pooled = win / width - xa
        ya = _dot(pooled.astype(bf16), poolw_ref[...]) * row(pscale_ref)
        ain_ref[...] = (ya * _silu(hcols(OFF_ZA, D_POOL))).astype(bf16)
        for r in range(POOL_BUF - 1):
            pool_ref[r] = pb_ref[r + 1]
        pool_ref[POOL_BUF - 1] = xa
        vn = _layer_norm(hcols(OFF_V, D_SGU), row(slng_ref), row(slnb_ref))
        vn_ref[...] = vn
        yb = hcols(OFF_U, D_SGU) * (row(sw0_ref) * vn + row(sb0_ref))
        bin_ref[...] = (yb * _silu(hcols(OFF_ZB, D_SGU))).astype(bf16)
        cos = cos_ref[...]
        sin = sin_ref[...]
        for c in range(D_ATTN // LANES):
            q_ref[:, c * LANES:(c + 1) * LANES] = _rope128(hcols(OFF_Q + c * LANES, LANES), cos, sin) * SCALE
        kr = _rope128(hcols(OFF_K, D_KV), cos, sin)
        vv = hcols(OFF_VV, D_KV)
        k_ref[...] = kr
        v_ref[...] = vv
        kt_ref[...] = kr.T
        vt_ref[...] = vv.T
        szc_ref[...] = _silu(hcols(OFF_ZC, D_ATTN))
        for br in range(N_BRANCHES):
            cols = slice(br * D_MODEL, (br + 1) * D_MODEL)
            g_ref[:, cols] = _gate2(hcols(OFF_G + br * D_MODEL, D_MODEL), bg_ref[br:br + 1, :])


def _sample_proj(layer, x, cos, sin, w_in, b_gate, pool_t, poolw, pscale, slng, slnb, sw0, sb0):
    nb = x.shape[0]
    out_shape = (
        jax.ShapeDtypeStruct((D_MODEL, D_IN), bf16),
        jax.ShapeDtypeStruct((nb, D_ATTN), f32),
        jax.ShapeDtypeStruct((nb, D_KV), f32),
        jax.ShapeDtypeStruct((nb, D_KV), f32),
        jax.ShapeDtypeStruct((D_KV, nb), f32),
        jax.ShapeDtypeStruct((D_KV, nb), f32),
        jax.ShapeDtypeStruct((POOL_BUF, nb, D_POOL), f32),
        jax.ShapeDtypeStruct((nb, D_SGU), f32),
        jax.ShapeDtypeStruct((nb, D_POOL), bf16),
        jax.ShapeDtypeStruct((nb, D_SGU), bf16),
        jax.ShapeDtypeStruct((nb, D_ATTN), f32),
        jax.ShapeDtypeStruct((nb, N_BRANCHES * D_MODEL), f32),
    )
    lspec = lambda a: _layer_spec(a, layer)
    args = (x, cos, sin, w_in, b_gate, pool_t, poolw, pscale, slng, slnb, sw0, sb0)
    in_specs = [pl.BlockSpec((nb, RK), lambda j: (0, j)), _full_spec(cos), _full_spec(sin),
                pl.BlockSpec((None, RK, D_IN), lambda j: (layer, j, 0)),
                lspec(b_gate), lspec(pool_t), lspec(poolw), _full_spec(pscale), _full_spec(slng),
                _full_spec(slnb), _full_spec(sw0), _full_spec(sb0)]
    out_specs = (pl.BlockSpec((RK, D_IN), lambda j: (j, 0)),) + tuple(_full_spec(s) for s in out_shape[1:])
    return pl.pallas_call(
        functools.partial(_sample_proj_kernel, layer),
        out_shape=out_shape,
        grid=(RK_STEPS,),
        in_specs=in_specs,
        out_specs=out_specs,
        scratch_shapes=[pltpu.VMEM((nb, D_IN), f32)],
        compiler_params=pltpu.CompilerParams(dimension_semantics=("arbitrary",),
                                             vmem_limit_bytes=VMEM_LIMIT),
        name="sample_proj",
    )(*args)


def _sample_attn_kernel(sink_ref, q_ref, kc_ref, vc_ref, kn_ref, vn_ref, o_ref):
    for kv in range(N_KV_HEADS):
        qb = q_ref[:, kv].astype(bf16)
        kn = kn_ref[:, kv]
        vn = vn_ref[:, kv]
        sink = sink_ref[kv][None]
        s = jnp.einsum('bgd,bdw->bgw', qb, kc_ref[:, kv].astype(bf16), preferred_element_type=f32)
        s_new = jnp.sum(qb.astype(f32) * kn.astype(bf16).astype(f32), axis=-1, keepdims=True)
        m = jnp.maximum(jnp.maximum(jnp.max(s, axis=-1, keepdims=True), s_new), sink)
        p = jnp.exp(s - m)
        p_new = jnp.exp(s_new - m)
        den = jnp.sum(p, axis=-1, keepdims=True) + p_new + jnp.exp(sink - m)
        o = jnp.einsum('bgw,bdw->bgd', (p / den).astype(bf16), vc_ref[:, kv].astype(bf16),
                       preferred_element_type=f32)
        o_ref[:, kv] = o + (p_new / den) * vn


def _sample_attn(layer, sink, q4, kc, vc, kn, vn):
    nb = q4.shape[0]
    cache_spec = pl.BlockSpec((None, SB, N_KV_HEADS, HEAD_DIM, WINDOW), lambda b: (layer, b, 0, 0, 0))
    blk = lambda s2, s3: pl.BlockSpec((SB, N_KV_HEADS, s2, s3), lambda b: (b, 0, 0, 0))
    return pl.pallas_call(
        _sample_attn_kernel,
        out_shape=jax.ShapeDtypeStruct((nb, N_KV_HEADS, Q_PER_KV, HEAD_DIM), f32),
        grid=(nb // SB,),
        in_specs=[_layer_spec(sink, layer), blk(Q_PER_KV, HEAD_DIM), cache_spec, cache_spec,
                  blk(1, HEAD_DIM), blk(1, HEAD_DIM)],
        out_specs=blk(Q_PER_KV, HEAD_DIM),
        compiler_params=pltpu.CompilerParams(dimension_semantics=("arbitrary",),
                                             vmem_limit_bytes=VMEM_LIMIT),
        name="sample_attn",
    )(sink, q4, kc, vc, kn, vn)


def _sample_merge_kernel(layer, x_ref, ain_ref, bin_ref, yc_ref, szc_ref, g_ref, wpa_ref, wpb_ref, wpc_ref,
                         wout_ref, lng_ref, lnb_ref, y_ref, wpa_b_ref, wpb_b_ref, wpc_b_ref, wout_b_ref):
    row = lambda ref: ref[layer:layer + 1, :]
    wpa = (0.5 * wpa_ref[...]).astype(bf16)
    wpb = (0.5 * wpb_ref[...]).astype(bf16)
    wpc = (0.5 * wpc_ref[...]).astype(bf16)
    wout = wout_ref[...].astype(bf16)
    wpa_b_ref[...] = wpa
    wpb_b_ref[...] = wpb
    wpc_b_ref[...] = wpc
    wout_b_ref[...] = wout
    cin = (yc_ref[...] * szc_ref[...]).astype(bf16)
    merged = (g_ref[:, 0:D_MODEL] * _dot(ain_ref[...], wpa)
              + g_ref[:, D_MODEL:2 * D_MODEL] * _dot(bin_ref[...], wpb)
              + g_ref[:, 2 * D_MODEL:3 * D_MODEL] * _dot(cin, wpc))
    out = _dot(merged.astype(bf16), wout)
    y_ref[...] = _layer_norm(ALPHA * x_ref[...] + out, row(lng_ref), row(lnb_ref))


def _sample_merge(layer, x, ain, bin_, yc, szc, g, wpa, wpb, wpc, wout, lng, lnb):
    lspec = lambda a: _layer_spec(a, layer)
    out_shape = (jax.ShapeDtypeStruct(x.shape, f32),) + tuple(
        jax.ShapeDtypeStruct(w.shape[1:], bf16) for w in (wpa, wpb, wpc, wout))
    return pl.pallas_call(
        functools.partial(_sample_merge_kernel, layer),
        out_shape=out_shape,
        grid=(1,),
        in_specs=[_full_spec(x), _full_spec(ain), _full_spec(bin_), _full_spec(yc), _full_spec(szc),
                  _full_spec(g), lspec(wpa), lspec(wpb), lspec(wpc), lspec(wout), _full_spec(lng),
                  _full_spec(lnb)],
        out_specs=tuple(_full_spec(s) for s in out_shape),
        compiler_params=pltpu.CompilerParams(dimension_semantics=("arbitrary",),
                                             vmem_limit_bytes=VMEM_LIMIT),
        name="sample_merge",
    )(x, ain, bin_, yc, szc, g, wpa, wpb, wpc, wout, lng, lnb)


def _rope_tables(positions):
    halfd = HEAD_DIM // 2
    inv = ROPE_THETA ** (-np.arange(halfd, dtype=np.float64) / halfd)
    ang = np.asarray(positions, dtype=np.float64)[:, None] * inv[None, :]
    cos = np.tile(np.cos(ang), (1, LANES // halfd))
    sin = np.tile(np.concatenate([-np.sin(ang), np.sin(ang)], axis=1), (1, LANES // HEAD_DIM))
    return jnp.asarray(cos, f32), jnp.asarray(sin, f32)


def _block_diag(w):
    nl, g, c, _ = w.shape
    eye = jnp.eye(g, dtype=w.dtype)
    return (eye[None, :, None, :, None] * w[:, :, :, None, :]).reshape(nl, g * c, g * c)


def kernel(x_prompt, x_sample, state_pool, cache_k_win, cache_v_win, w_in, b_gate, pool_w, pool_scale, sgu_ln_g, sgu_ln_b, sgu_w, sgu_b, attn_sinks, w_proj_a, w_proj_b, w_proj_c, w_out, ln_g, ln_b):
    B, L, _ = x_prompt.shape
    nb = x_sample.shape[0]
    cos_p, sin_p = _rope_tables(np.arange(L))
    cos_s, sin_s = _rope_tables(np.array([PAST_LEN]))

    poolw = _block_diag(pool_w).astype(bf16)
    sguw = sgu_w.reshape(DEPTH, N_SGU_GROUPS * CHUNK, CHUNK)
    sgub = jnp.repeat(jnp.swapaxes(sgu_b, 1, 2), POOL_GC, axis=2)
    sw0 = jnp.repeat(sgu_w[:, :, 0, 0], POOL_GC, axis=1)
    sb0 = jnp.repeat(sgu_b[:, :, 0], POOL_GC, axis=1)
    sink4 = attn_sinks.reshape(DEPTH, N_KV_HEADS, Q_PER_KV, 1)
    kc_t = jnp.transpose(cache_k_win, (0, 1, 3, 4, 2))
    vc_t = jnp.transpose(cache_v_win, (0, 1, 3, 4, 2))
    pool_t = jnp.transpose(state_pool, (0, 2, 1, 3))

    y_s = x_sample.reshape(nb, D_MODEL)
    pool_s, chunk_v, kts, vts, weights = ([] for _ in range(5))
    for l in range(DEPTH):
        w_in_b, q, kn, vn, kt, vt, ps, cv, ain, bin_, szc, g = _sample_proj(
            l, y_s, cos_s, sin_s, w_in, b_gate, pool_t, poolw, pool_scale, sgu_ln_g, sgu_ln_b, sw0, sb0)
        o = _sample_attn(
            l, sink4, q.reshape(nb, N_KV_HEADS, Q_PER_KV, HEAD_DIM), kc_t, vc_t,
            kn.reshape(nb, N_KV_HEADS, 1, HEAD_DIM), vn.reshape(nb, N_KV_HEADS, 1, HEAD_DIM))
        y_s, wpa, wpb, wpc, wout = _sample_merge(
            l, y_s, ain, bin_, o.reshape(nb, D_ATTN), szc, g, w_proj_a, w_proj_b, w_proj_c, w_out, ln_g, ln_b)
        pool_s.append(ps); chunk_v.append(cv); kts.append(kt); vts.append(vt)
        weights.append((w_in_b, wpa, wpb, wpc, wout))

    y_p = x_prompt
    pool_p, k_p, v_p = ([] for _ in range(3))
    for l in range(DEPTH):
        w_in_b, wpa, wpb, wpc, wout = weights[l]
        slide_args = (jnp.stack(kts), jnp.stack(vts), kc_t, vc_t) if l == DEPTH - 1 else None
        y_p, pp, kp, vp, *slid = _prompt_layer(
            l, y_p, cos_p, sin_p, attn_sinks, w_in_b, b_gate, poolw, pool_scale, sgu_ln_g, sgu_ln_b, sguw,
            sgub, wpa, wpb, wpc, wout, ln_g, ln_b, slide_args=slide_args)
        pool_p.append(pp); k_p.append(kp); v_p.append(vp)
    k_s, v_s = slid

    to_cache = lambda a: jnp.transpose(a, (0, 1, 4, 2, 3))
    prompt_cache = lambda lst: to_cache(jnp.stack(lst).reshape(DEPTH, B, N_KV_HEADS, HEAD_DIM, WINDOW))
    return (y_p, y_s.reshape(nb, 1, D_MODEL),
            jnp.stack(pool_p), prompt_cache(k_p), prompt_cache(v_p),
            jnp.transpose(jnp.stack(pool_s), (0, 2, 1, 3)), to_cache(k_s), to_cache(v_s),
            jnp.stack(chunk_v).reshape(DEPTH, nb, 1, D_SGU))
```

```python
import functools

import numpy as np
import jax
import jax.numpy as jnp
from jax import lax
from jax.experimental import pallas as pl
from jax.experimental.pallas import tpu as pltpu

D_MODEL = 1024
DEPTH = 2
PAST_LEN = 8192
D_POOL = 256
POOL_WINDOWS = (2, 4, 8, 16)
POOL_GC = 64
POOL_BUF = 15
D_SGU = 256
CHUNK = 128
N_SGU_GROUPS = 4
HEAD_DIM = 64
N_HEADS = 8
N_KV_HEADS = 2
Q_PER_KV = 4
D_ATTN = 512
D_KV = 128
WINDOW = 128
BLOCK = 128
ROPE_THETA = 10000.0
N_BRANCHES = 3
D_IN = 2 * D_POOL + 3 * D_SGU + 2 * D_ATTN + 2 * D_KV + N_BRANCHES * D_MODEL
ALPHA = (2.0 * DEPTH) ** 0.25
LN_EPS = 1e-5
NEG_INF = -1e30
SCALE = HEAD_DIM ** -0.5

OFF_XA, OFF_ZA, OFF_U, OFF_V, OFF_ZB = 0, 256, 512, 768, 1024
OFF_Q, OFF_K, OFF_VV, OFF_ZC, OFF_G = 1280, 1792, 1920, 2048, 2560

LANES = 128
TM = 512
SUB = 2
NBLK = TM // BLOCK
HIST = 32
NCHUNK = 256
OUT_ROWS = 256
VMEM_LIMIT = 56 * 1024 * 1024
SB = 16
RK = 128
RK_STEPS = D_MODEL // RK

bf16 = jnp.bfloat16
f32 = jnp.float32


def _dot(a, b):
    return jnp.dot(a, b, preferred_element_type=f32)


def _dot_nt(a, b):
    return lax.dot_general(a, b, (((1,), (1,)), ((), ())), preferred_element_type=f32)


def _sigmoid(z):
    return 0.5 * jnp.tanh(0.5 * z) + 0.5


def _silu(z):
    return z * _sigmoid(z)


def _gate2(half_pre, bias):
    return jnp.tanh(half_pre + 0.5 * bias) + 1.0


def _layer_norm(x, g, b):
    mu = jnp.mean(x, axis=-1, keepdims=True)
    xc = x - mu
    var = jnp.mean(xc * xc, axis=-1, keepdims=True)
    return xc * lax.rsqrt(var + LN_EPS) * g + b


def _rope128(x, cos, sin_signed):
    lane = lax.broadcasted_iota(jnp.int32, x.shape, 1)
    first_half = (lane % HEAD_DIM) < (HEAD_DIM // 2)
    partner = jnp.where(first_half,
                        pltpu.roll(x, LANES - HEAD_DIM // 2, 1),
                        pltpu.roll(x, HEAD_DIM // 2, 1))
    return x * cos + partner * sin_signed


def _group_select(lane, a0, a1, a2, a3):
    return jnp.where(lane < 64, a0, jnp.where(lane < 128, a1, jnp.where(lane < 192, a2, a3)))


def _layer_spec(arr, layer, single_buffer=False):
    block = (None,) + arr.shape[1:]
    zeros = (0,) * (arr.ndim - 1)
    index_map = lambda *_: (layer,) + zeros
    if single_buffer:
        return pl.BlockSpec(block, index_map, pipeline_mode=pl.Buffered(1))
    return pl.BlockSpec(block, index_map)


def _full_spec(arr, single_buffer=False):
    zeros = (0,) * len(arr.shape)
    if single_buffer:
        return pl.BlockSpec(arr.shape, lambda *_: zeros, pipeline_mode=pl.Buffered(1))
    return pl.BlockSpec(arr.shape, lambda *_: zeros)


def _slide_windows(step, per_step, new_refs, old_refs, out_refs):
    lane = lax.broadcasted_iota(jnp.int32, (D_KV, WINDOW), 1)
    shift = jnp.where(step == 0, 0, LANES - step * per_step)
    for new_ref, old_ref, out_ref in zip(new_refs, old_refs, out_refs):
        for l in range(DEPTH):
            new_cols = pltpu.roll(new_ref[l], shift, 1)
            for s in range(per_step):
                slid = pltpu.roll(old_ref[l, s].reshape(D_KV, WINDOW), WINDOW - 1, 1)
                newest = jnp.broadcast_to(new_cols[:, s:s + 1], (D_KV, WINDOW))
                out_ref[l, s] = jnp.where(lane == WINDOW - 1, newest, slid).reshape(
                    N_KV_HEADS, HEAD_DIM, WINDOW)


def _prompt_kernel(layer, slide, *refs):
    def sub_tile(s, carry):
        _prompt_tile(layer, slide, s, refs)
        return carry
    lax.fori_loop(0, SUB, sub_tile, 0)


def _prompt_tile(layer, slide, s, refs):
    n_in = 18 + (4 if slide else 0)
    n_out = 4 + (2 if slide else 0)
    (sinks_ref, x_ref, cos_ref, sin_ref, w_in_ref, bg_ref, poolw_ref, pscale_ref,
     slng_ref, slnb_ref, sguw_ref, sgub_ref, wpa_ref, wpb_ref, wpc_ref, wout_ref,
     lng_ref, lnb_ref) = refs[:18]
    y_ref, pool_out_ref, k_out_ref, v_out_ref = refs[n_in:n_in + 4]
    (h_ref, ext_ref, s2_ref, s4_ref, s8_ref, qs_ref,
     ka_ref, kb_ref, kc_ref, kd_ref, va_ref, vb_ref, vc_ref, vd_ref,
     ain_ref, bin_ref, cin_ref, mg_ref, gate_ref, klast_ref, vlast_ref) = refs[n_in + n_out:]
    tile_rows = pl.ds(pl.multiple_of(s * TM, TM), TM)
    x_ref, y_ref, cos_ref, sin_ref = (r.at[tile_rows] for r in (x_ref, y_ref, cos_ref, sin_ref))
    i = pl.program_id(1) * SUB + s
    last = pl.num_programs(1) * SUB - 1
    kv_refs = (ka_ref, kb_ref, kc_ref, kd_ref, va_ref, vb_ref, vc_ref, vd_ref)
    row = lambda ref: ref[layer:layer + 1, :]

    @pl.when(i == 0)
    def _():
        ext_ref[0:HIST, :] = jnp.zeros((HIST, D_POOL), f32)
        for r in kv_refs:
            r[0:BLOCK, :] = jnp.zeros((BLOCK, LANES), bf16)

    xb = x_ref[...].astype(bf16)
    half = OFF_G // 2
    h_ref[:, 0:half] = _dot(xb, w_in_ref[:, 0:half])
    h_ref[:, half:OFF_G] = _dot(xb, w_in_ref[:, half:OFF_G])

    if slide:
        step = pl.program_id(0) * (last + 1) + i
        per_step = refs[20].shape[1] // SUB
        mine = lambda r: r.at[:, pl.ds(s * per_step, per_step)]
        _slide_windows(step, per_step, refs[18:20], [mine(r) for r in refs[20:22]],
                       [mine(r) for r in refs[n_in + 4:n_in + 6]])

    xa = h_ref[:, OFF_XA:OFF_XA + D_POOL]
    ext_ref[HIST:HIST + TM, :] = xa
    n = HIST + TM
    s2_ref[8:n, :] = ext_ref[8:n, :] + ext_ref[7:n - 1, :]
    s4_ref[16:n, :] = s2_ref[16:n, :] + s2_ref[14:n - 2, :]
    s8_ref[24:n, :] = s4_ref[24:n, :] + s4_ref[20:n - 4, :]
    w16 = s8_ref[HIST:n, :] + s8_ref[HIST - 8:n - 8, :]
    lane_p = lax.broadcasted_iota(jnp.int32, (TM, D_POOL), 1)
    row_p = lax.broadcasted_iota(jnp.int32, (TM, D_POOL), 0)
    win = _group_select(lane_p, s2_ref[HIST:n, :], s4_ref[HIST:n, :], s8_ref[HIST:n, :], w16)
    width = _group_select(lane_p, POOL_WINDOWS[0], POOL_WINDOWS[1], POOL_WINDOWS[2], POOL_WINDOWS[3])
    cnt = jnp.minimum(row_p + (i * TM + 1), width).astype(f32)
    pooled = win / cnt - xa
    ya = _dot(pooled.astype(bf16), poolw_ref[...]) * row(pscale_ref)
    za = h_ref[:, OFF_ZA:OFF_ZA + D_POOL]
    ain_ref[...] = (ya * _silu(za)).astype(bf16)
    ext_ref[HIST - 16:HIST, :] = ext_ref[n - 16:n, :]

    vn = _layer_norm(h_ref[:, OFF_V:OFF_V + D_SGU], row(slng_ref), row(slnb_ref)).astype(bf16)
    wr = lax.broadcasted_iota(jnp.int32, (N_SGU_GROUPS * CHUNK, CHUNK), 0) % CHUNK
    wc = lax.broadcasted_iota(jnp.int32, (N_SGU_GROUPS * CHUNK, CHUNK), 1)
    w_s = jnp.where(wc <= wr, sguw_ref[...], 0.0).astype(bf16)
    lane_c = lax.broadcasted_iota(jnp.int32, (CHUNK, D_SGU), 1)
    for j in range(NBLK):
        rows = slice(j * CHUNK, (j + 1) * CHUNK)
        r = _dot(w_s, vn[rows, :])
        s = _group_select(lane_c, r[0:CHUNK], r[CHUNK:2 * CHUNK], r[2 * CHUNK:3 * CHUNK],
                          r[3 * CHUNK:4 * CHUNK]) + sgub_ref[...]
        yb = h_ref[rows, OFF_U:OFF_U + D_SGU] * s
        bin_ref[rows, :] = (yb * _silu(h_ref[rows, OFF_ZB:OFF_ZB + D_SGU])).astype(bf16)

    cos = cos_ref[...]
    sin = sin_ref[...]
    for c in range(D_ATTN // LANES):
        qc = _rope128(h_ref[:, OFF_Q + c * LANES:OFF_Q + (c + 1) * LANES], cos, sin)
        qs_ref[:, c * LANES:(c + 1) * LANES] = (qc * SCALE).astype(bf16)
    kr = _rope128(h_ref[:, OFF_K:OFF_K + D_KV], cos, sin)
    vv = h_ref[:, OFF_VV:OFF_VV + D_KV]
    klast_ref[...] = kr[TM - WINDOW:TM, :]
    vlast_ref[...] = vv[TM - WINDOW:TM, :]
    lane_k = lax.broadcasted_iota(jnp.int32, (TM, LANES), 1)
    lo = lane_k < HEAD_DIM
    for src, (a_ref, b_ref, c_ref, d_ref) in ((kr, kv_refs[0:4]), (vv, kv_refs[4:8])):
        sw = pltpu.roll(src, HEAD_DIM, 1)
        a_ref[BLOCK:BLOCK + TM, :] = jnp.where(lo, src, 0.0).astype(bf16)
        b_ref[BLOCK:BLOCK + TM, :] = jnp.where(lo, 0.0, src).astype(bf16)
        c_ref[BLOCK:BLOCK + TM, :] = jnp.where(lo, sw, 0.0).astype(bf16)
        d_ref[BLOCK:BLOCK + TM, :] = jnp.where(lo, 0.0, sw).astype(bf16)

    qrow = lax.broadcasted_iota(jnp.int32, (2 * BLOCK, 2 * BLOCK), 0) % BLOCK
    kcol = lax.broadcasted_iota(jnp.int32, (2 * BLOCK, 2 * BLOCK), 1)
    band = (kcol >= qrow) & (kcol <= qrow + WINDOW)
    band_first = band & (kcol >= jnp.where(i > 0, 0, BLOCK))
    top = lax.broadcasted_iota(jnp.int32, (2 * BLOCK, 1), 0) < BLOCK
    n_gate = N_BRANCHES * D_MODEL // NCHUNK
    n_unit = NBLK * N_KV_HEADS
    gate_sched = [range(u * n_gate // n_unit, (u + 1) * n_gate // n_unit) for u in range(n_unit)]

    def scores(u):
        j, kv = divmod(u, N_KV_HEADS)
        rows = slice(j * BLOCK, (j + 1) * BLOCK)
        keys = slice(j * BLOCK, j * BLOCK + 2 * BLOCK)
        c0 = kv * Q_PER_KV * HEAD_DIM
        qst = jnp.concatenate([qs_ref[rows, c0:c0 + LANES], qs_ref[rows, c0 + LANES:c0 + 2 * LANES]], axis=0)
        k_even, k_odd = (ka_ref, kd_ref) if kv == 0 else (kc_ref, kb_ref)
        kcat = jnp.concatenate([k_even[keys, :], k_odd[keys, :]], axis=0)
        return _dot_nt(qst, kcat)

    def attend(u, sc):
        j, kv = divmod(u, N_KV_HEADS)
        rows = slice(j * BLOCK, (j + 1) * BLOCK)
        keys = slice(j * BLOCK, j * BLOCK + 2 * BLOCK)
        allowed = band_first if j == 0 else band
        c0 = kv * Q_PER_KV * HEAD_DIM
        h0 = kv * Q_PER_KV
        v_even, v_odd = (va_ref, vd_ref) if kv == 0 else (vc_ref, vb_ref)
        probs = []
        for par in range(2):
            sink = jnp.where(top, sinks_ref[layer, h0 + par], sinks_ref[layer, h0 + 2 + par])
            sm = jnp.where(allowed, sc[:, par * 2 * BLOCK:(par + 1) * 2 * BLOCK], NEG_INF)
            m = jnp.maximum(jnp.max(sm, axis=-1, keepdims=True), sink)
            p = jnp.exp(sm - m)
            den = jnp.sum(p, axis=-1, keepdims=True) + jnp.exp(sink - m)
            probs.append((p / den).astype(bf16))
        pcat = jnp.concatenate(probs, axis=1)
        vcat = jnp.concatenate([v_even[keys, :], v_odd[keys, :]], axis=0)
        o = _dot(pcat, vcat)
        for pr in range(2):
            cols = slice(c0 + pr * LANES, c0 + (pr + 1) * LANES)
            zc = h_ref[rows, OFF_ZC + c0 + pr * LANES:OFF_ZC + c0 + (pr + 1) * LANES]
            cin_ref[rows, cols] = (o[pr * BLOCK:(pr + 1) * BLOCK] * _silu(zc)).astype(bf16)

    sc_next = scores(0)
    for u in range(n_unit):
        sc = sc_next
        for gc in gate_sched[u]:
            gcols = slice(gc * NCHUNK, (gc + 1) * NCHUNK)
            gate_ref[:, gcols] = _dot(xb, w_in_ref[:, OFF_G + gc * NCHUNK:OFF_G + (gc + 1) * NCHUNK])
        if u + 1 < n_unit:
            sc_next = scores(u + 1)
        attend(u, sc)

    for r in kv_refs:
        r[0:BLOCK, :] = r[TM:TM + BLOCK, :]

    for c in range(D_MODEL // NCHUNK):
        cols = slice(c * NCHUNK, (c + 1) * NCHUNK)
        acc = None
        for br, (in_ref, wp_ref) in enumerate(((ain_ref, wpa_ref), (bin_ref, wpb_ref), (cin_ref, wpc_ref))):
            g0 = br * D_MODEL + c * NCHUNK
            term = _gate2(gate_ref[:, g0:g0 + NCHUNK], bg_ref[br:br + 1, cols]) * _dot(in_ref[...], wp_ref[:, cols])
            acc = term if acc is None else acc + term
        mg_ref[:, cols] = acc.astype(bf16)

    for r0 in range(0, TM, OUT_ROWS):
        rows = slice(r0, r0 + OUT_ROWS)
        out = _dot(mg_ref[rows, :], wout_ref[...])
        y_ref[rows, :] = _layer_norm(ALPHA * x_ref[rows, :] + out, row(lng_ref), row(lnb_ref))

    @pl.when(i == last)
    def _():
        pool_out_ref[...] = ext_ref[n - POOL_BUF:n, :]
        k_out_ref[...] = klast_ref[...].T
        v_out_ref[...] = vlast_ref[...].T


def _prompt_layer(layer, x, cos, sin, sinks, w_in, b_gate, poolw, pscale, slng, slnb, sguw, sgub,
                  wpa, wpb, wpc, wout, lng, lnb, slide_args=None):
    B, L, _ = x.shape
    rows = TM * SUB
    nt = L // rows
    grid = (B, nt)
    row_spec = lambda w: pl.BlockSpec((rows, w), lambda b, i: (i, 0))
    lspec = lambda a: _layer_spec(a, layer, single_buffer=True)
    wspec = lambda a: _full_spec(a, single_buffer=True)
    in_specs = [
        pl.BlockSpec(memory_space=pltpu.SMEM),
        pl.BlockSpec((None, rows, D_MODEL), lambda b, i: (b, i, 0)),
        row_spec(LANES), row_spec(LANES),
        wspec(w_in), lspec(b_gate), lspec(poolw), _full_spec(pscale), _full_spec(slng), _full_spec(slnb),
        lspec(sguw), lspec(sgub), wspec(wpa), wspec(wpb), wspec(wpc), wspec(wout),
        _full_spec(lng), _full_spec(lnb),
    ]
    out_shape = (
        jax.ShapeDtypeStruct((B, L, D_MODEL), f32),
        jax.ShapeDtypeStruct((B, POOL_BUF, D_POOL), f32),
        jax.ShapeDtypeStruct((B, D_KV, WINDOW), f32),
        jax.ShapeDtypeStruct((B, D_KV, WINDOW), f32),
    )
    out_specs = (
        pl.BlockSpec((None, rows, D_MODEL), lambda b, i: (b, i, 0)),
        pl.BlockSpec((None, POOL_BUF, D_POOL), lambda b, i: (b, 0, 0)),
        pl.BlockSpec((None, D_KV, WINDOW), lambda b, i: (b, 0, 0)),
        pl.BlockSpec((None, D_KV, WINDOW), lambda b, i: (b, 0, 0)),
    )
    args = [sinks, x, cos, sin, w_in, b_gate, poolw, pscale, slng, slnb, sguw, sgub,
            wpa, wpb, wpc, wout, lng, lnb]
    if slide_args is not None:
        new_k, new_v, cache_k, cache_v = slide_args
        depth, nb = cache_k.shape[:2]
        per_step = nb // (B * nt)
        assert per_step * B * nt == nb and depth == DEPTH
        cache_spec = pl.BlockSpec((depth, per_step) + cache_k.shape[2:], lambda b, i: (0, b * nt + i, 0, 0, 0))
        in_specs += [_full_spec(new_k), _full_spec(new_v), cache_spec, cache_spec]
        args += [new_k, new_v, cache_k, cache_v]
        out_shape += (jax.ShapeDtypeStruct(cache_k.shape, f32), jax.ShapeDtypeStruct(cache_v.shape, f32))
        out_specs += (cache_spec, cache_spec)
    kv_scratch = [pltpu.VMEM((BLOCK + TM, LANES), bf16) for _ in range(8)]
    scratch = [
        pltpu.VMEM((TM, OFF_G), f32),
        pltpu.VMEM((HIST + TM, D_POOL), f32),
        pltpu.VMEM((HIST + TM, D_POOL), f32),
        pltpu.VMEM((HIST + TM, D_POOL), f32),
        pltpu.VMEM((HIST + TM, D_POOL), f32),
        pltpu.VMEM((TM, D_ATTN), bf16),
        *kv_scratch,
        pltpu.VMEM((TM, D_POOL), bf16),
        pltpu.VMEM((TM, D_SGU), bf16),
        pltpu.VMEM((TM, D_ATTN), bf16),
        pltpu.VMEM((TM, D_MODEL), bf16),
        pltpu.VMEM((TM, N_BRANCHES * D_MODEL), f32),
        pltpu.VMEM((WINDOW, D_KV), f32),
        pltpu.VMEM((WINDOW, D_KV), f32),
    ]
    return pl.pallas_call(
        functools.partial(_prompt_kernel, layer, slide_args is not None),
        out_shape=out_shape,
        grid=grid,
        in_specs=in_specs,
        out_specs=out_specs,
        scratch_shapes=scratch,
        compiler_params=pltpu.CompilerParams(
            dimension_semantics=("arbitrary", "arbitrary"),
            vmem_limit_bytes=VMEM_LIMIT),
        name="prompt_layer",
    )(*args)


def _sample_proj_kernel(layer, x_ref, cos_ref, sin_ref, w_in_ref, bg_ref, pb_ref, poolw_ref, pscale_ref,
                        slng_ref, slnb_ref, sw0_ref, sb0_ref,
                        wb_ref, q_ref, k_ref, v_ref, kt_ref, vt_ref, pool_ref, vn_ref, ain_ref, bin_ref,
                        szc_ref, g_ref, h_ref):
    j = pl.program_id(0)
    row = lambda ref: ref[layer:layer + 1, :]
    wb_ref[:, 0:OFF_G] = w_in_ref[:, 0:OFF_G].astype(bf16)
    wb_ref[:, OFF_G:D_IN] = (0.5 * w_in_ref[:, OFF_G:D_IN]).astype(bf16)

    @pl.when(j == 0)
    def _():
        h_ref[...] = jnp.zeros(h_ref.shape, f32)

    h_ref[...] += _dot(x_ref[...].astype(bf16), wb_ref[...])

    def hcols(off, width):
        return h_ref[:, off:off + width]

    @pl.when(j == RK_STEPS - 1)
    def _():
        xa = hcols(OFF_XA, D_POOL)
        lane = lax.broadcasted_iota(jnp.int32, xa.shape, 1)
        first_row = _group_select(lane, *(POOL_BUF - (w - 1) for w in POOL_WINDOWS))
        win = xa
        for r in range(POOL_BUF):
            win = win + jnp.where(first_row <= r, pb_ref[r], 0.0)
        width = _group_select(lane, *POOL_WINDOWS).astype(f32)
        pooled = win / width - xa
        ya = _dot(pooled.astype(bf16), poolw_ref[...]) * row(pscale_ref)
        ain_ref[...] = (ya * _silu(hcols(OFF_ZA, D_POOL))).astype(bf16)
        for r in range(POOL_BUF - 1):
            pool_ref[r] = pb_ref[r + 1]
        pool_ref[POOL_BUF - 1] = xa
        vn = _layer_norm(hcols(OFF_V, D_SGU), row(slng_ref), row(slnb_ref))
        vn_ref[...] = vn
        yb = hcols(OFF_U, D_SGU) * (row(sw0_ref) * vn + row(sb0_ref))
        bin_ref[...] = (yb * _silu(hcols(OFF_ZB, D_SGU))).astype(bf16)
        cos = cos_ref[...]
        sin = sin_ref[...]
        for c in range(D_ATTN // LANES):
            q_ref[:, c * LANES:(c + 1) * LANES] = _rope128(hcols(OFF_Q + c * LANES, LANES), cos, sin) * SCALE
        kr = _rope128(hcols(OFF_K, D_KV), cos, sin)
        vv = hcols(OFF_VV, D_KV)
        k_ref[...] = kr
        v_ref[...] = vv
        kt_ref[...] = kr.T
        vt_ref[...] = vv.T
        szc_ref[...] = _silu(hcols(OFF_ZC, D_ATTN))
        for br in range(N_BRANCHES):
            cols = slice(br * D_MODEL, (br + 1) * D_MODEL)
            g_ref[:, cols] = _gate2(hcols(OFF_G + br * D_MODEL, D_MODEL), bg_ref[br:br + 1, :])


def _sample_proj(layer, x, cos, sin, w_in, b_gate, pool_t, poolw, pscale, slng, slnb, sw0, sb0):
    nb = x.shape[0]
    out_shape = (
        jax.ShapeDtypeStruct((D_MODEL, D_IN), bf16),
        jax.ShapeDtypeStruct((nb, D_ATTN), f32),
        jax.ShapeDtypeStruct((nb, D_KV), f32),
        jax.ShapeDtypeStruct((nb, D_KV), f32),
        jax.ShapeDtypeStruct((D_KV, nb), f32),
        jax.ShapeDtypeStruct((D_KV, nb), f32),
        jax.ShapeDtypeStruct((POOL_BUF, nb, D_POOL), f32),
        jax.ShapeDtypeStruct((nb, D_SGU), f32),
        jax.ShapeDtypeStruct((nb, D_POOL), bf16),
        jax.ShapeDtypeStruct((nb, D_SGU), bf16),
        jax.ShapeDtypeStruct((nb, D_ATTN), f32),
        jax.ShapeDtypeStruct((nb, N_BRANCHES * D_MODEL), f32),
    )
    lspec = lambda a: _layer_spec(a, layer)
    args = (x, cos, sin, w_in, b_gate, pool_t, poolw, pscale, slng, slnb, sw0, sb0)
    in_specs = [pl.BlockSpec((nb, RK), lambda j: (0, j)), _full_spec(cos), _full_spec(sin),
                pl.BlockSpec((None, RK, D_IN), lambda j: (layer, j, 0)),
                lspec(b_gate), lspec(pool_t), lspec(poolw), _full_spec(pscale), _full_spec(slng),
                _full_spec(slnb), _full_spec(sw0), _full_spec(sb0)]
    out_specs = (pl.BlockSpec((RK, D_IN), lambda j: (j, 0)),) + tuple(_full_spec(s) for s in out_shape[1:])
    return pl.pallas_call(
        functools.partial(_sample_proj_kernel, layer),
        out_shape=out_shape,
        grid=(RK_STEPS,),
        in_specs=in_specs,
        out_specs=out_specs,
        scratch_shapes=[pltpu.VMEM((nb, D_IN), f32)],
        compiler_params=pltpu.CompilerParams(dimension_semantics=("arbitrary",),
                                             vmem_limit_bytes=VMEM_LIMIT),
        name="sample_proj",
    )(*args)


def _sample_attn_kernel(sink_ref, q_ref, kc_ref, vc_ref, kn_ref, vn_ref, o_ref):
    for kv in range(N_KV_HEADS):
        qb = q_ref[:, kv].astype(bf16)
        kn = kn_ref[:, kv]
        vn = vn_ref[:, kv]
        sink = sink_ref[kv][None]
        s = jnp.einsum('bgd,bdw->bgw', qb, kc_ref[:, kv].astype(bf16), preferred_element_type=f32)
        s_new = jnp.sum(qb.astype(f32) * kn.astype(bf16).astype(f32), axis=-1, keepdims=True)
        m = jnp.maximum(jnp.maximum(jnp.max(s, axis=-1, keepdims=True), s_new), sink)
        p = jnp.exp(s - m)
        p_new = jnp.exp(s_new - m)
        den = jnp.sum(p, axis=-1, keepdims=True) + p_new + jnp.exp(sink - m)
        o = jnp.einsum('bgw,bdw->bgd', (p / den).astype(bf16), vc_ref[:, kv].astype(bf16),
                       preferred_element_type=f32)
        o_ref[:, kv] = o + (p_new / den) * vn


def _sample_attn(layer, sink, q4, kc, vc, kn, vn):
    nb = q4.shape[0]
    cache_spec = pl.BlockSpec((None, SB, N_KV_HEADS, HEAD_DIM, WINDOW), lambda b: (layer, b, 0, 0, 0))
    blk = lambda s2, s3: pl.BlockSpec((SB, N_KV_HEADS, s2, s3), lambda b: (b, 0, 0, 0))
    return pl.pallas_call(
        _sample_attn_kernel,
        out_shape=jax.ShapeDtypeStruct((nb, N_KV_HEADS, Q_PER_KV, HEAD_DIM), f32),
        grid=(nb // SB,),
        in_specs=[_layer_spec(sink, layer), blk(Q_PER_KV, HEAD_DIM), cache_spec, cache_spec,
                  blk(1, HEAD_DIM), blk(1, HEAD_DIM)],
        out_specs=blk(Q_PER_KV, HEAD_DIM),
        compiler_params=pltpu.CompilerParams(dimension_semantics=("arbitrary",),
                                             vmem_limit_bytes=VMEM_LIMIT),
        name="sample_attn",
    )(sink, q4, kc, vc, kn, vn)


def _sample_merge_kernel(layer, x_ref, ain_ref, bin_ref, yc_ref, szc_ref, g_ref, wpa_ref, wpb_ref, wpc_ref,
                         wout_ref, lng_ref, lnb_ref, y_ref, wpa_b_ref, wpb_b_ref, wpc_b_ref, wout_b_ref):
    row = lambda ref: ref[layer:layer + 1, :]
    wpa = (0.5 * wpa_ref[...]).astype(bf16)
    wpb = (0.5 * wpb_ref[...]).astype(bf16)
    wpc = (0.5 * wpc_ref[...]).astype(bf16)
    wout = wout_ref[...].astype(bf16)
    wpa_b_ref[...] = wpa
    wpb_b_ref[...] = wpb
    wpc_b_ref[...] = wpc
    wout_b_ref[...] = wout
    cin = (yc_ref[...] * szc_ref[...]).astype(bf16)
    merged = (g_ref[:, 0:D_MODEL] * _dot(ain_ref[...], wpa)
              + g_ref[:, D_MODEL:2 * D_MODEL] * _dot(bin_ref[...], wpb)
              + g_ref[:, 2 * D_MODEL:3 * D_MODEL] * _dot(cin, wpc))
    out = _dot(merged.astype(bf16), wout)
    y_ref[...] = _layer_norm(ALPHA * x_ref[...] + out, row(lng_ref), row(lnb_ref))


def _sample_merge(layer, x, ain, bin_, yc, szc, g, wpa, wpb, wpc, wout, lng, lnb):
    lspec = lambda a: _layer_spec(a, layer)
    out_shape = (jax.ShapeDtypeStruct(x.shape, f32),) + tuple(
        jax.ShapeDtypeStruct(w.shape[1:], bf16) for w in (wpa, wpb, wpc, wout))
    return pl.pallas_call(
        functools.partial(_sample_merge_kernel, layer),
        out_shape=out_shape,
        grid=(1,),
        in_specs=[_full_spec(x), _full_spec(ain), _full_spec(bin_), _full_spec(yc), _full_spec(szc),
                  _full_spec(g), lspec(wpa), lspec(wpb), lspec(wpc), lspec(wout), _full_spec(lng),
                  _full_spec(lnb)],
        out_specs=tuple(_full_spec(s) for s in out_shape),
        compiler_params=pltpu.CompilerParams(dimension_semantics=("arbitrary",),
                                             vmem_limit_bytes=VMEM_LIMIT),
        name="sample_merge",
    )(x, ain, bin_, yc, szc, g, wpa, wpb, wpc, wout, lng, lnb)


def _rope_tables(positions):
    halfd = HEAD_DIM // 2
    inv = ROPE_THETA ** (-np.arange(halfd, dtype=np.float64) / halfd)
    ang = np.asarray(positions, dtype=np.float64)[:, None] * inv[None, :]
    cos = np.tile(np.cos(ang), (1, LANES // halfd))
    sin = np.tile(np.concatenate([-np.sin(ang), np.sin(ang)], axis=1), (1, LANES // HEAD_DIM))
    return jnp.asarray(cos, f32), jnp.asarray(sin, f32)


def _block_diag(w):
    nl, g, c, _ = w.shape
    eye = jnp.eye(g, dtype=w.dtype)
    return (eye[None, :, None, :, None] * w[:, :, :, None, :]).reshape(nl, g * c, g * c)


def kernel(x_prompt, x_sample, state_pool, cache_k_win, cache_v_win, w_in, b_gate, pool_w, pool_scale, sgu_ln_g, sgu_ln_b, sgu_w, sgu_b, attn_sinks, w_proj_a, w_proj_b, w_proj_c, w_out, ln_g, ln_b):
    B, L, _ = x_prompt.shape
    nb = x_sample.shape[0]
    cos_p, sin_p = _rope_tables(np.arange(L))
    cos_s, sin_s = _rope_tables(np.array([PAST_LEN]))

    poolw = _block_diag(pool_w).astype(bf16)
    sguw = sgu_w.reshape(DEPTH, N_SGU_GROUPS * CHUNK, CHUNK)
    sgub = jnp.repeat(jnp.swapaxes(sgu_b, 1, 2), POOL_GC, axis=2)
    sw0 = jnp.repeat(sgu_w[:, :, 0, 0], POOL_GC, axis=1)
    sb0 = jnp.repeat(sgu_b[:, :, 0], POOL_GC, axis=1)
    sink4 = attn_sinks.reshape(DEPTH, N_KV_HEADS, Q_PER_KV, 1)
    kc_t = jnp.transpose(cache_k_win, (0, 1, 3, 4, 2))
    vc_t = jnp.transpose(cache_v_win, (0, 1, 3, 4, 2))
    pool_t = jnp.transpose(state_pool, (0, 2, 1, 3))

    y_s = x_sample.reshape(nb, D_MODEL)
    pool_s, chunk_v, kts, vts, weights = ([] for _ in range(5))
    for l in range(DEPTH):
        w_in_b, q, kn, vn, kt, vt, ps, cv, ain, bin_, szc, g = _sample_proj(
            l, y_s, cos_s, sin_s, w_in, b_gate, pool_t, poolw, pool_scale, sgu_ln_g, sgu_ln_b, sw0, sb0)
        o = _sample_attn(
            l, sink4, q.reshape(nb, N_KV_HEADS, Q_PER_KV, HEAD_DIM), kc_t, vc_t,
            kn.reshape(nb, N_KV_HEADS, 1, HEAD_DIM), vn.reshape(nb, N_KV_HEADS, 1, HEAD_DIM))
        y_s, wpa, wpb, wpc, wout = _sample_merge(
            l, y_s, ain, bin_, o.reshape(nb, D_ATTN), szc, g, w_proj_a, w_proj_b, w_proj_c, w_out, ln_g, ln_b)
        pool_s.append(ps); chunk_v.append(cv); kts.append(kt); vts.append(vt)
        weights.append((w_in_b, wpa, wpb, wpc, wout))

    y_p = x_prompt
    pool_p, k_p, v_p = ([] for _ in range(3))
    for l in range(DEPTH):
        w_in_b, wpa, wpb, wpc, wout = weights[l]
        slide_args = (jnp.stack(kts), jnp.stack(vts), kc_t, vc_t) if l == DEPTH - 1 else None
        y_p, pp, kp, vp, *slid = _prompt_layer(
            l, y_p, cos_p, sin_p, attn_sinks, w_in_b, b_gate, poolw, pool_scale, sgu_ln_g, sgu_ln_b, sguw,
            sgub, wpa, wpb, wpc, wout, ln_g, ln_b, slide_args=slide_args)
        pool_p.append(pp); k_p.append(kp); v_p.append(vp)
    k_s, v_s = slid

    to_cache = lambda a: jnp.transpose(a, (0, 1, 4, 2, 3))
    prompt_cache = lambda lst: to_cache(jnp.stack(lst).reshape(DEPTH, B, N_KV_HEADS, HEAD_DIM, WINDOW))
    return (y_p, y_s.reshape(nb, 1, D_MODEL),
            jnp.stack(pool_p), prompt_cache(k_p), prompt_cache(v_p),
            jnp.transpose(jnp.stack(pool_s), (0, 2, 1, 3)), to_cache(k_s), to_cache(v_s),
            jnp.stack(chunk_v).reshape(DEPTH, nb, 1, D_SGU))
```

```python
import functools

import numpy as np
import jax
import jax.numpy as jnp
from jax import lax
from jax.experimental import pallas as pl
from jax.experimental.pallas import tpu as pltpu

D_MODEL = 1024
DEPTH = 2
PAST_LEN = 8192
D_POOL = 256
POOL_WINDOWS = (2, 4, 8, 16)
POOL_GC = 64
POOL_BUF = 15
D_SGU = 256
CHUNK = 128
N_SGU_GROUPS = 4
HEAD_DIM = 64
N_HEADS = 8
N_KV_HEADS = 2
Q_PER_KV = 4
D_ATTN = 512
D_KV = 128
WINDOW = 128
BLOCK = 128
ROPE_THETA = 10000.0
N_BRANCHES = 3
D_IN = 2 * D_POOL + 3 * D_SGU + 2 * D_ATTN + 2 * D_KV + N_BRANCHES * D_MODEL
ALPHA = (2.0 * DEPTH) ** 0.25
LN_EPS = 1e-5
NEG_INF = -1e30
SCALE = HEAD_DIM ** -0.5

OFF_XA, OFF_ZA, OFF_U, OFF_V, OFF_ZB = 0, 256, 512, 768, 1024
OFF_Q, OFF_K, OFF_VV, OFF_ZC, OFF_G = 1280, 1792, 1920, 2048, 2560

LANES = 128
TM = 512
SUB = 2
NBLK = TM // BLOCK
HIST = 32
NCHUNK = 256
OUT_CHUNKS = ((0, 256), (256, 512))
VMEM_LIMIT = 56 * 1024 * 1024
SB = 32
RK = 256
RK_STEPS = D_MODEL // RK

bf16 = jnp.bfloat16
f32 = jnp.float32


def _dot(a, b):
    return jnp.dot(a, b, preferred_element_type=f32)


def _dot_nt(a, b):
    return lax.dot_general(a, b, (((1,), (1,)), ((), ())), preferred_element_type=f32)


def _sigmoid(z):
    return 0.5 * jnp.tanh(0.5 * z) + 0.5


def _silu(z):
    return z * _sigmoid(z)


def _gate2(half_pre, bias):
    return jnp.tanh(half_pre + 0.5 * bias) + 1.0


def _layer_norm(x, g, b):
    mu = jnp.mean(x, axis=-1, keepdims=True)
    xc = x - mu
    var = jnp.mean(xc * xc, axis=-1, keepdims=True)
    return xc * lax.rsqrt(var + LN_EPS) * g + b


def _rope128(x, cos, sin_signed):
    lane = lax.broadcasted_iota(jnp.int32, x.shape, 1)
    first_half = (lane % HEAD_DIM) < (HEAD_DIM // 2)
    partner = jnp.where(first_half,
                        pltpu.roll(x, LANES - HEAD_DIM // 2, 1),
                        pltpu.roll(x, HEAD_DIM // 2, 1))
    return x * cos + partner * sin_signed


def _group_select(lane, a0, a1, a2, a3):
    return jnp.where(lane < 64, a0, jnp.where(lane < 128, a1, jnp.where(lane < 192, a2, a3)))


def _layer_spec(arr, layer, single_buffer=False):
    block = (None,) + arr.shape[1:]
    zeros = (0,) * (arr.ndim - 1)
    index_map = lambda *_: (layer,) + zeros
    if single_buffer:
        return pl.BlockSpec(block, index_map, pipeline_mode=pl.Buffered(1))
    return pl.BlockSpec(block, index_map)


def _full_spec(arr, single_buffer=False):
    zeros = (0,) * len(arr.shape)
    if single_buffer:
        return pl.BlockSpec(arr.shape, lambda *_: zeros, pipeline_mode=pl.Buffered(1))
    return pl.BlockSpec(arr.shape, lambda *_: zeros)


def _slide_windows(step, per_step, new_refs, old_refs, out_refs):
    lane = lax.broadcasted_iota(jnp.int32, (D_KV, WINDOW), 1)
    shift = jnp.where(step == 0, 0, LANES - step * per_step)
    for new_ref, old_ref, out_ref in zip(new_refs, old_refs, out_refs):
        for l in range(DEPTH):
            new_cols = pltpu.roll(new_ref[l], shift, 1)
            for s in range(per_step):
                slid = pltpu.roll(old_ref[l, s].reshape(D_KV, WINDOW), WINDOW - 1, 1)
                newest = jnp.broadcast_to(new_cols[:, s:s + 1], (D_KV, WINDOW))
                out_ref[l, s] = jnp.where(lane == WINDOW - 1, newest, slid).reshape(
                    N_KV_HEADS, HEAD_DIM, WINDOW)


def _prompt_kernel(layer, slide, *refs):
    def sub_tile(s, carry):
        _prompt_tile(layer, slide, s, refs)
        return carry
    lax.fori_loop(0, SUB, sub_tile, 0)


def _prompt_tile(layer, slide, s, refs):
    n_in = 18 + (4 if slide else 0)
    n_out = 4 + (2 if slide else 0)
    (sinks_ref, x_ref, cos_ref, sin_ref, w_in_ref, bg_ref, poolw_ref, pscale_ref,
     slng_ref, slnb_ref, sguw_ref, sgub_ref, wpa_ref, wpb_ref, wpc_ref, wout_ref,
     lng_ref, lnb_ref) = refs[:18]
    y_ref, pool_out_ref, k_out_ref, v_out_ref = refs[n_in:n_in + 4]
    (h_ref, ext_ref, s2_ref, s4_ref, s8_ref, qs_ref,
     ka_ref, kb_ref, kc_ref, kd_ref, va_ref, vb_ref, vc_ref, vd_ref,
     ain_ref, bin_ref, cin_ref, mg_ref, gate_ref, klast_ref, vlast_ref) = refs[n_in + n_out:]
    tile_rows = pl.ds(pl.multiple_of(s * TM, TM), TM)
    x_ref, y_ref, cos_ref, sin_ref = (r.at[tile_rows] for r in (x_ref, y_ref, cos_ref, sin_ref))
    i = pl.program_id(1) * SUB + s
    last = pl.num_programs(1) * SUB - 1
    kv_refs = (ka_ref, kb_ref, kc_ref, kd_ref, va_ref, vb_ref, vc_ref, vd_ref)
    row = lambda ref: ref[layer:layer + 1, :]

    @pl.when(i == 0)
    def _():
        ext_ref[0:HIST, :] = jnp.zeros((HIST, D_POOL), f32)
        for r in kv_refs:
            r[0:BLOCK, :] = jnp.zeros((BLOCK, LANES), bf16)

    xb = x_ref[...].astype(bf16)
    half = OFF_G // 2
    h_ref[:, 0:half] = _dot(xb, w_in_ref[:, 0:half])
    h_ref[:, half:OFF_G] = _dot(xb, w_in_ref[:, half:OFF_G])

    if slide:
        step = pl.program_id(0) * (last + 1) + i
        per_step = refs[20].shape[1] // SUB
        mine = lambda r: r.at[:, pl.ds(s * per_step, per_step)]
        _slide_windows(step, per_step, refs[18:20], [mine(r) for r in refs[20:22]],
                       [mine(r) for r in refs[n_in + 4:n_in + 6]])

    xa = h_ref[:, OFF_XA:OFF_XA + D_POOL]
    ext_ref[HIST:HIST + TM, :] = xa
    n = HIST + TM
    s2_ref[8:n, :] = ext_ref[8:n, :] + ext_ref[7:n - 1, :]
    s4_ref[16:n, :] = s2_ref[16:n, :] + s2_ref[14:n - 2, :]
    s8_ref[24:n, :] = s4_ref[24:n, :] + s4_ref[20:n - 4, :]
    w16 = s8_ref[HIST:n, :] + s8_ref[HIST - 8:n - 8, :]
    lane_p = lax.broadcasted_iota(jnp.int32, (TM, D_POOL), 1)
    row_p = lax.broadcasted_iota(jnp.int32, (TM, D_POOL), 0)
    win = _group_select(lane_p, s2_ref[HIST:n, :], s4_ref[HIST:n, :], s8_ref[HIST:n, :], w16)
    width = _group_select(lane_p, POOL_WINDOWS[0], POOL_WINDOWS[1], POOL_WINDOWS[2], POOL_WINDOWS[3])
    cnt = jnp.minimum(row_p + (i * TM + 1), width).astype(f32)
    pooled = win / cnt - xa
    ya = _dot(pooled.astype(bf16), poolw_ref[...]) * row(pscale_ref)
    za = h_ref[:, OFF_ZA:OFF_ZA + D_POOL]
    ain_ref[...] = (ya * _silu(za)).astype(bf16)
    ext_ref[HIST - 16:HIST, :] = ext_ref[n - 16:n, :]

    vn = _layer_norm(h_ref[:, OFF_V:OFF_V + D_SGU], row(slng_ref), row(slnb_ref)).astype(bf16)
    wr = lax.broadcasted_iota(jnp.int32, (N_SGU_GROUPS * CHUNK, CHUNK), 0) % CHUNK
    wc = lax.broadcasted_iota(jnp.int32, (N_SGU_GROUPS * CHUNK, CHUNK), 1)
    w_s = jnp.where(wc <= wr, sguw_ref[...], 0.0).astype(bf16)
    lane_c = lax.broadcasted_iota(jnp.int32, (CHUNK, D_SGU), 1)
    for j in range(NBLK):
        rows = slice(j * CHUNK, (j + 1) * CHUNK)
        r = _dot(w_s, vn[rows, :])
        s = _group_select(lane_c, r[0:CHUNK], r[CHUNK:2 * CHUNK], r[2 * CHUNK:3 * CHUNK],
                          r[3 * CHUNK:4 * CHUNK]) + sgub_ref[...]
        yb = h_ref[rows, OFF_U:OFF_U + D_SGU] * s
        bin_ref[rows, :] = (yb * _silu(h_ref[rows, OFF_ZB:OFF_ZB + D_SGU])).astype(bf16)

    cos = cos_ref[...]
    sin = sin_ref[...]
    for c in range(D_ATTN // LANES):
        qc = _rope128(h_ref[:, OFF_Q + c * LANES:OFF_Q + (c + 1) * LANES], cos, sin)
        qs_ref[:, c * LANES:(c + 1) * LANES] = (qc * SCALE).astype(bf16)
    kr = _rope128(h_ref[:, OFF_K:OFF_K + D_KV], cos, sin)
    vv = h_ref[:, OFF_VV:OFF_VV + D_KV]
    klast_ref[...] = kr[TM - WINDOW:TM, :]
    vlast_ref[...] = vv[TM - WINDOW:TM, :]
    lane_k = lax.broadcasted_iota(jnp.int32, (TM, LANES), 1)
    lo = lane_k < HEAD_DIM
    for src, (a_ref, b_ref, c_ref, d_ref) in ((kr, kv_refs[0:4]), (vv, kv_refs[4:8])):
        sw = pltpu.roll(src, HEAD_DIM, 1)
        a_ref[BLOCK:BLOCK + TM, :] = jnp.where(lo, src, 0.0).astype(bf16)
        b_ref[BLOCK:BLOCK + TM, :] = jnp.where(lo, 0.0, src).astype(bf16)
        c_ref[BLOCK:BLOCK + TM, :] = jnp.where(lo, sw, 0.0).astype(bf16)
        d_ref[BLOCK:BLOCK + TM, :] = jnp.where(lo, 0.0, sw).astype(bf16)

    qrow = lax.broadcasted_iota(jnp.int32, (2 * BLOCK, 2 * BLOCK), 0) % BLOCK
    kcol = lax.broadcasted_iota(jnp.int32, (2 * BLOCK, 2 * BLOCK), 1)
    band = (kcol >= qrow) & (kcol <= qrow + WINDOW)
    band_first = band & (kcol >= jnp.where(i > 0, 0, BLOCK))
    top = lax.broadcasted_iota(jnp.int32, (2 * BLOCK, 1), 0) < BLOCK
    n_gate = N_BRANCHES * D_MODEL // NCHUNK
    n_unit = NBLK * N_KV_HEADS
    gate_sched = [range(u * n_gate // n_unit, (u + 1) * n_gate // n_unit) for u in range(n_unit)]

    def scores(u):
        j, kv = divmod(u, N_KV_HEADS)
        rows = slice(j * BLOCK, (j + 1) * BLOCK)
        keys = slice(j * BLOCK, j * BLOCK + 2 * BLOCK)
        c0 = kv * Q_PER_KV * HEAD_DIM
        qst = jnp.concatenate([qs_ref[rows, c0:c0 + LANES], qs_ref[rows, c0 + LANES:c0 + 2 * LANES]], axis=0)
        k_even, k_odd = (ka_ref, kd_ref) if kv == 0 else (kc_ref, kb_ref)
        kcat = jnp.concatenate([k_even[keys, :], k_odd[keys, :]], axis=0)
        return _dot_nt(qst, kcat)

    def attend(u, sc):
        j, kv = divmod(u, N_KV_HEADS)
        rows = slice(j * BLOCK, (j + 1) * BLOCK)
        keys = slice(j * BLOCK, j * BLOCK + 2 * BLOCK)
        allowed = band_first if j == 0 else band
        c0 = kv * Q_PER_KV * HEAD_DIM
        h0 = kv * Q_PER_KV
        v_even, v_odd = (va_ref, vd_ref) if kv == 0 else (vc_ref, vb_ref)
        probs = []
        for par in range(2):
            sink = jnp.where(top, sinks_ref[layer, h0 + par], sinks_ref[layer, h0 + 2 + par])
            sm = jnp.where(allowed, sc[:, par * 2 * BLOCK:(par + 1) * 2 * BLOCK], NEG_INF)
            m = jnp.maximum(jnp.max(sm, axis=-1, keepdims=True), sink)
            p = jnp.exp(sm - m)
            den = jnp.sum(p, axis=-1, keepdims=True) + jnp.exp(sink - m)
            probs.append((p / den).astype(bf16))
        pcat = jnp.concatenate(probs, axis=1)
        vcat = jnp.concatenate([v_even[keys, :], v_odd[keys, :]], axis=0)
        o = _dot(pcat, vcat)
        for pr in range(2):
            cols = slice(c0 + pr * LANES, c0 + (pr + 1) * LANES)
            zc = h_ref[rows, OFF_ZC + c0 + pr * LANES:OFF_ZC + c0 + (pr + 1) * LANES]
            cin_ref[rows, cols] = (o[pr * BLOCK:(pr + 1) * BLOCK] * _silu(zc)).astype(bf16)

    sc_next = scores(0)
    for u in range(n_unit):
        sc = sc_next
        for gc in gate_sched[u]:
            gcols = slice(gc * NCHUNK, (gc + 1) * NCHUNK)
            gate_ref[:, gcols] = _dot(xb, w_in_ref[:, OFF_G + gc * NCHUNK:OFF_G + (gc + 1) * NCHUNK])
        if u + 1 < n_unit:
            sc_next = scores(u + 1)
        attend(u, sc)

    for r in kv_refs:
        r[0:BLOCK, :] = r[TM:TM + BLOCK, :]

    for c in range(D_MODEL // NCHUNK):
        cols = slice(c * NCHUNK, (c + 1) * NCHUNK)
        acc = None
        for br, (in_ref, wp_ref) in enumerate(((ain_ref, wpa_ref), (bin_ref, wpb_ref), (cin_ref, wpc_ref))):
            g0 = br * D_MODEL + c * NCHUNK
            term = _gate2(gate_ref[:, g0:g0 + NCHUNK], bg_ref[br:br + 1, cols]) * _dot(in_ref[...], wp_ref[:, cols])
            acc = term if acc is None else acc + term
        mg_ref[:, cols] = acc.astype(bf16)

    for r0, r1 in OUT_CHUNKS:
        rows = slice(r0, r1)
        out = _dot(mg_ref[rows, :], wout_ref[...])
        y_ref[rows, :] = _layer_norm(ALPHA * x_ref[rows, :] + out, row(lng_ref), row(lnb_ref))

    @pl.when(i == last)
    def _():
        pool_out_ref[...] = ext_ref[n - POOL_BUF:n, :]
        k_out_ref[...] = klast_ref[...].T
        v_out_ref[...] = vlast_ref[...].T


def _prompt_layer(layer, x, cos, sin, sinks, w_in, b_gate, poolw, pscale, slng, slnb, sguw, sgub,
                  wpa, wpb, wpc, wout, lng, lnb, slide_args=None):
    B, L, _ = x.shape
    rows = TM * SUB
    nt = L // rows
    grid = (B, nt)
    row_spec = lambda w: pl.BlockSpec((rows, w), lambda b, i: (i, 0))
    lspec = lambda a: _layer_spec(a, layer, single_buffer=True)
    wspec = lambda a: _full_spec(a, single_buffer=True)
    in_specs = [
        pl.BlockSpec(memory_space=pltpu.SMEM),
        pl.BlockSpec((None, rows, D_MODEL), lambda b, i: (b, i, 0)),
        row_spec(LANES), row_spec(LANES),
        wspec(w_in), lspec(b_gate), lspec(poolw), _full_spec(pscale), _full_spec(slng), _full_spec(slnb),
        lspec(sguw), lspec(sgub), wspec(wpa), wspec(wpb), wspec(wpc), wspec(wout),
        _full_spec(lng), _full_spec(lnb),
    ]
    out_shape = (
        jax.ShapeDtypeStruct((B, L, D_MODEL), f32),
        jax.ShapeDtypeStruct((B, POOL_BUF, D_POOL), f32),
        jax.ShapeDtypeStruct((B, D_KV, WINDOW), f32),
        jax.ShapeDtypeStruct((B, D_KV, WINDOW), f32),
    )
    out_specs = (
        pl.BlockSpec((None, rows, D_MODEL), lambda b, i: (b, i, 0)),
        pl.BlockSpec((None, POOL_BUF, D_POOL), lambda b, i: (b, 0, 0)),
        pl.BlockSpec((None, D_KV, WINDOW), lambda b, i: (b, 0, 0)),
        pl.BlockSpec((None, D_KV, WINDOW), lambda b, i: (b, 0, 0)),
    )
    args = [sinks, x, cos, sin, w_in, b_gate, poolw, pscale, slng, slnb, sguw, sgub,
            wpa, wpb, wpc, wout, lng, lnb]
    if slide_args is not None:
        new_k, new_v, cache_k, cache_v = slide_args
        depth, nb = cache_k.shape[:2]
        per_step = nb // (B * nt)
        assert per_step * B * nt == nb and depth == DEPTH
        cache_spec = pl.BlockSpec((depth, per_step) + cache_k.shape[2:], lambda b, i: (0, b * nt + i, 0, 0, 0))
        in_specs += [_full_spec(new_k), _full_spec(new_v), cache_spec, cache_spec]
        args += [new_k, new_v, cache_k, cache_v]
        out_shape += (jax.ShapeDtypeStruct(cache_k.shape, f32), jax.ShapeDtypeStruct(cache_v.shape, f32))
        out_specs += (cache_spec, cache_spec)
    kv_scratch = [pltpu.VMEM((BLOCK + TM, LANES), bf16) for _ in range(8)]
    scratch = [
        pltpu.VMEM((TM, OFF_G), f32),
        pltpu.VMEM((HIST + TM, D_POOL), f32),
        pltpu.VMEM((HIST + TM, D_POOL), f32),
        pltpu.VMEM((HIST + TM, D_POOL), f32),
        pltpu.VMEM((HIST + TM, D_POOL), f32),
        pltpu.VMEM((TM, D_ATTN), bf16),
        *kv_scratch,
        pltpu.VMEM((TM, D_POOL), bf16),
        pltpu.VMEM((TM, D_SGU), bf16),
        pltpu.VMEM((TM, D_ATTN), bf16),
        pltpu.VMEM((TM, D_MODEL), bf16),
        pltpu.VMEM((TM, N_BRANCHES * D_MODEL), f32),
        pltpu.VMEM((WINDOW, D_KV), f32),
        pltpu.VMEM((WINDOW, D_KV), f32),
    ]
    return pl.pallas_call(
        functools.partial(_prompt_kernel, layer, slide_args is not None),
        out_shape=out_shape,
        grid=grid,
        in_specs=in_specs,
        out_specs=out_specs,
        scratch_shapes=scratch,
        compiler_params=pltpu.CompilerParams(
            dimension_semantics=("arbitrary", "arbitrary"),
            vmem_limit_bytes=VMEM_LIMIT),
        name="prompt_layer",
    )(*args)


def _sample_proj_kernel(layer, x_ref, cos_ref, sin_ref, w_in_ref, bg_ref, pb_ref, poolw_ref, pscale_ref,
                        slng_ref, slnb_ref, sw0_ref, sb0_ref,
                        wb_ref, q_ref, k_ref, v_ref, kt_ref, vt_ref, pool_ref, vn_ref, ain_ref, bin_ref,
                        szc_ref, g_ref, h_ref):
    j = pl.program_id(0)
    row = lambda ref: ref[layer:layer + 1, :]
    wb_ref[:, 0:OFF_G] = w_in_ref[:, 0:OFF_G].astype(bf16)
    wb_ref[:, OFF_G:D_IN] = (0.5 * w_in_ref[:, OFF_G:D_IN]).astype(bf16)

    @pl.when(j == 0)
    def _():
        h_ref[...] = jnp.zeros(h_ref.shape, f32)

    h_ref[...] += _dot(x_ref[...].astype(bf16), wb_ref[...])

    def hcols(off, width):
        return h_ref[:, off:off + width]

    @pl.when(j == RK_STEPS - 1)
    def _():
        xa = hcols(OFF_XA, D_POOL)
        lane = lax.broadcasted_iota(jnp.int32, xa.shape, 1)
        first_row = _group_select(lane, *(POOL_BUF - (w - 1) for w in POOL_WINDOWS))
        win = xa
        for r in range(POOL_BUF):
            win = win + jnp.where(first_row <= r, pb_ref[r], 0.0)
        width = _group_select(lane, *POOL_WINDOWS).astype(f32)
        pooled = win / width - xa
        ya = _dot(pooled.astype(bf16), poolw_ref[...]) * row(pscale_ref)
        ain_ref[...] = (ya * _silu(hcols(OFF_ZA, D_POOL))).astype(bf16)
        for r in range(POOL_BUF - 1):
            pool_ref[r] = pb_ref[r + 1]
        pool_ref[POOL_BUF - 1] = xa
        vn = _layer_norm(hcols(OFF_V, D_SGU), row(slng_ref), row(slnb_ref))
        vn_ref[...] = vn
        yb = hcols(OFF_U, D_SGU) * (row(sw0_ref) * vn + row(sb0_ref))
        bin_ref[...] = (yb * _silu(hcols(OFF_ZB, D_SGU))).astype(bf16)
        cos = cos_ref[...]
        sin = sin_ref[...]
        for c in range(D_ATTN // LANES):
            q_ref[:, c * LANES:(c + 1) * LANES] = _rope128(hcols(OFF_Q + c * LANES, LANES), cos, sin) * SCALE
        kr = _rope128(hcols(OFF_K, D_KV), cos, sin)
        vv = hcols(OFF_VV, D_KV)
        k_ref[...] = kr
        v_ref[...] = vv
        kt_ref[...] = kr.T
        vt_ref[...] = vv.T
        szc_ref[...] = _silu(hcols(OFF_ZC, D_ATTN))
        for br in range(N_BRANCHES):
            cols = slice(br * D_MODEL, (br + 1) * D_MODEL)
            g_ref[:, cols] = _gate2(hcols(OFF_G + br * D_MODEL, D_MODEL), bg_ref[br:br + 1, :])


def _sample_proj(layer, x, cos, sin, w_in, b_gate, pool_t, poolw, pscale, slng, slnb, sw0, sb0):
    nb = x.shape[0]
    out_shape = (
        jax.ShapeDtypeStruct((D_MODEL, D_IN), bf16),
        jax.ShapeDtypeStruct((nb, D_ATTN), f32),
        jax.ShapeDtypeStruct((nb, D_KV), f32),
        jax.ShapeDtypeStruct((nb, D_KV), f32),
        jax.ShapeDtypeStruct((D_KV, nb), f32),
        jax.ShapeDtypeStruct((D_KV, nb), f32),
        jax.ShapeDtypeStruct((POOL_BUF, nb, D_POOL), f32),
        jax.ShapeDtypeStruct((nb, D_SGU), f32),
        jax.ShapeDtypeStruct((nb, D_POOL), bf16),
        jax.ShapeDtypeStruct((nb, D_SGU), bf16),
        jax.ShapeDtypeStruct((nb, D_ATTN), f32),
        jax.ShapeDtypeStruct((nb, N_BRANCHES * D_MODEL), f32),
    )
    lspec = lambda a: _layer_spec(a, layer)
    args = (x, cos, sin, w_in, b_gate, pool_t, poolw, pscale, slng, slnb, sw0, sb0)
    in_specs = [pl.BlockSpec((nb, None, RK), lambda j: (0, 0, j)), _full_spec(cos), _full_spec(sin),
                pl.BlockSpec((None, RK, D_IN), lambda j: (layer, j, 0)),
                lspec(b_gate), lspec(pool_t), lspec(poolw), _full_spec(pscale), _full_spec(slng),
                _full_spec(slnb), _full_spec(sw0), _full_spec(sb0)]
    out_specs = (pl.BlockSpec((RK, D_IN), lambda j: (j, 0)),) + tuple(_full_spec(s) for s in out_shape[1:])
    return pl.pallas_call(
        functools.partial(_sample_proj_kernel, layer),
        out_shape=out_shape,
        grid=(RK_STEPS,),
        in_specs=in_specs,
        out_specs=out_specs,
        scratch_shapes=[pltpu.VMEM((nb, D_IN), f32)],
        compiler_params=pltpu.CompilerParams(dimension_semantics=("arbitrary",),
                                             vmem_limit_bytes=VMEM_LIMIT),
        name="sample_proj",
    )(*args)


def _sample_attn_kernel(sink_ref, q_ref, kc_ref, vc_ref, kn_ref, vn_ref, o_ref):
    for kv in range(N_KV_HEADS):
        qb = q_ref[:, kv].astype(bf16)
        kn = kn_ref[:, kv]
        vn = vn_ref[:, kv]
        sink = sink_ref[kv][None]
        s = jnp.einsum('bgd,bdw->bgw', qb, kc_ref[:, kv].astype(bf16), preferred_element_type=f32)
        s_new = jnp.sum(qb.astype(f32) * kn.astype(bf16).astype(f32), axis=-1, keepdims=True)
        m = jnp.maximum(jnp.maximum(jnp.max(s, axis=-1, keepdims=True), s_new), sink)
        p = jnp.exp(s - m)
        p_new = jnp.exp(s_new - m)
        den = jnp.sum(p, axis=-1, keepdims=True) + p_new + jnp.exp(sink - m)
        o = jnp.einsum('bgw,bdw->bgd', (p / den).astype(bf16), vc_ref[:, kv].astype(bf16),
                       preferred_element_type=f32)
        o_ref[:, kv] = o + (p_new / den) * vn


def _sample_attn(layer, sink, q4, kc, vc, kn, vn):
    nb = q4.shape[0]
    cache_spec = pl.BlockSpec((None, SB, N_KV_HEADS, HEAD_DIM, WINDOW), lambda b: (layer, b, 0, 0, 0))
    blk = lambda s2, s3: pl.BlockSpec((SB, N_KV_HEADS, s2, s3), lambda b: (b, 0, 0, 0))
    return pl.pallas_call(
        _sample_attn_kernel,
        out_shape=jax.ShapeDtypeStruct((nb, N_KV_HEADS, Q_PER_KV, HEAD_DIM), f32),
        grid=(nb // SB,),
        in_specs=[_layer_spec(sink, layer), blk(Q_PER_KV, HEAD_DIM), cache_spec, cache_spec,
                  blk(1, HEAD_DIM), blk(1, HEAD_DIM)],
        out_specs=blk(Q_PER_KV, HEAD_DIM),
        compiler_params=pltpu.CompilerParams(dimension_semantics=("arbitrary",),
                                             vmem_limit_bytes=VMEM_LIMIT),
        name="sample_attn",
    )(sink, q4, kc, vc, kn, vn)


def _sample_merge_kernel(layer, x_ref, ain_ref, bin_ref, yc_ref, szc_ref, g_ref, wpa_ref, wpb_ref, wpc_ref,
                         wout_ref, lng_ref, lnb_ref, y_ref, wpa_b_ref, wpb_b_ref, wpc_b_ref, wout_b_ref):
    row = lambda ref: ref[layer:layer + 1, :]
    wpa = (0.5 * wpa_ref[...]).astype(bf16)
    wpb = (0.5 * wpb_ref[...]).astype(bf16)
    wpc = (0.5 * wpc_ref[...]).astype(bf16)
    wout = wout_ref[...].astype(bf16)
    wpa_b_ref[...] = wpa
    wpb_b_ref[...] = wpb
    wpc_b_ref[...] = wpc
    wout_b_ref[...] = wout
    cin = (yc_ref[...] * szc_ref[...]).astype(bf16)
    merged = (g_ref[:, 0:D_MODEL] * _dot(ain_ref[...], wpa)
              + g_ref[:, D_MODEL:2 * D_MODEL] * _dot(bin_ref[...], wpb)
              + g_ref[:, 2 * D_MODEL:3 * D_MODEL] * _dot(cin, wpc))
    out = _dot(merged.astype(bf16), wout)
    y_ref[...] = _layer_norm(ALPHA * x_ref[...] + out, row(lng_ref), row(lnb_ref))


def _sample_merge(layer, x, ain, bin_, yc, szc, g, wpa, wpb, wpc, wout, lng, lnb):
    lspec = lambda a: _layer_spec(a, layer)
    nb = x.shape[0]
    x_spec = pl.BlockSpec((nb, None, D_MODEL), lambda i: (0, 0, 0))
    out_shape = (jax.ShapeDtypeStruct(x.shape, f32),) + tuple(
        jax.ShapeDtypeStruct(w.shape[1:], bf16) for w in (wpa, wpb, wpc, wout))
    return pl.pallas_call(
        functools.partial(_sample_merge_kernel, layer),
        out_shape=out_shape,
        grid=(1,),
        in_specs=[x_spec, _full_spec(ain), _full_spec(bin_), _full_spec(yc), _full_spec(szc),
                  _full_spec(g), lspec(wpa), lspec(wpb), lspec(wpc), lspec(wout), _full_spec(lng),
                  _full_spec(lnb)],
        out_specs=(x_spec,) + tuple(_full_spec(s) for s in out_shape[1:]),
        compiler_params=pltpu.CompilerParams(dimension_semantics=("arbitrary",),
                                             vmem_limit_bytes=VMEM_LIMIT),
        name="sample_merge",
    )(x, ain, bin_, yc, szc, g, wpa, wpb, wpc, wout, lng, lnb)


def _rope_tables(positions):
    halfd = HEAD_DIM // 2
    inv = ROPE_THETA ** (-np.arange(halfd, dtype=np.float64) / halfd)
    ang = np.asarray(positions, dtype=np.float64)[:, None] * inv[None, :]
    cos = np.tile(np.cos(ang), (1, LANES // halfd))
    sin = np.tile(np.concatenate([-np.sin(ang), np.sin(ang)], axis=1), (1, LANES // HEAD_DIM))
    return jnp.asarray(cos, f32), jnp.asarray(sin, f32)


def _block_diag(w):
    nl, g, c, _ = w.shape
    eye = jnp.eye(g, dtype=w.dtype)
    return (eye[None, :, None, :, None] * w[:, :, :, None, :]).reshape(nl, g * c, g * c)


def kernel(x_prompt, x_sample, state_pool, cache_k_win, cache_v_win, w_in, b_gate, pool_w, pool_scale, sgu_ln_g, sgu_ln_b, sgu_w, sgu_b, attn_sinks, w_proj_a, w_proj_b, w_proj_c, w_out, ln_g, ln_b):
    B, L, _ = x_prompt.shape
    nb = x_sample.shape[0]
    cos_p, sin_p = _rope_tables(np.arange(L))
    cos_s, sin_s = _rope_tables(np.array([PAST_LEN]))

    poolw = _block_diag(pool_w).astype(bf16)
    sguw = sgu_w.reshape(DEPTH, N_SGU_GROUPS * CHUNK, CHUNK)
    sgub = jnp.repeat(jnp.swapaxes(sgu_b, 1, 2), POOL_GC, axis=2)
    sw0 = jnp.repeat(sgu_w[:, :, 0, 0], POOL_GC, axis=1)
    sb0 = jnp.repeat(sgu_b[:, :, 0], POOL_GC, axis=1)
    sink4 = attn_sinks.reshape(DEPTH, N_KV_HEADS, Q_PER_KV, 1)
    kc_t = jnp.transpose(cache_k_win, (0, 1, 3, 4, 2))
    vc_t = jnp.transpose(cache_v_win, (0, 1, 3, 4, 2))
    pool_t = jnp.transpose(state_pool, (0, 2, 1, 3))

    y_s = x_sample
    pool_s, chunk_v, kts, vts, weights = ([] for _ in range(5))
    for l in range(DEPTH):
        w_in_b, q, kn, vn, kt, vt, ps, cv, ain, bin_, szc, g = _sample_proj(
            l, y_s, cos_s, sin_s, w_in, b_gate, pool_t, poolw, pool_scale, sgu_ln_g, sgu_ln_b, sw0, sb0)
        o = _sample_attn(
            l, sink4, q.reshape(nb, N_KV_HEADS, Q_PER_KV, HEAD_DIM), kc_t, vc_t,
            kn.reshape(nb, N_KV_HEADS, 1, HEAD_DIM), vn.reshape(nb, N_KV_HEADS, 1, HEAD_DIM))
        y_s, wpa, wpb, wpc, wout = _sample_merge(
            l, y_s, ain, bin_, o.reshape(nb, D_ATTN), szc, g, w_proj_a, w_proj_b, w_proj_c, w_out, ln_g, ln_b)
        pool_s.append(ps); chunk_v.append(cv); kts.append(kt); vts.append(vt)
        weights.append((w_in_b, wpa, wpb, wpc, wout))

    y_p = x_prompt
    pool_p, k_p, v_p = ([] for _ in range(3))
    for l in range(DEPTH):
        w_in_b, wpa, wpb, wpc, wout = weights[l]
        slide_args = (jnp.stack(kts), jnp.stack(vts), kc_t, vc_t) if l == DEPTH - 1 else None
        y_p, pp, kp, vp, *slid = _prompt_layer(
            l, y_p, cos_p, sin_p, attn_sinks, w_in_b, b_gate, poolw, pool_scale, sgu_ln_g, sgu_ln_b, sguw,
            sgub, wpa, wpb, wpc, wout, ln_g, ln_b, slide_args=slide_args)
        pool_p.append(pp); k_p.append(kp); v_p.append(vp)
    k_s, v_s = slid

    to_cache = lambda a: jnp.transpose(a, (0, 1, 4, 2, 3))
    prompt_cache = lambda lst: to_cache(jnp.stack(lst).reshape(DEPTH, B, N_KV_HEADS, HEAD_DIM, WINDOW))
    return (y_p, y_s,
            jnp.stack(pool_p), prompt_cache(k_p), prompt_cache(v_p),
            jnp.transpose(jnp.stack(pool_s), (0, 2, 1, 3)), to_cache(k_s), to_cache(v_s),
            jnp.stack(chunk_v).reshape(DEPTH, nb, 1, D_SGU))
```

```python
import functools

import numpy as np
import jax
import jax.numpy as jnp
from jax import lax
from jax.experimental import pallas as pl
from jax.experimental.pallas import tpu as pltpu

D_MODEL = 1024
DEPTH = 2
PAST_LEN = 8192
D_POOL = 256
POOL_WINDOWS = (2, 4, 8, 16)
POOL_GC = 64
POOL_BUF = 15
D_SGU = 256
CHUNK = 128
N_SGU_GROUPS = 4
HEAD_DIM = 64
N_HEADS = 8
N_KV_HEADS = 2
Q_PER_KV = 4
D_ATTN = 512
D_KV = 128
WINDOW = 128
BLOCK = 128
ROPE_THETA = 10000.0
N_BRANCHES = 3
D_IN = 2 * D_POOL + 3 * D_SGU + 2 * D_ATTN + 2 * D_KV + N_BRANCHES * D_MODEL
ALPHA = (2.0 * DEPTH) ** 0.25
LN_EPS = 1e-5
NEG_INF = -1e30
SCALE = HEAD_DIM ** -0.5

OFF_XA, OFF_ZA, OFF_U, OFF_V, OFF_ZB = 0, 256, 512, 768, 1024
OFF_Q, OFF_K, OFF_VV, OFF_ZC, OFF_G = 1280, 1792, 1920, 2048, 2560

LANES = 128
TM = 512
SUB = 2
NBLK = TM // BLOCK
HIST = 32
NCHUNK = 256
OUT_CHUNKS = ((0, 256), (256, 512))
VMEM_LIMIT = 56 * 1024 * 1024
SB = 64
RK = 512
RK_STEPS = D_MODEL // RK

bf16 = jnp.bfloat16
f32 = jnp.float32


def _dot(a, b):
    return jnp.dot(a, b, preferred_element_type=f32)


def _dot_nt(a, b):
    return lax.dot_general(a, b, (((1,), (1,)), ((), ())), preferred_element_type=f32)


def _sigmoid(z):
    return 0.5 * jnp.tanh(0.5 * z) + 0.5


def _silu(z):
    return z * _sigmoid(z)


def _gate2(half_pre, bias):
    return jnp.tanh(half_pre + 0.5 * bias) + 1.0


def _layer_norm(x, g, b):
    mu = jnp.mean(x, axis=-1, keepdims=True)
    xc = x - mu
    var = jnp.mean(xc * xc, axis=-1, keepdims=True)
    return xc * lax.rsqrt(var + LN_EPS) * g + b


def _rope128(x, cos, sin_signed):
    lane = lax.broadcasted_iota(jnp.int32, x.shape, 1)
    first_half = (lane % HEAD_DIM) < (HEAD_DIM // 2)
    partner = jnp.where(first_half,
                        pltpu.roll(x, LANES - HEAD_DIM // 2, 1),
                        pltpu.roll(x, HEAD_DIM // 2, 1))
    return x * cos + partner * sin_signed


def _group_select(lane, a0, a1, a2, a3):
    return jnp.where(lane < 64, a0, jnp.where(lane < 128, a1, jnp.where(lane < 192, a2, a3)))


def _layer_spec(arr, layer, single_buffer=False):
    block = (None,) + arr.shape[1:]
    zeros = (0,) * (arr.ndim - 1)
    index_map = lambda *_: (layer,) + zeros
    if single_buffer:
        return pl.BlockSpec(block, index_map, pipeline_mode=pl.Buffered(1))
    return pl.BlockSpec(block, index_map)


def _full_spec(arr, single_buffer=False):
    zeros = (0,) * len(arr.shape)
    if single_buffer:
        return pl.BlockSpec(arr.shape, lambda *_: zeros, pipeline_mode=pl.Buffered(1))
    return pl.BlockSpec(arr.shape, lambda *_: zeros)


def _slide_windows(step, per_step, new_refs, old_refs, out_refs):
    lane = lax.broadcasted_iota(jnp.int32, (D_KV, WINDOW), 1)
    shift = jnp.where(step == 0, 0, LANES - step * per_step)
    for new_ref, old_ref, out_ref in zip(new_refs, old_refs, out_refs):
        for l in range(DEPTH):
            new_cols = pltpu.roll(new_ref[l], shift, 1)
            for s in range(per_step):
                slid = pltpu.roll(old_ref[l, s].reshape(D_KV, WINDOW), WINDOW - 1, 1)
                newest = jnp.broadcast_to(new_cols[:, s:s + 1], (D_KV, WINDOW))
                out_ref[l, s] = jnp.where(lane == WINDOW - 1, newest, slid).reshape(
                    N_KV_HEADS, HEAD_DIM, WINDOW)


def _prompt_kernel(layer, slide, *refs):
    def sub_tile(s, carry):
        _prompt_tile(layer, slide, s, refs)
        return carry
    lax.fori_loop(0, SUB, sub_tile, 0)


def _prompt_tile(layer, slide, s, refs):
    n_in = 18 + (4 if slide else 0)
    n_out = 4 + (2 if slide else 0)
    (sinks_ref, x_ref, cos_ref, sin_ref, w_in_ref, bg_ref, poolw_ref, pscale_ref,
     slng_ref, slnb_ref, sguw_ref, sgub_ref, wpa_ref, wpb_ref, wpc_ref, wout_ref,
     lng_ref, lnb_ref) = refs[:18]
    y_ref, pool_out_ref, k_out_ref, v_out_ref = refs[n_in:n_in + 4]
    (h_ref, ext_ref, s2_ref, s4_ref, s8_ref, qs_ref,
     ka_ref, kb_ref, kc_ref, kd_ref, va_ref, vb_ref, vc_ref, vd_ref,
     ain_ref, bin_ref, cin_ref, mg_ref, gate_ref, klast_ref, vlast_ref) = refs[n_in + n_out:]
    tile_rows = pl.ds(pl.multiple_of(s * TM, TM), TM)
    x_ref, y_ref, cos_ref, sin_ref = (r.at[tile_rows] for r in (x_ref, y_ref, cos_ref, sin_ref))
    i = pl.program_id(1) * SUB + s
    last = pl.num_programs(1) * SUB - 1
    kv_refs = (ka_ref, kb_ref, kc_ref, kd_ref, va_ref, vb_ref, vc_ref, vd_ref)
    row = lambda ref: ref[layer:layer + 1, :]

    @pl.when(i == 0)
    def _():
        ext_ref[0:HIST, :] = jnp.zeros((HIST, D_POOL), f32)
        for r in kv_refs:
            r[0:BLOCK, :] = jnp.zeros((BLOCK, LANES), bf16)

    xb = x_ref[...].astype(bf16)
    half = OFF_G // 2
    h_ref[:, 0:half] = _dot(xb, w_in_ref[:, 0:half])
    h_ref[:, half:OFF_G] = _dot(xb, w_in_ref[:, half:OFF_G])

    if slide:
        step = pl.program_id(0) * (last + 1) + i
        per_step = refs[20].shape[1] // SUB
        mine = lambda r: r.at[:, pl.ds(s * per_step, per_step)]
        _slide_windows(step, per_step, refs[18:20], [mine(r) for r in refs[20:22]],
                       [mine(r) for r in refs[n_in + 4:n_in + 6]])

    xa = h_ref[:, OFF_XA:OFF_XA + D_POOL]
    ext_ref[HIST:HIST + TM, :] = xa
    n = HIST + TM
    s2_ref[8:n, :] = ext_ref[8:n, :] + ext_ref[7:n - 1, :]
    s4_ref[16:n, :] = s2_ref[16:n, :] + s2_ref[14:n - 2, :]
    s8_ref[24:n, :] = s4_ref[24:n, :] + s4_ref[20:n - 4, :]
    w16 = s8_ref[HIST:n, :] + s8_ref[HIST - 8:n - 8, :]
    lane_p = lax.broadcasted_iota(jnp.int32, (TM, D_POOL), 1)
    row_p = lax.broadcasted_iota(jnp.int32, (TM, D_POOL), 0)
    win = _group_select(lane_p, s2_ref[HIST:n, :], s4_ref[HIST:n, :], s8_ref[HIST:n, :], w16)
    width = _group_select(lane_p, POOL_WINDOWS[0], POOL_WINDOWS[1], POOL_WINDOWS[2], POOL_WINDOWS[3])
    cnt = jnp.minimum(row_p + (i * TM + 1), width).astype(f32)
    pooled = win / cnt - xa
    ya = _dot(pooled.astype(bf16), poolw_ref[...]) * row(pscale_ref)
    za = h_ref[:, OFF_ZA:OFF_ZA + D_POOL]
    ain_ref[...] = (ya * _silu(za)).astype(bf16)
    ext_ref[HIST - 16:HIST, :] = ext_ref[n - 16:n, :]

    vn = _layer_norm(h_ref[:, OFF_V:OFF_V + D_SGU], row(slng_ref), row(slnb_ref)).astype(bf16)
    wr = lax.broadcasted_iota(jnp.int32, (N_SGU_GROUPS * CHUNK, CHUNK), 0) % CHUNK
    wc = lax.broadcasted_iota(jnp.int32, (N_SGU_GROUPS * CHUNK, CHUNK), 1)
    w_s = jnp.where(wc <= wr, sguw_ref[...], 0.0).astype(bf16)
    lane_c = lax.broadcasted_iota(jnp.int32, (CHUNK, D_SGU), 1)
    for j in range(NBLK):
        rows = slice(j * CHUNK, (j + 1) * CHUNK)
        r = _dot(w_s, vn[rows, :])
        s = _group_select(lane_c, r[0:CHUNK], r[CHUNK:2 * CHUNK], r[2 * CHUNK:3 * CHUNK],
                          r[3 * CHUNK:4 * CHUNK]) + sgub_ref[...]
        yb = h_ref[rows, OFF_U:OFF_U + D_SGU] * s
        bin_ref[rows, :] = (yb * _silu(h_ref[rows, OFF_ZB:OFF_ZB + D_SGU])).astype(bf16)

    cos = cos_ref[...]
    sin = sin_ref[...]
    for c in range(D_ATTN // LANES):
        qc = _rope128(h_ref[:, OFF_Q + c * LANES:OFF_Q + (c + 1) * LANES], cos, sin)
        qs_ref[:, c * LANES:(c + 1) * LANES] = (qc * SCALE).astype(bf16)
    kr = _rope128(h_ref[:, OFF_K:OFF_K + D_KV], cos, sin)
    vv = h_ref[:, OFF_VV:OFF_VV + D_KV]
    klast_ref[...] = kr[TM - WINDOW:TM, :]
    vlast_ref[...] = vv[TM - WINDOW:TM, :]
    lane_k = lax.broadcasted_iota(jnp.int32, (TM, LANES), 1)
    lo = lane_k < HEAD_DIM
    for src, (a_ref, b_ref, c_ref, d_ref) in ((kr, kv_refs[0:4]), (vv, kv_refs[4:8])):
        sw = pltpu.roll(src, HEAD_DIM, 1)
        a_ref[BLOCK:BLOCK + TM, :] = jnp.where(lo, src, 0.0).astype(bf16)
        b_ref[BLOCK:BLOCK + TM, :] = jnp.where(lo, 0.0, src).astype(bf16)
        c_ref[BLOCK:BLOCK + TM, :] = jnp.where(lo, sw, 0.0).astype(bf16)
        d_ref[BLOCK:BLOCK + TM, :] = jnp.where(lo, 0.0, sw).astype(bf16)

    qrow = lax.broadcasted_iota(jnp.int32, (2 * BLOCK, 2 * BLOCK), 0) % BLOCK
    kcol = lax.broadcasted_iota(jnp.int32, (2 * BLOCK, 2 * BLOCK), 1)
    band = (kcol >= qrow) & (kcol <= qrow + WINDOW)
    band_first = band & (kcol >= jnp.where(i > 0, 0, BLOCK))
    top = lax.broadcasted_iota(jnp.int32, (2 * BLOCK, 1), 0) < BLOCK
    n_gate = N_BRANCHES * D_MODEL // NCHUNK
    n_unit = NBLK * N_KV_HEADS
    gate_sched = [range(u * n_gate // n_unit, (u + 1) * n_gate // n_unit) for u in range(n_unit)]

    def scores(u):
        j, kv = divmod(u, N_KV_HEADS)
        rows = slice(j * BLOCK, (j + 1) * BLOCK)
        keys = slice(j * BLOCK, j * BLOCK + 2 * BLOCK)
        c0 = kv * Q_PER_KV * HEAD_DIM
        qst = jnp.concatenate([qs_ref[rows, c0:c0 + LANES], qs_ref[rows, c0 + LANES:c0 + 2 * LANES]], axis=0)
        k_even, k_odd = (ka_ref, kd_ref) if kv == 0 else (kc_ref, kb_ref)
        kcat = jnp.concatenate([k_even[keys, :], k_odd[keys, :]], axis=0)
        return _dot_nt(qst, kcat)

    def attend(u, sc):
        j, kv = divmod(u, N_KV_HEADS)
        rows = slice(j * BLOCK, (j + 1) * BLOCK)
        keys = slice(j * BLOCK, j * BLOCK + 2 * BLOCK)
        allowed = band_first if j == 0 else band
        c0 = kv * Q_PER_KV * HEAD_DIM
        h0 = kv * Q_PER_KV
        v_even, v_odd = (va_ref, vd_ref) if kv == 0 else (vc_ref, vb_ref)
        probs = []
        for par in range(2):
            sink = jnp.where(top, sinks_ref[layer, h0 + par], sinks_ref[layer, h0 + 2 + par])
            sm = jnp.where(allowed, sc[:, par * 2 * BLOCK:(par + 1) * 2 * BLOCK], NEG_INF)
            m = jnp.maximum(jnp.max(sm, axis=-1, keepdims=True), sink)
            p = jnp.exp(sm - m)
            den = jnp.sum(p, axis=-1, keepdims=True) + jnp.exp(sink - m)
            probs.append((p / den).astype(bf16))
        pcat = jnp.concatenate(probs, axis=1)
        vcat = jnp.concatenate([v_even[keys, :], v_odd[keys, :]], axis=0)
        o = _dot(pcat, vcat)
        for pr in range(2):
            cols = slice(c0 + pr * LANES, c0 + (pr + 1) * LANES)
            zc = h_ref[rows, OFF_ZC + c0 + pr * LANES:OFF_ZC + c0 + (pr + 1) * LANES]
            cin_ref[rows, cols] = (o[pr * BLOCK:(pr + 1) * BLOCK] * _silu(zc)).astype(bf16)

    sc_next = scores(0)
    for u in range(n_unit):
        sc = sc_next
        for gc in gate_sched[u]:
            gcols = slice(gc * NCHUNK, (gc + 1) * NCHUNK)
            gate_ref[:, gcols] = _dot(xb, w_in_ref[:, OFF_G + gc * NCHUNK:OFF_G + (gc + 1) * NCHUNK])
        if u + 1 < n_unit:
            sc_next = scores(u + 1)
        attend(u, sc)

    for r in kv_refs:
        r[0:BLOCK, :] = r[TM:TM + BLOCK, :]

    for c in range(D_MODEL // NCHUNK):
        cols = slice(c * NCHUNK, (c + 1) * NCHUNK)
        acc = None
        for br, (in_ref, wp_ref) in enumerate(((ain_ref, wpa_ref), (bin_ref, wpb_ref), (cin_ref, wpc_ref))):
            g0 = br * D_MODEL + c * NCHUNK
            term = _gate2(gate_ref[:, g0:g0 + NCHUNK], bg_ref[br:br + 1, cols]) * _dot(in_ref[...], wp_ref[:, cols])
            acc = term if acc is None else acc + term
        mg_ref[:, cols] = acc.astype(bf16)

    for r0, r1 in OUT_CHUNKS:
        rows = slice(r0, r1)
        out = _dot(mg_ref[rows, :], wout_ref[...])
        y_ref[rows, :] = _layer_norm(ALPHA * x_ref[rows, :] + out, row(lng_ref), row(lnb_ref))

    @pl.when(i == last)
    def _():
        pool_out_ref[...] = ext_ref[n - POOL_BUF:n, :]
        k_out_ref[...] = klast_ref[...].T
        v_out_ref[...] = vlast_ref[...].T


def _prompt_layer(layer, x, cos, sin, sinks, w_in, b_gate, poolw, pscale, slng, slnb, sguw, sgub,
                  wpa, wpb, wpc, wout, lng, lnb, slide_args=None):
    B, L, _ = x.shape
    rows = TM * SUB
    nt = L // rows
    grid = (B, nt)
    row_spec = lambda w: pl.BlockSpec((rows, w), lambda b, i: (i, 0))
    lspec = lambda a: _layer_spec(a, layer, single_buffer=True)
    wspec = lambda a: _full_spec(a, single_buffer=True)
    in_specs = [
        pl.BlockSpec(memory_space=pltpu.SMEM),
        pl.BlockSpec((None, rows, D_MODEL), lambda b, i: (b, i, 0)),
        row_spec(LANES), row_spec(LANES),
        wspec(w_in), lspec(b_gate), lspec(poolw), _full_spec(pscale), _full_spec(slng), _full_spec(slnb),
        lspec(sguw), lspec(sgub), wspec(wpa), wspec(wpb), wspec(wpc), wspec(wout),
        _full_spec(lng), _full_spec(lnb),
    ]
    out_shape = (
        jax.ShapeDtypeStruct((B, L, D_MODEL), f32),
        jax.ShapeDtypeStruct((B, POOL_BUF, D_POOL), f32),
        jax.ShapeDtypeStruct((B, D_KV, WINDOW), f32),
        jax.ShapeDtypeStruct((B, D_KV, WINDOW), f32),
    )
    out_specs = (
        pl.BlockSpec((None, rows, D_MODEL), lambda b, i: (b, i, 0)),
        pl.BlockSpec((None, POOL_BUF, D_POOL), lambda b, i: (b, 0, 0)),
        pl.BlockSpec((None, D_KV, WINDOW), lambda b, i: (b, 0, 0)),
        pl.BlockSpec((None, D_KV, WINDOW), lambda b, i: (b, 0, 0)),
    )
    args = [sinks, x, cos, sin, w_in, b_gate, poolw, pscale, slng, slnb, sguw, sgub,
            wpa, wpb, wpc, wout, lng, lnb]
    if slide_args is not None:
        new_k, new_v, cache_k, cache_v = slide_args
        depth, nb = cache_k.shape[:2]
        per_step = nb // (B * nt)
        assert per_step * B * nt == nb and depth == DEPTH
        cache_spec = pl.BlockSpec((depth, per_step) + cache_k.shape[2:], lambda b, i: (0, b * nt + i, 0, 0, 0))
        in_specs += [_full_spec(new_k), _full_spec(new_v), cache_spec, cache_spec]
        args += [new_k, new_v, cache_k, cache_v]
        out_shape += (jax.ShapeDtypeStruct(cache_k.shape, f32), jax.ShapeDtypeStruct(cache_v.shape, f32))
        out_specs += (cache_spec, cache_spec)
    kv_scratch = [pltpu.VMEM((BLOCK + TM, LANES), bf16) for _ in range(8)]
    scratch = [
        pltpu.VMEM((TM, OFF_G), f32),
        pltpu.VMEM((HIST + TM, D_POOL), f32),
        pltpu.VMEM((HIST + TM, D_POOL), f32),
        pltpu.VMEM((HIST + TM, D_POOL), f32),
        pltpu.VMEM((HIST + TM, D_POOL), f32),
        pltpu.VMEM((TM, D_ATTN), bf16),
        *kv_scratch,
        pltpu.VMEM((TM, D_POOL), bf16),
        pltpu.VMEM((TM, D_SGU), bf16),
        pltpu.VMEM((TM, D_ATTN), bf16),
        pltpu.VMEM((TM, D_MODEL), bf16),
        pltpu.VMEM((TM, N_BRANCHES * D_MODEL), f32),
        pltpu.VMEM((WINDOW, D_KV), f32),
        pltpu.VMEM((WINDOW, D_KV), f32),
    ]
    return pl.pallas_call(
        functools.partial(_prompt_kernel, layer, slide_args is not None),
        out_shape=out_shape,
        grid=grid,
        in_specs=in_specs,
        out_specs=out_specs,
        scratch_shapes=scratch,
        compiler_params=pltpu.CompilerParams(
            dimension_semantics=("arbitrary", "arbitrary"),
            vmem_limit_bytes=VMEM_LIMIT),
        name="prompt_layer",
    )(*args)


def _sample_proj_kernel(layer, x_ref, cos_ref, sin_ref, w_in_ref, bg_ref, pb_ref, poolw_ref, pscale_ref,
                        slng_ref, slnb_ref, sw0_ref, sb0_ref,
                        wb_ref, q_ref, k_ref, v_ref, kt_ref, vt_ref, pool_ref, vn_ref, ain_ref, bin_ref,
                        szc_ref, g_ref, h_ref):
    j = pl.program_id(0)
    row = lambda ref: ref[layer:layer + 1, :]
    wb_ref[:, 0:OFF_G] = w_in_ref[:, 0:OFF_G].astype(bf16)
    wb_ref[:, OFF_G:D_IN] = (0.5 * w_in_ref[:, OFF_G:D_IN]).astype(bf16)

    @pl.when(j == 0)
    def _():
        h_ref[...] = jnp.zeros(h_ref.shape, f32)

    h_ref[...] += _dot(x_ref[...].astype(bf16), wb_ref[...])

    def hcols(off, width):
        return h_ref[:, off:off + width]

    @pl.when(j == RK_STEPS - 1)
    def _():
        xa = hcols(OFF_XA, D_POOL)
        lane = lax.broadcasted_iota(jnp.int32, xa.shape, 1)
        first_row = _group_select(lane, *(POOL_BUF - (w - 1) for w in POOL_WINDOWS))
        win = xa
        for r in range(POOL_BUF):
            win = win + jnp.where(first_row <= r, pb_ref[r], 0.0)
        width = _group_select(lane, *POOL_WINDOWS).astype(f32)
        pooled = win / width - xa
        ya = _dot(pooled.astype(bf16), poolw_ref[...]) * row(pscale_ref)
        ain_ref[...] = (ya * _silu(hcols(OFF_ZA, D_POOL))).astype(bf16)
        for r in range(POOL_BUF - 1):
            pool_ref[r] = pb_ref[r + 1]
        pool_ref[POOL_BUF - 1] = xa
        vn = _layer_norm(hcols(OFF_V, D_SGU), row(slng_ref), row(slnb_ref))
        vn_ref[...] = vn
        yb = hcols(OFF_U, D_SGU) * (row(sw0_ref) * vn + row(sb0_ref))
        bin_ref[...] = (yb * _silu(hcols(OFF_ZB, D_SGU))).astype(bf16)
        cos = cos_ref[...]
        sin = sin_ref[...]
        for c in range(D_ATTN // LANES):
            q_ref[:, c * LANES:(c + 1) * LANES] = _rope128(hcols(OFF_Q + c * LANES, LANES), cos, sin) * SCALE
        kr = _rope128(hcols(OFF_K, D_KV), cos, sin)
        vv = hcols(OFF_VV, D_KV)
        k_ref[...] = kr
        v_ref[...] = vv
        kt_ref[...] = kr.T
        vt_ref[...] = vv.T
        szc_ref[...] = _silu(hcols(OFF_ZC, D_ATTN))
        for br in range(N_BRANCHES):
            cols = slice(br * D_MODEL, (br + 1) * D_MODEL)
            g_ref[:, cols] = _gate2(hcols(OFF_G + br * D_MODEL, D_MODEL), bg_ref[br:br + 1, :])


def _sample_proj(layer, x, cos, sin, w_in, b_gate, pool_t, poolw, pscale, slng, slnb, sw0, sb0):
    nb = x.shape[0]
    out_shape = (
        jax.ShapeDtypeStruct((D_MODEL, D_IN), bf16),
        jax.ShapeDtypeStruct((nb, D_ATTN), f32),
        jax.ShapeDtypeStruct((nb, D_KV), f32),
        jax.ShapeDtypeStruct((nb, D_KV), f32),
        jax.ShapeDtypeStruct((D_KV, nb), f32),
        jax.ShapeDtypeStruct((D_KV, nb), f32),
        jax.ShapeDtypeStruct((POOL_BUF, nb, D_POOL), f32),
        jax.ShapeDtypeStruct((nb, D_SGU), f32),
        jax.ShapeDtypeStruct((nb, D_POOL), bf16),
        jax.ShapeDtypeStruct((nb, D_SGU), bf16),
        jax.ShapeDtypeStruct((nb, D_ATTN), f32),
        jax.ShapeDtypeStruct((nb, N_BRANCHES * D_MODEL), f32),
    )
    lspec = lambda a: _layer_spec(a, layer)
    args = (x, cos, sin, w_in, b_gate, pool_t, poolw, pscale, slng, slnb, sw0, sb0)
    in_specs = [pl.BlockSpec((nb, None, RK), lambda j: (0, 0, j)), _full_spec(cos), _full_spec(sin),
                pl.BlockSpec((None, RK, D_IN), lambda j: (layer, j, 0)),
                lspec(b_gate), lspec(pool_t), lspec(poolw), _full_spec(pscale), _full_spec(slng),
                _full_spec(slnb), _full_spec(sw0), _full_spec(sb0)]
    out_specs = (pl.BlockSpec((RK, D_IN), lambda j: (j, 0)),) + tuple(_full_spec(s) for s in out_shape[1:])
    return pl.pallas_call(
        functools.partial(_sample_proj_kernel, layer),
        out_shape=out_shape,
        grid=(RK_STEPS,),
        in_specs=in_specs,
        out_specs=out_specs,
        scratch_shapes=[pltpu.VMEM((nb, D_IN), f32)],
        compiler_params=pltpu.CompilerParams(dimension_semantics=("arbitrary",),
                                             vmem_limit_bytes=VMEM_LIMIT),
        name="sample_proj",
    )(*args)


def _sample_attn_kernel(sink_ref, q_ref, kc_ref, vc_ref, kn_ref, vn_ref, o_ref):
    for kv in range(N_KV_HEADS):
        qb = q_ref[:, kv].astype(bf16)
        kn = kn_ref[:, kv]
        vn = vn_ref[:, kv]
        sink = sink_ref[kv][None]
        s = jnp.einsum('bgd,bdw->bgw', qb, kc_ref[:, kv].astype(bf16), preferred_element_type=f32)
        s_new = jnp.sum(qb.astype(f32) * kn.astype(bf16).astype(f32), axis=-1, keepdims=True)
        m = jnp.maximum(jnp.maximum(jnp.max(s, axis=-1, keepdims=True), s_new), sink)
        p = jnp.exp(s - m)
        p_new = jnp.exp(s_new - m)
        den = jnp.sum(p, axis=-1, keepdims=True) + p_new + jnp.exp(sink - m)
        o = jnp.einsum('bgw,bdw->bgd', (p / den).astype(bf16), vc_ref[:, kv].astype(bf16),
                       preferred_element_type=f32)
        o_ref[:, kv] = o + (p_new / den) * vn


def _sample_attn(layer, sink, q4, kc, vc, kn, vn):
    nb = q4.shape[0]
    cache_spec = pl.BlockSpec((None, SB, N_KV_HEADS, HEAD_DIM, WINDOW), lambda b: (layer, b, 0, 0, 0))
    blk = lambda s2, s3: pl.BlockSpec((SB, N_KV_HEADS, s2, s3), lambda b: (b, 0, 0, 0))
    return pl.pallas_call(
        _sample_attn_kernel,
        out_shape=jax.ShapeDtypeStruct((nb, N_KV_HEADS, Q_PER_KV, HEAD_DIM), f32),
        grid=(nb // SB,),
        in_specs=[_layer_spec(sink, layer), blk(Q_PER_KV, HEAD_DIM), cache_spec, cache_spec,
                  blk(1, HEAD_DIM), blk(1, HEAD_DIM)],
        out_specs=blk(Q_PER_KV, HEAD_DIM),
        compiler_params=pltpu.CompilerParams(dimension_semantics=("arbitrary",),
                                             vmem_limit_bytes=VMEM_LIMIT),
        name="sample_attn",
    )(sink, q4, kc, vc, kn, vn)


def _sample_merge_kernel(layer, x_ref, ain_ref, bin_ref, yc_ref, szc_ref, g_ref, wpa_ref, wpb_ref, wpc_ref,
                         wout_ref, lng_ref, lnb_ref, y_ref, wpa_b_ref, wpb_b_ref, wpc_b_ref, wout_b_ref):
    row = lambda ref: ref[layer:layer + 1, :]
    wpa = (0.5 * wpa_ref[...]).astype(bf16)
    wpb = (0.5 * wpb_ref[...]).astype(bf16)
    wpc = (0.5 * wpc_ref[...]).astype(bf16)
    wout = wout_ref[...].astype(bf16)
    wpa_b_ref[...] = wpa
    wpb_b_ref[...] = wpb
    wpc_b_ref[...] = wpc
    wout_b_ref[...] = wout
    cin = (yc_ref[...] * szc_ref[...]).astype(bf16)
    merged = (g_ref[:, 0:D_MODEL] * _dot(ain_ref[...], wpa)
              + g_ref[:, D_MODEL:2 * D_MODEL] * _dot(bin_ref[...], wpb)
              + g_ref[:, 2 * D_MODEL:3 * D_MODEL] * _dot(cin, wpc))
    out = _dot(merged.astype(bf16), wout)
    y_ref[...] = _layer_norm(ALPHA * x_ref[...] + out, row(lng_ref), row(lnb_ref))


def _sample_merge(layer, x, ain, bin_, yc, szc, g, wpa, wpb, wpc, wout, lng, lnb):
    lspec = lambda a: _layer_spec(a, layer)
    nb = x.shape[0]
    x_spec = pl.BlockSpec((nb, None, D_MODEL), lambda i: (0, 0, 0))
    out_shape = (jax.ShapeDtypeStruct(x.shape, f32),) + tuple(
        jax.ShapeDtypeStruct(w.shape[1:], bf16) for w in (wpa, wpb, wpc, wout))
    return pl.pallas_call(
        functools.partial(_sample_merge_kernel, layer),
        out_shape=out_shape,
        grid=(1,),
        in_specs=[x_spec, _full_spec(ain), _full_spec(bin_), _full_spec(yc), _full_spec(szc),
                  _full_spec(g), lspec(wpa), lspec(wpb), lspec(wpc), lspec(wout), _full_spec(lng),
                  _full_spec(lnb)],
        out_specs=(x_spec,) + tuple(_full_spec(s) for s in out_shape[1:]),
        compiler_params=pltpu.CompilerParams(dimension_semantics=("arbitrary",),
                                             vmem_limit_bytes=VMEM_LIMIT),
        name="sample_merge",
    )(x, ain, bin_, yc, szc, g, wpa, wpb, wpc, wout, lng, lnb)


def _rope_tables(positions):
    halfd = HEAD_DIM // 2
    inv = ROPE_THETA ** (-np.arange(halfd, dtype=np.float64) / halfd)
    ang = np.asarray(positions, dtype=np.float64)[:, None] * inv[None, :]
    cos = np.tile(np.cos(ang), (1, LANES // halfd))
    sin = np.tile(np.concatenate([-np.sin(ang), np.sin(ang)], axis=1), (1, LANES // HEAD_DIM))
    return jnp.asarray(cos, f32), jnp.asarray(sin, f32)


def _block_diag(w):
    nl, g, c, _ = w.shape
    eye = jnp.eye(g, dtype=w.dtype)
    return (eye[None, :, None, :, None] * w[:, :, :, None, :]).reshape(nl, g * c, g * c)


def kernel(x_prompt, x_sample, state_pool, cache_k_win, cache_v_win, w_in, b_gate, pool_w, pool_scale, sgu_ln_g, sgu_ln_b, sgu_w, sgu_b, attn_sinks, w_proj_a, w_proj_b, w_proj_c, w_out, ln_g, ln_b):
    B, L, _ = x_prompt.shape
    nb = x_sample.shape[0]
    cos_p, sin_p = _rope_tables(np.arange(L))
    cos_s, sin_s = _rope_tables(np.array([PAST_LEN]))

    poolw = _block_diag(pool_w).astype(bf16)
    sguw = sgu_w.reshape(DEPTH, N_SGU_GROUPS * CHUNK, CHUNK)
    sgub = jnp.repeat(jnp.swapaxes(sgu_b, 1, 2), POOL_GC, axis=2)
    sw0 = jnp.repeat(sgu_w[:, :, 0, 0], POOL_GC, axis=1)
    sb0 = jnp.repeat(sgu_b[:, :, 0], POOL_GC, axis=1)
    sink4 = attn_sinks.reshape(DEPTH, N_KV_HEADS, Q_PER_KV, 1)
    kc_t = jnp.transpose(cache_k_win, (0, 1, 3, 4, 2))
    vc_t = jnp.transpose(cache_v_win, (0, 1, 3, 4, 2))
    pool_t = jnp.transpose(state_pool, (0, 2, 1, 3))

    y_s = x_sample
    pool_s, chunk_v, kts, vts, weights = ([] for _ in range(5))
    for l in range(DEPTH):
        w_in_b, q, kn, vn, kt, vt, ps, cv, ain, bin_, szc, g = _sample_proj(
            l, y_s, cos_s, sin_s, w_in, b_gate, pool_t, poolw, pool_scale, sgu_ln_g, sgu_ln_b, sw0, sb0)
        o = _sample_attn(
            l, sink4, q.reshape(nb, N_KV_HEADS, Q_PER_KV, HEAD_DIM), kc_t, vc_t,
            kn.reshape(nb, N_KV_HEADS, 1, HEAD_DIM), vn.reshape(nb, N_KV_HEADS, 1, HEAD_DIM))
        y_s, wpa, wpb, wpc, wout = _sample_merge(
            l, y_s, ain, bin_, o.reshape(nb, D_ATTN), szc, g, w_proj_a, w_proj_b, w_proj_c, w_out, ln_g, ln_b)
        pool_s.append(ps); chunk_v.append(cv); kts.append(kt); vts.append(vt)
        weights.append((w_in_b, wpa, wpb, wpc, wout))

    y_p = x_prompt
    pool_p, k_p, v_p = ([] for _ in range(3))
    for l in range(DEPTH):
        w_in_b, wpa, wpb, wpc, wout = weights[l]
        slide_args = (jnp.stack(kts), jnp.stack(vts), kc_t, vc_t) if l == DEPTH - 1 else None
        y_p, pp, kp, vp, *slid = _prompt_layer(
            l, y_p, cos_p, sin_p, attn_sinks, w_in_b, b_gate, poolw, pool_scale, sgu_ln_g, sgu_ln_b, sguw,
            sgub, wpa, wpb, wpc, wout, ln_g, ln_b, slide_args=slide_args)
        pool_p.append(pp); k_p.append(kp); v_p.append(vp)
    k_s, v_s = slid

    to_cache = lambda a: jnp.transpose(a, (0, 1, 4, 2, 3))
    prompt_cache = lambda lst: to_cache(jnp.stack(lst).reshape(DEPTH, B, N_KV_HEADS, HEAD_DIM, WINDOW))
    return (y_p, y_s,
            jnp.stack(pool_p), prompt_cache(k_p), prompt_cache(v_p),
            jnp.transpose(jnp.stack(pool_s), (0, 2, 1, 3)), to_cache(k_s), to_cache(v_s),
            jnp.stack(chunk_v).reshape(DEPTH, nb, 1, D_SGU))
```

```python
import functools

import numpy as np
import jax
import jax.numpy as jnp
from jax import lax
from jax.experimental import pallas as pl
from jax.experimental.pallas import tpu as pltpu

D_MODEL = 1024
DEPTH = 2
PAST_LEN = 8192
D_POOL = 256
POOL_WINDOWS = (2, 4, 8, 16)
POOL_GC = 64
POOL_BUF = 15
D_SGU = 256
CHUNK = 128
N_SGU_GROUPS = 4
HEAD_DIM = 64
N_HEADS = 8
N_KV_HEADS = 2
Q_PER_KV = 4
D_ATTN = 512
D_KV = 128
WINDOW = 128
BLOCK = 128
ROPE_THETA = 10000.0
N_BRANCHES = 3
D_IN = 2 * D_POOL + 3 * D_SGU + 2 * D_ATTN + 2 * D_KV + N_BRANCHES * D_MODEL
ALPHA = (2.0 * DEPTH) ** 0.25
LN_EPS = 1e-5
NEG_INF = -1e30
SCALE = HEAD_DIM ** -0.5

OFF_XA, OFF_ZA, OFF_U, OFF_V, OFF_ZB = 0, 256, 512, 768, 1024
OFF_Q, OFF_K, OFF_VV, OFF_ZC, OFF_G = 1280, 1792, 1920, 2048, 2560

LANES = 128
TM = 512
SUB = 2
NBLK = TM // BLOCK
HIST = 32
NCHUNK = 256
OUT_CHUNKS = ((0, 256), (256, 512))
VMEM_LIMIT = 56 * 1024 * 1024
SB = 32
RK = 512
RK_STEPS = D_MODEL // RK

bf16 = jnp.bfloat16
f32 = jnp.float32


def _dot(a, b):
    return jnp.dot(a, b, preferred_element_type=f32)


def _dot_nt(a, b):
    return lax.dot_general(a, b, (((1,), (1,)), ((), ())), preferred_element_type=f32)


def _sigmoid(z):
    return 0.5 * jnp.tanh(0.5 * z) + 0.5


def _silu(z):
    return z * _sigmoid(z)


def _gate2(half_pre, bias):
    return jnp.tanh(half_pre + 0.5 * bias) + 1.0


def _layer_norm(x, g, b):
    mu = jnp.mean(x, axis=-1, keepdims=True)
    xc = x - mu
    var = jnp.mean(xc * xc, axis=-1, keepdims=True)
    return xc * lax.rsqrt(var + LN_EPS) * g + b


def _rope128(x, cos, sin_signed):
    lane = lax.broadcasted_iota(jnp.int32, x.shape, 1)
    first_half = (lane % HEAD_DIM) < (HEAD_DIM // 2)
    partner = jnp.where(first_half,
                        pltpu.roll(x, LANES - HEAD_DIM // 2, 1),
                        pltpu.roll(x, HEAD_DIM // 2, 1))
    return x * cos + partner * sin_signed


def _group_select(lane, a0, a1, a2, a3):
    return jnp.where(lane < 64, a0, jnp.where(lane < 128, a1, jnp.where(lane < 192, a2, a3)))


def _layer_spec(arr, layer, single_buffer=False):
    block = (None,) + arr.shape[1:]
    zeros = (0,) * (arr.ndim - 1)
    index_map = lambda *_: (layer,) + zeros
    if single_buffer:
        return pl.BlockSpec(block, index_map, pipeline_mode=pl.Buffered(1))
    return pl.BlockSpec(block, index_map)


def _full_spec(arr, single_buffer=False):
    zeros = (0,) * len(arr.shape)
    if single_buffer:
        return pl.BlockSpec(arr.shape, lambda *_: zeros, pipeline_mode=pl.Buffered(1))
    return pl.BlockSpec(arr.shape, lambda *_: zeros)


def _slide_windows(step, per_step, new_refs, old_refs, out_refs):
    lane = lax.broadcasted_iota(jnp.int32, (D_KV, WINDOW), 1)
    shift = jnp.where(step == 0, 0, LANES - step * per_step)
    for new_ref, old_ref, out_ref in zip(new_refs, old_refs, out_refs):
        for l in range(DEPTH):
            new_cols = pltpu.roll(new_ref[l][...], shift, 1)
            for s in range(per_step):
                slid = pltpu.roll(old_ref[l, s].reshape(D_KV, WINDOW), WINDOW - 1, 1)
                newest = jnp.broadcast_to(new_cols[:, s:s + 1], (D_KV, WINDOW))
                out_ref[l, s] = jnp.where(lane == WINDOW - 1, newest, slid).reshape(
                    N_KV_HEADS, HEAD_DIM, WINDOW)


def _prompt_kernel(layer, slide, *refs):
    def sub_tile(s, carry):
        _prompt_tile(layer, slide, s, refs)
        return carry
    lax.fori_loop(0, SUB, sub_tile, 0)


def _prompt_tile(layer, slide, s, refs):
    n_in = 18 + (2 * DEPTH + 2 if slide else 0)
    n_out = 4 + (2 if slide else 0)
    (sinks_ref, x_ref, cos_ref, sin_ref, w_in_ref, bg_ref, poolw_ref, pscale_ref,
     slng_ref, slnb_ref, sguw_ref, sgub_ref, wpa_ref, wpb_ref, wpc_ref, wout_ref,
     lng_ref, lnb_ref) = refs[:18]
    y_ref, pool_out_ref, k_out_ref, v_out_ref = refs[n_in:n_in + 4]
    (h_ref, ext_ref, s2_ref, s4_ref, s8_ref, qs_ref,
     ka_ref, kb_ref, kc_ref, kd_ref, va_ref, vb_ref, vc_ref, vd_ref,
     ain_ref, bin_ref, cin_ref, mg_ref, gate_ref, klast_ref, vlast_ref) = refs[n_in + n_out:]
    tile_rows = pl.ds(pl.multiple_of(s * TM, TM), TM)
    x_ref, y_ref, cos_ref, sin_ref = (r.at[tile_rows] for r in (x_ref, y_ref, cos_ref, sin_ref))
    i = pl.program_id(1) * SUB + s
    last = pl.num_programs(1) * SUB - 1
    kv_refs = (ka_ref, kb_ref, kc_ref, kd_ref, va_ref, vb_ref, vc_ref, vd_ref)
    row = lambda ref: ref[layer:layer + 1, :]

    @pl.when(i == 0)
    def _():
        ext_ref[0:HIST, :] = jnp.zeros((HIST, D_POOL), f32)
        for r in kv_refs:
            r[0:BLOCK, :] = jnp.zeros((BLOCK, LANES), bf16)

    xb = x_ref[...].astype(bf16)
    half = OFF_G // 2
    h_ref[:, 0:half] = _dot(xb, w_in_ref[:, 0:half])
    h_ref[:, half:OFF_G] = _dot(xb, w_in_ref[:, half:OFF_G])

    if slide:
        step = pl.program_id(0) * (last + 1) + i
        new_refs = (refs[18:18 + DEPTH], refs[18 + DEPTH:18 + 2 * DEPTH])
        old_refs = refs[18 + 2 * DEPTH:n_in]
        per_step = old_refs[0].shape[1] // SUB
        mine = lambda r: r.at[:, pl.ds(s * per_step, per_step)]
        _slide_windows(step, per_step, new_refs, [mine(r) for r in old_refs],
                       [mine(r) for r in refs[n_in + 4:n_in + 6]])

    xa = h_ref[:, OFF_XA:OFF_XA + D_POOL]
    ext_ref[HIST:HIST + TM, :] = xa
    n = HIST + TM
    s2_ref[8:n, :] = ext_ref[8:n, :] + ext_ref[7:n - 1, :]
    s4_ref[16:n, :] = s2_ref[16:n, :] + s2_ref[14:n - 2, :]
    s8_ref[24:n, :] = s4_ref[24:n, :] + s4_ref[20:n - 4, :]
    w16 = s8_ref[HIST:n, :] + s8_ref[HIST - 8:n - 8, :]
    lane_p = lax.broadcasted_iota(jnp.int32, (TM, D_POOL), 1)
    row_p = lax.broadcasted_iota(jnp.int32, (TM, D_POOL), 0)
    win = _group_select(lane_p, s2_ref[HIST:n, :], s4_ref[HIST:n, :], s8_ref[HIST:n, :], w16)
    width = _group_select(lane_p, POOL_WINDOWS[0], POOL_WINDOWS[1], POOL_WINDOWS[2], POOL_WINDOWS[3])
    cnt = jnp.minimum(row_p + (i * TM + 1), width).astype(f32)
    pooled = win / cnt - xa
    ya = _dot(pooled.astype(bf16), poolw_ref[...]) * row(pscale_ref)
    za = h_ref[:, OFF_ZA:OFF_ZA + D_POOL]
    ain_ref[...] = (ya * _silu(za)).astype(bf16)
    ext_ref[HIST - 16:HIST, :] = ext_ref[n - 16:n, :]

    vn = _layer_norm(h_ref[:, OFF_V:OFF_V + D_SGU], row(slng_ref), row(slnb_ref)).astype(bf16)
    wr = lax.broadcasted_iota(jnp.int32, (N_SGU_GROUPS * CHUNK, CHUNK), 0) % CHUNK
    wc = lax.broadcasted_iota(jnp.int32, (N_SGU_GROUPS * CHUNK, CHUNK), 1)
    w_s = jnp.where(wc <= wr, sguw_ref[...], 0.0).astype(bf16)
    lane_c = lax.broadcasted_iota(jnp.int32, (CHUNK, D_SGU), 1)
    for j in range(NBLK):
        rows = slice(j * CHUNK, (j + 1) * CHUNK)
        r = _dot(w_s, vn[rows, :])
        mixed = _group_select(lane_c, r[0:CHUNK], r[CHUNK:2 * CHUNK], r[2 * CHUNK:3 * CHUNK],
                              r[3 * CHUNK:4 * CHUNK]) + sgub_ref[...]
        yb = h_ref[rows, OFF_U:OFF_U + D_SGU] * mixed
        bin_ref[rows, :] = (yb * _silu(h_ref[rows, OFF_ZB:OFF_ZB + D_SGU])).astype(bf16)

    cos = cos_ref[...]
    sin = sin_ref[...]
    for c in range(D_ATTN // LANES):
        qc = _rope128(h_ref[:, OFF_Q + c * LANES:OFF_Q + (c + 1) * LANES], cos, sin)
        qs_ref[:, c * LANES:(c + 1) * LANES] = (qc * SCALE).astype(bf16)
    kr = _rope128(h_ref[:, OFF_K:OFF_K + D_KV], cos, sin)
    vv = h_ref[:, OFF_VV:OFF_VV + D_KV]
    klast_ref[...] = kr[TM - WINDOW:TM, :]
    vlast_ref[...] = vv[TM - WINDOW:TM, :]
    lane_k = lax.broadcasted_iota(jnp.int32, (TM, LANES), 1)
    lo = lane_k < HEAD_DIM
    for src, (a_ref, b_ref, c_ref, d_ref) in ((kr, kv_refs[0:4]), (vv, kv_refs[4:8])):
        sw = pltpu.roll(src, HEAD_DIM, 1)
        a_ref[BLOCK:BLOCK + TM, :] = jnp.where(lo, src, 0.0).astype(bf16)
        b_ref[BLOCK:BLOCK + TM, :] = jnp.where(lo, 0.0, src).astype(bf16)
        c_ref[BLOCK:BLOCK + TM, :] = jnp.where(lo, sw, 0.0).astype(bf16)
        d_ref[BLOCK:BLOCK + TM, :] = jnp.where(lo, 0.0, sw).astype(bf16)

    qrow = lax.broadcasted_iota(jnp.int32, (2 * BLOCK, 2 * BLOCK), 0) % BLOCK
    kcol = lax.broadcasted_iota(jnp.int32, (2 * BLOCK, 2 * BLOCK), 1)
    band = (kcol >= qrow) & (kcol <= qrow + WINDOW)
    band_first = band & (kcol >= jnp.where(i > 0, 0, BLOCK))
    top = lax.broadcasted_iota(jnp.int32, (2 * BLOCK, 1), 0) < BLOCK
    n_gate = N_BRANCHES * D_MODEL // NCHUNK
    n_unit = NBLK * N_KV_HEADS
    gate_sched = [range(u * n_gate // n_unit, (u + 1) * n_gate // n_unit) for u in range(n_unit)]

    def scores(u):
        j, kv = divmod(u, N_KV_HEADS)
        rows = slice(j * BLOCK, (j + 1) * BLOCK)
        keys = slice(j * BLOCK, j * BLOCK + 2 * BLOCK)
        c0 = kv * Q_PER_KV * HEAD_DIM
        qst = jnp.concatenate([qs_ref[rows, c0:c0 + LANES], qs_ref[rows, c0 + LANES:c0 + 2 * LANES]], axis=0)
        k_even, k_odd = (ka_ref, kd_ref) if kv == 0 else (kc_ref, kb_ref)
        kcat = jnp.concatenate([k_even[keys, :], k_odd[keys, :]], axis=0)
        return _dot_nt(qst, kcat)

    def attend(u, sc):
        j, kv = divmod(u, N_KV_HEADS)
        rows = slice(j * BLOCK, (j + 1) * BLOCK)
        keys = slice(j * BLOCK, j * BLOCK + 2 * BLOCK)
        allowed = band_first if j == 0 else band
        c0 = kv * Q_PER_KV * HEAD_DIM
        h0 = kv * Q_PER_KV
        v_even, v_odd = (va_ref, vd_ref) if kv == 0 else (vc_ref, vb_ref)
        probs = []
        for par in range(2):
            sink = jnp.where(top, sinks_ref[layer, h0 + par], sinks_ref[layer, h0 + 2 + par])
            sm = jnp.where(allowed, sc[:, par * 2 * BLOCK:(par + 1) * 2 * BLOCK], NEG_INF)
            m = jnp.maximum(jnp.max(sm, axis=-1, keepdims=True), sink)
            p = jnp.exp(sm - m)
            den = jnp.sum(p, axis=-1, keepdims=True) + jnp.exp(sink - m)
            probs.append((p / den).astype(bf16))
        pcat = jnp.concatenate(probs, axis=1)
        vcat = jnp.concatenate([v_even[keys, :], v_odd[keys, :]], axis=0)
        o = _dot(pcat, vcat)
        for pr in range(2):
            cols = slice(c0 + pr * LANES, c0 + (pr + 1) * LANES)
            zc = h_ref[rows, OFF_ZC + c0 + pr * LANES:OFF_ZC + c0 + (pr + 1) * LANES]
            cin_ref[rows, cols] = (o[pr * BLOCK:(pr + 1) * BLOCK] * _silu(zc)).astype(bf16)

    sc_next = scores(0)
    for u in range(n_unit):
        sc = sc_next
        for gc in gate_sched[u]:
            gcols = slice(gc * NCHUNK, (gc + 1) * NCHUNK)
            gate_ref[:, gcols] = _dot(xb, w_in_ref[:, OFF_G + gc * NCHUNK:OFF_G + (gc + 1) * NCHUNK])
        if u + 1 < n_unit:
            sc_next = scores(u + 1)
        attend(u, sc)

    for r in kv_refs:
        r[0:BLOCK, :] = r[TM:TM + BLOCK, :]

    for c in range(D_MODEL // NCHUNK):
        cols = slice(c * NCHUNK, (c + 1) * NCHUNK)
        acc = None
        for br, (in_ref, wp_ref) in enumerate(((ain_ref, wpa_ref), (bin_ref, wpb_ref), (cin_ref, wpc_ref))):
            g0 = br * D_MODEL + c * NCHUNK
            term = _gate2(gate_ref[:, g0:g0 + NCHUNK], bg_ref[br:br + 1, cols]) * _dot(in_ref[...], wp_ref[:, cols])
            acc = term if acc is None else acc + term
        mg_ref[:, cols] = acc.astype(bf16)

    for r0, r1 in OUT_CHUNKS:
        rows = slice(r0, r1)
        out = _dot(mg_ref[rows, :], wout_ref[...])
        y_ref[rows, :] = _layer_norm(ALPHA * x_ref[rows, :] + out, row(lng_ref), row(lnb_ref))

    @pl.when(i == last)
    def _():
        pool_out_ref[...] = ext_ref[n - POOL_BUF:n, :]
        k_out_ref[...] = klast_ref[...].T
        v_out_ref[...] = vlast_ref[...].T


def _prompt_layer(layer, x, cos, sin, sinks, w_in, b_gate, poolw, pscale, slng, slnb, sguw, sgub,
                  wpa, wpb, wpc, wout, lng, lnb, slide_args=None):
    B, L, _ = x.shape
    rows = TM * SUB
    nt = L // rows
    grid = (B, nt)
    row_spec = lambda w: pl.BlockSpec((rows, w), lambda b, i: (i, 0))
    lspec = lambda a: _layer_spec(a, layer, single_buffer=True)
    wspec = lambda a: _full_spec(a, single_buffer=True)
    in_specs = [
        pl.BlockSpec(memory_space=pltpu.SMEM),
        pl.BlockSpec((None, rows, D_MODEL), lambda b, i: (b, i, 0)),
        row_spec(LANES), row_spec(LANES),
        wspec(w_in), lspec(b_gate), lspec(poolw), _full_spec(pscale), _full_spec(slng), _full_spec(slnb),
        lspec(sguw), lspec(sgub), wspec(wpa), wspec(wpb), wspec(wpc), wspec(wout),
        _full_spec(lng), _full_spec(lnb),
    ]
    out_shape = (
        jax.ShapeDtypeStruct((B, L, D_MODEL), f32),
        jax.ShapeDtypeStruct((B, POOL_BUF, D_POOL), f32),
        jax.ShapeDtypeStruct((B, D_KV, WINDOW), f32),
        jax.ShapeDtypeStruct((B, D_KV, WINDOW), f32),
    )
    out_specs = (
        pl.BlockSpec((None, rows, D_MODEL), lambda b, i: (b, i, 0)),
        pl.BlockSpec((None, POOL_BUF, D_POOL), lambda b, i: (b, 0, 0)),
        pl.BlockSpec((None, D_KV, WINDOW), lambda b, i: (b, 0, 0)),
        pl.BlockSpec((None, D_KV, WINDOW), lambda b, i: (b, 0, 0)),
    )
    args = [sinks, x, cos, sin, w_in, b_gate, poolw, pscale, slng, slnb, sguw, sgub,
            wpa, wpb, wpc, wout, lng, lnb]
    if slide_args is not None:
        new_k, new_v, cache_k, cache_v = slide_args
        depth, nb = cache_k.shape[:2]
        per_step = nb // (B * nt)
        assert per_step * B * nt == nb and depth == DEPTH
        cache_spec = pl.BlockSpec((depth, per_step) + cache_k.shape[2:], lambda b, i: (0, b * nt + i, 0, 0, 0))
        in_specs += [_full_spec(a) for a in (*new_k, *new_v)] + [cache_spec, cache_spec]
        args += [*new_k, *new_v, cache_k, cache_v]
        out_shape += (jax.ShapeDtypeStruct(cache_k.shape, f32), jax.ShapeDtypeStruct(cache_v.shape, f32))
        out_specs += (cache_spec, cache_spec)
    kv_scratch = [pltpu.VMEM((BLOCK + TM, LANES), bf16) for _ in range(8)]
    scratch = [
        pltpu.VMEM((TM, OFF_G), f32),
        pltpu.VMEM((HIST + TM, D_POOL), f32),
        pltpu.VMEM((HIST + TM, D_POOL), f32),
        pltpu.VMEM((HIST + TM, D_POOL), f32),
        pltpu.VMEM((HIST + TM, D_POOL), f32),
        pltpu.VMEM((TM, D_ATTN), bf16),
        *kv_scratch,
        pltpu.VMEM((TM, D_POOL), bf16),
        pltpu.VMEM((TM, D_SGU), bf16),
        pltpu.VMEM((TM, D_ATTN), bf16),
        pltpu.VMEM((TM, D_MODEL), bf16),
        pltpu.VMEM((TM, N_BRANCHES * D_MODEL), f32),
        pltpu.VMEM((WINDOW, D_KV), f32),
        pltpu.VMEM((WINDOW, D_KV), f32),
    ]
    return pl.pallas_call(
        functools.partial(_prompt_kernel, layer, slide_args is not None),
        out_shape=out_shape,
        grid=grid,
        in_specs=in_specs,
        out_specs=out_specs,
        scratch_shapes=scratch,
        compiler_params=pltpu.CompilerParams(
            dimension_semantics=("arbitrary", "arbitrary"),
            vmem_limit_bytes=VMEM_LIMIT),
        name="prompt_layer",
    )(*args)


def _sample_proj_kernel(layer, x_ref, cos_ref, sin_ref, w_in_ref, bg_ref, pb_ref, poolw_ref, pscale_ref,
                        slng_ref, slnb_ref, sw0_ref, sb0_ref,
                        wb_ref, q_ref, k_ref, v_ref, kt_ref, vt_ref, pool_ref, vn_ref, ain_ref, bin_ref,
                        szc_ref, g_ref, h_ref):
    j = pl.program_id(0)
    row = lambda ref: ref[layer:layer + 1, :]
    wb_ref[:, 0:OFF_G] = w_in_ref[:, 0:OFF_G].astype(bf16)
    wb_ref[:, OFF_G:D_IN] = (0.5 * w_in_ref[:, OFF_G:D_IN]).astype(bf16)

    @pl.when(j == 0)
    def _():
        h_ref[...] = jnp.zeros(h_ref.shape, f32)

    h_ref[...] += _dot(x_ref[...].astype(bf16), wb_ref[...])

    def hcols(off, width):
        return h_ref[:, off:off + width]

    @pl.when(j == RK_STEPS - 1)
    def _():
        xa = hcols(OFF_XA, D_POOL)
        lane = lax.broadcasted_iota(jnp.int32, xa.shape, 1)
        first_row = _group_select(lane, *(POOL_BUF - (w - 1) for w in POOL_WINDOWS))
        win = xa
        for r in range(POOL_BUF):
            win = win + jnp.where(first_row <= r, pb_ref[r], 0.0)
        width = _group_select(lane, *POOL_WINDOWS).astype(f32)
        pooled = win / width - xa
        ya = _dot(pooled.astype(bf16), poolw_ref[...]) * row(pscale_ref)
        ain_ref[...] = (ya * _silu(hcols(OFF_ZA, D_POOL))).astype(bf16)
        for r in range(POOL_BUF - 1):
            pool_ref[r] = pb_ref[r + 1]
        pool_ref[POOL_BUF - 1] = xa
        vn = _layer_norm(hcols(OFF_V, D_SGU), row(slng_ref), row(slnb_ref))
        vn_ref[...] = vn
        yb = hcols(OFF_U, D_SGU) * (row(sw0_ref) * vn + row(sb0_ref))
        bin_ref[...] = (yb * _silu(hcols(OFF_ZB, D_SGU))).astype(bf16)
        cos = cos_ref[...]
        sin = sin_ref[...]
        for c in range(D_ATTN // LANES):
            q_ref[:, c * LANES:(c + 1) * LANES] = _rope128(hcols(OFF_Q + c * LANES, LANES), cos, sin) * SCALE
        kr = _rope128(hcols(OFF_K, D_KV), cos, sin)
        vv = hcols(OFF_VV, D_KV)
        k_ref[...] = kr
        v_ref[...] = vv
        kt_ref[...] = kr.T
        vt_ref[...] = vv.T
        szc_ref[...] = _silu(hcols(OFF_ZC, D_ATTN))
        for br in range(N_BRANCHES):
            cols = slice(br * D_MODEL, (br + 1) * D_MODEL)
            g_ref[:, cols] = _gate2(hcols(OFF_G + br * D_MODEL, D_MODEL), bg_ref[br:br + 1, :])


def _sample_proj(layer, x, cos, sin, w_in, b_gate, pool_t, poolw, pscale, slng, slnb, sw0, sb0):
    nb = x.shape[0]
    out_shape = (
        jax.ShapeDtypeStruct((D_MODEL, D_IN), bf16),
        jax.ShapeDtypeStruct((nb, D_ATTN), f32),
        jax.ShapeDtypeStruct((nb, D_KV), f32),
        jax.ShapeDtypeStruct((nb, D_KV), f32),
        jax.ShapeDtypeStruct((D_KV, nb), f32),
        jax.ShapeDtypeStruct((D_KV, nb), f32),
        jax.ShapeDtypeStruct((POOL_BUF, nb, D_POOL), f32),
        jax.ShapeDtypeStruct((nb, D_SGU), f32),
        jax.ShapeDtypeStruct((nb, D_POOL), bf16),
        jax.ShapeDtypeStruct((nb, D_SGU), bf16),
        jax.ShapeDtypeStruct((nb, D_ATTN), f32),
        jax.ShapeDtypeStruct((nb, N_BRANCHES * D_MODEL), f32),
    )
    lspec = lambda a: _layer_spec(a, layer)
    args = (x, cos, sin, w_in, b_gate, pool_t, poolw, pscale, slng, slnb, sw0, sb0)
    in_specs = [pl.BlockSpec((nb, None, RK), lambda j: (0, 0, j)), _full_spec(cos), _full_spec(sin),
                pl.BlockSpec((None, RK, D_IN), lambda j: (layer, j, 0)),
                lspec(b_gate), lspec(pool_t), lspec(poolw), _full_spec(pscale), _full_spec(slng),
                _full_spec(slnb), _full_spec(sw0), _full_spec(sb0)]
    out_specs = (pl.BlockSpec((RK, D_IN), lambda j: (j, 0)),) + tuple(_full_spec(s) for s in out_shape[1:])
    return pl.pallas_call(
        functools.partial(_sample_proj_kernel, layer),
        out_shape=out_shape,
        grid=(RK_STEPS,),
        in_specs=in_specs,
        out_specs=out_specs,
        scratch_shapes=[pltpu.VMEM((nb, D_IN), f32)],
        compiler_params=pltpu.CompilerParams(dimension_semantics=("arbitrary",),
                                             vmem_limit_bytes=VMEM_LIMIT),
        name="sample_proj",
    )(*args)


def _sample_attn_kernel(sink_ref, q_ref, kc_ref, vc_ref, kn_ref, vn_ref, o_ref):
    for kv in range(N_KV_HEADS):
        qb = q_ref[:, kv].astype(bf16)
        kn = kn_ref[:, kv]
        vn = vn_ref[:, kv]
        sink = sink_ref[kv][None]
        s = jnp.einsum('bgd,bdw->bgw', qb, kc_ref[:, kv].astype(bf16), preferred_element_type=f32)
        s_new = jnp.sum(qb.astype(f32) * kn.astype(bf16).astype(f32), axis=-1, keepdims=True)
        m = jnp.maximum(jnp.maximum(jnp.max(s, axis=-1, keepdims=True), s_new), sink)
        p = jnp.exp(s - m)
        p_new = jnp.exp(s_new - m)
        den = jnp.sum(p, axis=-1, keepdims=True) + p_new + jnp.exp(sink - m)
        o = jnp.einsum('bgw,bdw->bgd', (p / den).astype(bf16), vc_ref[:, kv].astype(bf16),
                       preferred_element_type=f32)
        o_ref[:, kv] = o + (p_new / den) * vn


def _sample_attn(layer, sink, q4, kc, vc, kn, vn):
    nb = q4.shape[0]
    cache_spec = pl.BlockSpec((None, SB, N_KV_HEADS, HEAD_DIM, WINDOW), lambda b: (layer, b, 0, 0, 0))
    blk = lambda s2, s3: pl.BlockSpec((SB, N_KV_HEADS, s2, s3), lambda b: (b, 0, 0, 0))
    return pl.pallas_call(
        _sample_attn_kernel,
        out_shape=jax.ShapeDtypeStruct((nb, N_KV_HEADS, Q_PER_KV, HEAD_DIM), f32),
        grid=(nb // SB,),
        in_specs=[_layer_spec(sink, layer), blk(Q_PER_KV, HEAD_DIM), cache_spec, cache_spec,
                  blk(1, HEAD_DIM), blk(1, HEAD_DIM)],
        out_specs=blk(Q_PER_KV, HEAD_DIM),
        compiler_params=pltpu.CompilerParams(dimension_semantics=("arbitrary",),
                                             vmem_limit_bytes=VMEM_LIMIT),
        name="sample_attn",
    )(sink, q4, kc, vc, kn, vn)


def _sample_merge_kernel(layer, x_ref, ain_ref, bin_ref, yc_ref, szc_ref, g_ref, wpa_ref, wpb_ref, wpc_ref,
                         wout_ref, lng_ref, lnb_ref, y_ref, wpa_b_ref, wpb_b_ref, wpc_b_ref, wout_b_ref):
    row = lambda ref: ref[layer:layer + 1, :]
    wpa = (0.5 * wpa_ref[...]).astype(bf16)
    wpb = (0.5 * wpb_ref[...]).astype(bf16)
    wpc = (0.5 * wpc_ref[...]).astype(bf16)
    wout = wout_ref[...].astype(bf16)
    wpa_b_ref[...] = wpa
    wpb_b_ref[...] = wpb
    wpc_b_ref[...] = wpc
    wout_b_ref[...] = wout
    cin = (yc_ref[...] * szc_ref[...]).astype(bf16)
    merged = (g_ref[:, 0:D_MODEL] * _dot(ain_ref[...], wpa)
              + g_ref[:, D_MODEL:2 * D_MODEL] * _dot(bin_ref[...], wpb)
              + g_ref[:, 2 * D_MODEL:3 * D_MODEL] * _dot(cin, wpc))
    out = _dot(merged.astype(bf16), wout)
    y_ref[...] = _layer_norm(ALPHA * x_ref[...] + out, row(lng_ref), row(lnb_ref))


def _sample_merge(layer, x, ain, bin_, yc, szc, g, wpa, wpb, wpc, wout, lng, lnb):
    lspec = lambda a: _layer_spec(a, layer)
    nb = x.shape[0]
    x_spec = pl.BlockSpec((nb, None, D_MODEL), lambda i: (0, 0, 0))
    out_shape = (jax.ShapeDtypeStruct(x.shape, f32),) + tuple(
        jax.ShapeDtypeStruct(w.shape[1:], bf16) for w in (wpa, wpb, wpc, wout))
    return pl.pallas_call(
        functools.partial(_sample_merge_kernel, layer),
        out_shape=out_shape,
        grid=(1,),
        in_specs=[x_spec, _full_spec(ain), _full_spec(bin_), _full_spec(yc), _full_spec(szc),
                  _full_spec(g), lspec(wpa), lspec(wpb), lspec(wpc), lspec(wout), _full_spec(lng),
                  _full_spec(lnb)],
        out_specs=(x_spec,) + tuple(_full_spec(s) for s in out_shape[1:]),
        compiler_params=pltpu.CompilerParams(dimension_semantics=("arbitrary",),
                                             vmem_limit_bytes=VMEM_LIMIT),
        name="sample_merge",
    )(x, ain, bin_, yc, szc, g, wpa, wpb, wpc, wout, lng, lnb)


def _rope_tables(positions):
    halfd = HEAD_DIM // 2
    inv = ROPE_THETA ** (-np.arange(halfd, dtype=np.float64) / halfd)
    ang = np.asarray(positions, dtype=np.float64)[:, None] * inv[None, :]
    cos = np.tile(np.cos(ang), (1, LANES // halfd))
    sin = np.tile(np.concatenate([-np.sin(ang), np.sin(ang)], axis=1), (1, LANES // HEAD_DIM))
    return jnp.asarray(cos, f32), jnp.asarray(sin, f32)


def _block_diag(w):
    nl, g, c, _ = w.shape
    eye = jnp.eye(g, dtype=w.dtype)
    return (eye[None, :, None, :, None] * w[:, :, :, None, :]).reshape(nl, g * c, g * c)


def kernel(x_prompt, x_sample, state_pool, cache_k_win, cache_v_win, w_in, b_gate, pool_w, pool_scale, sgu_ln_g, sgu_ln_b, sgu_w, sgu_b, attn_sinks, w_proj_a, w_proj_b, w_proj_c, w_out, ln_g, ln_b):
    B, L, _ = x_prompt.shape
    nb = x_sample.shape[0]
    cos_p, sin_p = _rope_tables(np.arange(L))
    cos_s, sin_s = _rope_tables(np.array([PAST_LEN]))

    poolw = _block_diag(pool_w).astype(bf16)
    sguw = sgu_w.reshape(DEPTH, N_SGU_GROUPS * CHUNK, CHUNK)
    sgub = jnp.repeat(jnp.swapaxes(sgu_b, 1, 2), POOL_GC, axis=2)
    sw0 = jnp.repeat(sgu_w[:, :, 0, 0], POOL_GC, axis=1)
    sb0 = jnp.repeat(sgu_b[:, :, 0], POOL_GC, axis=1)
    sink4 = attn_sinks.reshape(DEPTH, N_KV_HEADS, Q_PER_KV, 1)
    kc_t = jnp.transpose(cache_k_win, (0, 1, 3, 4, 2))
    vc_t = jnp.transpose(cache_v_win, (0, 1, 3, 4, 2))
    pool_t = jnp.transpose(state_pool, (0, 2, 1, 3))

    y_s = x_sample
    pool_s, chunk_v, kts, vts, weights = ([] for _ in range(5))
    for l in range(DEPTH):
        w_in_b, q, kn, vn, kt, vt, ps, cv, ain, bin_, szc, g = _sample_proj(
            l, y_s, cos_s, sin_s, w_in, b_gate, pool_t, poolw, pool_scale, sgu_ln_g, sgu_ln_b, sw0, sb0)
        o = _sample_attn(
            l, sink4, q.reshape(nb, N_KV_HEADS, Q_PER_KV, HEAD_DIM), kc_t, vc_t,
            kn.reshape(nb, N_KV_HEADS, 1, HEAD_DIM), vn.reshape(nb, N_KV_HEADS, 1, HEAD_DIM))
        y_s, wpa, wpb, wpc, wout = _sample_merge(
            l, y_s, ain, bin_, o.reshape(nb, D_ATTN), szc, g, w_proj_a, w_proj_b, w_proj_c, w_out, ln_g, ln_b)
        pool_s.append(ps); chunk_v.append(cv); kts.append(kt); vts.append(vt)
        weights.append((w_in_b, wpa, wpb, wpc, wout))

    y_p = x_prompt
    pool_p, k_p, v_p = ([] for _ in range(3))
    for l in range(DEPTH):
        w_in_b, wpa, wpb, wpc, wout = weights[l]
        slide_args = (kts, vts, kc_t, vc_t) if l == DEPTH - 1 else None
        y_p, pp, kp, vp, *slid = _prompt_layer(
            l, y_p, cos_p, sin_p, attn_sinks, w_in_b, b_gate, poolw, pool_scale, sgu_ln_g, sgu_ln_b, sguw,
            sgub, wpa, wpb, wpc, wout, ln_g, ln_b, slide_args=slide_args)
        pool_p.append(pp); k_p.append(kp); v_p.append(vp)
    k_s, v_s = slid

    to_cache = lambda a: jnp.transpose(a, (0, 1, 4, 2, 3))
    prompt_cache = lambda lst: to_cache(jnp.stack(lst).reshape(DEPTH, B, N_KV_HEADS, HEAD_DIM, WINDOW))
    return (y_p, y_s,
            jnp.stack(pool_p), prompt_cache(k_p), prompt_cache(v_p),
            jnp.transpose(jnp.stack(pool_s), (0, 2, 1, 3)), to_cache(k_s), to_cache(v_s),
            jnp.stack(chunk_v).reshape(DEPTH, nb, 1, D_SGU))
```

```python
import functools

import numpy as np
import jax
import jax.numpy as jnp
from jax import lax
from jax.experimental import pallas as pl
from jax.experimental.pallas import tpu as pltpu

D_MODEL = 1024
DEPTH = 2
PAST_LEN = 8192
D_POOL = 256
POOL_WINDOWS = (2, 4, 8, 16)
POOL_GC = 64
POOL_BUF = 15
D_SGU = 256
CHUNK = 128
N_SGU_GROUPS = 4
HEAD_DIM = 64
N_HEADS = 8
N_KV_HEADS = 2
Q_PER_KV = 4
D_ATTN = 512
D_KV = 128
WINDOW = 128
BLOCK = 128
ROPE_THETA = 10000.0
N_BRANCHES = 3
D_IN = 2 * D_POOL + 3 * D_SGU + 2 * D_ATTN + 2 * D_KV + N_BRANCHES * D_MODEL
ALPHA = (2.0 * DEPTH) ** 0.25
LN_EPS = 1e-5
NEG_INF = -1e30
SCALE = HEAD_DIM ** -0.5

OFF_XA, OFF_ZA, OFF_U, OFF_V, OFF_ZB = 0, 256, 512, 768, 1024
OFF_Q, OFF_K, OFF_VV, OFF_ZC, OFF_G = 1280, 1792, 1920, 2048, 2560

LANES = 128
TM = 512
SUB = 2
NBLK = TM // BLOCK
HIST = 32
NCHUNK = 256
OUT_CHUNKS = ((0, 256), (256, 512))
VMEM_LIMIT = 56 * 1024 * 1024
SB = 32
RK = 512
RK_STEPS = D_MODEL // RK

bf16 = jnp.bfloat16
f32 = jnp.float32


def _dot(a, b):
    return jnp.dot(a, b, preferred_element_type=f32)


def _dot_nt(a, b):
    return lax.dot_general(a, b, (((1,), (1,)), ((), ())), preferred_element_type=f32)


def _sigmoid(z):
    return 0.5 * jnp.tanh(0.5 * z) + 0.5


def _silu(z):
    return z * _sigmoid(z)


def _gate2(half_pre, bias):
    return jnp.tanh(half_pre + 0.5 * bias) + 1.0


def _layer_norm(x, g, b):
    mu = jnp.mean(x, axis=-1, keepdims=True)
    xc = x - mu
    var = jnp.mean(xc * xc, axis=-1, keepdims=True)
    return xc * lax.rsqrt(var + LN_EPS) * g + b


def _rope128(x, cos, sin_signed):
    lane = lax.broadcasted_iota(jnp.int32, x.shape, 1)
    first_half = (lane % HEAD_DIM) < (HEAD_DIM // 2)
    partner = jnp.where(first_half,
                        pltpu.roll(x, LANES - HEAD_DIM // 2, 1),
                        pltpu.roll(x, HEAD_DIM // 2, 1))
    return x * cos + partner * sin_signed


def _group_select(lane, a0, a1, a2, a3):
    return jnp.where(lane < 64, a0, jnp.where(lane < 128, a1, jnp.where(lane < 192, a2, a3)))


def _layer_spec(arr, layer, single_buffer=False):
    block = (None,) + arr.shape[1:]
    zeros = (0,) * (arr.ndim - 1)
    index_map = lambda *_: (layer,) + zeros
    if single_buffer:
        return pl.BlockSpec(block, index_map, pipeline_mode=pl.Buffered(1))
    return pl.BlockSpec(block, index_map)


def _full_spec(arr, single_buffer=False):
    zeros = (0,) * len(arr.shape)
    if single_buffer:
        return pl.BlockSpec(arr.shape, lambda *_: zeros, pipeline_mode=pl.Buffered(1))
    return pl.BlockSpec(arr.shape, lambda *_: zeros)


def _slide_windows(step, per_step, new_refs, old_refs, out_refs):
    lane = lax.broadcasted_iota(jnp.int32, (D_KV, WINDOW), 1)
    shift = jnp.where(step == 0, 0, LANES - step * per_step)
    for new_ref, old_ref, out_ref in zip(new_refs, old_refs, out_refs):
        for l in range(DEPTH):
            new_cols = pltpu.roll(new_ref[l][...], shift, 1)
            for s in range(per_step):
                slid = pltpu.roll(old_ref[l, s].reshape(D_KV, WINDOW), WINDOW - 1, 1)
                newest = jnp.broadcast_to(new_cols[:, s:s + 1], (D_KV, WINDOW))
                out_ref[l, s] = jnp.where(lane == WINDOW - 1, newest, slid).reshape(
                    N_KV_HEADS, HEAD_DIM, WINDOW)


def _prompt_kernel(layer, slide, *refs):
    def sub_tile(s, carry):
        _prompt_tile(layer, slide, s, refs)
        return carry
    lax.fori_loop(0, SUB, sub_tile, 0)


def _prompt_tile(layer, slide, s, refs):
    n_in = 18 + (2 * DEPTH + 2 if slide else 0)
    n_out = 4 + (2 if slide else 0)
    (sinks_ref, x_ref, cos_ref, sin_ref, w_in_ref, bg_ref, poolw_ref, pscale_ref,
     slng_ref, slnb_ref, sguw_ref, sgub_ref, wpa_ref, wpb_ref, wpc_ref, wout_ref,
     lng_ref, lnb_ref) = refs[:18]
    y_ref, pool_out_ref, k_out_ref, v_out_ref = refs[n_in:n_in + 4]
    (h_ref, ext_ref, s2_ref, s4_ref, s8_ref, qs_ref,
     ka_ref, kb_ref, kc_ref, kd_ref, va_ref, vb_ref, vc_ref, vd_ref,
     ain_ref, bin_ref, cin_ref, mg_ref, gate_ref, klast_ref, vlast_ref) = refs[n_in + n_out:]
    tile_rows = pl.ds(pl.multiple_of(s * TM, TM), TM)
    x_ref, y_ref, cos_ref, sin_ref = (r.at[tile_rows] for r in (x_ref, y_ref, cos_ref, sin_ref))
    i = pl.program_id(1) * SUB + s
    last = pl.num_programs(1) * SUB - 1
    kv_refs = (ka_ref, kb_ref, kc_ref, kd_ref, va_ref, vb_ref, vc_ref, vd_ref)
    row = lambda ref: ref[layer:layer + 1, :]

    @pl.when(i == 0)
    def _():
        ext_ref[0:HIST, :] = jnp.zeros((HIST, D_POOL), f32)
        for r in kv_refs:
            r[0:BLOCK, :] = jnp.zeros((BLOCK, LANES), bf16)

    xb = x_ref[...].astype(bf16)
    half = OFF_G // 2
    h_ref[:, 0:half] = _dot(xb, w_in_ref[:, 0:half])
    h_ref[:, half:OFF_G] = _dot(xb, w_in_ref[:, half:OFF_G])

    if slide:
        step = pl.program_id(0) * (last + 1) + i
        new_refs = (refs[18:18 + DEPTH], refs[18 + DEPTH:18 + 2 * DEPTH])
        old_refs = refs[18 + 2 * DEPTH:n_in]
        per_step = old_refs[0].shape[1] // SUB
        mine = lambda r: r.at[:, pl.ds(s * per_step, per_step)]
        _slide_windows(step, per_step, new_refs, [mine(r) for r in old_refs],
                       [mine(r) for r in refs[n_in + 4:n_in + 6]])

    xa = h_ref[:, OFF_XA:OFF_XA + D_POOL]
    ext_ref[HIST:HIST + TM, :] = xa
    n = HIST + TM
    s2_ref[8:n, :] = ext_ref[8:n, :] + ext_ref[7:n - 1, :]
    s4_ref[16:n, :] = s2_ref[16:n, :] + s2_ref[14:n - 2, :]
    s8_ref[24:n, :] = s4_ref[24:n, :] + s4_ref[20:n - 4, :]
    w16 = s8_ref[HIST:n, :] + s8_ref[HIST - 8:n - 8, :]
    lane_p = lax.broadcasted_iota(jnp.int32, (TM, D_POOL), 1)
    row_p = lax.broadcasted_iota(jnp.int32, (TM, D_POOL), 0)
    win = _group_select(lane_p, s2_ref[HIST:n, :], s4_ref[HIST:n, :], s8_ref[HIST:n, :], w16)
    width = _group_select(lane_p, POOL_WINDOWS[0], POOL_WINDOWS[1], POOL_WINDOWS[2], POOL_WINDOWS[3])
    cnt = jnp.minimum(row_p + (i * TM + 1), width).astype(f32)
    pooled = win / cnt - xa
    ya = _dot(pooled.astype(bf16), poolw_ref[...]) * row(pscale_ref)
    za = h_ref[:, OFF_ZA:OFF_ZA + D_POOL]
    ain_ref[...] = (ya * _silu(za)).astype(bf16)
    ext_ref[HIST - 16:HIST, :] = ext_ref[n - 16:n, :]

    vn = _layer_norm(h_ref[:, OFF_V:OFF_V + D_SGU], row(slng_ref), row(slnb_ref)).astype(bf16)
    wr = lax.broadcasted_iota(jnp.int32, (N_SGU_GROUPS * CHUNK, CHUNK), 0) % CHUNK
    wc = lax.broadcasted_iota(jnp.int32, (N_SGU_GROUPS * CHUNK, CHUNK), 1)
    w_s = jnp.where(wc <= wr, sguw_ref[...], 0.0).astype(bf16)
    lane_c = lax.broadcasted_iota(jnp.int32, (CHUNK, D_SGU), 1)
    for j in range(NBLK):
        rows = slice(j * CHUNK, (j + 1) * CHUNK)
        r = _dot(w_s, vn[rows, :])
        mixed = _group_select(lane_c, r[0:CHUNK], r[CHUNK:2 * CHUNK], r[2 * CHUNK:3 * CHUNK],
                              r[3 * CHUNK:4 * CHUNK]) + sgub_ref[...]
        yb = h_ref[rows, OFF_U:OFF_U + D_SGU] * mixed
        bin_ref[rows, :] = (yb * _silu(h_ref[rows, OFF_ZB:OFF_ZB + D_SGU])).astype(bf16)

    cos = cos_ref[...]
    sin = sin_ref[...]
    for c in range(D_ATTN // LANES):
        qc = _rope128(h_ref[:, OFF_Q + c * LANES:OFF_Q + (c + 1) * LANES], cos, sin)
        qs_ref[:, c * LANES:(c + 1) * LANES] = (qc * SCALE).astype(bf16)
    kr = _rope128(h_ref[:, OFF_K:OFF_K + D_KV], cos, sin)
    vv = h_ref[:, OFF_VV:OFF_VV + D_KV]
    klast_ref[...] = kr[TM - WINDOW:TM, :]
    vlast_ref[...] = vv[TM - WINDOW:TM, :]
    lane_k = lax.broadcasted_iota(jnp.int32, (TM, LANES), 1)
    lo = lane_k < HEAD_DIM
    for src, (a_ref, b_ref, c_ref, d_ref) in ((kr, kv_refs[0:4]), (vv, kv_refs[4:8])):
        sw = pltpu.roll(src, HEAD_DIM, 1)
        a_ref[BLOCK:BLOCK + TM, :] = jnp.where(lo, src, 0.0).astype(bf16)
        b_ref[BLOCK:BLOCK + TM, :] = jnp.where(lo, 0.0, src).astype(bf16)
        c_ref[BLOCK:BLOCK + TM, :] = jnp.where(lo, sw, 0.0).astype(bf16)
        d_ref[BLOCK:BLOCK + TM, :] = jnp.where(lo, 0.0, sw).astype(bf16)

    qrow = lax.broadcasted_iota(jnp.int32, (2 * BLOCK, 2 * BLOCK), 0) % BLOCK
    kcol = lax.broadcasted_iota(jnp.int32, (2 * BLOCK, 2 * BLOCK), 1)
    band = (kcol >= qrow) & (kcol <= qrow + WINDOW)
    band_first = band & (kcol >= jnp.where(i > 0, 0, BLOCK))
    top = lax.broadcasted_iota(jnp.int32, (2 * BLOCK, 1), 0) < BLOCK
    n_gate = N_BRANCHES * D_MODEL // NCHUNK
    n_unit = NBLK * N_KV_HEADS
    gate_sched = [range(u * n_gate // n_unit, (u + 1) * n_gate // n_unit) for u in range(n_unit)]

    def scores(u):
        j, kv = divmod(u, N_KV_HEADS)
        rows = slice(j * BLOCK, (j + 1) * BLOCK)
        keys = slice(j * BLOCK, j * BLOCK + 2 * BLOCK)
        c0 = kv * Q_PER_KV * HEAD_DIM
        qst = jnp.concatenate([qs_ref[rows, c0:c0 + LANES], qs_ref[rows, c0 + LANES:c0 + 2 * LANES]], axis=0)
        k_even, k_odd = (ka_ref, kd_ref) if kv == 0 else (kc_ref, kb_ref)
        kcat = jnp.concatenate([k_even[keys, :], k_odd[keys, :]], axis=0)
        return _dot_nt(qst, kcat)

    def attend(u, sc):
        j, kv = divmod(u, N_KV_HEADS)
        rows = slice(j * BLOCK, (j + 1) * BLOCK)
        keys = slice(j * BLOCK, j * BLOCK + 2 * BLOCK)
        allowed = band_first if j == 0 else band
        c0 = kv * Q_PER_KV * HEAD_DIM
        h0 = kv * Q_PER_KV
        v_even, v_odd = (va_ref, vd_ref) if kv == 0 else (vc_ref, vb_ref)
        probs = []
        for par in range(2):
            sink = jnp.where(top, sinks_ref[layer, h0 + par], sinks_ref[layer, h0 + 2 + par])
            sm = jnp.where(allowed, sc[:, par * 2 * BLOCK:(par + 1) * 2 * BLOCK], NEG_INF)
            m = jnp.maximum(jnp.max(sm, axis=-1, keepdims=True), sink)
            p = jnp.exp(sm - m)
            den = jnp.sum(p, axis=-1, keepdims=True) + jnp.exp(sink - m)
            probs.append((p / den).astype(bf16))
        pcat = jnp.concatenate(probs, axis=1)
        vcat = jnp.concatenate([v_even[keys, :], v_odd[keys, :]], axis=0)
        o = _dot(pcat, vcat)
        for pr in range(2):
            cols = slice(c0 + pr * LANES, c0 + (pr + 1) * LANES)
            zc = h_ref[rows, OFF_ZC + c0 + pr * LANES:OFF_ZC + c0 + (pr + 1) * LANES]
            cin_ref[rows, cols] = (o[pr * BLOCK:(pr + 1) * BLOCK] * _silu(zc)).astype(bf16)

    sc_next = scores(0)
    for u in range(n_unit):
        sc = sc_next
        for gc in gate_sched[u]:
            gcols = slice(gc * NCHUNK, (gc + 1) * NCHUNK)
            gate_ref[:, gcols] = _dot(xb, w_in_ref[:, OFF_G + gc * NCHUNK:OFF_G + (gc + 1) * NCHUNK])
        if u + 1 < n_unit:
            sc_next = scores(u + 1)
        attend(u, sc)

    for r in kv_refs:
        r[0:BLOCK, :] = r[TM:TM + BLOCK, :]

    for c in range(D_MODEL // NCHUNK):
        cols = slice(c * NCHUNK, (c + 1) * NCHUNK)
        acc = None
        for br, (in_ref, wp_ref) in enumerate(((ain_ref, wpa_ref), (bin_ref, wpb_ref), (cin_ref, wpc_ref))):
            g0 = br * D_MODEL + c * NCHUNK
            term = (_gate2(gate_ref[:, g0:g0 + NCHUNK], bg_ref[br, layer:layer + 1, cols])
                    * _dot(in_ref[...], wp_ref[:, cols]))
            acc = term if acc is None else acc + term
        mg_ref[:, cols] = acc.astype(bf16)

    for r0, r1 in OUT_CHUNKS:
        rows = slice(r0, r1)
        out = _dot(mg_ref[rows, :], wout_ref[...])
        y_ref[rows, :] = _layer_norm(ALPHA * x_ref[rows, :] + out, row(lng_ref), row(lnb_ref))

    @pl.when(i == last)
    def _():
        pool_out_ref[...] = ext_ref[n - POOL_BUF:n, :]
        k_out_ref[...] = klast_ref[...].T
        v_out_ref[...] = vlast_ref[...].T


def _prompt_layer(layer, x, cos, sin, sinks, w_in, b_gate, poolw, pscale, slng, slnb, sguw, sgub,
                  wpa, wpb, wpc, wout, lng, lnb, slide_args=None):
    B, L, _ = x.shape
    rows = TM * SUB
    nt = L // rows
    grid = (B, nt)
    row_spec = lambda w: pl.BlockSpec((rows, w), lambda b, i: (i, 0))
    lspec = lambda a: _layer_spec(a, layer, single_buffer=True)
    wspec = lambda a: _full_spec(a, single_buffer=True)
    in_specs = [
        pl.BlockSpec(memory_space=pltpu.SMEM),
        pl.BlockSpec((None, rows, D_MODEL), lambda b, i: (b, i, 0)),
        row_spec(LANES), row_spec(LANES),
        wspec(w_in), _full_spec(b_gate), lspec(poolw), _full_spec(pscale), _full_spec(slng), _full_spec(slnb),
        lspec(sguw), lspec(sgub), wspec(wpa), wspec(wpb), wspec(wpc), wspec(wout),
        _full_spec(lng), _full_spec(lnb),
    ]
    out_shape = (
        jax.ShapeDtypeStruct((B, L, D_MODEL), f32),
        jax.ShapeDtypeStruct((B, POOL_BUF, D_POOL), f32),
        jax.ShapeDtypeStruct((B, D_KV, WINDOW), f32),
        jax.ShapeDtypeStruct((B, D_KV, WINDOW), f32),
    )
    out_specs = (
        pl.BlockSpec((None, rows, D_MODEL), lambda b, i: (b, i, 0)),
        pl.BlockSpec((None, POOL_BUF, D_POOL), lambda b, i: (b, 0, 0)),
        pl.BlockSpec((None, D_KV, WINDOW), lambda b, i: (b, 0, 0)),
        pl.BlockSpec((None, D_KV, WINDOW), lambda b, i: (b, 0, 0)),
    )
    args = [sinks, x, cos, sin, w_in, b_gate, poolw, pscale, slng, slnb, sguw, sgub,
            wpa, wpb, wpc, wout, lng, lnb]
    if slide_args is not None:
        new_k, new_v, cache_k, cache_v = slide_args
        depth, nb = cache_k.shape[:2]
        per_step = nb // (B * nt)
        assert per_step * B * nt == nb and depth == DEPTH
        cache_spec = pl.BlockSpec((depth, per_step) + cache_k.shape[2:], lambda b, i: (0, b * nt + i, 0, 0, 0))
        in_specs += [_full_spec(a) for a in (*new_k, *new_v)] + [cache_spec, cache_spec]
        args += [*new_k, *new_v, cache_k, cache_v]
        out_shape += (jax.ShapeDtypeStruct(cache_k.shape, f32), jax.ShapeDtypeStruct(cache_v.shape, f32))
        out_specs += (cache_spec, cache_spec)
    kv_scratch = [pltpu.VMEM((BLOCK + TM, LANES), bf16) for _ in range(8)]
    scratch = [
        pltpu.VMEM((TM, OFF_G), f32),
        pltpu.VMEM((HIST + TM, D_POOL), f32),
        pltpu.VMEM((HIST + TM, D_POOL), f32),
        pltpu.VMEM((HIST + TM, D_POOL), f32),
        pltpu.VMEM((HIST + TM, D_POOL), f32),
        pltpu.VMEM((TM, D_ATTN), bf16),
        *kv_scratch,
        pltpu.VMEM((TM, D_POOL), bf16),
        pltpu.VMEM((TM, D_SGU), bf16),
        pltpu.VMEM((TM, D_ATTN), bf16),
        pltpu.VMEM((TM, D_MODEL), bf16),
        pltpu.VMEM((TM, N_BRANCHES * D_MODEL), f32),
        pltpu.VMEM((WINDOW, D_KV), f32),
        pltpu.VMEM((WINDOW, D_KV), f32),
    ]
    return pl.pallas_call(
        functools.partial(_prompt_kernel, layer, slide_args is not None),
        out_shape=out_shape,
        grid=grid,
        in_specs=in_specs,
        out_specs=out_specs,
        scratch_shapes=scratch,
        compiler_params=pltpu.CompilerParams(
            dimension_semantics=("arbitrary", "arbitrary"),
            vmem_limit_bytes=VMEM_LIMIT),
        name="prompt_layer",
    )(*args)


def _sample_proj_kernel(layer, x_ref, cos_ref, sin_ref, w_in_ref, bg_ref, pb_ref, poolw_ref, pscale_ref,
                        slng_ref, slnb_ref, sw0_ref, sb0_ref,
                        wb_ref, q_ref, k_ref, v_ref, kt_ref, vt_ref, pool_ref, vn_ref, ain_ref, bin_ref,
                        szc_ref, g_ref, h_ref):
    j = pl.program_id(0)
    row = lambda ref: ref[layer:layer + 1, :]
    wb_ref[:, 0:OFF_G] = w_in_ref[:, 0:OFF_G].astype(bf16)
    wb_ref[:, OFF_G:D_IN] = (0.5 * w_in_ref[:, OFF_G:D_IN]).astype(bf16)

    @pl.when(j == 0)
    def _():
        h_ref[...] = jnp.zeros(h_ref.shape, f32)

    h_ref[...] += _dot(x_ref[...].astype(bf16), wb_ref[...])

    def hcols(off, width):
        return h_ref[:, off:off + width]

    @pl.when(j == RK_STEPS - 1)
    def _():
        xa = hcols(OFF_XA, D_POOL)
        lane = lax.broadcasted_iota(jnp.int32, xa.shape, 1)
        first_row = _group_select(lane, *(POOL_BUF - (w - 1) for w in POOL_WINDOWS))
        win = xa
        for r in range(POOL_BUF):
            win = win + jnp.where(first_row <= r, pb_ref[r], 0.0)
        width = _group_select(lane, *POOL_WINDOWS).astype(f32)
        pooled = win / width - xa
        ya = _dot(pooled.astype(bf16), poolw_ref[...]) * row(pscale_ref)
        ain_ref[...] = (ya * _silu(hcols(OFF_ZA, D_POOL))).astype(bf16)
        for r in range(POOL_BUF - 1):
            pool_ref[r] = pb_ref[r + 1]
        pool_ref[POOL_BUF - 1] = xa
        vn = _layer_norm(hcols(OFF_V, D_SGU), row(slng_ref), row(slnb_ref))
        vn_ref[...] = vn
        lane_g = lax.broadcasted_iota(jnp.int32, (1, D_SGU), 1)
        sw0 = _group_select(lane_g, *(sw0_ref[layer, g] for g in range(N_SGU_GROUPS)))
        sb0 = _group_select(lane_g, *(sb0_ref[layer, g] for g in range(N_SGU_GROUPS)))
        yb = hcols(OFF_U, D_SGU) * (sw0 * vn + sb0)
        bin_ref[...] = (yb * _silu(hcols(OFF_ZB, D_SGU))).astype(bf16)
        cos = cos_ref[...]
        sin = sin_ref[...]
        for c in range(D_ATTN // LANES):
            q_ref[:, c * LANES:(c + 1) * LANES] = _rope128(hcols(OFF_Q + c * LANES, LANES), cos, sin) * SCALE
        kr = _rope128(hcols(OFF_K, D_KV), cos, sin)
        vv = hcols(OFF_VV, D_KV)
        k_ref[...] = kr
        v_ref[...] = vv
        kt_ref[...] = kr.T
        vt_ref[...] = vv.T
        szc_ref[...] = _silu(hcols(OFF_ZC, D_ATTN))
        for br in range(N_BRANCHES):
            cols = slice(br * D_MODEL, (br + 1) * D_MODEL)
            g_ref[:, cols] = _gate2(hcols(OFF_G + br * D_MODEL, D_MODEL), bg_ref[br, layer:layer + 1, :])


def _sample_proj(layer, x, cos, sin, w_in, b_gate, pool_t, poolw, pscale, slng, slnb, sw0, sb0):
    nb = x.shape[0]
    out_shape = (
        jax.ShapeDtypeStruct((D_MODEL, D_IN), bf16),
        jax.ShapeDtypeStruct((nb, D_ATTN), f32),
        jax.ShapeDtypeStruct((nb, D_KV), f32),
        jax.ShapeDtypeStruct((nb, D_KV), f32),
        jax.ShapeDtypeStruct((D_KV, nb), f32),
        jax.ShapeDtypeStruct((D_KV, nb), f32),
        jax.ShapeDtypeStruct((POOL_BUF, nb, D_POOL), f32),
        jax.ShapeDtypeStruct((nb, D_SGU), f32),
        jax.ShapeDtypeStruct((nb, D_POOL), bf16),
        jax.ShapeDtypeStruct((nb, D_SGU), bf16),
        jax.ShapeDtypeStruct((nb, D_ATTN), f32),
        jax.ShapeDtypeStruct((nb, N_BRANCHES * D_MODEL), f32),
    )
    lspec = lambda a: _layer_spec(a, layer)
    args = (x, cos, sin, w_in, b_gate, pool_t, poolw, pscale, slng, slnb, sw0, sb0)
    in_specs = [pl.BlockSpec((nb, None, RK), lambda j: (0, 0, j)), _full_spec(cos), _full_spec(sin),
                pl.BlockSpec((None, RK, D_IN), lambda j: (layer, j, 0)),
                _full_spec(b_gate), lspec(pool_t), lspec(poolw), _full_spec(pscale), _full_spec(slng),
                _full_spec(slnb), pl.BlockSpec(memory_space=pltpu.SMEM), pl.BlockSpec(memory_space=pltpu.SMEM)]
    out_specs = (pl.BlockSpec((RK, D_IN), lambda j: (j, 0)),) + tuple(_full_spec(s) for s in out_shape[1:])
    return pl.pallas_call(
        functools.partial(_sample_proj_kernel, layer),
        out_shape=out_shape,
        grid=(RK_STEPS,),
        in_specs=in_specs,
        out_specs=out_specs,
        scratch_shapes=[pltpu.VMEM((nb, D_IN), f32)],
        compiler_params=pltpu.CompilerParams(dimension_semantics=("arbitrary",),
                                             vmem_limit_bytes=VMEM_LIMIT),
        name="sample_proj",
    )(*args)


def _sample_attn_kernel(layer, sink_ref, q_ref, kc_ref, vc_ref, kn_ref, vn_ref, o_ref):
    head = lax.broadcasted_iota(jnp.int32, (1, Q_PER_KV, 1), 1)
    for kv in range(N_KV_HEADS):
        qb = q_ref[:, kv].astype(bf16)
        kn = kn_ref[:, kv]
        vn = vn_ref[:, kv]
        sink = jnp.zeros((1, Q_PER_KV, 1), f32)
        for g in range(Q_PER_KV):
            sink = jnp.where(head == g, sink_ref[layer, kv * Q_PER_KV + g], sink)
        s = jnp.einsum('bgd,bdw->bgw', qb, kc_ref[:, kv].astype(bf16), preferred_element_type=f32)
        s_new = jnp.sum(qb.astype(f32) * kn.astype(bf16).astype(f32), axis=-1, keepdims=True)
        m = jnp.maximum(jnp.maximum(jnp.max(s, axis=-1, keepdims=True), s_new), sink)
        p = jnp.exp(s - m)
        p_new = jnp.exp(s_new - m)
        den = jnp.sum(p, axis=-1, keepdims=True) + p_new + jnp.exp(sink - m)
        o = jnp.einsum('bgw,bdw->bgd', (p / den).astype(bf16), vc_ref[:, kv].astype(bf16),
                       preferred_element_type=f32)
        o_ref[:, kv] = o + (p_new / den) * vn


def _sample_attn(layer, sink, q4, kc, vc, kn, vn):
    nb = q4.shape[0]
    cache_spec = pl.BlockSpec((None, SB, N_KV_HEADS, HEAD_DIM, WINDOW), lambda b: (layer, b, 0, 0, 0))
    blk = lambda s2, s3: pl.BlockSpec((SB, N_KV_HEADS, s2, s3), lambda b: (b, 0, 0, 0))
    return pl.pallas_call(
        functools.partial(_sample_attn_kernel, layer),
        out_shape=jax.ShapeDtypeStruct((nb, N_KV_HEADS, Q_PER_KV, HEAD_DIM), f32),
        grid=(nb // SB,),
        in_specs=[pl.BlockSpec(memory_space=pltpu.SMEM), blk(Q_PER_KV, HEAD_DIM), cache_spec, cache_spec,
                  blk(1, HEAD_DIM), blk(1, HEAD_DIM)],
        out_specs=blk(Q_PER_KV, HEAD_DIM),
        compiler_params=pltpu.CompilerParams(dimension_semantics=("arbitrary",),
                                             vmem_limit_bytes=VMEM_LIMIT),
        name="sample_attn",
    )(sink, q4, kc, vc, kn, vn)


def _sample_merge_kernel(layer, x_ref, ain_ref, bin_ref, yc_ref, szc_ref, g_ref, wpa_ref, wpb_ref, wpc_ref,
                         wout_ref, lng_ref, lnb_ref, y_ref, wpa_b_ref, wpb_b_ref, wpc_b_ref, wout_b_ref):
    row = lambda ref: ref[layer:layer + 1, :]
    wpa = (0.5 * wpa_ref[...]).astype(bf16)
    wpb = (0.5 * wpb_ref[...]).astype(bf16)
    wpc = (0.5 * wpc_ref[...]).astype(bf16)
    wout = wout_ref[...].astype(bf16)
    wpa_b_ref[...] = wpa
    wpb_b_ref[...] = wpb
    wpc_b_ref[...] = wpc
    wout_b_ref[...] = wout
    cin = (yc_ref[...] * szc_ref[...]).astype(bf16)
    merged = (g_ref[:, 0:D_MODEL] * _dot(ain_ref[...], wpa)
              + g_ref[:, D_MODEL:2 * D_MODEL] * _dot(bin_ref[...], wpb)
              + g_ref[:, 2 * D_MODEL:3 * D_MODEL] * _dot(cin, wpc))
    out = _dot(merged.astype(bf16), wout)
    y_ref[...] = _layer_norm(ALPHA * x_ref[...] + out, row(lng_ref), row(lnb_ref))


def _sample_merge(layer, x, ain, bin_, yc, szc, g, wpa, wpb, wpc, wout, lng, lnb):
    lspec = lambda a: _layer_spec(a, layer)
    nb = x.shape[0]
    x_spec = pl.BlockSpec((nb, None, D_MODEL), lambda i: (0, 0, 0))
    out_shape = (jax.ShapeDtypeStruct(x.shape, f32),) + tuple(
        jax.ShapeDtypeStruct(w.shape[1:], bf16) for w in (wpa, wpb, wpc, wout))
    return pl.pallas_call(
        functools.partial(_sample_merge_kernel, layer),
        out_shape=out_shape,
        grid=(1,),
        in_specs=[x_spec, _full_spec(ain), _full_spec(bin_), _full_spec(yc), _full_spec(szc),
                  _full_spec(g), lspec(wpa), lspec(wpb), lspec(wpc), lspec(wout), _full_spec(lng),
                  _full_spec(lnb)],
        out_specs=(x_spec,) + tuple(_full_spec(s) for s in out_shape[1:]),
        compiler_params=pltpu.CompilerParams(dimension_semantics=("arbitrary",),
                                             vmem_limit_bytes=VMEM_LIMIT),
        name="sample_merge",
    )(x, ain, bin_, yc, szc, g, wpa, wpb, wpc, wout, lng, lnb)


def _rope_tables(positions):
    halfd = HEAD_DIM // 2
    inv = ROPE_THETA ** (-np.arange(halfd, dtype=np.float64) / halfd)
    ang = np.asarray(positions, dtype=np.float64)[:, None] * inv[None, :]
    cos = np.tile(np.cos(ang), (1, LANES // halfd))
    sin = np.tile(np.concatenate([-np.sin(ang), np.sin(ang)], axis=1), (1, LANES // HEAD_DIM))
    return jnp.asarray(cos, f32), jnp.asarray(sin, f32)


def _block_diag(w):
    nl, g, c, _ = w.shape
    eye = jnp.eye(g, dtype=w.dtype)
    return (eye[None, :, None, :, None] * w[:, :, :, None, :]).reshape(nl, g * c, g * c)


def kernel(x_prompt, x_sample, state_pool, cache_k_win, cache_v_win, w_in, b_gate, pool_w, pool_scale, sgu_ln_g, sgu_ln_b, sgu_w, sgu_b, attn_sinks, w_proj_a, w_proj_b, w_proj_c, w_out, ln_g, ln_b):
    B, L, _ = x_prompt.shape
    nb = x_sample.shape[0]
    cos_p, sin_p = _rope_tables(np.arange(L))
    cos_s, sin_s = _rope_tables(np.array([PAST_LEN]))

    poolw = _block_diag(pool_w).astype(bf16)
    sguw = sgu_w.reshape(DEPTH, N_SGU_GROUPS * CHUNK, CHUNK)
    sgub = jnp.repeat(jnp.swapaxes(sgu_b, 1, 2), POOL_GC, axis=2)
    sw0 = sgu_w[:, :, 0, 0]
    sb0 = sgu_b[:, :, 0]
    bg_t = jnp.transpose(b_gate, (1, 0, 2))
    kc_t = jnp.transpose(cache_k_win, (0, 1, 3, 4, 2))
    vc_t = jnp.transpose(cache_v_win, (0, 1, 3, 4, 2))
    pool_t = jnp.transpose(state_pool, (0, 2, 1, 3))

    y_s = x_sample
    pool_s, chunk_v, kts, vts, weights = ([] for _ in range(5))
    for l in range(DEPTH):
        w_in_b, q, kn, vn, kt, vt, ps, cv, ain, bin_, szc, g = _sample_proj(
            l, y_s, cos_s, sin_s, w_in, bg_t, pool_t, poolw, pool_scale, sgu_ln_g, sgu_ln_b, sw0, sb0)
        o = _sample_attn(
            l, attn_sinks, q.reshape(nb, N_KV_HEADS, Q_PER_KV, HEAD_DIM), kc_t, vc_t,
            kn.reshape(nb, N_KV_HEADS, 1, HEAD_DIM), vn.reshape(nb, N_KV_HEADS, 1, HEAD_DIM))
        y_s, wpa, wpb, wpc, wout = _sample_merge(
            l, y_s, ain, bin_, o.reshape(nb, D_ATTN), szc, g, w_proj_a, w_proj_b, w_proj_c, w_out, ln_g, ln_b)
        pool_s.append(ps); chunk_v.append(cv); kts.append(kt); vts.append(vt)
        weights.append((w_in_b, wpa, wpb, wpc, wout))

    y_p = x_prompt
    pool_p, k_p, v_p = ([] for _ in range(3))
    for l in range(DEPTH):
        w_in_b, wpa, wpb, wpc, wout = weights[l]
        slide_args = (kts, vts, kc_t, vc_t) if l == DEPTH - 1 else None
        y_p, pp, kp, vp, *slid = _prompt_layer(
            l, y_p, cos_p, sin_p, attn_sinks, w_in_b, bg_t, poolw, pool_scale, sgu_ln_g, sgu_ln_b, sguw,
            sgub, wpa, wpb, wpc, wout, ln_g, ln_b, slide_args=slide_args)
        pool_p.append(pp); k_p.append(kp); v_p.append(vp)
    k_s, v_s = slid

    to_cache = lambda a: jnp.transpose(a, (0, 1, 4, 2, 3))
    prompt_cache = lambda lst: to_cache(jnp.stack(lst).reshape(DEPTH, B, N_KV_HEADS, HEAD_DIM, WINDOW))
    return (y_p, y_s,
            jnp.stack(pool_p), prompt_cache(k_p), prompt_cache(v_p),
            jnp.transpose(jnp.stack(pool_s), (0, 2, 1, 3)), to_cache(k_s), to_cache(v_s),
            jnp.stack(chunk_v).reshape(DEPTH, nb, 1, D_SGU))
```

```python
import functools

import numpy as np
import jax
import jax.numpy as jnp
from jax import lax
from jax.experimental import pallas as pl
from jax.experimental.pallas import tpu as pltpu

D_MODEL = 1024
DEPTH = 2
PAST_LEN = 8192
D_POOL = 256
POOL_WINDOWS = (2, 4, 8, 16)
POOL_GC = 64
POOL_BUF = 15
D_SGU = 256
CHUNK = 128
N_SGU_GROUPS = 4
HEAD_DIM = 64
N_HEADS = 8
N_KV_HEADS = 2
Q_PER_KV = 4
D_ATTN = 512
D_KV = 128
WINDOW = 128
BLOCK = 128
ROPE_THETA = 10000.0
N_BRANCHES = 3
D_IN = 2 * D_POOL + 3 * D_SGU + 2 * D_ATTN + 2 * D_KV + N_BRANCHES * D_MODEL
ALPHA = (2.0 * DEPTH) ** 0.25
LN_EPS = 1e-5
NEG_INF = -1e30
SCALE = HEAD_DIM ** -0.5

OFF_XA, OFF_ZA, OFF_U, OFF_V, OFF_ZB = 0, 256, 512, 768, 1024
OFF_Q, OFF_K, OFF_VV, OFF_ZC, OFF_G = 1280, 1792, 1920, 2048, 2560

LANES = 128
TM = 512
SUB = 2
NBLK = TM // BLOCK
HIST = 32
NCHUNK = 256
OUT_CHUNKS = ((0, 256), (256, 512))
VMEM_LIMIT = 56 * 1024 * 1024
SB = 32
RK = 512
RK_STEPS = D_MODEL // RK

bf16 = jnp.bfloat16
f32 = jnp.float32


def _dot(a, b):
    return jnp.dot(a, b, preferred_element_type=f32)


def _dot_nt(a, b):
    return lax.dot_general(a, b, (((1,), (1,)), ((), ())), preferred_element_type=f32)


def _sigmoid(z):
    return 0.5 * jnp.tanh(0.5 * z) + 0.5


def _silu(z):
    return z * _sigmoid(z)


def _gate2(half_pre, bias):
    return jnp.tanh(half_pre + 0.5 * bias) + 1.0


def _layer_norm(x, g, b):
    mu = jnp.mean(x, axis=-1, keepdims=True)
    xc = x - mu
    var = jnp.mean(xc * xc, axis=-1, keepdims=True)
    return xc * lax.rsqrt(var + LN_EPS) * g + b


def _rope128(x, cos, sin_signed):
    lane = lax.broadcasted_iota(jnp.int32, x.shape, 1)
    first_half = (lane % HEAD_DIM) < (HEAD_DIM // 2)
    partner = jnp.where(first_half,
                        pltpu.roll(x, LANES - HEAD_DIM // 2, 1),
                        pltpu.roll(x, HEAD_DIM // 2, 1))
    return x * cos + partner * sin_signed


def _group_select(lane, a0, a1, a2, a3):
    return jnp.where(lane < 64, a0, jnp.where(lane < 128, a1, jnp.where(lane < 192, a2, a3)))


def _layer_spec(arr, layer, single_buffer=False):
    block = (None,) + arr.shape[1:]
    zeros = (0,) * (arr.ndim - 1)
    index_map = lambda *_: (layer,) + zeros
    if single_buffer:
        return pl.BlockSpec(block, index_map, pipeline_mode=pl.Buffered(1))
    return pl.BlockSpec(block, index_map)


def _full_spec(arr, single_buffer=False):
    zeros = (0,) * len(arr.shape)
    if single_buffer:
        return pl.BlockSpec(arr.shape, lambda *_: zeros, pipeline_mode=pl.Buffered(1))
    return pl.BlockSpec(arr.shape, lambda *_: zeros)


def _slide_windows(step, per_step, new_refs, old_refs, out_refs):
    lane = lax.broadcasted_iota(jnp.int32, (D_KV, WINDOW), 1)
    shift = jnp.where(step == 0, 0, LANES - step * per_step)
    for new_ref, old_ref, out_ref in zip(new_refs, old_refs, out_refs):
        for l in range(DEPTH):
            new_cols = pltpu.roll(new_ref[l][...], shift, 1)
            for s in range(per_step):
                slid = pltpu.roll(old_ref[l, s].reshape(D_KV, WINDOW), WINDOW - 1, 1)
                newest = jnp.broadcast_to(new_cols[:, s:s + 1], (D_KV, WINDOW))
                out_ref[l, s] = jnp.where(lane == WINDOW - 1, newest, slid).reshape(
                    N_KV_HEADS, HEAD_DIM, WINDOW)


def _prompt_kernel(layer, slide, *refs):
    def sub_tile(s, carry):
        _prompt_tile(layer, slide, s, refs)
        return carry
    lax.fori_loop(0, SUB, sub_tile, 0)


def _prompt_tile(layer, slide, s, refs):
    n_in = 18 + (2 * DEPTH + 2 if slide else 0)
    n_out = 4 + (2 if slide else 0)
    (sinks_ref, x_ref, cos_ref, sin_ref, w_in_ref, bg_ref, poolw_ref, pscale_ref,
     slng_ref, slnb_ref, sguw_ref, sgub_ref, wpa_ref, wpb_ref, wpc_ref, wout_ref,
     lng_ref, lnb_ref) = refs[:18]
    y_ref, pool_out_ref, k_out_ref, v_out_ref = refs[n_in:n_in + 4]
    (h_ref, ext_ref, s2_ref, s4_ref, s8_ref, qs_ref,
     ka_ref, kb_ref, kc_ref, kd_ref, va_ref, vb_ref, vc_ref, vd_ref,
     ain_ref, bin_ref, cin_ref, mg_ref, gate_ref, klast_ref, vlast_ref) = refs[n_in + n_out:]
    tile_rows = pl.ds(pl.multiple_of(s * TM, TM), TM)
    x_ref, y_ref, cos_ref, sin_ref = (r.at[tile_rows] for r in (x_ref, y_ref, cos_ref, sin_ref))
    i = pl.program_id(1) * SUB + s
    last = pl.num_programs(1) * SUB - 1
    kv_refs = (ka_ref, kb_ref, kc_ref, kd_ref, va_ref, vb_ref, vc_ref, vd_ref)
    row = lambda ref: ref[layer:layer + 1, :]

    @pl.when(i == 0)
    def _():
        ext_ref[0:HIST, :] = jnp.zeros((HIST, D_POOL), f32)
        for r in kv_refs:
            r[0:BLOCK, :] = jnp.zeros((BLOCK, LANES), bf16)

    xb = x_ref[...].astype(bf16)
    half = OFF_G // 2
    h_ref[:, 0:half] = _dot(xb, w_in_ref[:, 0:half])
    h_ref[:, half:OFF_G] = _dot(xb, w_in_ref[:, half:OFF_G])

    if slide:
        step = pl.program_id(0) * (last + 1) + i
        new_refs = (refs[18:18 + DEPTH], refs[18 + DEPTH:18 + 2 * DEPTH])
        old_refs = refs[18 + 2 * DEPTH:n_in]
        per_step = old_refs[0].shape[1] // SUB
        mine = lambda r: r.at[:, pl.ds(s * per_step, per_step)]
        _slide_windows(step, per_step, new_refs, [mine(r) for r in old_refs],
                       [mine(r) for r in refs[n_in + 4:n_in + 6]])

    xa = h_ref[:, OFF_XA:OFF_XA + D_POOL]
    ext_ref[HIST:HIST + TM, :] = xa
    n = HIST + TM
    s2_ref[8:n, :] = ext_ref[8:n, :] + ext_ref[7:n - 1, :]
    s4_ref[16:n, :] = s2_ref[16:n, :] + s2_ref[14:n - 2, :]
    s8_ref[24:n, :] = s4_ref[24:n, :] + s4_ref[20:n - 4, :]
    w16 = s8_ref[HIST:n, :] + s8_ref[HIST - 8:n - 8, :]
    lane_p = lax.broadcasted_iota(jnp.int32, (TM, D_POOL), 1)
    row_p = lax.broadcasted_iota(jnp.int32, (TM, D_POOL), 0)
    win = _group_select(lane_p, s2_ref[HIST:n, :], s4_ref[HIST:n, :], s8_ref[HIST:n, :], w16)
    width = _group_select(lane_p, POOL_WINDOWS[0], POOL_WINDOWS[1], POOL_WINDOWS[2], POOL_WINDOWS[3])
    cnt = jnp.minimum(row_p + (i * TM + 1), width).astype(f32)
    pooled = win / cnt - xa
    ya = _dot(pooled.astype(bf16), poolw_ref[...]) * row(pscale_ref)
    za = h_ref[:, OFF_ZA:OFF_ZA + D_POOL]
    ain_ref[...] = (ya * _silu(za)).astype(bf16)
    ext_ref[HIST - 16:HIST, :] = ext_ref[n - 16:n, :]

    vn = _layer_norm(h_ref[:, OFF_V:OFF_V + D_SGU], row(slng_ref), row(slnb_ref)).astype(bf16)
    wr = lax.broadcasted_iota(jnp.int32, (N_SGU_GROUPS * CHUNK, CHUNK), 0) % CHUNK
    wc = lax.broadcasted_iota(jnp.int32, (N_SGU_GROUPS * CHUNK, CHUNK), 1)
    w_s = jnp.where(wc <= wr, sguw_ref[...], 0.0).astype(bf16)
    lane_c = lax.broadcasted_iota(jnp.int32, (CHUNK, D_SGU), 1)
    for j in range(NBLK):
        rows = slice(j * CHUNK, (j + 1) * CHUNK)
        r = _dot(w_s, vn[rows, :])
        mixed = _group_select(lane_c, r[0:CHUNK], r[CHUNK:2 * CHUNK], r[2 * CHUNK:3 * CHUNK],
                              r[3 * CHUNK:4 * CHUNK]) + sgub_ref[...]
        yb = h_ref[rows, OFF_U:OFF_U + D_SGU] * mixed
        bin_ref[rows, :] = (yb * _silu(h_ref[rows, OFF_ZB:OFF_ZB + D_SGU])).astype(bf16)

    cos = cos_ref[...]
    sin = sin_ref[...]
    for c in range(D_ATTN // LANES):
        qc = _rope128(h_ref[:, OFF_Q + c * LANES:OFF_Q + (c + 1) * LANES], cos, sin)
        qs_ref[:, c * LANES:(c + 1) * LANES] = (qc * SCALE).astype(bf16)
    kr = _rope128(h_ref[:, OFF_K:OFF_K + D_KV], cos, sin)
    vv = h_ref[:, OFF_VV:OFF_VV + D_KV]
    klast_ref[...] = kr[TM - WINDOW:TM, :]
    vlast_ref[...] = vv[TM - WINDOW:TM, :]
    lane_k = lax.broadcasted_iota(jnp.int32, (TM, LANES), 1)
    lo = lane_k < HEAD_DIM
    for src, (a_ref, b_ref, c_ref, d_ref) in ((kr, kv_refs[0:4]), (vv, kv_refs[4:8])):
        sw = pltpu.roll(src, HEAD_DIM, 1)
        a_ref[BLOCK:BLOCK + TM, :] = jnp.where(lo, src, 0.0).astype(bf16)
        b_ref[BLOCK:BLOCK + TM, :] = jnp.where(lo, 0.0, src).astype(bf16)
        c_ref[BLOCK:BLOCK + TM, :] = jnp.where(lo, sw, 0.0).astype(bf16)
        d_ref[BLOCK:BLOCK + TM, :] = jnp.where(lo, 0.0, sw).astype(bf16)

    qrow = lax.broadcasted_iota(jnp.int32, (2 * BLOCK, 2 * BLOCK), 0) % BLOCK
    kcol = lax.broadcasted_iota(jnp.int32, (2 * BLOCK, 2 * BLOCK), 1)
    band = (kcol >= qrow) & (kcol <= qrow + WINDOW)
    band_first = band & (kcol >= jnp.where(i > 0, 0, BLOCK))
    top = lax.broadcasted_iota(jnp.int32, (2 * BLOCK, 1), 0) < BLOCK
    n_gate = N_BRANCHES * D_MODEL // NCHUNK
    n_unit = NBLK * N_KV_HEADS
    gate_sched = [range(u * n_gate // n_unit, (u + 1) * n_gate // n_unit) for u in range(n_unit)]

    def scores(u):
        j, kv = divmod(u, N_KV_HEADS)
        rows = slice(j * BLOCK, (j + 1) * BLOCK)
        keys = slice(j * BLOCK, j * BLOCK + 2 * BLOCK)
        c0 = kv * Q_PER_KV * HEAD_DIM
        qst = jnp.concatenate([qs_ref[rows, c0:c0 + LANES], qs_ref[rows, c0 + LANES:c0 + 2 * LANES]], axis=0)
        k_even, k_odd = (ka_ref, kd_ref) if kv == 0 else (kc_ref, kb_ref)
        kcat = jnp.concatenate([k_even[keys, :], k_odd[keys, :]], axis=0)
        return _dot_nt(qst, kcat)

    def attend(u, sc):
        j, kv = divmod(u, N_KV_HEADS)
        rows = slice(j * BLOCK, (j + 1) * BLOCK)
        keys = slice(j * BLOCK, j * BLOCK + 2 * BLOCK)
        allowed = band_first if j == 0 else band
        c0 = kv * Q_PER_KV * HEAD_DIM
        h0 = kv * Q_PER_KV
        v_even, v_odd = (va_ref, vd_ref) if kv == 0 else (vc_ref, vb_ref)
        probs = []
        for par in range(2):
            sink = jnp.where(top, sinks_ref[layer, h0 + par], sinks_ref[layer, h0 + 2 + par])
            sm = jnp.where(allowed, sc[:, par * 2 * BLOCK:(par + 1) * 2 * BLOCK], NEG_INF)
            m = jnp.maximum(jnp.max(sm, axis=-1, keepdims=True), sink)
            p = jnp.exp(sm - m)
            den = jnp.sum(p, axis=-1, keepdims=True) + jnp.exp(sink - m)
            probs.append((p / den).astype(bf16))
        pcat = jnp.concatenate(probs, axis=1)
        vcat = jnp.concatenate([v_even[keys, :], v_odd[keys, :]], axis=0)
        o = _dot(pcat, vcat)
        for pr in range(2):
            cols = slice(c0 + pr * LANES, c0 + (pr + 1) * LANES)
            zc = h_ref[rows, OFF_ZC + c0 + pr * LANES:OFF_ZC + c0 + (pr + 1) * LANES]
            cin_ref[rows, cols] = (o[pr * BLOCK:(pr + 1) * BLOCK] * _silu(zc)).astype(bf16)

    sc_next = scores(0)
    for u in range(n_unit):
        sc = sc_next
        for gc in gate_sched[u]:
            gcols = slice(gc * NCHUNK, (gc + 1) * NCHUNK)
            gate_ref[:, gcols] = _dot(xb, w_in_ref[:, OFF_G + gc * NCHUNK:OFF_G + (gc + 1) * NCHUNK])
        if u + 1 < n_unit:
            sc_next = scores(u + 1)
        attend(u, sc)

    for r in kv_refs:
        r[0:BLOCK, :] = r[TM:TM + BLOCK, :]

    for c in range(D_MODEL // NCHUNK):
        cols = slice(c * NCHUNK, (c + 1) * NCHUNK)
        acc = None
        for br, (in_ref, wp_ref) in enumerate(((ain_ref, wpa_ref), (bin_ref, wpb_ref), (cin_ref, wpc_ref))):
            g0 = br * D_MODEL + c * NCHUNK
            term = (_gate2(gate_ref[:, g0:g0 + NCHUNK], bg_ref[br, layer:layer + 1, cols])
                    * _dot(in_ref[...], wp_ref[:, cols]))
            acc = term if acc is None else acc + term
        mg_ref[:, cols] = acc.astype(bf16)

    for r0, r1 in OUT_CHUNKS:
        rows = slice(r0, r1)
        out = _dot(mg_ref[rows, :], wout_ref[...])
        y_ref[rows, :] = _layer_norm(ALPHA * x_ref[rows, :] + out, row(lng_ref), row(lnb_ref))

    @pl.when(i == last)
    def _():
        pool_out_ref[...] = ext_ref[n - POOL_BUF:n, :]
        k_out_ref[...] = klast_ref[...].T
        v_out_ref[...] = vlast_ref[...].T


def _prompt_layer(layer, x, cos, sin, sinks, w_in, b_gate, poolw, pscale, slng, slnb, sguw, sgub,
                  wpa, wpb, wpc, wout, lng, lnb, slide_args=None):
    B, L, _ = x.shape
    rows = TM * SUB
    nt = L // rows
    grid = (B, nt)
    row_spec = lambda w: pl.BlockSpec((rows, w), lambda b, i: (i, 0))
    lspec = lambda a: _layer_spec(a, layer, single_buffer=True)
    wspec = lambda a: _full_spec(a, single_buffer=True)
    in_specs = [
        pl.BlockSpec(memory_space=pltpu.SMEM),
        pl.BlockSpec((None, rows, D_MODEL), lambda b, i: (b, i, 0)),
        row_spec(LANES), row_spec(LANES),
        wspec(w_in), _full_spec(b_gate), lspec(poolw), _full_spec(pscale), _full_spec(slng), _full_spec(slnb),
        lspec(sguw), lspec(sgub), wspec(wpa), wspec(wpb), wspec(wpc), wspec(wout),
        _full_spec(lng), _full_spec(lnb),
    ]
    out_shape = (
        jax.ShapeDtypeStruct((B, L, D_MODEL), f32),
        jax.ShapeDtypeStruct((B, POOL_BUF, D_POOL), f32),
        jax.ShapeDtypeStruct((B, D_KV, WINDOW), f32),
        jax.ShapeDtypeStruct((B, D_KV, WINDOW), f32),
    )
    out_specs = (
        pl.BlockSpec((None, rows, D_MODEL), lambda b, i: (b, i, 0)),
        pl.BlockSpec((None, POOL_BUF, D_POOL), lambda b, i: (b, 0, 0)),
        pl.BlockSpec((None, D_KV, WINDOW), lambda b, i: (b, 0, 0)),
        pl.BlockSpec((None, D_KV, WINDOW), lambda b, i: (b, 0, 0)),
    )
    args = [sinks, x, cos, sin, w_in, b_gate, poolw, pscale, slng, slnb, sguw, sgub,
            wpa, wpb, wpc, wout, lng, lnb]
    if slide_args is not None:
        new_k, new_v, cache_k, cache_v = slide_args
        depth, nb = cache_k.shape[:2]
        per_step = nb // (B * nt)
        assert per_step * B * nt == nb and depth == DEPTH
        cache_spec = pl.BlockSpec((depth, per_step) + cache_k.shape[2:], lambda b, i: (0, b * nt + i, 0, 0, 0))
        in_specs += [_full_spec(a) for a in (*new_k, *new_v)] + [cache_spec, cache_spec]
        args += [*new_k, *new_v, cache_k, cache_v]
        out_shape += (jax.ShapeDtypeStruct(cache_k.shape, f32), jax.ShapeDtypeStruct(cache_v.shape, f32))
        out_specs += (cache_spec, cache_spec)
    kv_scratch = [pltpu.VMEM((BLOCK + TM, LANES), bf16) for _ in range(8)]
    scratch = [
        pltpu.VMEM((TM, OFF_G), f32),
        pltpu.VMEM((HIST + TM, D_POOL), f32),
        pltpu.VMEM((HIST + TM, D_POOL), f32),
        pltpu.VMEM((HIST + TM, D_POOL), f32),
        pltpu.VMEM((HIST + TM, D_POOL), f32),
        pltpu.VMEM((TM, D_ATTN), bf16),
        *kv_scratch,
        pltpu.VMEM((TM, D_POOL), bf16),
        pltpu.VMEM((TM, D_SGU), bf16),
        pltpu.VMEM((TM, D_ATTN), bf16),
        pltpu.VMEM((TM, D_MODEL), bf16),
        pltpu.VMEM((TM, N_BRANCHES * D_MODEL), f32),
        pltpu.VMEM((WINDOW, D_KV), f32),
        pltpu.VMEM((WINDOW, D_KV), f32),
    ]
    return pl.pallas_call(
        functools.partial(_prompt_kernel, layer, slide_args is not None),
        out_shape=out_shape,
        grid=grid,
        in_specs=in_specs,
        out_specs=out_specs,
        scratch_shapes=scratch,
        compiler_params=pltpu.CompilerParams(
            dimension_semantics=("arbitrary", "arbitrary"),
            vmem_limit_bytes=VMEM_LIMIT),
        name="prompt_layer",
    )(*args)


def _sample_proj_kernel(layer, x_ref, cos_ref, sin_ref, w_in_ref, bg_ref, pb_ref, poolw_ref, pscale_ref,
                        slng_ref, slnb_ref, sw0_ref, sb0_ref,
                        wb_ref, qkv_ref, kt_ref, vt_ref, pool_ref, vn_ref, ain_ref, bin_ref,
                        szc_ref, g_ref, h_ref):
    j = pl.program_id(0)
    row = lambda ref: ref[layer:layer + 1, :]
    wb_ref[:, 0:OFF_G] = w_in_ref[:, 0:OFF_G].astype(bf16)
    wb_ref[:, OFF_G:D_IN] = (0.5 * w_in_ref[:, OFF_G:D_IN]).astype(bf16)

    @pl.when(j == 0)
    def _():
        h_ref[...] = jnp.zeros(h_ref.shape, f32)

    h_ref[...] += _dot(x_ref[...].astype(bf16), wb_ref[...])

    def hcols(off, width):
        return h_ref[:, off:off + width]

    @pl.when(j == RK_STEPS - 1)
    def _():
        xa = hcols(OFF_XA, D_POOL)
        lane = lax.broadcasted_iota(jnp.int32, xa.shape, 1)
        first_row = _group_select(lane, *(POOL_BUF - (w - 1) for w in POOL_WINDOWS))
        win = xa
        for r in range(POOL_BUF):
            win = win + jnp.where(first_row <= r, pb_ref[r], 0.0)
        width = _group_select(lane, *POOL_WINDOWS).astype(f32)
        pooled = win / width - xa
        ya = _dot(pooled.astype(bf16), poolw_ref[...]) * row(pscale_ref)
        ain_ref[...] = (ya * _silu(hcols(OFF_ZA, D_POOL))).astype(bf16)
        for r in range(POOL_BUF - 1):
            pool_ref[r] = pb_ref[r + 1]
        pool_ref[POOL_BUF - 1] = xa
        vn = _layer_norm(hcols(OFF_V, D_SGU), row(slng_ref), row(slnb_ref))
        vn_ref[...] = vn
        lane_g = lax.broadcasted_iota(jnp.int32, (1, D_SGU), 1)
        sw0 = _group_select(lane_g, *(sw0_ref[layer, g] for g in range(N_SGU_GROUPS)))
        sb0 = _group_select(lane_g, *(sb0_ref[layer, g] for g in range(N_SGU_GROUPS)))
        yb = hcols(OFF_U, D_SGU) * (sw0 * vn + sb0)
        bin_ref[...] = (yb * _silu(hcols(OFF_ZB, D_SGU))).astype(bf16)
        cos = cos_ref[...]
        sin = sin_ref[...]
        for c in range(D_ATTN // LANES):
            qkv_ref[:, c * LANES:(c + 1) * LANES] = _rope128(hcols(OFF_Q + c * LANES, LANES), cos, sin) * SCALE
        kr = _rope128(hcols(OFF_K, D_KV), cos, sin)
        vv = hcols(OFF_VV, D_KV)
        qkv_ref[:, D_ATTN:D_ATTN + D_KV] = kr
        qkv_ref[:, D_ATTN + D_KV:D_ATTN + 2 * D_KV] = vv
        kt_ref[...] = kr.T
        vt_ref[...] = vv.T
        szc_ref[...] = _silu(hcols(OFF_ZC, D_ATTN))
        for br in range(N_BRANCHES):
            cols = slice(br * D_MODEL, (br + 1) * D_MODEL)
            g_ref[:, cols] = _gate2(hcols(OFF_G + br * D_MODEL, D_MODEL), bg_ref[br, layer:layer + 1, :])


def _sample_proj(layer, x, cos, sin, w_in, b_gate, pool_t, poolw, pscale, slng, slnb, sw0, sb0):
    nb = x.shape[0]
    out_shape = (
        jax.ShapeDtypeStruct((D_MODEL, D_IN), bf16),
        jax.ShapeDtypeStruct((nb, D_ATTN + 2 * D_KV), f32),
        jax.ShapeDtypeStruct((D_KV, nb), f32),
        jax.ShapeDtypeStruct((D_KV, nb), f32),
        jax.ShapeDtypeStruct((POOL_BUF, nb, D_POOL), f32),
        jax.ShapeDtypeStruct((nb, 1, D_SGU), f32),
        jax.ShapeDtypeStruct((nb, D_POOL), bf16),
        jax.ShapeDtypeStruct((nb, D_SGU), bf16),
        jax.ShapeDtypeStruct((nb, D_ATTN), f32),
        jax.ShapeDtypeStruct((nb, N_BRANCHES * D_MODEL), f32),
    )
    lspec = lambda a: _layer_spec(a, layer)
    args = (x, cos, sin, w_in, b_gate, pool_t, poolw, pscale, slng, slnb, sw0, sb0)
    in_specs = [pl.BlockSpec((nb, None, RK), lambda j: (0, 0, j)), _full_spec(cos), _full_spec(sin),
                pl.BlockSpec((None, RK, D_IN), lambda j: (layer, j, 0)),
                _full_spec(b_gate), lspec(pool_t), lspec(poolw), _full_spec(pscale), _full_spec(slng),
                _full_spec(slnb), pl.BlockSpec(memory_space=pltpu.SMEM), pl.BlockSpec(memory_space=pltpu.SMEM)]
    squeeze_mid = lambda s: pl.BlockSpec((s.shape[0], None, s.shape[2]), lambda j: (0, 0, 0))
    out_specs = (pl.BlockSpec((RK, D_IN), lambda j: (j, 0)),) + tuple(
        squeeze_mid(s) if len(s.shape) == 3 and s.shape[1] == 1 else _full_spec(s) for s in out_shape[1:])
    return pl.pallas_call(
        functools.partial(_sample_proj_kernel, layer),
        out_shape=out_shape,
        grid=(RK_STEPS,),
        in_specs=in_specs,
        out_specs=out_specs,
        scratch_shapes=[pltpu.VMEM((nb, D_IN), f32)],
        compiler_params=pltpu.CompilerParams(dimension_semantics=("arbitrary",),
                                             vmem_limit_bytes=VMEM_LIMIT),
        name="sample_proj",
    )(*args)


def _sample_attn_kernel(layer, sink_ref, qkv_ref, kc_ref, vc_ref, o_ref):
    head = lax.broadcasted_iota(jnp.int32, (1, Q_PER_KV, 1), 1)
    for kv in range(N_KV_HEADS):
        qb = qkv_ref[:, kv * Q_PER_KV:(kv + 1) * Q_PER_KV, :].astype(bf16)
        kn = qkv_ref[:, N_HEADS + kv:N_HEADS + kv + 1, :]
        vn = qkv_ref[:, N_HEADS + N_KV_HEADS + kv:N_HEADS + N_KV_HEADS + kv + 1, :]
        sink = jnp.zeros((1, Q_PER_KV, 1), f32)
        for g in range(Q_PER_KV):
            sink = jnp.where(head == g, sink_ref[layer, kv * Q_PER_KV + g], sink)
        s = jnp.einsum('bgd,bdw->bgw', qb, kc_ref[:, kv].astype(bf16), preferred_element_type=f32)
        s_new = jnp.sum(qb.astype(f32) * kn.astype(bf16).astype(f32), axis=-1, keepdims=True)
        m = jnp.maximum(jnp.maximum(jnp.max(s, axis=-1, keepdims=True), s_new), sink)
        p = jnp.exp(s - m)
        p_new = jnp.exp(s_new - m)
        den = jnp.sum(p, axis=-1, keepdims=True) + p_new + jnp.exp(sink - m)
        o = jnp.einsum('bgw,bdw->bgd', (p / den).astype(bf16), vc_ref[:, kv].astype(bf16),
                       preferred_element_type=f32)
        o_ref[:, kv] = o + (p_new / den) * vn


def _sample_attn(layer, sink, qkv3, kc, vc):
    nb, n_rows, _ = qkv3.shape
    cache_spec = pl.BlockSpec((None, SB, N_KV_HEADS, HEAD_DIM, WINDOW), lambda b: (layer, b, 0, 0, 0))
    return pl.pallas_call(
        functools.partial(_sample_attn_kernel, layer),
        out_shape=jax.ShapeDtypeStruct((nb, N_KV_HEADS, Q_PER_KV, HEAD_DIM), f32),
        grid=(nb // SB,),
        in_specs=[pl.BlockSpec(memory_space=pltpu.SMEM),
                  pl.BlockSpec((SB, n_rows, HEAD_DIM), lambda b: (b, 0, 0)), cache_spec, cache_spec],
        out_specs=pl.BlockSpec((SB, N_KV_HEADS, Q_PER_KV, HEAD_DIM), lambda b: (b, 0, 0, 0)),
        compiler_params=pltpu.CompilerParams(dimension_semantics=("arbitrary",),
                                             vmem_limit_bytes=VMEM_LIMIT),
        name="sample_attn",
    )(sink, qkv3, kc, vc)


def _sample_merge_kernel(layer, x_ref, ain_ref, bin_ref, yc_ref, szc_ref, g_ref, wpa_ref, wpb_ref, wpc_ref,
                         wout_ref, lng_ref, lnb_ref, y_ref, wpa_b_ref, wpb_b_ref, wpc_b_ref, wout_b_ref):
    row = lambda ref: ref[layer:layer + 1, :]
    wpa = (0.5 * wpa_ref[...]).astype(bf16)
    wpb = (0.5 * wpb_ref[...]).astype(bf16)
    wpc = (0.5 * wpc_ref[...]).astype(bf16)
    wout = wout_ref[...].astype(bf16)
    wpa_b_ref[...] = wpa
    wpb_b_ref[...] = wpb
    wpc_b_ref[...] = wpc
    wout_b_ref[...] = wout
    cin = (yc_ref[...] * szc_ref[...]).astype(bf16)
    merged = (g_ref[:, 0:D_MODEL] * _dot(ain_ref[...], wpa)
              + g_ref[:, D_MODEL:2 * D_MODEL] * _dot(bin_ref[...], wpb)
              + g_ref[:, 2 * D_MODEL:3 * D_MODEL] * _dot(cin, wpc))
    out = _dot(merged.astype(bf16), wout)
    y_ref[...] = _layer_norm(ALPHA * x_ref[...] + out, row(lng_ref), row(lnb_ref))


def _sample_merge(layer, x, ain, bin_, yc, szc, g, wpa, wpb, wpc, wout, lng, lnb):
    lspec = lambda a: _layer_spec(a, layer)
    nb = x.shape[0]
    x_spec = pl.BlockSpec((nb, None, D_MODEL), lambda i: (0, 0, 0))
    out_shape = (jax.ShapeDtypeStruct(x.shape, f32),) + tuple(
        jax.ShapeDtypeStruct(w.shape[1:], bf16) for w in (wpa, wpb, wpc, wout))
    return pl.pallas_call(
        functools.partial(_sample_merge_kernel, layer),
        out_shape=out_shape,
        grid=(1,),
        in_specs=[x_spec, _full_spec(ain), _full_spec(bin_), _full_spec(yc), _full_spec(szc),
                  _full_spec(g), lspec(wpa), lspec(wpb), lspec(wpc), lspec(wout), _full_spec(lng),
                  _full_spec(lnb)],
        out_specs=(x_spec,) + tuple(_full_spec(s) for s in out_shape[1:]),
        compiler_params=pltpu.CompilerParams(dimension_semantics=("arbitrary",),
                                             vmem_limit_bytes=VMEM_LIMIT),
        name="sample_merge",
    )(x, ain, bin_, yc, szc, g, wpa, wpb, wpc, wout, lng, lnb)


def _rope_tables(positions):
    halfd = HEAD_DIM // 2
    inv = ROPE_THETA ** (-np.arange(halfd, dtype=np.float64) / halfd)
    ang = np.asarray(positions, dtype=np.float64)[:, None] * inv[None, :]
    cos = np.tile(np.cos(ang), (1, LANES // halfd))
    sin = np.tile(np.concatenate([-np.sin(ang), np.sin(ang)], axis=1), (1, LANES // HEAD_DIM))
    return jnp.asarray(cos, f32), jnp.asarray(sin, f32)


def _block_diag(w):
    nl, g, c, _ = w.shape
    eye = jnp.eye(g, dtype=w.dtype)
    return (eye[None, :, None, :, None] * w[:, :, :, None, :]).reshape(nl, g * c, g * c)


def kernel(x_prompt, x_sample, state_pool, cache_k_win, cache_v_win, w_in, b_gate, pool_w, pool_scale, sgu_ln_g, sgu_ln_b, sgu_w, sgu_b, attn_sinks, w_proj_a, w_proj_b, w_proj_c, w_out, ln_g, ln_b):
    B, L, _ = x_prompt.shape
    nb = x_sample.shape[0]
    cos_p, sin_p = _rope_tables(np.arange(L))
    cos_s, sin_s = _rope_tables(np.array([PAST_LEN]))

    poolw = _block_diag(pool_w).astype(bf16)
    sguw = sgu_w.reshape(DEPTH, N_SGU_GROUPS * CHUNK, CHUNK)
    sgub = jnp.repeat(jnp.swapaxes(sgu_b, 1, 2), POOL_GC, axis=2)
    sw0 = sgu_w[:, :, 0, 0]
    sb0 = sgu_b[:, :, 0]
    bg_t = jnp.transpose(b_gate, (1, 0, 2))
    kc_t = jnp.transpose(cache_k_win, (0, 1, 3, 4, 2))
    vc_t = jnp.transpose(cache_v_win, (0, 1, 3, 4, 2))
    pool_t = jnp.transpose(state_pool, (0, 2, 1, 3))

    y_s = x_sample
    pool_s, chunk_v, kts, vts, weights = ([] for _ in range(5))
    for l in range(DEPTH):
        w_in_b, qkv, kt, vt, ps, cv, ain, bin_, szc, g = _sample_proj(
            l, y_s, cos_s, sin_s, w_in, bg_t, pool_t, poolw, pool_scale, sgu_ln_g, sgu_ln_b, sw0, sb0)
        o = _sample_attn(l, attn_sinks, qkv.reshape(nb, N_HEADS + 2 * N_KV_HEADS, HEAD_DIM), kc_t, vc_t)
        y_s, wpa, wpb, wpc, wout = _sample_merge(
            l, y_s, ain, bin_, o.reshape(nb, D_ATTN), szc, g, w_proj_a, w_proj_b, w_proj_c, w_out, ln_g, ln_b)
        pool_s.append(ps); chunk_v.append(cv); kts.append(kt); vts.append(vt)
        weights.append((w_in_b, wpa, wpb, wpc, wout))

    y_p = x_prompt
    pool_p, k_p, v_p = ([] for _ in range(3))
    for l in range(DEPTH):
        w_in_b, wpa, wpb, wpc, wout = weights[l]
        slide_args = (kts, vts, kc_t, vc_t) if l == DEPTH - 1 else None
        y_p, pp, kp, vp, *slid = _prompt_layer(
            l, y_p, cos_p, sin_p, attn_sinks, w_in_b, bg_t, poolw, pool_scale, sgu_ln_g, sgu_ln_b, sguw,
            sgub, wpa, wpb, wpc, wout, ln_g, ln_b, slide_args=slide_args)
        pool_p.append(pp); k_p.append(kp); v_p.append(vp)
    k_s, v_s = slid

    to_cache = lambda a: jnp.transpose(a, (0, 1, 4, 2, 3))
    prompt_cache = lambda lst: to_cache(jnp.stack(lst).reshape(DEPTH, B, N_KV_HEADS, HEAD_DIM, WINDOW))
    return (y_p, y_s,
            jnp.stack(pool_p), prompt_cache(k_p), prompt_cache(v_p),
            jnp.transpose(jnp.stack(pool_s), (0, 2, 1, 3)), to_cache(k_s), to_cache(v_s),
            jnp.stack(chunk_v))
```

```python
import functools

import numpy as np
import jax
import jax.numpy as jnp
from jax import lax
from jax.experimental import pallas as pl
from jax.experimental.pallas import tpu as pltpu

D_MODEL = 1024
DEPTH = 2
PAST_LEN = 8192
D_POOL = 256
POOL_WINDOWS = (2, 4, 8, 16)
POOL_GC = 64
POOL_BUF = 15
D_SGU = 256
CHUNK = 128
N_SGU_GROUPS = 4
HEAD_DIM = 64
N_HEADS = 8
N_KV_HEADS = 2
Q_PER_KV = 4
D_ATTN = 512
D_KV = 128
WINDOW = 128
BLOCK = 128
ROPE_THETA = 10000.0
N_BRANCHES = 3
D_IN = 2 * D_POOL + 3 * D_SGU + 2 * D_ATTN + 2 * D_KV + N_BRANCHES * D_MODEL
ALPHA = (2.0 * DEPTH) ** 0.25
LN_EPS = 1e-5
NEG_INF = -1e30
SCALE = HEAD_DIM ** -0.5

OFF_XA, OFF_ZA, OFF_U, OFF_V, OFF_ZB = 0, 256, 512, 768, 1024
OFF_Q, OFF_K, OFF_VV, OFF_ZC, OFF_G = 1280, 1792, 1920, 2048, 2560

LANES = 128
TM = 512
SUB = 2
NBLK = TM // BLOCK
HIST = 32
NCHUNK = 256
OUT_CHUNKS = ((0, 256), (256, 512))
VMEM_LIMIT = 56 * 1024 * 1024
SB = 32
RK = 512
RK_STEPS = D_MODEL // RK

bf16 = jnp.bfloat16
f32 = jnp.float32


def _dot(a, b):
    return jnp.dot(a, b, preferred_element_type=f32)


def _dot_nt(a, b):
    return lax.dot_general(a, b, (((1,), (1,)), ((), ())), preferred_element_type=f32)


def _sigmoid(z):
    return 0.5 * jnp.tanh(0.5 * z) + 0.5


def _silu(z):
    return z * _sigmoid(z)


def _gate2(half_pre, bias):
    return jnp.tanh(half_pre + 0.5 * bias) + 1.0


def _layer_norm(x, g, b):
    mu = jnp.mean(x, axis=-1, keepdims=True)
    xc = x - mu
    var = jnp.mean(xc * xc, axis=-1, keepdims=True)
    return xc * lax.rsqrt(var + LN_EPS) * g + b


def _rope128(x, cos, sin_signed):
    lane = lax.broadcasted_iota(jnp.int32, x.shape, 1)
    first_half = (lane % HEAD_DIM) < (HEAD_DIM // 2)
    partner = jnp.where(first_half,
                        pltpu.roll(x, LANES - HEAD_DIM // 2, 1),
                        pltpu.roll(x, HEAD_DIM // 2, 1))
    return x * cos + partner * sin_signed


def _group_select(lane, a0, a1, a2, a3):
    return jnp.where(lane < 64, a0, jnp.where(lane < 128, a1, jnp.where(lane < 192, a2, a3)))


def _layer_spec(arr, layer, single_buffer=False):
    block = (None,) + arr.shape[1:]
    zeros = (0,) * (arr.ndim - 1)
    index_map = lambda *_: (layer,) + zeros
    if single_buffer:
        return pl.BlockSpec(block, index_map, pipeline_mode=pl.Buffered(1))
    return pl.BlockSpec(block, index_map)


def _full_spec(arr, single_buffer=False):
    zeros = (0,) * len(arr.shape)
    if single_buffer:
        return pl.BlockSpec(arr.shape, lambda *_: zeros, pipeline_mode=pl.Buffered(1))
    return pl.BlockSpec(arr.shape, lambda *_: zeros)


def _slide_windows(step, per_step, new_refs, old_refs, out_refs):
    lane = lax.broadcasted_iota(jnp.int32, (D_KV, WINDOW), 1)
    shift = jnp.where(step == 0, 0, LANES - step * per_step)
    for new_ref, old_ref, out_ref in zip(new_refs, old_refs, out_refs):
        for l in range(DEPTH):
            new_cols = pltpu.roll(new_ref[l][...], shift, 1)
            for s in range(per_step):
                slid = pltpu.roll(old_ref[l, s].reshape(D_KV, WINDOW), WINDOW - 1, 1)
                newest = jnp.broadcast_to(new_cols[:, s:s + 1], (D_KV, WINDOW))
                out_ref[l, s] = jnp.where(lane == WINDOW - 1, newest, slid).reshape(
                    N_KV_HEADS, HEAD_DIM, WINDOW)


def _prompt_kernel(layer, slide, *refs):
    def sub_tile(s, carry):
        _prompt_tile(layer, slide, s, refs)
        return carry
    lax.fori_loop(0, SUB, sub_tile, 0)


def _prompt_tile(layer, slide, s, refs):
    n_in = 18 + (2 * DEPTH + 2 if slide else 0)
    n_out = 4 + (2 if slide else 0)
    (sinks_ref, x_ref, cos_ref, sin_ref, w_in_ref, bg_ref, poolw_ref, pscale_ref,
     slng_ref, slnb_ref, sguw_ref, sgub_ref, wpa_ref, wpb_ref, wpc_ref, wout_ref,
     lng_ref, lnb_ref) = refs[:18]
    y_ref, pool_out_ref, k_out_ref, v_out_ref = refs[n_in:n_in + 4]
    (h_ref, ext_ref, s2_ref, s4_ref, s8_ref, qs_ref,
     ka_ref, kb_ref, kc_ref, kd_ref, va_ref, vb_ref, vc_ref, vd_ref,
     ain_ref, bin_ref, cin_ref, mg_ref, gate_ref, klast_ref, vlast_ref) = refs[n_in + n_out:]
    tile_rows = pl.ds(pl.multiple_of(s * TM, TM), TM)
    x_ref, y_ref, cos_ref, sin_ref = (r.at[tile_rows] for r in (x_ref, y_ref, cos_ref, sin_ref))
    i = pl.program_id(1) * SUB + s
    last = pl.num_programs(1) * SUB - 1
    kv_refs = (ka_ref, kb_ref, kc_ref, kd_ref, va_ref, vb_ref, vc_ref, vd_ref)
    row = lambda ref: ref[layer:layer + 1, :]

    @pl.when(i == 0)
    def _():
        ext_ref[0:HIST, :] = jnp.zeros((HIST, D_POOL), f32)
        for r in kv_refs:
            r[0:BLOCK, :] = jnp.zeros((BLOCK, LANES), bf16)

    xb = x_ref[...].astype(bf16)
    half = OFF_G // 2
    h_ref[:, 0:half] = _dot(xb, w_in_ref[:, 0:half])
    h_ref[:, half:OFF_G] = _dot(xb, w_in_ref[:, half:OFF_G])

    if slide:
        step = pl.program_id(0) * (last + 1) + i
        new_refs = (refs[18:18 + DEPTH], refs[18 + DEPTH:18 + 2 * DEPTH])
        old_refs = refs[18 + 2 * DEPTH:n_in]
        per_step = old_refs[0].shape[1] // SUB
        mine = lambda r: r.at[:, pl.ds(s * per_step, per_step)]
        _slide_windows(step, per_step, new_refs, [mine(r) for r in old_refs],
                       [mine(r) for r in refs[n_in + 4:n_in + 6]])

    xa = h_ref[:, OFF_XA:OFF_XA + D_POOL]
    ext_ref[HIST:HIST + TM, :] = xa
    n = HIST + TM
    s2_ref[8:n, :] = ext_ref[8:n, :] + ext_ref[7:n - 1, :]
    s4_ref[16:n, :] = s2_ref[16:n, :] + s2_ref[14:n - 2, :]
    s8_ref[24:n, :] = s4_ref[24:n, :] + s4_ref[20:n - 4, :]
    w16 = s8_ref[HIST:n, :] + s8_ref[HIST - 8:n - 8, :]
    lane_p = lax.broadcasted_iota(jnp.int32, (TM, D_POOL), 1)
    row_p = lax.broadcasted_iota(jnp.int32, (TM, D_POOL), 0)
    win = _group_select(lane_p, s2_ref[HIST:n, :], s4_ref[HIST:n, :], s8_ref[HIST:n, :], w16)
    width = _group_select(lane_p, POOL_WINDOWS[0], POOL_WINDOWS[1], POOL_WINDOWS[2], POOL_WINDOWS[3])
    cnt = jnp.minimum(row_p + (i * TM + 1), width).astype(f32)
    pooled = win / cnt - xa
    ya = _dot(pooled.astype(bf16), poolw_ref[...]) * row(pscale_ref)
    za = h_ref[:, OFF_ZA:OFF_ZA + D_POOL]
    ain_ref[...] = (ya * _silu(za)).astype(bf16)
    ext_ref[HIST - 16:HIST, :] = ext_ref[n - 16:n, :]

    vn = _layer_norm(h_ref[:, OFF_V:OFF_V + D_SGU], row(slng_ref), row(slnb_ref)).astype(bf16)
    wr = lax.broadcasted_iota(jnp.int32, (N_SGU_GROUPS * CHUNK, CHUNK), 0) % CHUNK
    wc = lax.broadcasted_iota(jnp.int32, (N_SGU_GROUPS * CHUNK, CHUNK), 1)
    w_s = jnp.where(wc <= wr, sguw_ref[...], 0.0).astype(bf16)
    lane_c = lax.broadcasted_iota(jnp.int32, (CHUNK, D_SGU), 1)
    for j in range(NBLK):
        rows = slice(j * CHUNK, (j + 1) * CHUNK)
        r = _dot(w_s, vn[rows, :])
        mixed = _group_select(lane_c, r[0:CHUNK], r[CHUNK:2 * CHUNK], r[2 * CHUNK:3 * CHUNK],
                              r[3 * CHUNK:4 * CHUNK]) + sgub_ref[...]
        yb = h_ref[rows, OFF_U:OFF_U + D_SGU] * mixed
        bin_ref[rows, :] = (yb * _silu(h_ref[rows, OFF_ZB:OFF_ZB + D_SGU])).astype(bf16)

    cos = cos_ref[...]
    sin = sin_ref[...]
    for c in range(D_ATTN // LANES):
        qc = _rope128(h_ref[:, OFF_Q + c * LANES:OFF_Q + (c + 1) * LANES], cos, sin)
        qs_ref[:, c * LANES:(c + 1) * LANES] = (qc * SCALE).astype(bf16)
    kr = _rope128(h_ref[:, OFF_K:OFF_K + D_KV], cos, sin)
    vv = h_ref[:, OFF_VV:OFF_VV + D_KV]
    klast_ref[...] = kr[TM - WINDOW:TM, :]
    vlast_ref[...] = vv[TM - WINDOW:TM, :]
    lane_k = lax.broadcasted_iota(jnp.int32, (TM, LANES), 1)
    lo = lane_k < HEAD_DIM
    for src, (a_ref, b_ref, c_ref, d_ref) in ((kr, kv_refs[0:4]), (vv, kv_refs[4:8])):
        sw = pltpu.roll(src, HEAD_DIM, 1)
        a_ref[BLOCK:BLOCK + TM, :] = jnp.where(lo, src, 0.0).astype(bf16)
        b_ref[BLOCK:BLOCK + TM, :] = jnp.where(lo, 0.0, src).astype(bf16)
        c_ref[BLOCK:BLOCK + TM, :] = jnp.where(lo, sw, 0.0).astype(bf16)
        d_ref[BLOCK:BLOCK + TM, :] = jnp.where(lo, 0.0, sw).astype(bf16)

    qrow = lax.broadcasted_iota(jnp.int32, (2 * BLOCK, 2 * BLOCK), 0) % BLOCK
    kcol = lax.broadcasted_iota(jnp.int32, (2 * BLOCK, 2 * BLOCK), 1)
    band = (kcol >= qrow) & (kcol <= qrow + WINDOW)
    band_first = band & (kcol >= jnp.where(i > 0, 0, BLOCK))
    top = lax.broadcasted_iota(jnp.int32, (2 * BLOCK, 1), 0) < BLOCK
    n_gate = N_BRANCHES * D_MODEL // NCHUNK
    n_unit = NBLK * N_KV_HEADS
    gate_sched = [range(u * n_gate // n_unit, (u + 1) * n_gate // n_unit) for u in range(n_unit)]

    def scores(u):
        j, kv = divmod(u, N_KV_HEADS)
        rows = slice(j * BLOCK, (j + 1) * BLOCK)
        keys = slice(j * BLOCK, j * BLOCK + 2 * BLOCK)
        c0 = kv * Q_PER_KV * HEAD_DIM
        qst = jnp.concatenate([qs_ref[rows, c0:c0 + LANES], qs_ref[rows, c0 + LANES:c0 + 2 * LANES]], axis=0)
        k_even, k_odd = (ka_ref, kd_ref) if kv == 0 else (kc_ref, kb_ref)
        kcat = jnp.concatenate([k_even[keys, :], k_odd[keys, :]], axis=0)
        return _dot_nt(qst, kcat)

    def attend(u, sc):
        j, kv = divmod(u, N_KV_HEADS)
        rows = slice(j * BLOCK, (j + 1) * BLOCK)
        keys = slice(j * BLOCK, j * BLOCK + 2 * BLOCK)
        allowed = band_first if j == 0 else band
        c0 = kv * Q_PER_KV * HEAD_DIM
        h0 = kv * Q_PER_KV
        v_even, v_odd = (va_ref, vd_ref) if kv == 0 else (vc_ref, vb_ref)
        probs = []
        for par in range(2):
            sink = jnp.where(top, sinks_ref[layer, h0 + par], sinks_ref[layer, h0 + 2 + par])
            sm = jnp.where(allowed, sc[:, par * 2 * BLOCK:(par + 1) * 2 * BLOCK], NEG_INF)
            m = jnp.maximum(jnp.max(sm, axis=-1, keepdims=True), sink)
            p = jnp.exp(sm - m)
            den = jnp.sum(p, axis=-1, keepdims=True) + jnp.exp(sink - m)
            probs.append((p / den).astype(bf16))
        pcat = jnp.concatenate(probs, axis=1)
        vcat = jnp.concatenate([v_even[keys, :], v_odd[keys, :]], axis=0)
        o = _dot(pcat, vcat)
        for pr in range(2):
            cols = slice(c0 + pr * LANES, c0 + (pr + 1) * LANES)
            zc = h_ref[rows, OFF_ZC + c0 + pr * LANES:OFF_ZC + c0 + (pr + 1) * LANES]
            cin_ref[rows, cols] = (o[pr * BLOCK:(pr + 1) * BLOCK] * _silu(zc)).astype(bf16)

    sc_next = scores(0)
    for u in range(n_unit):
        sc = sc_next
        for gc in gate_sched[u]:
            gcols = slice(gc * NCHUNK, (gc + 1) * NCHUNK)
            gate_ref[:, gcols] = _dot(xb, w_in_ref[:, OFF_G + gc * NCHUNK:OFF_G + (gc + 1) * NCHUNK])
        if u + 1 < n_unit:
            sc_next = scores(u + 1)
        attend(u, sc)

    for r in kv_refs:
        r[0:BLOCK, :] = r[TM:TM + BLOCK, :]

    for c in range(D_MODEL // NCHUNK):
        cols = slice(c * NCHUNK, (c + 1) * NCHUNK)
        acc = None
        for br, (in_ref, wp_ref) in enumerate(((ain_ref, wpa_ref), (bin_ref, wpb_ref), (cin_ref, wpc_ref))):
            g0 = br * D_MODEL + c * NCHUNK
            term = (_gate2(gate_ref[:, g0:g0 + NCHUNK], bg_ref[br, layer:layer + 1, cols])
                    * _dot(in_ref[...], wp_ref[:, cols]))
            acc = term if acc is None else acc + term
        mg_ref[:, cols] = acc.astype(bf16)

    for r0, r1 in OUT_CHUNKS:
        rows = slice(r0, r1)
        out = _dot(mg_ref[rows, :], wout_ref[...])
        y_ref[rows, :] = _layer_norm(ALPHA * x_ref[rows, :] + out, row(lng_ref), row(lnb_ref))

    @pl.when(i == last)
    def _():
        pool_out_ref[...] = ext_ref[n - POOL_BUF:n, :]
        k_out_ref[...] = klast_ref[...].T
        v_out_ref[...] = vlast_ref[...].T


def _prompt_layer(layer, x, cos, sin, sinks, w_in, b_gate, poolw, pscale, slng, slnb, sguw, sgub,
                  wpa, wpb, wpc, wout, lng, lnb, slide_args=None):
    B, L, _ = x.shape
    rows = TM * SUB
    nt = L // rows
    grid = (B, nt)
    row_spec = lambda w: pl.BlockSpec((rows, w), lambda b, i: (i, 0))
    lspec = lambda a: _layer_spec(a, layer, single_buffer=True)
    wspec = lambda a: _full_spec(a, single_buffer=True)
    in_specs = [
        pl.BlockSpec(memory_space=pltpu.SMEM),
        pl.BlockSpec((None, rows, D_MODEL), lambda b, i: (b, i, 0)),
        row_spec(LANES), row_spec(LANES),
        wspec(w_in), _full_spec(b_gate), wspec(poolw), _full_spec(pscale), _full_spec(slng), _full_spec(slnb),
        lspec(sguw), lspec(sgub), wspec(wpa), wspec(wpb), wspec(wpc), wspec(wout),
        _full_spec(lng), _full_spec(lnb),
    ]
    out_shape = (
        jax.ShapeDtypeStruct((B, L, D_MODEL), f32),
        jax.ShapeDtypeStruct((B, POOL_BUF, D_POOL), f32),
        jax.ShapeDtypeStruct((B, D_KV, WINDOW), f32),
        jax.ShapeDtypeStruct((B, D_KV, WINDOW), f32),
    )
    out_specs = (
        pl.BlockSpec((None, rows, D_MODEL), lambda b, i: (b, i, 0)),
        pl.BlockSpec((None, POOL_BUF, D_POOL), lambda b, i: (b, 0, 0)),
        pl.BlockSpec((None, D_KV, WINDOW), lambda b, i: (b, 0, 0)),
        pl.BlockSpec((None, D_KV, WINDOW), lambda b, i: (b, 0, 0)),
    )
    args = [sinks, x, cos, sin, w_in, b_gate, poolw, pscale, slng, slnb, sguw, sgub,
            wpa, wpb, wpc, wout, lng, lnb]
    if slide_args is not None:
        new_k, new_v, cache_k, cache_v = slide_args
        depth, nb = cache_k.shape[:2]
        per_step = nb // (B * nt)
        assert per_step * B * nt == nb and depth == DEPTH
        cache_spec = pl.BlockSpec((depth, per_step) + cache_k.shape[2:], lambda b, i: (0, b * nt + i, 0, 0, 0))
        in_specs += [_full_spec(a) for a in (*new_k, *new_v)] + [cache_spec, cache_spec]
        args += [*new_k, *new_v, cache_k, cache_v]
        out_shape += (jax.ShapeDtypeStruct(cache_k.shape, f32), jax.ShapeDtypeStruct(cache_v.shape, f32))
        out_specs += (cache_spec, cache_spec)
    kv_scratch = [pltpu.VMEM((BLOCK + TM, LANES), bf16) for _ in range(8)]
    scratch = [
        pltpu.VMEM((TM, OFF_G), f32),
        pltpu.VMEM((HIST + TM, D_POOL), f32),
        pltpu.VMEM((HIST + TM, D_POOL), f32),
        pltpu.VMEM((HIST + TM, D_POOL), f32),
        pltpu.VMEM((HIST + TM, D_POOL), f32),
        pltpu.VMEM((TM, D_ATTN), bf16),
        *kv_scratch,
        pltpu.VMEM((TM, D_POOL), bf16),
        pltpu.VMEM((TM, D_SGU), bf16),
        pltpu.VMEM((TM, D_ATTN), bf16),
        pltpu.VMEM((TM, D_MODEL), bf16),
        pltpu.VMEM((TM, N_BRANCHES * D_MODEL), f32),
        pltpu.VMEM((WINDOW, D_KV), f32),
        pltpu.VMEM((WINDOW, D_KV), f32),
    ]
    return pl.pallas_call(
        functools.partial(_prompt_kernel, layer, slide_args is not None),
        out_shape=out_shape,
        grid=grid,
        in_specs=in_specs,
        out_specs=out_specs,
        scratch_shapes=scratch,
        compiler_params=pltpu.CompilerParams(
            dimension_semantics=("arbitrary", "arbitrary"),
            vmem_limit_bytes=VMEM_LIMIT),
        name="prompt_layer",
    )(*args)


def _sample_proj_kernel(layer, x_ref, cos_ref, sin_ref, w_in_ref, bg_ref, pb_ref, poolw_ref, pscale_ref,
                        slng_ref, slnb_ref, sw0_ref, sb0_ref,
                        wb_ref, qkv_ref, kt_ref, vt_ref, pool_ref, vn_ref, ain_ref, bin_ref,
                        szc_ref, g_ref, poolbd_ref, h_ref):
    j = pl.program_id(0)
    row = lambda ref: ref[layer:layer + 1, :]
    wb_ref[:, 0:OFF_G] = w_in_ref[:, 0:OFF_G].astype(bf16)
    wb_ref[:, OFF_G:D_IN] = (0.5 * w_in_ref[:, OFF_G:D_IN]).astype(bf16)

    @pl.when(j == 0)
    def _():
        h_ref[...] = jnp.zeros(h_ref.shape, f32)

    h_ref[...] += _dot(x_ref[...].astype(bf16), wb_ref[...])

    def hcols(off, width):
        return h_ref[:, off:off + width]

    @pl.when(j == RK_STEPS - 1)
    def _():
        xa = hcols(OFF_XA, D_POOL)
        lane = lax.broadcasted_iota(jnp.int32, xa.shape, 1)
        first_row = _group_select(lane, *(POOL_BUF - (w - 1) for w in POOL_WINDOWS))
        win = xa
        for r in range(POOL_BUF):
            win = win + jnp.where(first_row <= r, pb_ref[r], 0.0)
        width = _group_select(lane, *POOL_WINDOWS).astype(f32)
        pooled = win / width - xa
        zero_blk = jnp.zeros((POOL_GC, POOL_GC), f32)
        n_grp = len(POOL_WINDOWS)
        pool_bd = jnp.concatenate(
            [jnp.concatenate([poolw_ref[g] if gg == g else zero_blk for gg in range(n_grp)], axis=1)
             for g in range(n_grp)], axis=0).astype(bf16)
        poolbd_ref[...] = pool_bd
        ya = _dot(pooled.astype(bf16), pool_bd) * row(pscale_ref)
        ain_ref[...] = (ya * _silu(hcols(OFF_ZA, D_POOL))).astype(bf16)
        for r in range(POOL_BUF - 1):
            pool_ref[r] = pb_ref[r + 1]
        pool_ref[POOL_BUF - 1] = xa
        vn = _layer_norm(hcols(OFF_V, D_SGU), row(slng_ref), row(slnb_ref))
        vn_ref[...] = vn
        lane_g = lax.broadcasted_iota(jnp.int32, (1, D_SGU), 1)
        sw0 = _group_select(lane_g, *(sw0_ref[layer, g] for g in range(N_SGU_GROUPS)))
        sb0 = _group_select(lane_g, *(sb0_ref[layer, g] for g in range(N_SGU_GROUPS)))
        yb = hcols(OFF_U, D_SGU) * (sw0 * vn + sb0)
        bin_ref[...] = (yb * _silu(hcols(OFF_ZB, D_SGU))).astype(bf16)
        cos = cos_ref[...]
        sin = sin_ref[...]
        for c in range(D_ATTN // LANES):
            qkv_ref[:, c * LANES:(c + 1) * LANES] = _rope128(hcols(OFF_Q + c * LANES, LANES), cos, sin) * SCALE
        kr = _rope128(hcols(OFF_K, D_KV), cos, sin)
        vv = hcols(OFF_VV, D_KV)
        qkv_ref[:, D_ATTN:D_ATTN + D_KV] = kr
        qkv_ref[:, D_ATTN + D_KV:D_ATTN + 2 * D_KV] = vv
        kt_ref[...] = kr.T
        vt_ref[...] = vv.T
        szc_ref[...] = _silu(hcols(OFF_ZC, D_ATTN))
        for br in range(N_BRANCHES):
            cols = slice(br * D_MODEL, (br + 1) * D_MODEL)
            g_ref[:, cols] = _gate2(hcols(OFF_G + br * D_MODEL, D_MODEL), bg_ref[br, layer:layer + 1, :])


def _sample_proj(layer, x, cos, sin, w_in, b_gate, pool_t, poolw, pscale, slng, slnb, sw0, sb0):
    nb = x.shape[0]
    out_shape = (
        jax.ShapeDtypeStruct((D_MODEL, D_IN), bf16),
        jax.ShapeDtypeStruct((nb, D_ATTN + 2 * D_KV), f32),
        jax.ShapeDtypeStruct((D_KV, nb), f32),
        jax.ShapeDtypeStruct((D_KV, nb), f32),
        jax.ShapeDtypeStruct((POOL_BUF, nb, D_POOL), f32),
        jax.ShapeDtypeStruct((nb, 1, D_SGU), f32),
        jax.ShapeDtypeStruct((nb, D_POOL), bf16),
        jax.ShapeDtypeStruct((nb, D_SGU), bf16),
        jax.ShapeDtypeStruct((nb, D_ATTN), f32),
        jax.ShapeDtypeStruct((nb, N_BRANCHES * D_MODEL), f32),
        jax.ShapeDtypeStruct((D_POOL, D_POOL), bf16),
    )
    lspec = lambda a: _layer_spec(a, layer)
    args = (x, cos, sin, w_in, b_gate, pool_t, poolw, pscale, slng, slnb, sw0, sb0)
    in_specs = [pl.BlockSpec((nb, None, RK), lambda j: (0, 0, j)), _full_spec(cos), _full_spec(sin),
                pl.BlockSpec((None, RK, D_IN), lambda j: (layer, j, 0)),
                _full_spec(b_gate), lspec(pool_t), lspec(poolw), _full_spec(pscale), _full_spec(slng),
                _full_spec(slnb), pl.BlockSpec(memory_space=pltpu.SMEM), pl.BlockSpec(memory_space=pltpu.SMEM)]
    squeeze_mid = lambda s: pl.BlockSpec((s.shape[0], None, s.shape[2]), lambda j: (0, 0, 0))
    out_specs = (pl.BlockSpec((RK, D_IN), lambda j: (j, 0)),) + tuple(
        squeeze_mid(s) if len(s.shape) == 3 and s.shape[1] == 1 else _full_spec(s) for s in out_shape[1:])
    return pl.pallas_call(
        functools.partial(_sample_proj_kernel, layer),
        out_shape=out_shape,
        grid=(RK_STEPS,),
        in_specs=in_specs,
        out_specs=out_specs,
        scratch_shapes=[pltpu.VMEM((nb, D_IN), f32)],
        compiler_params=pltpu.CompilerParams(dimension_semantics=("arbitrary",),
                                             vmem_limit_bytes=VMEM_LIMIT),
        name="sample_proj",
    )(*args)


def _sample_attn_kernel(layer, sink_ref, qkv_ref, kc_ref, vc_ref, o_ref):
    head = lax.broadcasted_iota(jnp.int32, (1, Q_PER_KV, 1), 1)
    for kv in range(N_KV_HEADS):
        qb = qkv_ref[:, kv * Q_PER_KV:(kv + 1) * Q_PER_KV, :].astype(bf16)
        kn = qkv_ref[:, N_HEADS + kv:N_HEADS + kv + 1, :]
        vn = qkv_ref[:, N_HEADS + N_KV_HEADS + kv:N_HEADS + N_KV_HEADS + kv + 1, :]
        sink = jnp.zeros((1, Q_PER_KV, 1), f32)
        for g in range(Q_PER_KV):
            sink = jnp.where(head == g, sink_ref[layer, kv * Q_PER_KV + g], sink)
        s = jnp.einsum('bgd,bdw->bgw', qb, kc_ref[:, kv].astype(bf16), preferred_element_type=f32)
        s_new = jnp.sum(qb.astype(f32) * kn.astype(bf16).astype(f32), axis=-1, keepdims=True)
        m = jnp.maximum(jnp.maximum(jnp.max(s, axis=-1, keepdims=True), s_new), sink)
        p = jnp.exp(s - m)
        p_new = jnp.exp(s_new - m)
        den = jnp.sum(p, axis=-1, keepdims=True) + p_new + jnp.exp(sink - m)
        o = jnp.einsum('bgw,bdw->bgd', (p / den).astype(bf16), vc_ref[:, kv].astype(bf16),
                       preferred_element_type=f32)
        o_ref[:, kv] = o + (p_new / den) * vn


def _sample_attn(layer, sink, qkv3, kc, vc):
    nb, n_rows, _ = qkv3.shape
    cache_spec = pl.BlockSpec((None, SB, N_KV_HEADS, HEAD_DIM, WINDOW), lambda b: (layer, b, 0, 0, 0))
    return pl.pallas_call(
        functools.partial(_sample_attn_kernel, layer),
        out_shape=jax.ShapeDtypeStruct((nb, N_KV_HEADS, Q_PER_KV, HEAD_DIM), f32),
        grid=(nb // SB,),
        in_specs=[pl.BlockSpec(memory_space=pltpu.SMEM),
                  pl.BlockSpec((SB, n_rows, HEAD_DIM), lambda b: (b, 0, 0)), cache_spec, cache_spec],
        out_specs=pl.BlockSpec((SB, N_KV_HEADS, Q_PER_KV, HEAD_DIM), lambda b: (b, 0, 0, 0)),
        compiler_params=pltpu.CompilerParams(dimension_semantics=("arbitrary",),
                                             vmem_limit_bytes=VMEM_LIMIT),
        name="sample_attn",
    )(sink, qkv3, kc, vc)


def _sample_merge_kernel(layer, x_ref, ain_ref, bin_ref, yc_ref, szc_ref, g_ref, wpa_ref, wpb_ref, wpc_ref,
                         wout_ref, lng_ref, lnb_ref, y_ref, wpa_b_ref, wpb_b_ref, wpc_b_ref, wout_b_ref):
    row = lambda ref: ref[layer:layer + 1, :]
    wpa = (0.5 * wpa_ref[...]).astype(bf16)
    wpb = (0.5 * wpb_ref[...]).astype(bf16)
    wpc = (0.5 * wpc_ref[...]).astype(bf16)
    wout = wout_ref[...].astype(bf16)
    wpa_b_ref[...] = wpa
    wpb_b_ref[...] = wpb
    wpc_b_ref[...] = wpc
    wout_b_ref[...] = wout
    cin = (yc_ref[...] * szc_ref[...]).astype(bf16)
    merged = (g_ref[:, 0:D_MODEL] * _dot(ain_ref[...], wpa)
              + g_ref[:, D_MODEL:2 * D_MODEL] * _dot(bin_ref[...], wpb)
              + g_ref[:, 2 * D_MODEL:3 * D_MODEL] * _dot(cin, wpc))
    out = _dot(merged.astype(bf16), wout)
    y_ref[...] = _layer_norm(ALPHA * x_ref[...] + out, row(lng_ref), row(lnb_ref))


def _sample_merge(layer, x, ain, bin_, yc, szc, g, wpa, wpb, wpc, wout, lng, lnb):
    lspec = lambda a: _layer_spec(a, layer)
    nb = x.shape[0]
    x_spec = pl.BlockSpec((nb, None, D_MODEL), lambda i: (0, 0, 0))
    out_shape = (jax.ShapeDtypeStruct(x.shape, f32),) + tuple(
        jax.ShapeDtypeStruct(w.shape[1:], bf16) for w in (wpa, wpb, wpc, wout))
    return pl.pallas_call(
        functools.partial(_sample_merge_kernel, layer),
        out_shape=out_shape,
        grid=(1,),
        in_specs=[x_spec, _full_spec(ain), _full_spec(bin_), _full_spec(yc), _full_spec(szc),
                  _full_spec(g), lspec(wpa), lspec(wpb), lspec(wpc), lspec(wout), _full_spec(lng),
                  _full_spec(lnb)],
        out_specs=(x_spec,) + tuple(_full_spec(s) for s in out_shape[1:]),
        compiler_params=pltpu.CompilerParams(dimension_semantics=("arbitrary",),
                                             vmem_limit_bytes=VMEM_LIMIT),
        name="sample_merge",
    )(x, ain, bin_, yc, szc, g, wpa, wpb, wpc, wout, lng, lnb)


def _rope_tables(positions):
    halfd = HEAD_DIM // 2
    inv = ROPE_THETA ** (-np.arange(halfd, dtype=np.float64) / halfd)
    ang = np.asarray(positions, dtype=np.float64)[:, None] * inv[None, :]
    cos = np.tile(np.cos(ang), (1, LANES // halfd))
    sin = np.tile(np.concatenate([-np.sin(ang), np.sin(ang)], axis=1), (1, LANES // HEAD_DIM))
    return jnp.asarray(cos, f32), jnp.asarray(sin, f32)


def kernel(x_prompt, x_sample, state_pool, cache_k_win, cache_v_win, w_in, b_gate, pool_w, pool_scale, sgu_ln_g, sgu_ln_b, sgu_w, sgu_b, attn_sinks, w_proj_a, w_proj_b, w_proj_c, w_out, ln_g, ln_b):
    B, L, _ = x_prompt.shape
    nb = x_sample.shape[0]
    cos_p, sin_p = _rope_tables(np.arange(L))
    cos_s, sin_s = _rope_tables(np.array([PAST_LEN]))

    sguw = sgu_w.reshape(DEPTH, N_SGU_GROUPS * CHUNK, CHUNK)
    sgub = jnp.repeat(jnp.swapaxes(sgu_b, 1, 2), POOL_GC, axis=2)
    sw0 = sgu_w[:, :, 0, 0]
    sb0 = sgu_b[:, :, 0]
    bg_t = jnp.transpose(b_gate, (1, 0, 2))
    kc_t = jnp.transpose(cache_k_win, (0, 1, 3, 4, 2))
    vc_t = jnp.transpose(cache_v_win, (0, 1, 3, 4, 2))
    pool_t = jnp.transpose(state_pool, (0, 2, 1, 3))

    y_s = x_sample
    pool_s, chunk_v, kts, vts, weights = ([] for _ in range(5))
    for l in range(DEPTH):
        w_in_b, qkv, kt, vt, ps, cv, ain, bin_, szc, g, pool_bd = _sample_proj(
            l, y_s, cos_s, sin_s, w_in, bg_t, pool_t, pool_w, pool_scale, sgu_ln_g, sgu_ln_b, sw0, sb0)
        o = _sample_attn(l, attn_sinks, qkv.reshape(nb, N_HEADS + 2 * N_KV_HEADS, HEAD_DIM), kc_t, vc_t)
        y_s, wpa, wpb, wpc, wout = _sample_merge(
            l, y_s, ain, bin_, o.reshape(nb, D_ATTN), szc, g, w_proj_a, w_proj_b, w_proj_c, w_out, ln_g, ln_b)
        pool_s.append(ps); chunk_v.append(cv); kts.append(kt); vts.append(vt)
        weights.append((w_in_b, wpa, wpb, wpc, wout, pool_bd))

    y_p = x_prompt
    pool_p, k_p, v_p = ([] for _ in range(3))
    for l in range(DEPTH):
        w_in_b, wpa, wpb, wpc, wout, poolw = weights[l]
        slide_args = (kts, vts, kc_t, vc_t) if l == DEPTH - 1 else None
        y_p, pp, kp, vp, *slid = _prompt_layer(
            l, y_p, cos_p, sin_p, attn_sinks, w_in_b, bg_t, poolw, pool_scale, sgu_ln_g, sgu_ln_b, sguw,
            sgub, wpa, wpb, wpc, wout, ln_g, ln_b, slide_args=slide_args)
        pool_p.append(pp); k_p.append(kp); v_p.append(vp)
    k_s, v_s = slid

    to_cache = lambda a: jnp.transpose(a, (0, 1, 4, 2, 3))
    prompt_cache = lambda lst: to_cache(jnp.stack(lst).reshape(DEPTH, B, N_KV_HEADS, HEAD_DIM, WINDOW))
    return (y_p, y_s,
            jnp.stack(pool_p), prompt_cache(k_p), prompt_cache(v_p),
            jnp.transpose(jnp.stack(pool_s), (0, 2, 1, 3)), to_cache(k_s), to_cache(v_s),
            jnp.stack(chunk_v))
```

```python
import functools

import numpy as np
import jax
import jax.numpy as jnp
from jax import lax
from jax.experimental import pallas as pl
from jax.experimental.pallas import tpu as pltpu

D_MODEL = 1024
DEPTH = 2
PAST_LEN = 8192
D_POOL = 256
POOL_WINDOWS = (2, 4, 8, 16)
POOL_GC = 64
POOL_BUF = 15
D_SGU = 256
CHUNK = 128
N_SGU_GROUPS = 4
HEAD_DIM = 64
N_HEADS = 8
N_KV_HEADS = 2
Q_PER_KV = 4
D_ATTN = 512
D_KV = 128
WINDOW = 128
BLOCK = 128
ROPE_THETA = 10000.0
N_BRANCHES = 3
D_IN = 2 * D_POOL + 3 * D_SGU + 2 * D_ATTN + 2 * D_KV + N_BRANCHES * D_MODEL
ALPHA = (2.0 * DEPTH) ** 0.25
LN_EPS = 1e-5
NEG_INF = -1e30
SCALE = HEAD_DIM ** -0.5

OFF_XA, OFF_ZA, OFF_U, OFF_V, OFF_ZB = 0, 256, 512, 768, 1024
OFF_Q, OFF_K, OFF_VV, OFF_ZC, OFF_G = 1280, 1792, 1920, 2048, 2560

LANES = 128
TM = 512
SUB = 2
NBLK = TM // BLOCK
HIST = 32
NCHUNK = 256
OUT_CHUNKS = ((0, 256), (256, 512))
VMEM_LIMIT = 56 * 1024 * 1024
SB = 32
RK = 512
RK_STEPS = D_MODEL // RK

bf16 = jnp.bfloat16
f32 = jnp.float32


def _dot(a, b):
    return jnp.dot(a, b, preferred_element_type=f32)


def _dot_nt(a, b):
    return lax.dot_general(a, b, (((1,), (1,)), ((), ())), preferred_element_type=f32)


def _sigmoid(z):
    return 0.5 * jnp.tanh(0.5 * z) + 0.5


def _silu(z):
    return z * _sigmoid(z)


def _gate2(half_pre, bias):
    return jnp.tanh(half_pre + 0.5 * bias) + 1.0


def _layer_norm(x, g, b):
    mu = jnp.mean(x, axis=-1, keepdims=True)
    xc = x - mu
    var = jnp.mean(xc * xc, axis=-1, keepdims=True)
    return xc * lax.rsqrt(var + LN_EPS) * g + b


def _rope128(x, cos, sin_signed):
    lane = lax.broadcasted_iota(jnp.int32, x.shape, 1)
    first_half = (lane % HEAD_DIM) < (HEAD_DIM // 2)
    partner = jnp.where(first_half,
                        pltpu.roll(x, LANES - HEAD_DIM // 2, 1),
                        pltpu.roll(x, HEAD_DIM // 2, 1))
    return x * cos + partner * sin_signed


def _group_select(lane, a0, a1, a2, a3):
    return jnp.where(lane < 64, a0, jnp.where(lane < 128, a1, jnp.where(lane < 192, a2, a3)))


def _layer_spec(arr, layer, single_buffer=False):
    block = (None,) + arr.shape[1:]
    zeros = (0,) * (arr.ndim - 1)
    index_map = lambda *_: (layer,) + zeros
    if single_buffer:
        return pl.BlockSpec(block, index_map, pipeline_mode=pl.Buffered(1))
    return pl.BlockSpec(block, index_map)


def _full_spec(arr, single_buffer=False):
    zeros = (0,) * len(arr.shape)
    if single_buffer:
        return pl.BlockSpec(arr.shape, lambda *_: zeros, pipeline_mode=pl.Buffered(1))
    return pl.BlockSpec(arr.shape, lambda *_: zeros)


def _slide_windows(step, per_step, new_refs, old_refs, out_refs):
    lane = lax.broadcasted_iota(jnp.int32, (D_KV, WINDOW), 1)
    shift = jnp.where(step == 0, 0, LANES - step * per_step)
    for new_ref, old_ref, out_ref in zip(new_refs, old_refs, out_refs):
        for l in range(DEPTH):
            new_cols = pltpu.roll(new_ref[l][...], shift, 1)
            for s in range(per_step):
                slid = pltpu.roll(old_ref[l, s].reshape(D_KV, WINDOW), WINDOW - 1, 1)
                newest = jnp.broadcast_to(new_cols[:, s:s + 1], (D_KV, WINDOW))
                out_ref[l, s] = jnp.where(lane == WINDOW - 1, newest, slid).reshape(
                    N_KV_HEADS, HEAD_DIM, WINDOW)


def _prompt_kernel(layer, slide, *refs):
    def sub_tile(s, carry):
        _prompt_tile(layer, slide, s, refs)
        return carry
    lax.fori_loop(0, SUB, sub_tile, 0)


def _prompt_tile(layer, slide, s, refs):
    n_in = 18 + (2 * DEPTH + 2 if slide else 0)
    n_out = 4 + (2 if slide else 0)
    (sinks_ref, x_ref, cos_ref, sin_ref, w_in_ref, bg_ref, poolw_ref, pscale_ref,
     slng_ref, slnb_ref, sguw_ref, sgub_ref, wpa_ref, wpb_ref, wpc_ref, wout_ref,
     lng_ref, lnb_ref) = refs[:18]
    y_ref, pool_out_ref, k_out_ref, v_out_ref = refs[n_in:n_in + 4]
    (h_ref, ext_ref, s2_ref, s4_ref, s8_ref, qs_ref,
     ka_ref, kb_ref, kc_ref, kd_ref, va_ref, vb_ref, vc_ref, vd_ref,
     ain_ref, bin_ref, cin_ref, mg_ref, gate_ref, klast_ref, vlast_ref) = refs[n_in + n_out:]
    tile_rows = pl.ds(pl.multiple_of(s * TM, TM), TM)
    x_ref, y_ref, cos_ref, sin_ref = (r.at[tile_rows] for r in (x_ref, y_ref, cos_ref, sin_ref))
    i = pl.program_id(1) * SUB + s
    last = pl.num_programs(1) * SUB - 1
    kv_refs = (ka_ref, kb_ref, kc_ref, kd_ref, va_ref, vb_ref, vc_ref, vd_ref)
    row = lambda ref: ref[layer:layer + 1, :]

    @pl.when(i == 0)
    def _():
        ext_ref[0:HIST, :] = jnp.zeros((HIST, D_POOL), f32)
        for r in kv_refs:
            r[0:BLOCK, :] = jnp.zeros((BLOCK, LANES), bf16)

    xb = x_ref[...].astype(bf16)
    half = OFF_G // 2
    h_ref[:, 0:half] = _dot(xb, w_in_ref[:, 0:half])
    h_ref[:, half:OFF_G] = _dot(xb, w_in_ref[:, half:OFF_G])

    if slide:
        step = pl.program_id(0) * (last + 1) + i
        new_refs = (refs[18:18 + DEPTH], refs[18 + DEPTH:18 + 2 * DEPTH])
        old_refs = refs[18 + 2 * DEPTH:n_in]
        per_step = old_refs[0].shape[1] // SUB
        mine = lambda r: r.at[:, pl.ds(s * per_step, per_step)]
        _slide_windows(step, per_step, new_refs, [mine(r) for r in old_refs],
                       [mine(r) for r in refs[n_in + 4:n_in + 6]])

    xa = h_ref[:, OFF_XA:OFF_XA + D_POOL]
    ext_ref[HIST:HIST + TM, :] = xa
    n = HIST + TM
    s2_ref[8:n, :] = ext_ref[8:n, :] + ext_ref[7:n - 1, :]
    s4_ref[16:n, :] = s2_ref[16:n, :] + s2_ref[14:n - 2, :]
    s8_ref[24:n, :] = s4_ref[24:n, :] + s4_ref[20:n - 4, :]
    w16 = s8_ref[HIST:n, :] + s8_ref[HIST - 8:n - 8, :]
    lane_p = lax.broadcasted_iota(jnp.int32, (TM, D_POOL), 1)
    row_p = lax.broadcasted_iota(jnp.int32, (TM, D_POOL), 0)
    win = _group_select(lane_p, s2_ref[HIST:n, :], s4_ref[HIST:n, :], s8_ref[HIST:n, :], w16)
    width = _group_select(lane_p, POOL_WINDOWS[0], POOL_WINDOWS[1], POOL_WINDOWS[2], POOL_WINDOWS[3])
    cnt = jnp.minimum(row_p + (i * TM + 1), width).astype(f32)
    pooled = win / cnt - xa
    ya = _dot(pooled.astype(bf16), poolw_ref[...]) * row(pscale_ref)
    za = h_ref[:, OFF_ZA:OFF_ZA + D_POOL]
    ain_ref[...] = (ya * _silu(za)).astype(bf16)
    ext_ref[HIST - 16:HIST, :] = ext_ref[n - 16:n, :]

    vn = _layer_norm(h_ref[:, OFF_V:OFF_V + D_SGU], row(slng_ref), row(slnb_ref)).astype(bf16)
    wr = lax.broadcasted_iota(jnp.int32, (N_SGU_GROUPS * CHUNK, CHUNK), 0) % CHUNK
    wc = lax.broadcasted_iota(jnp.int32, (N_SGU_GROUPS * CHUNK, CHUNK), 1)
    w_s = jnp.where(wc <= wr, sguw_ref[...], 0.0).astype(bf16)
    lane_c = lax.broadcasted_iota(jnp.int32, (CHUNK, D_SGU), 1)
    for j in range(NBLK):
        rows = slice(j * CHUNK, (j + 1) * CHUNK)
        r = _dot(w_s, vn[rows, :])
        mixed = _group_select(lane_c, r[0:CHUNK], r[CHUNK:2 * CHUNK], r[2 * CHUNK:3 * CHUNK],
                              r[3 * CHUNK:4 * CHUNK]) + sgub_ref[...]
        yb = h_ref[rows, OFF_U:OFF_U + D_SGU] * mixed
        bin_ref[rows, :] = (yb * _silu(h_ref[rows, OFF_ZB:OFF_ZB + D_SGU])).astype(bf16)

    cos = cos_ref[...]
    sin = sin_ref[...]
    for c in range(D_ATTN // LANES):
        qc = _rope128(h_ref[:, OFF_Q + c * LANES:OFF_Q + (c + 1) * LANES], cos, sin)
        qs_ref[:, c * LANES:(c + 1) * LANES] = (qc * SCALE).astype(bf16)
    kr = _rope128(h_ref[:, OFF_K:OFF_K + D_KV], cos, sin)
    vv = h_ref[:, OFF_VV:OFF_VV + D_KV]
    klast_ref[...] = kr[TM - WINDOW:TM, :]
    vlast_ref[...] = vv[TM - WINDOW:TM, :]
    lane_k = lax.broadcasted_iota(jnp.int32, (TM, LANES), 1)
    lo = lane_k < HEAD_DIM
    for src, (a_ref, b_ref, c_ref, d_ref) in ((kr, kv_refs[0:4]), (vv, kv_refs[4:8])):
        sw = pltpu.roll(src, HEAD_DIM, 1)
        a_ref[BLOCK:BLOCK + TM, :] = jnp.where(lo, src, 0.0).astype(bf16)
        b_ref[BLOCK:BLOCK + TM, :] = jnp.where(lo, 0.0, src).astype(bf16)
        c_ref[BLOCK:BLOCK + TM, :] = jnp.where(lo, sw, 0.0).astype(bf16)
        d_ref[BLOCK:BLOCK + TM, :] = jnp.where(lo, 0.0, sw).astype(bf16)

    qrow = lax.broadcasted_iota(jnp.int32, (2 * BLOCK, 2 * BLOCK), 0) % BLOCK
    kcol = lax.broadcasted_iota(jnp.int32, (2 * BLOCK, 2 * BLOCK), 1)
    band = (kcol >= qrow) & (kcol <= qrow + WINDOW)
    band_first = band & (kcol >= jnp.where(i > 0, 0, BLOCK))
    top = lax.broadcasted_iota(jnp.int32, (2 * BLOCK, 1), 0) < BLOCK
    n_gate = N_BRANCHES * D_MODEL // NCHUNK
    n_unit = NBLK * N_KV_HEADS
    gate_sched = [range(u * n_gate // n_unit, (u + 1) * n_gate // n_unit) for u in range(n_unit)]

    def scores(u):
        j, kv = divmod(u, N_KV_HEADS)
        rows = slice(j * BLOCK, (j + 1) * BLOCK)
        keys = slice(j * BLOCK, j * BLOCK + 2 * BLOCK)
        c0 = kv * Q_PER_KV * HEAD_DIM
        qst = jnp.concatenate([qs_ref[rows, c0:c0 + LANES], qs_ref[rows, c0 + LANES:c0 + 2 * LANES]], axis=0)
        k_even, k_odd = (ka_ref, kd_ref) if kv == 0 else (kc_ref, kb_ref)
        kcat = jnp.concatenate([k_even[keys, :], k_odd[keys, :]], axis=0)
        return _dot_nt(qst, kcat)

    def attend(u, sc):
        j, kv = divmod(u, N_KV_HEADS)
        rows = slice(j * BLOCK, (j + 1) * BLOCK)
        keys = slice(j * BLOCK, j * BLOCK + 2 * BLOCK)
        allowed = band_first if j == 0 else band
        c0 = kv * Q_PER_KV * HEAD_DIM
        h0 = kv * Q_PER_KV
        v_even, v_odd = (va_ref, vd_ref) if kv == 0 else (vc_ref, vb_ref)
        probs = []
        for par in range(2):
            sink = jnp.where(top, sinks_ref[layer, h0 + par], sinks_ref[layer, h0 + 2 + par])
            sm = jnp.where(allowed, sc[:, par * 2 * BLOCK:(par + 1) * 2 * BLOCK], NEG_INF)
            m = jnp.maximum(jnp.max(sm, axis=-1, keepdims=True), sink)
            p = jnp.exp(sm - m)
            den = jnp.sum(p, axis=-1, keepdims=True) + jnp.exp(sink - m)
            probs.append((p / den).astype(bf16))
        pcat = jnp.concatenate(probs, axis=1)
        vcat = jnp.concatenate([v_even[keys, :], v_odd[keys, :]], axis=0)
        o = _dot(pcat, vcat)
        for pr in range(2):
            cols = slice(c0 + pr * LANES, c0 + (pr + 1) * LANES)
            zc = h_ref[rows, OFF_ZC + c0 + pr * LANES:OFF_ZC + c0 + (pr + 1) * LANES]
            cin_ref[rows, cols] = (o[pr * BLOCK:(pr + 1) * BLOCK] * _silu(zc)).astype(bf16)

    sc_next = scores(0)
    for u in range(n_unit):
        sc = sc_next
        for gc in gate_sched[u]:
            gcols = slice(gc * NCHUNK, (gc + 1) * NCHUNK)
            gate_ref[:, gcols] = _dot(xb, w_in_ref[:, OFF_G + gc * NCHUNK:OFF_G + (gc + 1) * NCHUNK])
        if u + 1 < n_unit:
            sc_next = scores(u + 1)
        attend(u, sc)

    for r in kv_refs:
        r[0:BLOCK, :] = r[TM:TM + BLOCK, :]

    for c in range(D_MODEL // NCHUNK):
        cols = slice(c * NCHUNK, (c + 1) * NCHUNK)
        acc = None
        for br, (in_ref, wp_ref) in enumerate(((ain_ref, wpa_ref), (bin_ref, wpb_ref), (cin_ref, wpc_ref))):
            g0 = br * D_MODEL + c * NCHUNK
            term = (_gate2(gate_ref[:, g0:g0 + NCHUNK], bg_ref[br, layer:layer + 1, cols])
                    * _dot(in_ref[...], wp_ref[:, cols]))
            acc = term if acc is None else acc + term
        mg_ref[:, cols] = acc.astype(bf16)

    for r0, r1 in OUT_CHUNKS:
        rows = slice(r0, r1)
        out = _dot(mg_ref[rows, :], wout_ref[...])
        y_ref[rows, :] = _layer_norm(ALPHA * x_ref[rows, :] + out, row(lng_ref), row(lnb_ref))

    @pl.when(i == last)
    def _():
        pool_out_ref[...] = ext_ref[n - POOL_BUF:n, :]
        k_out_ref[...] = klast_ref[...].T
        v_out_ref[...] = vlast_ref[...].T


def _prompt_layer(layer, x, cos, sin, sinks, w_in, b_gate, poolw, pscale, slng, slnb, sguw, sgub,
                  wpa, wpb, wpc, wout, lng, lnb, slide_args=None):
    B, L, _ = x.shape
    rows = TM * SUB
    nt = L // rows
    grid = (B, nt)
    row_spec = lambda w: pl.BlockSpec((rows, w), lambda b, i: (i, 0))
    lspec = lambda a: _layer_spec(a, layer, single_buffer=True)
    wspec = lambda a: _full_spec(a, single_buffer=True)
    in_specs = [
        pl.BlockSpec(memory_space=pltpu.SMEM),
        pl.BlockSpec((None, rows, D_MODEL), lambda b, i: (b, i, 0)),
        row_spec(LANES), row_spec(LANES),
        wspec(w_in), _full_spec(b_gate), wspec(poolw), _full_spec(pscale), _full_spec(slng), _full_spec(slnb),
        lspec(sguw), lspec(sgub), wspec(wpa), wspec(wpb), wspec(wpc), wspec(wout),
        _full_spec(lng), _full_spec(lnb),
    ]
    out_shape = (
        jax.ShapeDtypeStruct((B, L, D_MODEL), f32),
        jax.ShapeDtypeStruct((B, POOL_BUF, D_POOL), f32),
        jax.ShapeDtypeStruct((B, D_KV, WINDOW), f32),
        jax.ShapeDtypeStruct((B, D_KV, WINDOW), f32),
    )
    out_specs = (
        pl.BlockSpec((None, rows, D_MODEL), lambda b, i: (b, i, 0)),
        pl.BlockSpec((None, POOL_BUF, D_POOL), lambda b, i: (b, 0, 0)),
        pl.BlockSpec((None, D_KV, WINDOW), lambda b, i: (b, 0, 0)),
        pl.BlockSpec((None, D_KV, WINDOW), lambda b, i: (b, 0, 0)),
    )
    args = [sinks, x, cos, sin, w_in, b_gate, poolw, pscale, slng, slnb, sguw, sgub,
            wpa, wpb, wpc, wout, lng, lnb]
    if slide_args is not None:
        new_k, new_v, cache_k, cache_v = slide_args
        depth, nb = cache_k.shape[:2]
        per_step = nb // (B * nt)
        assert per_step * B * nt == nb and depth == DEPTH
        cache_spec = pl.BlockSpec((depth, per_step) + cache_k.shape[2:], lambda b, i: (0, b * nt + i, 0, 0, 0))
        in_specs += [_full_spec(a) for a in (*new_k, *new_v)] + [cache_spec, cache_spec]
        args += [*new_k, *new_v, cache_k, cache_v]
        out_shape += (jax.ShapeDtypeStruct(cache_k.shape, f32), jax.ShapeDtypeStruct(cache_v.shape, f32))
        out_specs += (cache_spec, cache_spec)
    kv_scratch = [pltpu.VMEM((BLOCK + TM, LANES), bf16) for _ in range(8)]
    scratch = [
        pltpu.VMEM((TM, OFF_G), f32),
        pltpu.VMEM((HIST + TM, D_POOL), f32),
        pltpu.VMEM((HIST + TM, D_POOL), f32),
        pltpu.VMEM((HIST + TM, D_POOL), f32),
        pltpu.VMEM((HIST + TM, D_POOL), f32),
        pltpu.VMEM((TM, D_ATTN), bf16),
        *kv_scratch,
        pltpu.VMEM((TM, D_POOL), bf16),
        pltpu.VMEM((TM, D_SGU), bf16),
        pltpu.VMEM((TM, D_ATTN), bf16),
        pltpu.VMEM((TM, D_MODEL), bf16),
        pltpu.VMEM((TM, N_BRANCHES * D_MODEL), f32),
        pltpu.VMEM((WINDOW, D_KV), f32),
        pltpu.VMEM((WINDOW, D_KV), f32),
    ]
    return pl.pallas_call(
        functools.partial(_prompt_kernel, layer, slide_args is not None),
        out_shape=out_shape,
        grid=grid,
        in_specs=in_specs,
        out_specs=out_specs,
        scratch_shapes=scratch,
        compiler_params=pltpu.CompilerParams(
            dimension_semantics=("arbitrary", "arbitrary"),
            vmem_limit_bytes=VMEM_LIMIT),
        name="prompt_layer",
    )(*args)


def _sample_proj_kernel(layer, x_ref, cos_ref, sin_ref, w_in_ref, bg_ref, pb_ref, poolw_ref, pscale_ref,
                        slng_ref, slnb_ref, sw0_ref, sb0_ref,
                        wb_ref, qkv_ref, kt_ref, vt_ref, pool_ref, vn_ref, ain_ref, bin_ref,
                        szc_ref, g_ref, poolbd_ref, h_ref):
    j = pl.program_id(0)
    row = lambda ref: ref[layer:layer + 1, :]
    wb_ref[:, 0:OFF_G] = w_in_ref[:, 0:OFF_G].astype(bf16)
    wb_ref[:, OFF_G:D_IN] = (0.5 * w_in_ref[:, OFF_G:D_IN]).astype(bf16)

    @pl.when(j == 0)
    def _():
        h_ref[...] = jnp.zeros(h_ref.shape, f32)

    h_ref[...] += _dot(x_ref[...].astype(bf16), wb_ref[...])

    def hcols(off, width):
        return h_ref[:, off:off + width]

    @pl.when(j == RK_STEPS - 1)
    def _():
        xa = hcols(OFF_XA, D_POOL)
        lane = lax.broadcasted_iota(jnp.int32, xa.shape, 1)
        first_row = _group_select(lane, *(POOL_BUF - (w - 1) for w in POOL_WINDOWS))
        win = xa
        for r in range(POOL_BUF):
            win = win + jnp.where(first_row <= r, pb_ref[r], 0.0)
        width = _group_select(lane, *POOL_WINDOWS).astype(f32)
        pooled = win / width - xa
        zero_blk = jnp.zeros((POOL_GC, POOL_GC), f32)
        n_grp = len(POOL_WINDOWS)
        pool_bd = jnp.concatenate(
            [jnp.concatenate([poolw_ref[g] if gg == g else zero_blk for gg in range(n_grp)], axis=1)
             for g in range(n_grp)], axis=0).astype(bf16)
        poolbd_ref[...] = pool_bd
        ya = _dot(pooled.astype(bf16), pool_bd) * row(pscale_ref)
        ain_ref[...] = (ya * _silu(hcols(OFF_ZA, D_POOL))).astype(bf16)
        for r in range(POOL_BUF - 1):
            pool_ref[r] = pb_ref[r + 1]
        pool_ref[POOL_BUF - 1] = xa
        vn = _layer_norm(hcols(OFF_V, D_SGU), row(slng_ref), row(slnb_ref))
        vn_ref[...] = vn
        lane_g = lax.broadcasted_iota(jnp.int32, (1, D_SGU), 1)
        sw0 = _group_select(lane_g, *(sw0_ref[g, 0:1, 0:1] for g in range(N_SGU_GROUPS)))
        sb0 = _group_select(lane_g, *(sb0_ref[g:g + 1, 0:1] for g in range(N_SGU_GROUPS)))
        yb = hcols(OFF_U, D_SGU) * (sw0 * vn + sb0)
        bin_ref[...] = (yb * _silu(hcols(OFF_ZB, D_SGU))).astype(bf16)
        cos = cos_ref[...]
        sin = sin_ref[...]
        for c in range(D_ATTN // LANES):
            qkv_ref[:, c * LANES:(c + 1) * LANES] = _rope128(hcols(OFF_Q + c * LANES, LANES), cos, sin) * SCALE
        kr = _rope128(hcols(OFF_K, D_KV), cos, sin)
        vv = hcols(OFF_VV, D_KV)
        qkv_ref[:, D_ATTN:D_ATTN + D_KV] = kr
        qkv_ref[:, D_ATTN + D_KV:D_ATTN + 2 * D_KV] = vv
        kt_ref[...] = kr.T
        vt_ref[...] = vv.T
        szc_ref[...] = _silu(hcols(OFF_ZC, D_ATTN))
        for br in range(N_BRANCHES):
            cols = slice(br * D_MODEL, (br + 1) * D_MODEL)
            g_ref[:, cols] = _gate2(hcols(OFF_G + br * D_MODEL, D_MODEL), bg_ref[br, layer:layer + 1, :])


def _sample_proj(layer, x, cos, sin, w_in, b_gate, pool_t, poolw, pscale, slng, slnb, sw0, sb0):
    nb = x.shape[0]
    out_shape = (
        jax.ShapeDtypeStruct((D_MODEL, D_IN), bf16),
        jax.ShapeDtypeStruct((nb, D_ATTN + 2 * D_KV), f32),
        jax.ShapeDtypeStruct((D_KV, nb), f32),
        jax.ShapeDtypeStruct((D_KV, nb), f32),
        jax.ShapeDtypeStruct((POOL_BUF, nb, D_POOL), f32),
        jax.ShapeDtypeStruct((nb, 1, D_SGU), f32),
        jax.ShapeDtypeStruct((nb, D_POOL), bf16),
        jax.ShapeDtypeStruct((nb, D_SGU), bf16),
        jax.ShapeDtypeStruct((nb, D_ATTN), f32),
        jax.ShapeDtypeStruct((nb, N_BRANCHES * D_MODEL), f32),
        jax.ShapeDtypeStruct((D_POOL, D_POOL), bf16),
    )
    lspec = lambda a: _layer_spec(a, layer)
    args = (x, cos, sin, w_in, b_gate, pool_t, poolw, pscale, slng, slnb, sw0, sb0)
    in_specs = [pl.BlockSpec((nb, None, RK), lambda j: (0, 0, j)), _full_spec(cos), _full_spec(sin),
                pl.BlockSpec((None, RK, D_IN), lambda j: (layer, j, 0)),
                _full_spec(b_gate), lspec(pool_t), lspec(poolw), _full_spec(pscale), _full_spec(slng),
                _full_spec(slnb),
                pl.BlockSpec((None, N_SGU_GROUPS, 8, CHUNK), lambda j: (layer, 0, 0, 0)),
                pl.BlockSpec((None, N_SGU_GROUPS, CHUNK), lambda j: (layer, 0, 0))]
    squeeze_mid = lambda s: pl.BlockSpec((s.shape[0], None, s.shape[2]), lambda j: (0, 0, 0))
    out_specs = (pl.BlockSpec((RK, D_IN), lambda j: (j, 0)),) + tuple(
        squeeze_mid(s) if len(s.shape) == 3 and s.shape[1] == 1 else _full_spec(s) for s in out_shape[1:])
    return pl.pallas_call(
        functools.partial(_sample_proj_kernel, layer),
        out_shape=out_shape,
        grid=(RK_STEPS,),
        in_specs=in_specs,
        out_specs=out_specs,
        scratch_shapes=[pltpu.VMEM((nb, D_IN), f32)],
        compiler_params=pltpu.CompilerParams(dimension_semantics=("arbitrary",),
                                             vmem_limit_bytes=VMEM_LIMIT),
        name="sample_proj",
    )(*args)


def _sample_attn_kernel(layer, sink_ref, qkv_ref, kc_ref, vc_ref, o_ref):
    head = lax.broadcasted_iota(jnp.int32, (1, Q_PER_KV, 1), 1)
    for kv in range(N_KV_HEADS):
        qb = qkv_ref[:, kv * Q_PER_KV:(kv + 1) * Q_PER_KV, :].astype(bf16)
        kn = qkv_ref[:, N_HEADS + kv:N_HEADS + kv + 1, :]
        vn = qkv_ref[:, N_HEADS + N_KV_HEADS + kv:N_HEADS + N_KV_HEADS + kv + 1, :]
        sink = jnp.zeros((1, Q_PER_KV, 1), f32)
        for g in range(Q_PER_KV):
            sink = jnp.where(head == g, sink_ref[layer, kv * Q_PER_KV + g], sink)
        s = jnp.einsum('bgd,bdw->bgw', qb, kc_ref[:, kv].astype(bf16), preferred_element_type=f32)
        s_new = jnp.sum(qb.astype(f32) * kn.astype(bf16).astype(f32), axis=-1, keepdims=True)
        m = jnp.maximum(jnp.maximum(jnp.max(s, axis=-1, keepdims=True), s_new), sink)
        p = jnp.exp(s - m)
        p_new = jnp.exp(s_new - m)
        den = jnp.sum(p, axis=-1, keepdims=True) + p_new + jnp.exp(sink - m)
        o = jnp.einsum('bgw,bdw->bgd', (p / den).astype(bf16), vc_ref[:, kv].astype(bf16),
                       preferred_element_type=f32)
        o_ref[:, kv] = o + (p_new / den) * vn


def _sample_attn(layer, sink, qkv3, kc, vc):
    nb, n_rows, _ = qkv3.shape
    cache_spec = pl.BlockSpec((None, SB, N_KV_HEADS, HEAD_DIM, WINDOW), lambda b: (layer, b, 0, 0, 0))
    return pl.pallas_call(
        functools.partial(_sample_attn_kernel, layer),
        out_shape=jax.ShapeDtypeStruct((nb, N_KV_HEADS, Q_PER_KV, HEAD_DIM), f32),
        grid=(nb // SB,),
        in_specs=[pl.BlockSpec(memory_space=pltpu.SMEM),
                  pl.BlockSpec((SB, n_rows, HEAD_DIM), lambda b: (b, 0, 0)), cache_spec, cache_spec],
        out_specs=pl.BlockSpec((SB, N_KV_HEADS, Q_PER_KV, HEAD_DIM), lambda b: (b, 0, 0, 0)),
        compiler_params=pltpu.CompilerParams(dimension_semantics=("arbitrary",),
                                             vmem_limit_bytes=VMEM_LIMIT),
        name="sample_attn",
    )(sink, qkv3, kc, vc)


def _sample_merge_kernel(layer, x_ref, ain_ref, bin_ref, yc_ref, szc_ref, g_ref, wpa_ref, wpb_ref, wpc_ref,
                         wout_ref, lng_ref, lnb_ref, y_ref, wpa_b_ref, wpb_b_ref, wpc_b_ref, wout_b_ref):
    row = lambda ref: ref[layer:layer + 1, :]
    wpa = (0.5 * wpa_ref[...]).astype(bf16)
    wpb = (0.5 * wpb_ref[...]).astype(bf16)
    wpc = (0.5 * wpc_ref[...]).astype(bf16)
    wout = wout_ref[...].astype(bf16)
    wpa_b_ref[...] = wpa
    wpb_b_ref[...] = wpb
    wpc_b_ref[...] = wpc
    wout_b_ref[...] = wout
    cin = (yc_ref[...] * szc_ref[...]).astype(bf16)
    merged = (g_ref[:, 0:D_MODEL] * _dot(ain_ref[...], wpa)
              + g_ref[:, D_MODEL:2 * D_MODEL] * _dot(bin_ref[...], wpb)
              + g_ref[:, 2 * D_MODEL:3 * D_MODEL] * _dot(cin, wpc))
    out = _dot(merged.astype(bf16), wout)
    y_ref[...] = _layer_norm(ALPHA * x_ref[...] + out, row(lng_ref), row(lnb_ref))


def _sample_merge(layer, x, ain, bin_, yc, szc, g, wpa, wpb, wpc, wout, lng, lnb):
    lspec = lambda a: _layer_spec(a, layer)
    nb = x.shape[0]
    x_spec = pl.BlockSpec((nb, None, D_MODEL), lambda i: (0, 0, 0))
    out_shape = (jax.ShapeDtypeStruct(x.shape, f32),) + tuple(
        jax.ShapeDtypeStruct(w.shape[1:], bf16) for w in (wpa, wpb, wpc, wout))
    return pl.pallas_call(
        functools.partial(_sample_merge_kernel, layer),
        out_shape=out_shape,
        grid=(1,),
        in_specs=[x_spec, _full_spec(ain), _full_spec(bin_), _full_spec(yc), _full_spec(szc),
                  _full_spec(g), lspec(wpa), lspec(wpb), lspec(wpc), lspec(wout), _full_spec(lng),
                  _full_spec(lnb)],
        out_specs=(x_spec,) + tuple(_full_spec(s) for s in out_shape[1:]),
        compiler_params=pltpu.CompilerParams(dimension_semantics=("arbitrary",),
                                             vmem_limit_bytes=VMEM_LIMIT),
        name="sample_merge",
    )(x, ain, bin_, yc, szc, g, wpa, wpb, wpc, wout, lng, lnb)


def _rope_tables(positions):
    halfd = HEAD_DIM // 2
    inv = ROPE_THETA ** (-np.arange(halfd, dtype=np.float64) / halfd)
    ang = np.asarray(positions, dtype=np.float64)[:, None] * inv[None, :]
    cos = np.tile(np.cos(ang), (1, LANES // halfd))
    sin = np.tile(np.concatenate([-np.sin(ang), np.sin(ang)], axis=1), (1, LANES // HEAD_DIM))
    return jnp.asarray(cos, f32), jnp.asarray(sin, f32)


def kernel(x_prompt, x_sample, state_pool, cache_k_win, cache_v_win, w_in, b_gate, pool_w, pool_scale, sgu_ln_g, sgu_ln_b, sgu_w, sgu_b, attn_sinks, w_proj_a, w_proj_b, w_proj_c, w_out, ln_g, ln_b):
    B, L, _ = x_prompt.shape
    nb = x_sample.shape[0]
    cos_p, sin_p = _rope_tables(np.arange(L))
    cos_s, sin_s = _rope_tables(np.array([PAST_LEN]))

    sguw = sgu_w.reshape(DEPTH, N_SGU_GROUPS * CHUNK, CHUNK)
    sgub = jnp.repeat(jnp.swapaxes(sgu_b, 1, 2), POOL_GC, axis=2)
    bg_t = jnp.transpose(b_gate, (1, 0, 2))
    kc_t = jnp.transpose(cache_k_win, (0, 1, 3, 4, 2))
    vc_t = jnp.transpose(cache_v_win, (0, 1, 3, 4, 2))
    pool_t = jnp.transpose(state_pool, (0, 2, 1, 3))

    y_s = x_sample
    pool_s, chunk_v, kts, vts, weights = ([] for _ in range(5))
    for l in range(DEPTH):
        w_in_b, qkv, kt, vt, ps, cv, ain, bin_, szc, g, pool_bd = _sample_proj(
            l, y_s, cos_s, sin_s, w_in, bg_t, pool_t, pool_w, pool_scale, sgu_ln_g, sgu_ln_b, sgu_w, sgu_b)
        o = _sample_attn(l, attn_sinks, qkv.reshape(nb, N_HEADS + 2 * N_KV_HEADS, HEAD_DIM), kc_t, vc_t)
        y_s, wpa, wpb, wpc, wout = _sample_merge(
            l, y_s, ain, bin_, o.reshape(nb, D_ATTN), szc, g, w_proj_a, w_proj_b, w_proj_c, w_out, ln_g, ln_b)
        pool_s.append(ps); chunk_v.append(cv); kts.append(kt); vts.append(vt)
        weights.append((w_in_b, wpa, wpb, wpc, wout, pool_bd))

    y_p = x_prompt
    pool_p, k_p, v_p = ([] for _ in range(3))
    for l in range(DEPTH):
        w_in_b, wpa, wpb, wpc, wout, poolw = weights[l]
        slide_args = (kts, vts, kc_t, vc_t) if l == DEPTH - 1 else None
        y_p, pp, kp, vp, *slid = _prompt_layer(
            l, y_p, cos_p, sin_p, attn_sinks, w_in_b, bg_t, poolw, pool_scale, sgu_ln_g, sgu_ln_b, sguw,
            sgub, wpa, wpb, wpc, wout, ln_g, ln_b, slide_args=slide_args)
        pool_p.append(pp); k_p.append(kp); v_p.append(vp)
    k_s, v_s = slid

    to_cache = lambda a: jnp.transpose(a, (0, 1, 4, 2, 3))
    prompt_cache = lambda lst: to_cache(jnp.stack(lst).reshape(DEPTH, B, N_KV_HEADS, HEAD_DIM, WINDOW))
    return (y_p, y_s,
            jnp.stack(pool_p), prompt_cache(k_p), prompt_cache(v_p),
            jnp.transpose(jnp.stack(pool_s), (0, 2, 1, 3)), to_cache(k_s), to_cache(v_s),
            jnp.stack(chunk_v))
```

```python
import functools

import numpy as np
import jax
import jax.numpy as jnp
from jax import lax
from jax.experimental import pallas as pl
from jax.experimental.pallas import tpu as pltpu

D_MODEL = 1024
DEPTH = 2
PAST_LEN = 8192
D_POOL = 256
POOL_WINDOWS = (2, 4, 8, 16)
POOL_GC = 64
POOL_BUF = 15
D_SGU = 256
CHUNK = 128
N_SGU_GROUPS = 4
HEAD_DIM = 64
N_HEADS = 8
N_KV_HEADS = 2
Q_PER_KV = 4
D_ATTN = 512
D_KV = 128
WINDOW = 128
BLOCK = 128
ROPE_THETA = 10000.0
N_BRANCHES = 3
D_IN = 2 * D_POOL + 3 * D_SGU + 2 * D_ATTN + 2 * D_KV + N_BRANCHES * D_MODEL
ALPHA = (2.0 * DEPTH) ** 0.25
LN_EPS = 1e-5
NEG_INF = -1e30
SCALE = HEAD_DIM ** -0.5

OFF_XA, OFF_ZA, OFF_U, OFF_V, OFF_ZB = 0, 256, 512, 768, 1024
OFF_Q, OFF_K, OFF_VV, OFF_ZC, OFF_G = 1280, 1792, 1920, 2048, 2560

LANES = 128
TM = 512
SUB = 1
NBLK = TM // BLOCK
HIST = 32
NCHUNK = 256
OUT_CHUNKS = ((0, 256), (256, 512))
VMEM_LIMIT = 56 * 1024 * 1024
SB = 32
RK = 512
RK_STEPS = D_MODEL // RK

bf16 = jnp.bfloat16
f32 = jnp.float32


def _dot(a, b):
    return jnp.dot(a, b, preferred_element_type=f32)


def _dot_nt(a, b):
    return lax.dot_general(a, b, (((1,), (1,)), ((), ())), preferred_element_type=f32)


def _sigmoid(z):
    return 0.5 * jnp.tanh(0.5 * z) + 0.5


def _silu(z):
    return z * _sigmoid(z)


def _gate2(half_pre, bias):
    return jnp.tanh(half_pre + 0.5 * bias) + 1.0


def _layer_norm(x, g, b):
    mu = jnp.mean(x, axis=-1, keepdims=True)
    xc = x - mu
    var = jnp.mean(xc * xc, axis=-1, keepdims=True)
    return xc * lax.rsqrt(var + LN_EPS) * g + b


def _rope128(x, cos, sin_signed):
    lane = lax.broadcasted_iota(jnp.int32, x.shape, 1)
    first_half = (lane % HEAD_DIM) < (HEAD_DIM // 2)
    partner = jnp.where(first_half,
                        pltpu.roll(x, LANES - HEAD_DIM // 2, 1),
                        pltpu.roll(x, HEAD_DIM // 2, 1))
    return x * cos + partner * sin_signed


def _group_select(lane, a0, a1, a2, a3):
    return jnp.where(lane < 64, a0, jnp.where(lane < 128, a1, jnp.where(lane < 192, a2, a3)))


def _layer_spec(arr, layer, single_buffer=False):
    block = (None,) + arr.shape[1:]
    zeros = (0,) * (arr.ndim - 1)
    index_map = lambda *_: (layer,) + zeros
    if single_buffer:
        return pl.BlockSpec(block, index_map, pipeline_mode=pl.Buffered(1))
    return pl.BlockSpec(block, index_map)


def _full_spec(arr, single_buffer=False):
    zeros = (0,) * len(arr.shape)
    if single_buffer:
        return pl.BlockSpec(arr.shape, lambda *_: zeros, pipeline_mode=pl.Buffered(1))
    return pl.BlockSpec(arr.shape, lambda *_: zeros)


def _slide_windows(step, per_step, new_refs, old_refs, out_refs):
    lane = lax.broadcasted_iota(jnp.int32, (D_KV, WINDOW), 1)
    shift = jnp.where(step == 0, 0, LANES - step * per_step)
    for new_ref, old_ref, out_ref in zip(new_refs, old_refs, out_refs):
        for l in range(DEPTH):
            new_cols = pltpu.roll(new_ref[l][...], shift, 1)
            for s in range(per_step):
                slid = pltpu.roll(old_ref[l, s].reshape(D_KV, WINDOW), WINDOW - 1, 1)
                newest = jnp.broadcast_to(new_cols[:, s:s + 1], (D_KV, WINDOW))
                out_ref[l, s] = jnp.where(lane == WINDOW - 1, newest, slid).reshape(
                    N_KV_HEADS, HEAD_DIM, WINDOW)


def _prompt_kernel(layer, slide, *refs):
    def sub_tile(s, carry):
        _prompt_tile(layer, slide, s, refs)
        return carry
    lax.fori_loop(0, SUB, sub_tile, 0)


def _prompt_tile(layer, slide, s, refs):
    n_in = 18 + (2 * DEPTH + 2 if slide else 0)
    n_out = 4 + (2 if slide else 0)
    (sinks_ref, x_ref, cos_ref, sin_ref, w_in_ref, bg_ref, poolw_ref, pscale_ref,
     slng_ref, slnb_ref, sguw_ref, sgub_ref, wpa_ref, wpb_ref, wpc_ref, wout_ref,
     lng_ref, lnb_ref) = refs[:18]
    y_ref, pool_out_ref, k_out_ref, v_out_ref = refs[n_in:n_in + 4]
    (h_ref, ext_ref, s2_ref, s4_ref, s8_ref, qs_ref,
     ka_ref, kb_ref, kc_ref, kd_ref, va_ref, vb_ref, vc_ref, vd_ref,
     ain_ref, bin_ref, cin_ref, mg_ref, gate_ref, klast_ref, vlast_ref) = refs[n_in + n_out:]
    tile_rows = pl.ds(pl.multiple_of(s * TM, TM), TM)
    x_ref, y_ref, cos_ref, sin_ref = (r.at[tile_rows] for r in (x_ref, y_ref, cos_ref, sin_ref))
    i = pl.program_id(1) * SUB + s
    last = pl.num_programs(1) * SUB - 1
    kv_refs = (ka_ref, kb_ref, kc_ref, kd_ref, va_ref, vb_ref, vc_ref, vd_ref)
    row = lambda ref: ref[layer:layer + 1, :]

    @pl.when(i == 0)
    def _():
        ext_ref[0:HIST, :] = jnp.zeros((HIST, D_POOL), f32)
        for r in kv_refs:
            r[0:BLOCK, :] = jnp.zeros((BLOCK, LANES), bf16)

    xb = x_ref[...].astype(bf16)
    half = OFF_G // 2
    h_ref[:, 0:half] = _dot(xb, w_in_ref[:, 0:half])
    h_ref[:, half:OFF_G] = _dot(xb, w_in_ref[:, half:OFF_G])

    if slide:
        step = pl.program_id(0) * (last + 1) + i
        new_refs = (refs[18:18 + DEPTH], refs[18 + DEPTH:18 + 2 * DEPTH])
        old_refs = refs[18 + 2 * DEPTH:n_in]
        per_step = old_refs[0].shape[1] // SUB
        mine = lambda r: r.at[:, pl.ds(s * per_step, per_step)]
        _slide_windows(step, per_step, new_refs, [mine(r) for r in old_refs],
                       [mine(r) for r in refs[n_in + 4:n_in + 6]])

    xa = h_ref[:, OFF_XA:OFF_XA + D_POOL]
    ext_ref[HIST:HIST + TM, :] = xa
    n = HIST + TM
    s2_ref[8:n, :] = ext_ref[8:n, :] + ext_ref[7:n - 1, :]
    s4_ref[16:n, :] = s2_ref[16:n, :] + s2_ref[14:n - 2, :]
    s8_ref[24:n, :] = s4_ref[24:n, :] + s4_ref[20:n - 4, :]
    w16 = s8_ref[HIST:n, :] + s8_ref[HIST - 8:n - 8, :]
    lane_p = lax.broadcasted_iota(jnp.int32, (TM, D_POOL), 1)
    row_p = lax.broadcasted_iota(jnp.int32, (TM, D_POOL), 0)
    win = _group_select(lane_p, s2_ref[HIST:n, :], s4_ref[HIST:n, :], s8_ref[HIST:n, :], w16)
    width = _group_select(lane_p, POOL_WINDOWS[0], POOL_WINDOWS[1], POOL_WINDOWS[2], POOL_WINDOWS[3])
    cnt = jnp.minimum(row_p + (i * TM + 1), width).astype(f32)
    pooled = win / cnt - xa
    ya = _dot(pooled.astype(bf16), poolw_ref[...]) * row(pscale_ref)
    za = h_ref[:, OFF_ZA:OFF_ZA + D_POOL]
    ain_ref[...] = (ya * _silu(za)).astype(bf16)
    ext_ref[HIST - 16:HIST, :] = ext_ref[n - 16:n, :]

    vn = _layer_norm(h_ref[:, OFF_V:OFF_V + D_SGU], row(slng_ref), row(slnb_ref)).astype(bf16)
    wr = lax.broadcasted_iota(jnp.int32, (N_SGU_GROUPS * CHUNK, CHUNK), 0) % CHUNK
    wc = lax.broadcasted_iota(jnp.int32, (N_SGU_GROUPS * CHUNK, CHUNK), 1)
    w_s = jnp.where(wc <= wr, sguw_ref[...], 0.0).astype(bf16)
    lane_c = lax.broadcasted_iota(jnp.int32, (CHUNK, D_SGU), 1)
    for j in range(NBLK):
        rows = slice(j * CHUNK, (j + 1) * CHUNK)
        r = _dot(w_s, vn[rows, :])
        mixed = _group_select(lane_c, r[0:CHUNK], r[CHUNK:2 * CHUNK], r[2 * CHUNK:3 * CHUNK],
                              r[3 * CHUNK:4 * CHUNK]) + sgub_ref[...]
        yb = h_ref[rows, OFF_U:OFF_U + D_SGU] * mixed
        bin_ref[rows, :] = (yb * _silu(h_ref[rows, OFF_ZB:OFF_ZB + D_SGU])).astype(bf16)

    cos = cos_ref[...]
    sin = sin_ref[...]
    for c in range(D_ATTN // LANES):
        qc = _rope128(h_ref[:, OFF_Q + c * LANES:OFF_Q + (c + 1) * LANES], cos, sin)
        qs_ref[:, c * LANES:(c + 1) * LANES] = (qc * SCALE).astype(bf16)
    kr = _rope128(h_ref[:, OFF_K:OFF_K + D_KV], cos, sin)
    vv = h_ref[:, OFF_VV:OFF_VV + D_KV]
    klast_ref[...] = kr[TM - WINDOW:TM, :]
    vlast_ref[...] = vv[TM - WINDOW:TM, :]
    lane_k = lax.broadcasted_iota(jnp.int32, (TM, LANES), 1)
    lo = lane_k < HEAD_DIM
    for src, (a_ref, b_ref, c_ref, d_ref) in ((kr, kv_refs[0:4]), (vv, kv_refs[4:8])):
        sw = pltpu.roll(src, HEAD_DIM, 1)
        a_ref[BLOCK:BLOCK + TM, :] = jnp.where(lo, src, 0.0).astype(bf16)
        b_ref[BLOCK:BLOCK + TM, :] = jnp.where(lo, 0.0, src).astype(bf16)
        c_ref[BLOCK:BLOCK + TM, :] = jnp.where(lo, sw, 0.0).astype(bf16)
        d_ref[BLOCK:BLOCK + TM, :] = jnp.where(lo, 0.0, sw).astype(bf16)

    qrow = lax.broadcasted_iota(jnp.int32, (2 * BLOCK, 2 * BLOCK), 0) % BLOCK
    kcol = lax.broadcasted_iota(jnp.int32, (2 * BLOCK, 2 * BLOCK), 1)
    band = (kcol >= qrow) & (kcol <= qrow + WINDOW)
    band_first = band & (kcol >= jnp.where(i > 0, 0, BLOCK))
    top = lax.broadcasted_iota(jnp.int32, (2 * BLOCK, 1), 0) < BLOCK
    n_gate = N_BRANCHES * D_MODEL // NCHUNK
    n_unit = NBLK * N_KV_HEADS
    gate_sched = [range(u * n_gate // n_unit, (u + 1) * n_gate // n_unit) for u in range(n_unit)]

    def scores(u):
        j, kv = divmod(u, N_KV_HEADS)
        rows = slice(j * BLOCK, (j + 1) * BLOCK)
        keys = slice(j * BLOCK, j * BLOCK + 2 * BLOCK)
        c0 = kv * Q_PER_KV * HEAD_DIM
        qst = jnp.concatenate([qs_ref[rows, c0:c0 + LANES], qs_ref[rows, c0 + LANES:c0 + 2 * LANES]], axis=0)
        k_even, k_odd = (ka_ref, kd_ref) if kv == 0 else (kc_ref, kb_ref)
        kcat = jnp.concatenate([k_even[keys, :], k_odd[keys, :]], axis=0)
        return _dot_nt(qst, kcat)

    def attend(u, sc):
        j, kv = divmod(u, N_KV_HEADS)
        rows = slice(j * BLOCK, (j + 1) * BLOCK)
        keys = slice(j * BLOCK, j * BLOCK + 2 * BLOCK)
        allowed = band_first if j == 0 else band
        c0 = kv * Q_PER_KV * HEAD_DIM
        h0 = kv * Q_PER_KV
        v_even, v_odd = (va_ref, vd_ref) if kv == 0 else (vc_ref, vb_ref)
        probs = []
        for par in range(2):
            sink = jnp.where(top, sinks_ref[layer, h0 + par], sinks_ref[layer, h0 + 2 + par])
            sm = jnp.where(allowed, sc[:, par * 2 * BLOCK:(par + 1) * 2 * BLOCK], NEG_INF)
            m = jnp.maximum(jnp.max(sm, axis=-1, keepdims=True), sink)
            p = jnp.exp(sm - m)
            den = jnp.sum(p, axis=-1, keepdims=True) + jnp.exp(sink - m)
            probs.append((p / den).astype(bf16))
        pcat = jnp.concatenate(probs, axis=1)
        vcat = jnp.concatenate([v_even[keys, :], v_odd[keys, :]], axis=0)
        o = _dot(pcat, vcat)
        for pr in range(2):
            cols = slice(c0 + pr * LANES, c0 + (pr + 1) * LANES)
            zc = h_ref[rows, OFF_ZC + c0 + pr * LANES:OFF_ZC + c0 + (pr + 1) * LANES]
            cin_ref[rows, cols] = (o[pr * BLOCK:(pr + 1) * BLOCK] * _silu(zc)).astype(bf16)

    sc_next = scores(0)
    for u in range(n_unit):
        sc = sc_next
        for gc in gate_sched[u]:
            gcols = slice(gc * NCHUNK, (gc + 1) * NCHUNK)
            gate_ref[:, gcols] = _dot(xb, w_in_ref[:, OFF_G + gc * NCHUNK:OFF_G + (gc + 1) * NCHUNK])
        if u + 1 < n_unit:
            sc_next = scores(u + 1)
        attend(u, sc)

    for r in kv_refs:
        r[0:BLOCK, :] = r[TM:TM + BLOCK, :]

    for c in range(D_MODEL // NCHUNK):
        cols = slice(c * NCHUNK, (c + 1) * NCHUNK)
        acc = None
        for br, (in_ref, wp_ref) in enumerate(((ain_ref, wpa_ref), (bin_ref, wpb_ref), (cin_ref, wpc_ref))):
            g0 = br * D_MODEL + c * NCHUNK
            term = (_gate2(gate_ref[:, g0:g0 + NCHUNK], bg_ref[br, layer:layer + 1, cols])
                    * _dot(in_ref[...], wp_ref[:, cols]))
            acc = term if acc is None else acc + term
        mg_ref[:, cols] = acc.astype(bf16)

    for r0, r1 in OUT_CHUNKS:
        rows = slice(r0, r1)
        out = _dot(mg_ref[rows, :], wout_ref[...])
        y_ref[rows, :] = _layer_norm(ALPHA * x_ref[rows, :] + out, row(lng_ref), row(lnb_ref))

    @pl.when(i == last)
    def _():
        pool_out_ref[...] = ext_ref[n - POOL_BUF:n, :]
        k_out_ref[...] = klast_ref[...].T
        v_out_ref[...] = vlast_ref[...].T


def _prompt_layer(layer, x, cos, sin, sinks, w_in, b_gate, poolw, pscale, slng, slnb, sguw, sgub,
                  wpa, wpb, wpc, wout, lng, lnb, slide_args=None):
    B, L, _ = x.shape
    rows = TM * SUB
    nt = L // rows
    grid = (B, nt)
    row_spec = lambda w: pl.BlockSpec((rows, w), lambda b, i: (i, 0))
    lspec = lambda a: _layer_spec(a, layer, single_buffer=True)
    wspec = lambda a: _full_spec(a, single_buffer=True)
    in_specs = [
        pl.BlockSpec(memory_space=pltpu.SMEM),
        pl.BlockSpec((None, rows, D_MODEL), lambda b, i: (b, i, 0)),
        row_spec(LANES), row_spec(LANES),
        wspec(w_in), _full_spec(b_gate), wspec(poolw), _full_spec(pscale), _full_spec(slng), _full_spec(slnb),
        lspec(sguw), lspec(sgub), wspec(wpa), wspec(wpb), wspec(wpc), wspec(wout),
        _full_spec(lng), _full_spec(lnb),
    ]
    out_shape = (
        jax.ShapeDtypeStruct((B, L, D_MODEL), f32),
        jax.ShapeDtypeStruct((B, POOL_BUF, D_POOL), f32),
        jax.ShapeDtypeStruct((B, D_KV, WINDOW), f32),
        jax.ShapeDtypeStruct((B, D_KV, WINDOW), f32),
    )
    out_specs = (
        pl.BlockSpec((None, rows, D_MODEL), lambda b, i: (b, i, 0)),
        pl.BlockSpec((None, POOL_BUF, D_POOL), lambda b, i: (b, 0, 0)),
        pl.BlockSpec((None, D_KV, WINDOW), lambda b, i: (b, 0, 0)),
        pl.BlockSpec((None, D_KV, WINDOW), lambda b, i: (b, 0, 0)),
    )
    args = [sinks, x, cos, sin, w_in, b_gate, poolw, pscale, slng, slnb, sguw, sgub,
            wpa, wpb, wpc, wout, lng, lnb]
    if slide_args is not None:
        new_k, new_v, cache_k, cache_v = slide_args
        depth, nb = cache_k.shape[:2]
        per_step = nb // (B * nt)
        assert per_step * B * nt == nb and depth == DEPTH
        cache_spec = pl.BlockSpec((depth, per_step) + cache_k.shape[2:], lambda b, i: (0, b * nt + i, 0, 0, 0))
        in_specs += [_full_spec(a) for a in (*new_k, *new_v)] + [cache_spec, cache_spec]
        args += [*new_k, *new_v, cache_k, cache_v]
        out_shape += (jax.ShapeDtypeStruct(cache_k.shape, f32), jax.ShapeDtypeStruct(cache_v.shape, f32))
        out_specs += (cache_spec, cache_spec)
    kv_scratch = [pltpu.VMEM((BLOCK + TM, LANES), bf16) for _ in range(8)]
    scratch = [
        pltpu.VMEM((TM, OFF_G), f32),
        pltpu.VMEM((HIST + TM, D_POOL), f32),
        pltpu.VMEM((HIST + TM, D_POOL), f32),
        pltpu.VMEM((HIST + TM, D_POOL), f32),
        pltpu.VMEM((HIST + TM, D_POOL), f32),
        pltpu.VMEM((TM, D_ATTN), bf16),
        *kv_scratch,
        pltpu.VMEM((TM, D_POOL), bf16),
        pltpu.VMEM((TM, D_SGU), bf16),
        pltpu.VMEM((TM, D_ATTN), bf16),
        pltpu.VMEM((TM, D_MODEL), bf16),
        pltpu.VMEM((TM, N_BRANCHES * D_MODEL), f32),
        pltpu.VMEM((WINDOW, D_KV), f32),
        pltpu.VMEM((WINDOW, D_KV), f32),
    ]
    return pl.pallas_call(
        functools.partial(_prompt_kernel, layer, slide_args is not None),
        out_shape=out_shape,
        grid=grid,
        in_specs=in_specs,
        out_specs=out_specs,
        scratch_shapes=scratch,
        compiler_params=pltpu.CompilerParams(
            dimension_semantics=("arbitrary", "arbitrary"),
            vmem_limit_bytes=VMEM_LIMIT),
        name="prompt_layer",
    )(*args)


def _sample_proj_kernel(layer, x_ref, cos_ref, sin_ref, w_in_ref, bg_ref, pb_ref, poolw_ref, pscale_ref,
                        slng_ref, slnb_ref, sw0_ref, sb0_ref,
                        wb_ref, qkv_ref, kt_ref, vt_ref, pool_ref, vn_ref, ain_ref, bin_ref,
                        szc_ref, g_ref, poolbd_ref, h_ref):
    j = pl.program_id(0)
    row = lambda ref: ref[layer:layer + 1, :]
    wb_ref[:, 0:OFF_G] = w_in_ref[:, 0:OFF_G].astype(bf16)
    wb_ref[:, OFF_G:D_IN] = (0.5 * w_in_ref[:, OFF_G:D_IN]).astype(bf16)

    @pl.when(j == 0)
    def _():
        h_ref[...] = jnp.zeros(h_ref.shape, f32)

    h_ref[...] += _dot(x_ref[...].astype(bf16), wb_ref[...])

    def hcols(off, width):
        return h_ref[:, off:off + width]

    @pl.when(j == RK_STEPS - 1)
    def _():
        xa = hcols(OFF_XA, D_POOL)
        lane = lax.broadcasted_iota(jnp.int32, xa.shape, 1)
        first_row = _group_select(lane, *(POOL_BUF - (w - 1) for w in POOL_WINDOWS))
        win = xa
        for r in range(POOL_BUF):
            win = win + jnp.where(first_row <= r, pb_ref[r], 0.0)
        width = _group_select(lane, *POOL_WINDOWS).astype(f32)
        pooled = win / width - xa
        zero_blk = jnp.zeros((POOL_GC, POOL_GC), f32)
        n_grp = len(POOL_WINDOWS)
        pool_bd = jnp.concatenate(
            [jnp.concatenate([poolw_ref[g] if gg == g else zero_blk for gg in range(n_grp)], axis=1)
             for g in range(n_grp)], axis=0).astype(bf16)
        poolbd_ref[...] = pool_bd
        ya = _dot(pooled.astype(bf16), pool_bd) * row(pscale_ref)
        ain_ref[...] = (ya * _silu(hcols(OFF_ZA, D_POOL))).astype(bf16)
        for r in range(POOL_BUF - 1):
            pool_ref[r] = pb_ref[r + 1]
        pool_ref[POOL_BUF - 1] = xa
        vn = _layer_norm(hcols(OFF_V, D_SGU), row(slng_ref), row(slnb_ref))
        vn_ref[...] = vn
        lane_g = lax.broadcasted_iota(jnp.int32, (1, D_SGU), 1)
        sw0 = _group_select(lane_g, *(sw0_ref[g, 0:1, 0:1] for g in range(N_SGU_GROUPS)))
        sb0 = _group_select(lane_g, *(sb0_ref[g:g + 1, 0:1] for g in range(N_SGU_GROUPS)))
        yb = hcols(OFF_U, D_SGU) * (sw0 * vn + sb0)
        bin_ref[...] = (yb * _silu(hcols(OFF_ZB, D_SGU))).astype(bf16)
        cos = cos_ref[...]
        sin = sin_ref[...]
        for c in range(D_ATTN // LANES):
            qkv_ref[:, c * LANES:(c + 1) * LANES] = _rope128(hcols(OFF_Q + c * LANES, LANES), cos, sin) * SCALE
        kr = _rope128(hcols(OFF_K, D_KV), cos, sin)
        vv = hcols(OFF_VV, D_KV)
        qkv_ref[:, D_ATTN:D_ATTN + D_KV] = kr
        qkv_ref[:, D_ATTN + D_KV:D_ATTN + 2 * D_KV] = vv
        kt_ref[...] = kr.T
        vt_ref[...] = vv.T
        szc_ref[...] = _silu(hcols(OFF_ZC, D_ATTN))
        for br in range(N_BRANCHES):
            cols = slice(br * D_MODEL, (br + 1) * D_MODEL)
            g_ref[:, cols] = _gate2(hcols(OFF_G + br * D_MODEL, D_MODEL), bg_ref[br, layer:layer + 1, :])


def _sample_proj(layer, x, cos, sin, w_in, b_gate, pool_t, poolw, pscale, slng, slnb, sw0, sb0):
    nb = x.shape[0]
    out_shape = (
        jax.ShapeDtypeStruct((D_MODEL, D_IN), bf16),
        jax.ShapeDtypeStruct((nb, D_ATTN + 2 * D_KV), f32),
        jax.ShapeDtypeStruct((D_KV, nb), f32),
        jax.ShapeDtypeStruct((D_KV, nb), f32),
        jax.ShapeDtypeStruct((POOL_BUF, nb, D_POOL), f32),
        jax.ShapeDtypeStruct((nb, 1, D_SGU), f32),
        jax.ShapeDtypeStruct((nb, D_POOL), bf16),
        jax.ShapeDtypeStruct((nb, D_SGU), bf16),
        jax.ShapeDtypeStruct((nb, D_ATTN), f32),
        jax.ShapeDtypeStruct((nb, N_BRANCHES * D_MODEL), f32),
        jax.ShapeDtypeStruct((D_POOL, D_POOL), bf16),
    )
    lspec = lambda a: _layer_spec(a, layer)
    args = (x, cos, sin, w_in, b_gate, pool_t, poolw, pscale, slng, slnb, sw0, sb0)
    in_specs = [pl.BlockSpec((nb, None, RK), lambda j: (0, 0, j)), _full_spec(cos), _full_spec(sin),
                pl.BlockSpec((None, RK, D_IN), lambda j: (layer, j, 0)),
                _full_spec(b_gate), lspec(pool_t), lspec(poolw), _full_spec(pscale), _full_spec(slng),
                _full_spec(slnb),
                pl.BlockSpec((None, N_SGU_GROUPS, 8, CHUNK), lambda j: (layer, 0, 0, 0)),
                pl.BlockSpec((None, N_SGU_GROUPS, CHUNK), lambda j: (layer, 0, 0))]
    squeeze_mid = lambda s: pl.BlockSpec((s.shape[0], None, s.shape[2]), lambda j: (0, 0, 0))
    out_specs = (pl.BlockSpec((RK, D_IN), lambda j: (j, 0)),) + tuple(
        squeeze_mid(s) if len(s.shape) == 3 and s.shape[1] == 1 else _full_spec(s) for s in out_shape[1:])
    return pl.pallas_call(
        functools.partial(_sample_proj_kernel, layer),
        out_shape=out_shape,
        grid=(RK_STEPS,),
        in_specs=in_specs,
        out_specs=out_specs,
        scratch_shapes=[pltpu.VMEM((nb, D_IN), f32)],
        compiler_params=pltpu.CompilerParams(dimension_semantics=("arbitrary",),
                                             vmem_limit_bytes=VMEM_LIMIT),
        name="sample_proj",
    )(*args)


def _sample_attn_kernel(layer, sink_ref, qkv_ref, kc_ref, vc_ref, o_ref):
    head = lax.broadcasted_iota(jnp.int32, (1, Q_PER_KV, 1), 1)
    for kv in range(N_KV_HEADS):
        qb = qkv_ref[:, kv * Q_PER_KV:(kv + 1) * Q_PER_KV, :].astype(bf16)
        kn = qkv_ref[:, N_HEADS + kv:N_HEADS + kv + 1, :]
        vn = qkv_ref[:, N_HEADS + N_KV_HEADS + kv:N_HEADS + N_KV_HEADS + kv + 1, :]
        sink = jnp.zeros((1, Q_PER_KV, 1), f32)
        for g in range(Q_PER_KV):
            sink = jnp.where(head == g, sink_ref[layer, kv * Q_PER_KV + g], sink)
        s = jnp.einsum('bgd,bdw->bgw', qb, kc_ref[:, kv].astype(bf16), preferred_element_type=f32)
        s_new = jnp.sum(qb.astype(f32) * kn.astype(bf16).astype(f32), axis=-1, keepdims=True)
        m = jnp.maximum(jnp.maximum(jnp.max(s, axis=-1, keepdims=True), s_new), sink)
        p = jnp.exp(s - m)
        p_new = jnp.exp(s_new - m)
        den = jnp.sum(p, axis=-1, keepdims=True) + p_new + jnp.exp(sink - m)
        o = jnp.einsum('bgw,bdw->bgd', (p / den).astype(bf16), vc_ref[:, kv].astype(bf16),
                       preferred_element_type=f32)
        o_ref[:, kv] = o + (p_new / den) * vn


def _sample_attn(layer, sink, qkv3, kc, vc):
    nb, n_rows, _ = qkv3.shape
    cache_spec = pl.BlockSpec((None, SB, N_KV_HEADS, HEAD_DIM, WINDOW), lambda b: (layer, b, 0, 0, 0))
    return pl.pallas_call(
        functools.partial(_sample_attn_kernel, layer),
        out_shape=jax.ShapeDtypeStruct((nb, N_KV_HEADS, Q_PER_KV, HEAD_DIM), f32),
        grid=(nb // SB,),
        in_specs=[pl.BlockSpec(memory_space=pltpu.SMEM),
                  pl.BlockSpec((SB, n_rows, HEAD_DIM), lambda b: (b, 0, 0)), cache_spec, cache_spec],
        out_specs=pl.BlockSpec((SB, N_KV_HEADS, Q_PER_KV, HEAD_DIM), lambda b: (b, 0, 0, 0)),
        compiler_params=pltpu.CompilerParams(dimension_semantics=("arbitrary",),
                                             vmem_limit_bytes=VMEM_LIMIT),
        name="sample_attn",
    )(sink, qkv3, kc, vc)


def _sample_merge_kernel(layer, x_ref, ain_ref, bin_ref, yc_ref, szc_ref, g_ref, wpa_ref, wpb_ref, wpc_ref,
                         wout_ref, lng_ref, lnb_ref, y_ref, wpa_b_ref, wpb_b_ref, wpc_b_ref, wout_b_ref):
    row = lambda ref: ref[layer:layer + 1, :]
    wpa = (0.5 * wpa_ref[...]).astype(bf16)
    wpb = (0.5 * wpb_ref[...]).astype(bf16)
    wpc = (0.5 * wpc_ref[...]).astype(bf16)
    wout = wout_ref[...].astype(bf16)
    wpa_b_ref[...] = wpa
    wpb_b_ref[...] = wpb
    wpc_b_ref[...] = wpc
    wout_b_ref[...] = wout
    cin = (yc_ref[...] * szc_ref[...]).astype(bf16)
    merged = (g_ref[:, 0:D_MODEL] * _dot(ain_ref[...], wpa)
              + g_ref[:, D_MODEL:2 * D_MODEL] * _dot(bin_ref[...], wpb)
              + g_ref[:, 2 * D_MODEL:3 * D_MODEL] * _dot(cin, wpc))
    out = _dot(merged.astype(bf16), wout)
    y_ref[...] = _layer_norm(ALPHA * x_ref[...] + out, row(lng_ref), row(lnb_ref))


def _sample_merge(layer, x, ain, bin_, yc, szc, g, wpa, wpb, wpc, wout, lng, lnb):
    lspec = lambda a: _layer_spec(a, layer)
    nb = x.shape[0]
    x_spec = pl.BlockSpec((nb, None, D_MODEL), lambda i: (0, 0, 0))
    out_shape = (jax.ShapeDtypeStruct(x.shape, f32),) + tuple(
        jax.ShapeDtypeStruct(w.shape[1:], bf16) for w in (wpa, wpb, wpc, wout))
    return pl.pallas_call(
        functools.partial(_sample_merge_kernel, layer),
        out_shape=out_shape,
        grid=(1,),
        in_specs=[x_spec, _full_spec(ain), _full_spec(bin_), _full_spec(yc), _full_spec(szc),
                  _full_spec(g), lspec(wpa), lspec(wpb), lspec(wpc), lspec(wout), _full_spec(lng),
                  _full_spec(lnb)],
        out_specs=(x_spec,) + tuple(_full_spec(s) for s in out_shape[1:]),
        compiler_params=pltpu.CompilerParams(dimension_semantics=("arbitrary",),
                                             vmem_limit_bytes=VMEM_LIMIT),
        name="sample_merge",
    )(x, ain, bin_, yc, szc, g, wpa, wpb, wpc, wout, lng, lnb)


def _rope_tables(positions):
    halfd = HEAD_DIM // 2
    inv = ROPE_THETA ** (-np.arange(halfd, dtype=np.float64) / halfd)
    ang = np.asarray(positions, dtype=np.float64)[:, None] * inv[None, :]
    cos = np.tile(np.cos(ang), (1, LANES // halfd))
    sin = np.tile(np.concatenate([-np.sin(ang), np.sin(ang)], axis=1), (1, LANES // HEAD_DIM))
    return jnp.asarray(cos, f32), jnp.asarray(sin, f32)


def kernel(x_prompt, x_sample, state_pool, cache_k_win, cache_v_win, w_in, b_gate, pool_w, pool_scale, sgu_ln_g, sgu_ln_b, sgu_w, sgu_b, attn_sinks, w_proj_a, w_proj_b, w_proj_c, w_out, ln_g, ln_b):
    B, L, _ = x_prompt.shape
    nb = x_sample.shape[0]
    cos_p, sin_p = _rope_tables(np.arange(L))
    cos_s, sin_s = _rope_tables(np.array([PAST_LEN]))

    sguw = sgu_w.reshape(DEPTH, N_SGU_GROUPS * CHUNK, CHUNK)
    sgub = jnp.repeat(jnp.swapaxes(sgu_b, 1, 2), POOL_GC, axis=2)
    bg_t = jnp.transpose(b_gate, (1, 0, 2))
    kc_t = jnp.transpose(cache_k_win, (0, 1, 3, 4, 2))
    vc_t = jnp.transpose(cache_v_win, (0, 1, 3, 4, 2))
    pool_t = jnp.transpose(state_pool, (0, 2, 1, 3))

    y_s = x_sample
    pool_s, chunk_v, kts, vts, weights = ([] for _ in range(5))
    for l in range(DEPTH):
        w_in_b, qkv, kt, vt, ps, cv, ain, bin_, szc, g, pool_bd = _sample_proj(
            l, y_s, cos_s, sin_s, w_in, bg_t, pool_t, pool_w, pool_scale, sgu_ln_g, sgu_ln_b, sgu_w, sgu_b)
        o = _sample_attn(l, attn_sinks, qkv.reshape(nb, N_HEADS + 2 * N_KV_HEADS, HEAD_DIM), kc_t, vc_t)
        y_s, wpa, wpb, wpc, wout = _sample_merge(
            l, y_s, ain, bin_, o.reshape(nb, D_ATTN), szc, g, w_proj_a, w_proj_b, w_proj_c, w_out, ln_g, ln_b)
        pool_s.append(ps); chunk_v.append(cv); kts.append(kt); vts.append(vt)
        weights.append((w_in_b, wpa, wpb, wpc, wout, pool_bd))

    y_p = x_prompt
    pool_p, k_p, v_p = ([] for _ in range(3))
    for l in range(DEPTH):
        w_in_b, wpa, wpb, wpc, wout, poolw = weights[l]
        slide_args = (kts, vts, kc_t, vc_t) if l == DEPTH - 1 else None
        y_p, pp, kp, vp, *slid = _prompt_layer(
            l, y_p, cos_p, sin_p, attn_sinks, w_in_b, bg_t, poolw, pool_scale, sgu_ln_g, sgu_ln_b, sguw,
            sgub, wpa, wpb, wpc, wout, ln_g, ln_b, slide_args=slide_args)
        pool_p.append(pp); k_p.append(kp); v_p.append(vp)
    k_s, v_s = slid

    to_cache = lambda a: jnp.transpose(a, (0, 1, 4, 2, 3))
    prompt_cache = lambda lst: to_cache(jnp.stack(lst).reshape(DEPTH, B, N_KV_HEADS, HEAD_DIM, WINDOW))
    return (y_p, y_s,
            jnp.stack(pool_p), prompt_cache(k_p), prompt_cache(v_p),
            jnp.transpose(jnp.stack(pool_s), (0, 2, 1, 3)), to_cache(k_s), to_cache(v_s),
            jnp.stack(chunk_v))
```

```python
import functools

import numpy as np
import jax
import jax.numpy as jnp
from jax import lax
from jax.experimental import pallas as pl
from jax.experimental.pallas import tpu as pltpu

D_MODEL = 1024
DEPTH = 2
PAST_LEN = 8192
D_POOL = 256
POOL_WINDOWS = (2, 4, 8, 16)
POOL_GC = 64
POOL_BUF = 15
D_SGU = 256
CHUNK = 128
N_SGU_GROUPS = 4
HEAD_DIM = 64
N_HEADS = 8
N_KV_HEADS = 2
Q_PER_KV = 4
D_ATTN = 512
D_KV = 128
WINDOW = 128
BLOCK = 128
ROPE_THETA = 10000.0
N_BRANCHES = 3
D_IN = 2 * D_POOL + 3 * D_SGU + 2 * D_ATTN + 2 * D_KV + N_BRANCHES * D_MODEL
ALPHA = (2.0 * DEPTH) ** 0.25
LN_EPS = 1e-5
NEG_INF = -1e30
SCALE = HEAD_DIM ** -0.5

OFF_XA, OFF_ZA, OFF_U, OFF_V, OFF_ZB = 0, 256, 512, 768, 1024
OFF_Q, OFF_K, OFF_VV, OFF_ZC, OFF_G = 1280, 1792, 1920, 2048, 2560

LANES = 128
SUBLANES = 8
TM = 512
SUB = 2
NBLK = TM // BLOCK
HIST = 32
NCHUNK = 256
OUT_CHUNKS = ((0, 256), (256, 512))
VMEM_LIMIT = 56 * 1024 * 1024
SB = 32
RK = 512
RK_STEPS = D_MODEL // RK

bf16 = jnp.bfloat16
f32 = jnp.float32


def _dot(a, b):
    return jnp.dot(a, b, preferred_element_type=f32)


def _dot_nt(a, b):
    return lax.dot_general(a, b, (((1,), (1,)), ((), ())), preferred_element_type=f32)


def _sigmoid(z):
    return 0.5 * jnp.tanh(0.5 * z) + 0.5


def _silu(z):
    return z * _sigmoid(z)


def _gate2(half_pre, bias):
    return jnp.tanh(half_pre + 0.5 * bias) + 1.0


def _layer_norm(x, g, b):
    mu = jnp.mean(x, axis=-1, keepdims=True)
    xc = x - mu
    var = jnp.mean(xc * xc, axis=-1, keepdims=True)
    return xc * lax.rsqrt(var + LN_EPS) * g + b


def _rope128(x, cos, sin_signed):
    lane = lax.broadcasted_iota(jnp.int32, x.shape, 1)
    first_half = (lane % HEAD_DIM) < (HEAD_DIM // 2)
    partner = jnp.where(first_half,
                        pltpu.roll(x, LANES - HEAD_DIM // 2, 1),
                        pltpu.roll(x, HEAD_DIM // 2, 1))
    return x * cos + partner * sin_signed


def _group_select(lane, a0, a1, a2, a3):
    return jnp.where(lane < 64, a0, jnp.where(lane < 128, a1, jnp.where(lane < 192, a2, a3)))


def _layer_spec(arr, layer, single_buffer=False):
    block = (None,) + arr.shape[1:]
    zeros = (0,) * (arr.ndim - 1)
    index_map = lambda *_: (layer,) + zeros
    if single_buffer:
        return pl.BlockSpec(block, index_map, pipeline_mode=pl.Buffered(1))
    return pl.BlockSpec(block, index_map)


def _full_spec(arr, single_buffer=False):
    zeros = (0,) * len(arr.shape)
    if single_buffer:
        return pl.BlockSpec(arr.shape, lambda *_: zeros, pipeline_mode=pl.Buffered(1))
    return pl.BlockSpec(arr.shape, lambda *_: zeros)


def _slide_windows(step, per_step, new_refs, old_refs, out_refs):
    lane = lax.broadcasted_iota(jnp.int32, (D_KV, WINDOW), 1)
    shift = jnp.where(step == 0, 0, LANES - step * per_step)
    for new_ref, old_ref, out_ref in zip(new_refs, old_refs, out_refs):
        for l in range(DEPTH):
            new_cols = pltpu.roll(new_ref[l][...], shift, 1)
            for s in range(per_step):
                slid = pltpu.roll(old_ref[l, s].reshape(D_KV, WINDOW), WINDOW - 1, 1)
                newest = jnp.broadcast_to(new_cols[:, s:s + 1], (D_KV, WINDOW))
                out_ref[l, s] = jnp.where(lane == WINDOW - 1, newest, slid).reshape(
                    N_KV_HEADS, HEAD_DIM, WINDOW)


def _prompt_kernel(layer, slide, *refs):
    def sub_tile(s, carry):
        _prompt_tile(layer, slide, s, refs)
        return carry
    lax.fori_loop(0, SUB, sub_tile, 0)


def _prompt_tile(layer, slide, s, refs):
    n_in = 18 + (2 * DEPTH + 2 if slide else 0)
    n_out = 4 + (2 if slide else 0)
    (sinks_ref, x_ref, cos_ref, sin_ref, w_in_ref, bg_ref, poolw_ref, pscale_ref,
     slng_ref, slnb_ref, sguw_ref, sgub_ref, wpa_ref, wpb_ref, wpc_ref, wout_ref,
     lng_ref, lnb_ref) = refs[:18]
    y_ref, pool_out_ref, k_out_ref, v_out_ref = refs[n_in:n_in + 4]
    (h_ref, ext_ref, s2_ref, s4_ref, s8_ref, qs_ref,
     ka_ref, kb_ref, kc_ref, kd_ref, va_ref, vb_ref, vc_ref, vd_ref,
     ain_ref, bin_ref, cin_ref, mg_ref, gate_ref, klast_ref, vlast_ref) = refs[n_in + n_out:]
    tile_rows = pl.ds(pl.multiple_of(s * TM, TM), TM)
    x_ref, y_ref, cos_ref, sin_ref = (r.at[tile_rows] for r in (x_ref, y_ref, cos_ref, sin_ref))
    i = pl.program_id(1) * SUB + s
    last = pl.num_programs(1) * SUB - 1
    kv_refs = (ka_ref, kb_ref, kc_ref, kd_ref, va_ref, vb_ref, vc_ref, vd_ref)
    row = lambda ref: ref[layer:layer + 1, :]

    @pl.when(i == 0)
    def _():
        ext_ref[0:HIST, :] = jnp.zeros((HIST, D_POOL), f32)
        for r in kv_refs:
            r[0:BLOCK, :] = jnp.zeros((BLOCK, LANES), bf16)

    xb = x_ref[...].astype(bf16)
    half = OFF_G // 2
    h_ref[:, 0:half] = _dot(xb, w_in_ref[:, 0:half])
    h_ref[:, half:OFF_G] = _dot(xb, w_in_ref[:, half:OFF_G])

    if slide:
        step = pl.program_id(0) * (last + 1) + i
        new_refs = (refs[18:18 + DEPTH], refs[18 + DEPTH:18 + 2 * DEPTH])
        old_refs = refs[18 + 2 * DEPTH:n_in]
        per_step = old_refs[0].shape[1] // SUB
        mine = lambda r: r.at[:, pl.ds(s * per_step, per_step)]
        _slide_windows(step, per_step, new_refs, [mine(r) for r in old_refs],
                       [mine(r) for r in refs[n_in + 4:n_in + 6]])

    xa = h_ref[:, OFF_XA:OFF_XA + D_POOL]
    ext_ref[HIST:HIST + TM, :] = xa
    n = HIST + TM
    s2_ref[8:n, :] = ext_ref[8:n, :] + ext_ref[7:n - 1, :]
    s4_ref[16:n, :] = s2_ref[16:n, :] + s2_ref[14:n - 2, :]
    s8_ref[24:n, :] = s4_ref[24:n, :] + s4_ref[20:n - 4, :]
    w16 = s8_ref[HIST:n, :] + s8_ref[HIST - 8:n - 8, :]
    lane_p = lax.broadcasted_iota(jnp.int32, (TM, D_POOL), 1)
    row_p = lax.broadcasted_iota(jnp.int32, (TM, D_POOL), 0)
    win = _group_select(lane_p, s2_ref[HIST:n, :], s4_ref[HIST:n, :], s8_ref[HIST:n, :], w16)
    width = _group_select(lane_p, POOL_WINDOWS[0], POOL_WINDOWS[1], POOL_WINDOWS[2], POOL_WINDOWS[3])
    cnt = jnp.minimum(row_p + (i * TM + 1), width).astype(f32)
    pooled = win / cnt - xa
    ya = _dot(pooled.astype(bf16), poolw_ref[...]) * row(pscale_ref)
    za = h_ref[:, OFF_ZA:OFF_ZA + D_POOL]
    ain_ref[...] = (ya * _silu(za)).astype(bf16)
    ext_ref[HIST - 16:HIST, :] = ext_ref[n - 16:n, :]

    vn = _layer_norm(h_ref[:, OFF_V:OFF_V + D_SGU], row(slng_ref), row(slnb_ref)).astype(bf16)
    wr = lax.broadcasted_iota(jnp.int32, (N_SGU_GROUPS * CHUNK, CHUNK), 0) % CHUNK
    wc = lax.broadcasted_iota(jnp.int32, (N_SGU_GROUPS * CHUNK, CHUNK), 1)
    w_s = jnp.where(wc <= wr, sguw_ref[...], 0.0).astype(bf16)
    lane_c = lax.broadcasted_iota(jnp.int32, (CHUNK, D_SGU), 1)
    for j in range(NBLK):
        rows = slice(j * CHUNK, (j + 1) * CHUNK)
        r = _dot(w_s, vn[rows, :])
        mixed = _group_select(lane_c, r[0:CHUNK], r[CHUNK:2 * CHUNK], r[2 * CHUNK:3 * CHUNK],
                              r[3 * CHUNK:4 * CHUNK]) + sgub_ref[...]
        yb = h_ref[rows, OFF_U:OFF_U + D_SGU] * mixed
        bin_ref[rows, :] = (yb * _silu(h_ref[rows, OFF_ZB:OFF_ZB + D_SGU])).astype(bf16)

    cos = cos_ref[...]
    sin = sin_ref[...]
    for c in range(D_ATTN // LANES):
        qc = _rope128(h_ref[:, OFF_Q + c * LANES:OFF_Q + (c + 1) * LANES], cos, sin)
        qs_ref[:, c * LANES:(c + 1) * LANES] = (qc * SCALE).astype(bf16)
    kr = _rope128(h_ref[:, OFF_K:OFF_K + D_KV], cos, sin)
    vv = h_ref[:, OFF_VV:OFF_VV + D_KV]
    klast_ref[...] = kr[TM - WINDOW:TM, :]
    vlast_ref[...] = vv[TM - WINDOW:TM, :]
    lane_k = lax.broadcasted_iota(jnp.int32, (TM, LANES), 1)
    lo = lane_k < HEAD_DIM
    for src, (a_ref, b_ref, c_ref, d_ref) in ((kr, kv_refs[0:4]), (vv, kv_refs[4:8])):
        sw = pltpu.roll(src, HEAD_DIM, 1)
        a_ref[BLOCK:BLOCK + TM, :] = jnp.where(lo, src, 0.0).astype(bf16)
        b_ref[BLOCK:BLOCK + TM, :] = jnp.where(lo, 0.0, src).astype(bf16)
        c_ref[BLOCK:BLOCK + TM, :] = jnp.where(lo, sw, 0.0).astype(bf16)
        d_ref[BLOCK:BLOCK + TM, :] = jnp.where(lo, 0.0, sw).astype(bf16)

    qrow = lax.broadcasted_iota(jnp.int32, (2 * BLOCK, 2 * BLOCK), 0) % BLOCK
    kcol = lax.broadcasted_iota(jnp.int32, (2 * BLOCK, 2 * BLOCK), 1)
    band = (kcol >= qrow) & (kcol <= qrow + WINDOW)
    band_first = band & (kcol >= jnp.where(i > 0, 0, BLOCK))
    top = lax.broadcasted_iota(jnp.int32, (2 * BLOCK, 1), 0) < BLOCK
    n_gate = N_BRANCHES * D_MODEL // NCHUNK
    n_unit = NBLK * N_KV_HEADS
    gate_sched = [range(u * n_gate // n_unit, (u + 1) * n_gate // n_unit) for u in range(n_unit)]

    def scores(u):
        j, kv = divmod(u, N_KV_HEADS)
        rows = slice(j * BLOCK, (j + 1) * BLOCK)
        keys = slice(j * BLOCK, j * BLOCK + 2 * BLOCK)
        c0 = kv * Q_PER_KV * HEAD_DIM
        qst = jnp.concatenate([qs_ref[rows, c0:c0 + LANES], qs_ref[rows, c0 + LANES:c0 + 2 * LANES]], axis=0)
        k_even, k_odd = (ka_ref, kd_ref) if kv == 0 else (kc_ref, kb_ref)
        kcat = jnp.concatenate([k_even[keys, :], k_odd[keys, :]], axis=0)
        return _dot_nt(qst, kcat)

    def attend(u, sc):
        j, kv = divmod(u, N_KV_HEADS)
        rows = slice(j * BLOCK, (j + 1) * BLOCK)
        keys = slice(j * BLOCK, j * BLOCK + 2 * BLOCK)
        allowed = band_first if j == 0 else band
        c0 = kv * Q_PER_KV * HEAD_DIM
        h0 = kv * Q_PER_KV
        v_even, v_odd = (va_ref, vd_ref) if kv == 0 else (vc_ref, vb_ref)
        probs = []
        for par in range(2):
            sink = jnp.where(top, sinks_ref[layer, h0 + par], sinks_ref[layer, h0 + 2 + par])
            sm = jnp.where(allowed, sc[:, par * 2 * BLOCK:(par + 1) * 2 * BLOCK], NEG_INF)
            m = jnp.maximum(jnp.max(sm, axis=-1, keepdims=True), sink)
            p = jnp.exp(sm - m)
            den = jnp.sum(p, axis=-1, keepdims=True) + jnp.exp(sink - m)
            probs.append((p / den).astype(bf16))
        pcat = jnp.concatenate(probs, axis=1)
        vcat = jnp.concatenate([v_even[keys, :], v_odd[keys, :]], axis=0)
        o = _dot(pcat, vcat)
        for pr in range(2):
            cols = slice(c0 + pr * LANES, c0 + (pr + 1) * LANES)
            zc = h_ref[rows, OFF_ZC + c0 + pr * LANES:OFF_ZC + c0 + (pr + 1) * LANES]
            cin_ref[rows, cols] = (o[pr * BLOCK:(pr + 1) * BLOCK] * _silu(zc)).astype(bf16)

    sc_next = scores(0)
    for u in range(n_unit):
        sc = sc_next
        for gc in gate_sched[u]:
            gcols = slice(gc * NCHUNK, (gc + 1) * NCHUNK)
            gate_ref[:, gcols] = _dot(xb, w_in_ref[:, OFF_G + gc * NCHUNK:OFF_G + (gc + 1) * NCHUNK])
        if u + 1 < n_unit:
            sc_next = scores(u + 1)
        attend(u, sc)

    for r in kv_refs:
        r[0:BLOCK, :] = r[TM:TM + BLOCK, :]

    for c in range(D_MODEL // NCHUNK):
        cols = slice(c * NCHUNK, (c + 1) * NCHUNK)
        acc = None
        for br, (in_ref, wp_ref) in enumerate(((ain_ref, wpa_ref), (bin_ref, wpb_ref), (cin_ref, wpc_ref))):
            g0 = br * D_MODEL + c * NCHUNK
            term = (_gate2(gate_ref[:, g0:g0 + NCHUNK], bg_ref[br, layer:layer + 1, cols])
                    * _dot(in_ref[...], wp_ref[:, cols]))
            acc = term if acc is None else acc + term
        mg_ref[:, cols] = acc.astype(bf16)

    for r0, r1 in OUT_CHUNKS:
        rows = slice(r0, r1)
        out = _dot(mg_ref[rows, :], wout_ref[...])
        y_ref[rows, :] = _layer_norm(ALPHA * x_ref[rows, :] + out, row(lng_ref), row(lnb_ref))

    @pl.when(i == last)
    def _():
        pool_out_ref[...] = ext_ref[n - POOL_BUF:n, :]
        k_out_ref[...] = klast_ref[...].T
        v_out_ref[...] = vlast_ref[...].T


def _prompt_layer(layer, x, cos, sin, sinks, w_in, b_gate, poolw, pscale, slng, slnb, sguw, sgub,
                  wpa, wpb, wpc, wout, lng, lnb, slide_args=None):
    B, L, _ = x.shape
    rows = TM * SUB
    nt = L // rows
    grid = (B, nt)
    row_spec = lambda w: pl.BlockSpec((rows, w), lambda b, i: (i, 0))
    lspec = lambda a: _layer_spec(a, layer, single_buffer=True)
    wspec = lambda a: _full_spec(a, single_buffer=True)
    in_specs = [
        pl.BlockSpec(memory_space=pltpu.SMEM),
        pl.BlockSpec((None, rows, D_MODEL), lambda b, i: (b, i, 0)),
        row_spec(LANES), row_spec(LANES),
        wspec(w_in), _full_spec(b_gate), wspec(poolw), _full_spec(pscale), _full_spec(slng), _full_spec(slnb),
        lspec(sguw), lspec(sgub), wspec(wpa), wspec(wpb), wspec(wpc), wspec(wout),
        _full_spec(lng), _full_spec(lnb),
    ]
    out_shape = (
        jax.ShapeDtypeStruct((B, L, D_MODEL), f32),
        jax.ShapeDtypeStruct((B, POOL_BUF, D_POOL), f32),
        jax.ShapeDtypeStruct((B, D_KV, WINDOW), f32),
        jax.ShapeDtypeStruct((B, D_KV, WINDOW), f32),
    )
    out_specs = (
        pl.BlockSpec((None, rows, D_MODEL), lambda b, i: (b, i, 0)),
        pl.BlockSpec((None, POOL_BUF, D_POOL), lambda b, i: (b, 0, 0)),
        pl.BlockSpec((None, D_KV, WINDOW), lambda b, i: (b, 0, 0)),
        pl.BlockSpec((None, D_KV, WINDOW), lambda b, i: (b, 0, 0)),
    )
    args = [sinks, x, cos, sin, w_in, b_gate, poolw, pscale, slng, slnb, sguw, sgub,
            wpa, wpb, wpc, wout, lng, lnb]
    if slide_args is not None:
        new_k, new_v, cache_k, cache_v = slide_args
        depth, nb = cache_k.shape[:2]
        per_step = nb // (B * nt)
        assert per_step * B * nt == nb and depth == DEPTH
        cache_spec = pl.BlockSpec((depth, per_step) + cache_k.shape[2:], lambda b, i: (0, b * nt + i, 0, 0, 0))
        in_specs += [_full_spec(a) for a in (*new_k, *new_v)] + [cache_spec, cache_spec]
        args += [*new_k, *new_v, cache_k, cache_v]
        out_shape += (jax.ShapeDtypeStruct(cache_k.shape, f32), jax.ShapeDtypeStruct(cache_v.shape, f32))
        out_specs += (cache_spec, cache_spec)
    kv_scratch = [pltpu.VMEM((BLOCK + TM, LANES), bf16) for _ in range(8)]
    scratch = [
        pltpu.VMEM((TM, OFF_G), f32),
        pltpu.VMEM((HIST + TM, D_POOL), f32),
        pltpu.VMEM((HIST + TM, D_POOL), f32),
        pltpu.VMEM((HIST + TM, D_POOL), f32),
        pltpu.VMEM((HIST + TM, D_POOL), f32),
        pltpu.VMEM((TM, D_ATTN), bf16),
        *kv_scratch,
        pltpu.VMEM((TM, D_POOL), bf16),
        pltpu.VMEM((TM, D_SGU), bf16),
        pltpu.VMEM((TM, D_ATTN), bf16),
        pltpu.VMEM((TM, D_MODEL), bf16),
        pltpu.VMEM((TM, N_BRANCHES * D_MODEL), f32),
        pltpu.VMEM((WINDOW, D_KV), f32),
        pltpu.VMEM((WINDOW, D_KV), f32),
    ]
    return pl.pallas_call(
        functools.partial(_prompt_kernel, layer, slide_args is not None),
        out_shape=out_shape,
        grid=grid,
        in_specs=in_specs,
        out_specs=out_specs,
        scratch_shapes=scratch,
        compiler_params=pltpu.CompilerParams(
            dimension_semantics=("arbitrary", "arbitrary"),
            vmem_limit_bytes=VMEM_LIMIT),
        name="prompt_layer",
    )(*args)


def _sample_proj_kernel(layer, x_ref, cos_ref, sin_ref, w_in_ref, bg_ref, pb_ref, poolw_ref, pscale_ref,
                        slng_ref, slnb_ref, sw0_ref, sb0_ref,
                        wb_ref, qkv_ref, kt_ref, vt_ref, pool_ref, vn_ref, ain_ref, bin_ref,
                        szc_ref, g_ref, poolbd_ref, h_ref):
    j = pl.program_id(0)
    row = lambda ref: ref[layer:layer + 1, :]
    wb_ref[:, 0:OFF_G] = w_in_ref[:, 0:OFF_G].astype(bf16)
    wb_ref[:, OFF_G:D_IN] = (0.5 * w_in_ref[:, OFF_G:D_IN]).astype(bf16)

    @pl.when(j == 0)
    def _():
        h_ref[...] = jnp.zeros(h_ref.shape, f32)

    h_ref[...] += _dot(x_ref[...].astype(bf16), wb_ref[...])

    def hcols(off, width):
        return h_ref[:, off:off + width]

    @pl.when(j == RK_STEPS - 1)
    def _():
        xa = hcols(OFF_XA, D_POOL)
        lane = lax.broadcasted_iota(jnp.int32, xa.shape, 1)
        first_row = _group_select(lane, *(POOL_BUF - (w - 1) for w in POOL_WINDOWS))
        win = xa
        for r in range(POOL_BUF):
            win = win + jnp.where(first_row <= r, pb_ref[r], 0.0)
        width = _group_select(lane, *POOL_WINDOWS).astype(f32)
        pooled = win / width - xa
        zero_blk = jnp.zeros((POOL_GC, POOL_GC), f32)
        n_grp = len(POOL_WINDOWS)
        pool_bd = jnp.concatenate(
            [jnp.concatenate([poolw_ref[g] if gg == g else zero_blk for gg in range(n_grp)], axis=1)
             for g in range(n_grp)], axis=0).astype(bf16)
        poolbd_ref[...] = pool_bd
        ya = _dot(pooled.astype(bf16), pool_bd) * row(pscale_ref)
        ain_ref[...] = (ya * _silu(hcols(OFF_ZA, D_POOL))).astype(bf16)
        for r in range(POOL_BUF - 1):
            pool_ref[r] = pb_ref[r + 1]
        pool_ref[POOL_BUF - 1] = xa
        vn = _layer_norm(hcols(OFF_V, D_SGU), row(slng_ref), row(slnb_ref))
        vn_ref[...] = vn
        lane_g = lax.broadcasted_iota(jnp.int32, (1, D_SGU), 1)
        sw0 = _group_select(lane_g, *(sw0_ref[g, 0:1, 0:1] for g in range(N_SGU_GROUPS)))
        sb0 = _group_select(lane_g, *(sb0_ref[g:g + 1, 0:1] for g in range(N_SGU_GROUPS)))
        yb = hcols(OFF_U, D_SGU) * (sw0 * vn + sb0)
        bin_ref[...] = (yb * _silu(hcols(OFF_ZB, D_SGU))).astype(bf16)
        cos = cos_ref[...]
        sin = sin_ref[...]
        for c in range(D_ATTN // LANES):
            qkv_ref[:, c * LANES:(c + 1) * LANES] = _rope128(hcols(OFF_Q + c * LANES, LANES), cos, sin) * SCALE
        kr = _rope128(hcols(OFF_K, D_KV), cos, sin)
        vv = hcols(OFF_VV, D_KV)
        qkv_ref[:, D_ATTN:D_ATTN + D_KV] = kr
        qkv_ref[:, D_ATTN + D_KV:D_ATTN + 2 * D_KV] = vv
        kt_ref[...] = kr.T
        vt_ref[...] = vv.T
        szc_ref[...] = _silu(hcols(OFF_ZC, D_ATTN))
        for br in range(N_BRANCHES):
            cols = slice(br * D_MODEL, (br + 1) * D_MODEL)
            g_ref[:, cols] = _gate2(hcols(OFF_G + br * D_MODEL, D_MODEL), bg_ref[br, layer:layer + 1, :])


def _sample_proj(layer, x, cos, sin, w_in, b_gate, pool_t, poolw, pscale, slng, slnb, sw0, sb0):
    nb = x.shape[0]
    out_shape = (
        jax.ShapeDtypeStruct((D_MODEL, D_IN), bf16),
        jax.ShapeDtypeStruct((nb, D_ATTN + 2 * D_KV), f32),
        jax.ShapeDtypeStruct((D_KV, nb), f32),
        jax.ShapeDtypeStruct((D_KV, nb), f32),
        jax.ShapeDtypeStruct((POOL_BUF, nb, D_POOL), f32),
        jax.ShapeDtypeStruct((nb, 1, D_SGU), f32),
        jax.ShapeDtypeStruct((nb, D_POOL), bf16),
        jax.ShapeDtypeStruct((nb, D_SGU), bf16),
        jax.ShapeDtypeStruct((nb, D_ATTN), f32),
        jax.ShapeDtypeStruct((nb, N_BRANCHES * D_MODEL), f32),
        jax.ShapeDtypeStruct((D_POOL, D_POOL), bf16),
    )
    lspec = lambda a: _layer_spec(a, layer)
    args = (x, cos, sin, w_in, b_gate, pool_t, poolw, pscale, slng, slnb, sw0, sb0)
    in_specs = [pl.BlockSpec((nb, None, RK), lambda j: (0, 0, j)), _full_spec(cos), _full_spec(sin),
                pl.BlockSpec((None, RK, D_IN), lambda j: (layer, j, 0)),
                _full_spec(b_gate), lspec(pool_t), lspec(poolw), _full_spec(pscale), _full_spec(slng),
                _full_spec(slnb),
                pl.BlockSpec((None, N_SGU_GROUPS, SUBLANES, CHUNK), lambda j: (layer, 0, 0, 0)),
                pl.BlockSpec((None, N_SGU_GROUPS, CHUNK), lambda j: (layer, 0, 0))]
    squeeze_mid = lambda s: pl.BlockSpec((s.shape[0], None, s.shape[2]), lambda j: (0, 0, 0))
    out_specs = (pl.BlockSpec((RK, D_IN), lambda j: (j, 0)),) + tuple(
        squeeze_mid(s) if len(s.shape) == 3 and s.shape[1] == 1 else _full_spec(s) for s in out_shape[1:])
    return pl.pallas_call(
        functools.partial(_sample_proj_kernel, layer),
        out_shape=out_shape,
        grid=(RK_STEPS,),
        in_specs=in_specs,
        out_specs=out_specs,
        scratch_shapes=[pltpu.VMEM((nb, D_IN), f32)],
        compiler_params=pltpu.CompilerParams(dimension_semantics=("arbitrary",),
                                             vmem_limit_bytes=VMEM_LIMIT),
        name="sample_proj",
    )(*args)


def _sample_attn_kernel(layer, sink_ref, qkv_ref, kc_ref, vc_ref, o_ref):
    head = lax.broadcasted_iota(jnp.int32, (1, Q_PER_KV, 1), 1)
    for kv in range(N_KV_HEADS):
        qb = qkv_ref[:, kv * Q_PER_KV:(kv + 1) * Q_PER_KV, :].astype(bf16)
        kn = qkv_ref[:, N_HEADS + kv:N_HEADS + kv + 1, :]
        vn = qkv_ref[:, N_HEADS + N_KV_HEADS + kv:N_HEADS + N_KV_HEADS + kv + 1, :]
        sink = jnp.zeros((1, Q_PER_KV, 1), f32)
        for g in range(Q_PER_KV):
            sink = jnp.where(head == g, sink_ref[layer, kv * Q_PER_KV + g], sink)
        s = jnp.einsum('bgd,bdw->bgw', qb, kc_ref[:, kv].astype(bf16), preferred_element_type=f32)
        s_new = jnp.sum(qb.astype(f32) * kn.astype(bf16).astype(f32), axis=-1, keepdims=True)
        m = jnp.maximum(jnp.maximum(jnp.max(s, axis=-1, keepdims=True), s_new), sink)
        p = jnp.exp(s - m)
        p_new = jnp.exp(s_new - m)
        den = jnp.sum(p, axis=-1, keepdims=True) + p_new + jnp.exp(sink - m)
        o = jnp.einsum('bgw,bdw->bgd', (p / den).astype(bf16), vc_ref[:, kv].astype(bf16),
                       preferred_element_type=f32)
        o_ref[:, kv] = o + (p_new / den) * vn


def _sample_attn(layer, sink, qkv3, kc, vc):
    nb, n_rows, _ = qkv3.shape
    cache_spec = pl.BlockSpec((None, SB, N_KV_HEADS, HEAD_DIM, WINDOW), lambda b: (layer, b, 0, 0, 0))
    return pl.pallas_call(
        functools.partial(_sample_attn_kernel, layer),
        out_shape=jax.ShapeDtypeStruct((nb, N_KV_HEADS, Q_PER_KV, HEAD_DIM), f32),
        grid=(nb // SB,),
        in_specs=[pl.BlockSpec(memory_space=pltpu.SMEM),
                  pl.BlockSpec((SB, n_rows, HEAD_DIM), lambda b: (b, 0, 0)), cache_spec, cache_spec],
        out_specs=pl.BlockSpec((SB, N_KV_HEADS, Q_PER_KV, HEAD_DIM), lambda b: (b, 0, 0, 0)),
        compiler_params=pltpu.CompilerParams(dimension_semantics=("arbitrary",),
                                             vmem_limit_bytes=VMEM_LIMIT),
        name="sample_attn",
    )(sink, qkv3, kc, vc)


def _sample_merge_kernel(layer, x_ref, ain_ref, bin_ref, yc_ref, szc_ref, g_ref, wpa_ref, wpb_ref, wpc_ref,
                         wout_ref, lng_ref, lnb_ref, y_ref, wpa_b_ref, wpb_b_ref, wpc_b_ref, wout_b_ref):
    row = lambda ref: ref[layer:layer + 1, :]
    wpa = (0.5 * wpa_ref[...]).astype(bf16)
    wpb = (0.5 * wpb_ref[...]).astype(bf16)
    wpc = (0.5 * wpc_ref[...]).astype(bf16)
    wout = wout_ref[...].astype(bf16)
    wpa_b_ref[...] = wpa
    wpb_b_ref[...] = wpb
    wpc_b_ref[...] = wpc
    wout_b_ref[...] = wout
    cin = (yc_ref[...] * szc_ref[...]).astype(bf16)
    merged = (g_ref[:, 0:D_MODEL] * _dot(ain_ref[...], wpa)
              + g_ref[:, D_MODEL:2 * D_MODEL] * _dot(bin_ref[...], wpb)
              + g_ref[:, 2 * D_MODEL:3 * D_MODEL] * _dot(cin, wpc))
    out = _dot(merged.astype(bf16), wout)
    y_ref[...] = _layer_norm(ALPHA * x_ref[...] + out, row(lng_ref), row(lnb_ref))


def _sample_merge(layer, x, ain, bin_, yc, szc, g, wpa, wpb, wpc, wout, lng, lnb):
    lspec = lambda a: _layer_spec(a, layer)
    nb = x.shape[0]
    x_spec = pl.BlockSpec((nb, None, D_MODEL), lambda i: (0, 0, 0))
    out_shape = (jax.ShapeDtypeStruct(x.shape, f32),) + tuple(
        jax.ShapeDtypeStruct(w.shape[1:], bf16) for w in (wpa, wpb, wpc, wout))
    return pl.pallas_call(
        functools.partial(_sample_merge_kernel, layer),
        out_shape=out_shape,
        grid=(1,),
        in_specs=[x_spec, _full_spec(ain), _full_spec(bin_), _full_spec(yc), _full_spec(szc),
                  _full_spec(g), lspec(wpa), lspec(wpb), lspec(wpc), lspec(wout), _full_spec(lng),
                  _full_spec(lnb)],
        out_specs=(x_spec,) + tuple(_full_spec(s) for s in out_shape[1:]),
        compiler_params=pltpu.CompilerParams(dimension_semantics=("arbitrary",),
                                             vmem_limit_bytes=VMEM_LIMIT),
        name="sample_merge",
    )(x, ain, bin_, yc, szc, g, wpa, wpb, wpc, wout, lng, lnb)


def _rope_tables(positions):
    halfd = HEAD_DIM // 2
    inv = ROPE_THETA ** (-np.arange(halfd, dtype=np.float64) / halfd)
    ang = np.asarray(positions, dtype=np.float64)[:, None] * inv[None, :]
    cos = np.tile(np.cos(ang), (1, LANES // halfd))
    sin = np.tile(np.concatenate([-np.sin(ang), np.sin(ang)], axis=1), (1, LANES // HEAD_DIM))
    return jnp.asarray(cos, f32), jnp.asarray(sin, f32)


def kernel(x_prompt, x_sample, state_pool, cache_k_win, cache_v_win, w_in, b_gate, pool_w, pool_scale, sgu_ln_g, sgu_ln_b, sgu_w, sgu_b, attn_sinks, w_proj_a, w_proj_b, w_proj_c, w_out, ln_g, ln_b):
    B, L, _ = x_prompt.shape
    nb = x_sample.shape[0]
    cos_p, sin_p = _rope_tables(np.arange(L))
    cos_s, sin_s = _rope_tables(np.array([PAST_LEN]))

    sguw = sgu_w.reshape(DEPTH, N_SGU_GROUPS * CHUNK, CHUNK)
    sgub = jnp.repeat(jnp.swapaxes(sgu_b, 1, 2), POOL_GC, axis=2)
    bg_t = jnp.transpose(b_gate, (1, 0, 2))
    kc_t = jnp.transpose(cache_k_win, (0, 1, 3, 4, 2))
    vc_t = jnp.transpose(cache_v_win, (0, 1, 3, 4, 2))
    pool_t = jnp.transpose(state_pool, (0, 2, 1, 3))

    y_s = x_sample
    pool_s, chunk_v, kts, vts, weights = ([] for _ in range(5))
    for l in range(DEPTH):
        w_in_b, qkv, kt, vt, ps, cv, ain, bin_, szc, g, pool_bd = _sample_proj(
            l, y_s, cos_s, sin_s, w_in, bg_t, pool_t, pool_w, pool_scale, sgu_ln_g, sgu_ln_b, sgu_w, sgu_b)
        o = _sample_attn(l, attn_sinks, qkv.reshape(nb, N_HEADS + 2 * N_KV_HEADS, HEAD_DIM), kc_t, vc_t)
        y_s, wpa, wpb, wpc, wout = _sample_merge(
            l, y_s, ain, bin_, o.reshape(nb, D_ATTN), szc, g, w_proj_a, w_proj_b, w_proj_c, w_out, ln_g, ln_b)
        pool_s.append(ps); chunk_v.append(cv); kts.append(kt); vts.append(vt)
        weights.append((w_in_b, wpa, wpb, wpc, wout, pool_bd))

    y_p = x_prompt
    pool_p, k_p, v_p = ([] for _ in range(3))
    for l in range(DEPTH):
        w_in_b, wpa, wpb, wpc, wout, poolw = weights[l]
        slide_args = (kts, vts, kc_t, vc_t) if l == DEPTH - 1 else None
        y_p, pp, kp, vp, *slid = _prompt_layer(
            l, y_p, cos_p, sin_p, attn_sinks, w_in_b, bg_t, poolw, pool_scale, sgu_ln_g, sgu_ln_b, sguw,
            sgub, wpa, wpb, wpc, wout, ln_g, ln_b, slide_args=slide_args)
        pool_p.append(pp); k_p.append(kp); v_p.append(vp)
    k_s, v_s = slid

    to_cache = lambda a: jnp.transpose(a, (0, 1, 4, 2, 3))
    prompt_cache = lambda lst: to_cache(jnp.stack(lst).reshape(DEPTH, B, N_KV_HEADS, HEAD_DIM, WINDOW))
    return (y_p, y_s,
            jnp.stack(pool_p), prompt_cache(k_p), prompt_cache(v_p),
            jnp.transpose(jnp.stack(pool_s), (0, 2, 1, 3)), to_cache(k_s), to_cache(v_s),
            jnp.stack(chunk_v))
```

```python
import functools

import numpy as np
import jax
import jax.numpy as jnp
from jax import lax
from jax.experimental import pallas as pl
from jax.experimental.pallas import tpu as pltpu

D_MODEL = 1024
DEPTH = 2
PAST_LEN = 8192
D_POOL = 256
POOL_WINDOWS = (2, 4, 8, 16)
POOL_GC = 64
POOL_BUF = 15
D_SGU = 256
CHUNK = 128
N_SGU_GROUPS = 4
HEAD_DIM = 64
N_HEADS = 8
N_KV_HEADS = 2
Q_PER_KV = 4
D_ATTN = 512
D_KV = 128
WINDOW = 128
BLOCK = 128
ROPE_THETA = 10000.0
N_BRANCHES = 3
D_IN = 2 * D_POOL + 3 * D_SGU + 2 * D_ATTN + 2 * D_KV + N_BRANCHES * D_MODEL
ALPHA = (2.0 * DEPTH) ** 0.25
LN_EPS = 1e-5
NEG_INF = -1e30
SCALE = HEAD_DIM ** -0.5

OFF_XA, OFF_ZA, OFF_U, OFF_V, OFF_ZB = 0, 256, 512, 768, 1024
OFF_Q, OFF_K, OFF_VV, OFF_ZC, OFF_G = 1280, 1792, 1920, 2048, 2560

LANES = 128
SUBLANES = 8
TM = 512
SUB = 2
NBLK = TM // BLOCK
HIST = 32
NCHUNK = 256
OUT_CHUNKS = ((0, 256), (256, 512))
VMEM_LIMIT = 56 * 1024 * 1024
SB = 32
RK = 512
RK_STEPS = D_MODEL // RK

bf16 = jnp.bfloat16
f32 = jnp.float32


def _dot(a, b):
    return jnp.dot(a, b, preferred_element_type=f32)


def _dot_nt(a, b):
    return lax.dot_general(a, b, (((1,), (1,)), ((), ())), preferred_element_type=f32)


def _sigmoid(z):
    return 0.5 * jnp.tanh(0.5 * z) + 0.5


def _silu(z):
    return z * _sigmoid(z)


def _gate2(half_pre, bias):
    return jnp.tanh(half_pre + 0.5 * bias) + 1.0


def _layer_norm(x, g, b):
    mu = jnp.mean(x, axis=-1, keepdims=True)
    xc = x - mu
    var = jnp.mean(xc * xc, axis=-1, keepdims=True)
    return xc * lax.rsqrt(var + LN_EPS) * g + b


def _rope128(x, cos, sin_signed):
    lane = lax.broadcasted_iota(jnp.int32, x.shape, 1)
    first_half = (lane % HEAD_DIM) < (HEAD_DIM // 2)
    partner = jnp.where(first_half,
                        pltpu.roll(x, LANES - HEAD_DIM // 2, 1),
                        pltpu.roll(x, HEAD_DIM // 2, 1))
    return x * cos + partner * sin_signed


def _group_select(lane, a0, a1, a2, a3):
    return jnp.where(lane < 64, a0, jnp.where(lane < 128, a1, jnp.where(lane < 192, a2, a3)))


def _layer_spec(arr, layer, single_buffer=False):
    block = (None,) + arr.shape[1:]
    zeros = (0,) * (arr.ndim - 1)
    index_map = lambda *_: (layer,) + zeros
    if single_buffer:
        return pl.BlockSpec(block, index_map, pipeline_mode=pl.Buffered(1))
    return pl.BlockSpec(block, index_map)


def _full_spec(arr, single_buffer=False):
    zeros = (0,) * len(arr.shape)
    if single_buffer:
        return pl.BlockSpec(arr.shape, lambda *_: zeros, pipeline_mode=pl.Buffered(1))
    return pl.BlockSpec(arr.shape, lambda *_: zeros)


def _slide_windows(step, per_step, new_refs, old_refs, out_refs):
    lane = lax.broadcasted_iota(jnp.int32, (D_KV, WINDOW), 1)
    shift = jnp.where(step == 0, 0, LANES - step * per_step)
    for new_ref, old_ref, out_ref in zip(new_refs, old_refs, out_refs):
        for l in range(DEPTH):
            new_cols = pltpu.roll(new_ref[l][...], shift, 1)
            for s in range(per_step):
                slid = pltpu.roll(old_ref[l, s].reshape(D_KV, WINDOW), WINDOW - 1, 1)
                newest = jnp.broadcast_to(new_cols[:, s:s + 1], (D_KV, WINDOW))
                out_ref[l, s] = jnp.where(lane == WINDOW - 1, newest, slid).reshape(
                    N_KV_HEADS, HEAD_DIM, WINDOW)


def _prompt_kernel(layer, slide, *refs):
    def sub_tile(s, carry):
        _prompt_tile(layer, slide, s, refs)
        return carry
    lax.fori_loop(0, SUB, sub_tile, 0)


def _prompt_tile(layer, slide, s, refs):
    n_in = 18 + (2 * DEPTH + 2 if slide else 0)
    n_out = 4 + (2 if slide else 0)
    (sinks_ref, x_ref, cos_ref, sin_ref, w_in_ref, bg_ref, poolw_ref, pscale_ref,
     slng_ref, slnb_ref, sguw_ref, sgub_ref, wpa_ref, wpb_ref, wpc_ref, wout_ref,
     lng_ref, lnb_ref) = refs[:18]
    y_ref, pool_out_ref, k_out_ref, v_out_ref = refs[n_in:n_in + 4]
    (h_ref, ext_ref, s2_ref, s4_ref, s8_ref, qs_ref,
     ka_ref, kb_ref, kc_ref, kd_ref, va_ref, vb_ref, vc_ref, vd_ref,
     ain_ref, bin_ref, cin_ref, mg_ref, gate_ref, klast_ref, vlast_ref) = refs[n_in + n_out:]
    tile_rows = pl.ds(pl.multiple_of(s * TM, TM), TM)
    x_ref, y_ref, cos_ref, sin_ref = (r.at[tile_rows] for r in (x_ref, y_ref, cos_ref, sin_ref))
    i = pl.program_id(1) * SUB + s
    last = pl.num_programs(1) * SUB - 1
    kv_refs = (ka_ref, kb_ref, kc_ref, kd_ref, va_ref, vb_ref, vc_ref, vd_ref)
    row = lambda ref: ref[layer:layer + 1, :]

    @pl.when(i == 0)
    def _():
        ext_ref[0:HIST, :] = jnp.zeros((HIST, D_POOL), f32)
        for r in kv_refs:
            r[0:BLOCK, :] = jnp.zeros((BLOCK, LANES), bf16)

    xb = x_ref[...].astype(bf16)
    half = OFF_G // 2
    h_ref[:, 0:half] = _dot(xb, w_in_ref[:, 0:half])
    h_ref[:, half:OFF_G] = _dot(xb, w_in_ref[:, half:OFF_G])

    if slide:
        step = pl.program_id(0) * (last + 1) + i
        new_refs = (refs[18:18 + DEPTH], refs[18 + DEPTH:18 + 2 * DEPTH])
        old_refs = refs[18 + 2 * DEPTH:n_in]
        per_step = old_refs[0].shape[1] // SUB
        mine = lambda r: r.at[:, pl.ds(s * per_step, per_step)]
        _slide_windows(step, per_step, new_refs, [mine(r) for r in old_refs],
                       [mine(r) for r in refs[n_in + 4:n_in + 6]])

    xa = h_ref[:, OFF_XA:OFF_XA + D_POOL]
    ext_ref[HIST:HIST + TM, :] = xa
    n = HIST + TM
    s2_ref[8:n, :] = ext_ref[8:n, :] + ext_ref[7:n - 1, :]
    s4_ref[16:n, :] = s2_ref[16:n, :] + s2_ref[14:n - 2, :]
    s8_ref[24:n, :] = s4_ref[24:n, :] + s4_ref[20:n - 4, :]
    w16 = s8_ref[HIST:n, :] + s8_ref[HIST - 8:n - 8, :]
    lane_p = lax.broadcasted_iota(jnp.int32, (TM, D_POOL), 1)
    row_p = lax.broadcasted_iota(jnp.int32, (TM, D_POOL), 0)
    win = _group_select(lane_p, s2_ref[HIST:n, :], s4_ref[HIST:n, :], s8_ref[HIST:n, :], w16)
    width = _group_select(lane_p, POOL_WINDOWS[0], POOL_WINDOWS[1], POOL_WINDOWS[2], POOL_WINDOWS[3])
    cnt = jnp.minimum(row_p + (i * TM + 1), width).astype(f32)
    pooled = win / cnt - xa
    ya = _dot(pooled.astype(bf16), poolw_ref[...]) * row(pscale_ref)
    za = h_ref[:, OFF_ZA:OFF_ZA + D_POOL]
    ain_ref[...] = (ya * _silu(za)).astype(bf16)
    ext_ref[HIST - 16:HIST, :] = ext_ref[n - 16:n, :]

    vn = _layer_norm(h_ref[:, OFF_V:OFF_V + D_SGU], row(slng_ref), row(slnb_ref)).astype(bf16)
    wr = lax.broadcasted_iota(jnp.int32, (N_SGU_GROUPS * CHUNK, CHUNK), 0) % CHUNK
    wc = lax.broadcasted_iota(jnp.int32, (N_SGU_GROUPS * CHUNK, CHUNK), 1)
    w_s = jnp.where(wc <= wr, sguw_ref[...], 0.0).astype(bf16)
    lane_c = lax.broadcasted_iota(jnp.int32, (CHUNK, D_SGU), 1)
    for j in range(NBLK):
        rows = slice(j * CHUNK, (j + 1) * CHUNK)
        r = _dot(w_s, vn[rows, :])
        mixed = _group_select(lane_c, r[0:CHUNK], r[CHUNK:2 * CHUNK], r[2 * CHUNK:3 * CHUNK],
                              r[3 * CHUNK:4 * CHUNK]) + sgub_ref[...]
        yb = h_ref[rows, OFF_U:OFF_U + D_SGU] * mixed
        bin_ref[rows, :] = (yb * _silu(h_ref[rows, OFF_ZB:OFF_ZB + D_SGU])).astype(bf16)

    cos = cos_ref[...]
    sin = sin_ref[...]
    for c in range(D_ATTN // LANES):
        qc = _rope128(h_ref[:, OFF_Q + c * LANES:OFF_Q + (c + 1) * LANES], cos, sin)
        qs_ref[:, c * LANES:(c + 1) * LANES] = (qc * SCALE).astype(bf16)
    kr = _rope128(h_ref[:, OFF_K:OFF_K + D_KV], cos, sin)
    vv = h_ref[:, OFF_VV:OFF_VV + D_KV]
    klast_ref[...] = kr[TM - WINDOW:TM, :]
    vlast_ref[...] = vv[TM - WINDOW:TM, :]
    lane_k = lax.broadcasted_iota(jnp.int32, (TM, LANES), 1)
    lo = lane_k < HEAD_DIM
    for src, (a_ref, b_ref, c_ref, d_ref) in ((kr, kv_refs[0:4]), (vv, kv_refs[4:8])):
        sw = pltpu.roll(src, HEAD_DIM, 1)
        a_ref[BLOCK:BLOCK + TM, :] = jnp.where(lo, src, 0.0).astype(bf16)
        b_ref[BLOCK:BLOCK + TM, :] = jnp.where(lo, 0.0, src).astype(bf16)
        c_ref[BLOCK:BLOCK + TM, :] = jnp.where(lo, sw, 0.0).astype(bf16)
        d_ref[BLOCK:BLOCK + TM, :] = jnp.where(lo, 0.0, sw).astype(bf16)

    qrow = lax.broadcasted_iota(jnp.int32, (2 * BLOCK, 2 * BLOCK), 0) % BLOCK
    kcol = lax.broadcasted_iota(jnp.int32, (2 * BLOCK, 2 * BLOCK), 1)
    band = (kcol >= qrow) & (kcol <= qrow + WINDOW)
    band_first = band & (kcol >= jnp.where(i > 0, 0, BLOCK))
    top = lax.broadcasted_iota(jnp.int32, (2 * BLOCK, 1), 0) < BLOCK
    n_gate = N_BRANCHES * D_MODEL // NCHUNK
    n_unit = NBLK * N_KV_HEADS
    gate_sched = [range(u * n_gate // n_unit, (u + 1) * n_gate // n_unit) for u in range(n_unit)]

    def scores(u):
        j, kv = divmod(u, N_KV_HEADS)
        rows = slice(j * BLOCK, (j + 1) * BLOCK)
        keys = slice(j * BLOCK, j * BLOCK + 2 * BLOCK)
        c0 = kv * Q_PER_KV * HEAD_DIM
        qst = jnp.concatenate([qs_ref[rows, c0:c0 + LANES], qs_ref[rows, c0 + LANES:c0 + 2 * LANES]], axis=0)
        k_even, k_odd = (ka_ref, kd_ref) if kv == 0 else (kc_ref, kb_ref)
        kcat = jnp.concatenate([k_even[keys, :], k_odd[keys, :]], axis=0)
        return _dot_nt(qst, kcat)

    def attend(u, sc):
        j, kv = divmod(u, N_KV_HEADS)
        rows = slice(j * BLOCK, (j + 1) * BLOCK)
        keys = slice(j * BLOCK, j * BLOCK + 2 * BLOCK)
        allowed = band_first if j == 0 else band
        c0 = kv * Q_PER_KV * HEAD_DIM
        h0 = kv * Q_PER_KV
        v_even, v_odd = (va_ref, vd_ref) if kv == 0 else (vc_ref, vb_ref)
        probs = []
        for par in range(2):
            sink = jnp.where(top, sinks_ref[layer, h0 + par], sinks_ref[layer, h0 + 2 + par])
            sm = jnp.where(allowed, sc[:, par * 2 * BLOCK:(par + 1) * 2 * BLOCK], NEG_INF)
            m = jnp.maximum(jnp.max(sm, axis=-1, keepdims=True), sink)
            p = jnp.exp(sm - m)
            den = jnp.sum(p, axis=-1, keepdims=True) + jnp.exp(sink - m)
            probs.append((p / den).astype(bf16))
        pcat = jnp.concatenate(probs, axis=1)
        vcat = jnp.concatenate([v_even[keys, :], v_odd[keys, :]], axis=0)
        o = _dot(pcat, vcat)
        for pr in range(2):
            cols = slice(c0 + pr * LANES, c0 + (pr + 1) * LANES)
            zc = h_ref[rows, OFF_ZC + c0 + pr * LANES:OFF_ZC + c0 + (pr + 1) * LANES]
            cin_ref[rows, cols] = (o[pr * BLOCK:(pr + 1) * BLOCK] * _silu(zc)).astype(bf16)

    sc_next = scores(0)
    for u in range(n_unit):
        sc = sc_next
        for gc in gate_sched[u]:
            gcols = slice(gc * NCHUNK, (gc + 1) * NCHUNK)
            gate_ref[:, gcols] = _dot(xb, w_in_ref[:, OFF_G + gc * NCHUNK:OFF_G + (gc + 1) * NCHUNK])
        if u + 1 < n_unit:
            sc_next = scores(u + 1)
        attend(u, sc)

    for r in kv_refs:
        r[0:BLOCK, :] = r[TM:TM + BLOCK, :]

    for c in range(D_MODEL // NCHUNK):
        cols = slice(c * NCHUNK, (c + 1) * NCHUNK)
        acc = None
        for br, (in_ref, wp_ref) in enumerate(((ain_ref, wpa_ref), (bin_ref, wpb_ref), (cin_ref, wpc_ref))):
            g0 = br * D_MODEL + c * NCHUNK
            term = (_gate2(gate_ref[:, g0:g0 + NCHUNK], bg_ref[br, layer:layer + 1, cols])
                    * _dot(in_ref[...], wp_ref[:, cols]))
            acc = term if acc is None else acc + term
        mg_ref[:, cols] = acc.astype(bf16)

    for r0, r1 in OUT_CHUNKS:
        rows = slice(r0, r1)
        out = _dot(mg_ref[rows, :], wout_ref[...])
        y_ref[rows, :] = _layer_norm(ALPHA * x_ref[rows, :] + out, row(lng_ref), row(lnb_ref))

    @pl.when(i == last)
    def _():
        pool_out_ref[...] = ext_ref[n - POOL_BUF:n, :]
        k_out_ref[...] = klast_ref[...].T
        v_out_ref[...] = vlast_ref[...].T


def _prompt_layer(layer, x, cos, sin, sinks, w_in, b_gate, poolw, pscale, slng, slnb, sguw, sgub,
                  wpa, wpb, wpc, wout, lng, lnb, slide_args=None):
    B, L, _ = x.shape
    rows = TM * SUB
    nt = L // rows
    grid = (B, nt)
    row_spec = lambda w: pl.BlockSpec((rows, w), lambda b, i: (i, 0))
    lspec = lambda a: _layer_spec(a, layer, single_buffer=True)
    wspec = lambda a: _full_spec(a, single_buffer=True)
    in_specs = [
        pl.BlockSpec(memory_space=pltpu.SMEM),
        pl.BlockSpec((None, rows, D_MODEL), lambda b, i: (b, i, 0)),
        row_spec(LANES), row_spec(LANES),
        wspec(w_in), _full_spec(b_gate), wspec(poolw), _full_spec(pscale), _full_spec(slng), _full_spec(slnb),
        lspec(sguw), lspec(sgub), wspec(wpa), wspec(wpb), wspec(wpc), wspec(wout),
        _full_spec(lng), _full_spec(lnb),
    ]
    out_shape = (
        jax.ShapeDtypeStruct((B, L, D_MODEL), f32),
        jax.ShapeDtypeStruct((B, POOL_BUF, D_POOL), f32),
        jax.ShapeDtypeStruct((B, D_KV, WINDOW), f32),
        jax.ShapeDtypeStruct((B, D_KV, WINDOW), f32),
    )
    out_specs = (
        pl.BlockSpec((None, rows, D_MODEL), lambda b, i: (b, i, 0)),
        pl.BlockSpec((None, POOL_BUF, D_POOL), lambda b, i: (b, 0, 0)),
        pl.BlockSpec((None, D_KV, WINDOW), lambda b, i: (b, 0, 0)),
        pl.BlockSpec((None, D_KV, WINDOW), lambda b, i: (b, 0, 0)),
    )
    args = [sinks, x, cos, sin, w_in, b_gate, poolw, pscale, slng, slnb, sguw, sgub,
            wpa, wpb, wpc, wout, lng, lnb]
    if slide_args is not None:
        new_k, new_v, cache_k, cache_v = slide_args
        depth, nb = cache_k.shape[:2]
        per_step = nb // (B * nt)
        assert per_step * B * nt == nb and depth == DEPTH
        cache_spec = pl.BlockSpec((depth, per_step) + cache_k.shape[2:], lambda b, i: (0, b * nt + i, 0, 0, 0))
        in_specs += [_full_spec(a) for a in (*new_k, *new_v)] + [cache_spec, cache_spec]
        args += [*new_k, *new_v, cache_k, cache_v]
        out_shape += (jax.ShapeDtypeStruct(cache_k.shape, f32), jax.ShapeDtypeStruct(cache_v.shape, f32))
        out_specs += (cache_spec, cache_spec)
    kv_scratch = [pltpu.VMEM((BLOCK + TM, LANES), bf16) for _ in range(8)]
    scratch = [
        pltpu.VMEM((TM, OFF_G), f32),
        pltpu.VMEM((HIST + TM, D_POOL), f32),
        pltpu.VMEM((HIST + TM, D_POOL), f32),
        pltpu.VMEM((HIST + TM, D_POOL), f32),
        pltpu.VMEM((HIST + TM, D_POOL), f32),
        pltpu.VMEM((TM, D_ATTN), bf16),
        *kv_scratch,
        pltpu.VMEM((TM, D_POOL), bf16),
        pltpu.VMEM((TM, D_SGU), bf16),
        pltpu.VMEM((TM, D_ATTN), bf16),
        pltpu.VMEM((TM, D_MODEL), bf16),
        pltpu.VMEM((TM, N_BRANCHES * D_MODEL), f32),
        pltpu.VMEM((WINDOW, D_KV), f32),
        pltpu.VMEM((WINDOW, D_KV), f32),
    ]
    return pl.pallas_call(
        functools.partial(_prompt_kernel, layer, slide_args is not None),
        out_shape=out_shape,
        grid=grid,
        in_specs=in_specs,
        out_specs=out_specs,
        scratch_shapes=scratch,
        compiler_params=pltpu.CompilerParams(
            dimension_semantics=("arbitrary", "arbitrary"),
            vmem_limit_bytes=VMEM_LIMIT),
        name="prompt_layer",
    )(*args)


def _sample_proj_kernel(layer, x_ref, cos_ref, sin_ref, w_in_ref, bg_ref, pb_ref, poolw_ref, pscale_ref,
                        slng_ref, slnb_ref, sw0_ref, sb0_ref,
                        wb_ref, qkv_ref, kt_ref, vt_ref, pool_ref, vn_ref, ain_ref, bin_ref,
                        szc_ref, g_ref, poolbd_ref, h_ref):
    j = pl.program_id(0)
    row = lambda ref: ref[layer:layer + 1, :]
    wb_ref[:, 0:OFF_G] = w_in_ref[:, 0:OFF_G].astype(bf16)
    wb_ref[:, OFF_G:D_IN] = (0.5 * w_in_ref[:, OFF_G:D_IN]).astype(bf16)

    @pl.when(j == 0)
    def _():
        h_ref[...] = jnp.zeros(h_ref.shape, f32)

    h_ref[...] += _dot(x_ref[...].astype(bf16), wb_ref[...])

    def hcols(off, width):
        return h_ref[:, off:off + width]

    @pl.when(j == RK_STEPS - 1)
    def _():
        xa = hcols(OFF_XA, D_POOL)
        lane = lax.broadcasted_iota(jnp.int32, xa.shape, 1)
        first_row = _group_select(lane, *(POOL_BUF - (w - 1) for w in POOL_WINDOWS))
        win = xa
        for r in range(POOL_BUF):
            win = win + jnp.where(first_row <= r, pb_ref[r], 0.0)
        width = _group_select(lane, *POOL_WINDOWS).astype(f32)
        pooled = win / width - xa
        zero_blk = jnp.zeros((POOL_GC, POOL_GC), f32)
        n_grp = len(POOL_WINDOWS)
        pool_bd = jnp.concatenate(
            [jnp.concatenate([poolw_ref[g] if gg == g else zero_blk for gg in range(n_grp)], axis=1)
             for g in range(n_grp)], axis=0).astype(bf16)
        poolbd_ref[...] = pool_bd
        ya = _dot(pooled.astype(bf16), pool_bd) * row(pscale_ref)
        ain_ref[...] = (ya * _silu(hcols(OFF_ZA, D_POOL))).astype(bf16)
        for r in range(POOL_BUF - 1):
            pool_ref[r] = pb_ref[r + 1]
        pool_ref[POOL_BUF - 1] = xa
        vn = _layer_norm(hcols(OFF_V, D_SGU), row(slng_ref), row(slnb_ref))
        vn_ref[...] = vn
        lane_g = lax.broadcasted_iota(jnp.int32, (1, D_SGU), 1)
        sw0 = _group_select(lane_g, *(sw0_ref[g, 0:1, 0:1] for g in range(N_SGU_GROUPS)))
        sb0 = _group_select(lane_g, *(sb0_ref[g:g + 1, 0:1] for g in range(N_SGU_GROUPS)))
        yb = hcols(OFF_U, D_SGU) * (sw0 * vn + sb0)
        bin_ref[...] = (yb * _silu(hcols(OFF_ZB, D_SGU))).astype(bf16)
        cos = cos_ref[...]
        sin = sin_ref[...]
        for c in range(D_ATTN // LANES):
            qkv_ref[:, c * LANES:(c + 1) * LANES] = _rope128(hcols(OFF_Q + c * LANES, LANES), cos, sin) * SCALE
        kr = _rope128(hcols(OFF_K, D_KV), cos, sin)
        vv = hcols(OFF_VV, D_KV)
        qkv_ref[:, D_ATTN:D_ATTN + D_KV] = kr
        qkv_ref[:, D_ATTN + D_KV:D_ATTN + 2 * D_KV] = vv
        kt_ref[...] = kr.T
        vt_ref[...] = vv.T
        szc_ref[...] = _silu(hcols(OFF_ZC, D_ATTN))
        for br in range(N_BRANCHES):
            cols = slice(br * D_MODEL, (br + 1) * D_MODEL)
            g_ref[:, cols] = _gate2(hcols(OFF_G + br * D_MODEL, D_MODEL), bg_ref[br, layer:layer + 1, :])


def _sample_proj(layer, x, cos, sin, w_in, b_gate, pool_t, poolw, pscale, slng, slnb, sw0, sb0):
    nb = x.shape[0]
    out_shape = (
        jax.ShapeDtypeStruct((D_MODEL, D_IN), bf16),
        jax.ShapeDtypeStruct((nb, D_ATTN + 2 * D_KV), f32),
        jax.ShapeDtypeStruct((D_KV, nb), f32),
        jax.ShapeDtypeStruct((D_KV, nb), f32),
        jax.ShapeDtypeStruct((POOL_BUF, nb, D_POOL), f32),
        jax.ShapeDtypeStruct((nb, 1, D_SGU), f32),
        jax.ShapeDtypeStruct((nb, D_POOL), bf16),
        jax.ShapeDtypeStruct((nb, D_SGU), bf16),
        jax.ShapeDtypeStruct((nb, D_ATTN), f32),
        jax.ShapeDtypeStruct((nb, N_BRANCHES * D_MODEL), f32),
        jax.ShapeDtypeStruct((D_POOL, D_POOL), bf16),
    )
    lspec = lambda a: _layer_spec(a, layer)
    args = (x, cos, sin, w_in, b_gate, pool_t, poolw, pscale, slng, slnb, sw0, sb0)
    in_specs = [pl.BlockSpec((nb, None, RK), lambda j: (0, 0, j)), _full_spec(cos), _full_spec(sin),
                pl.BlockSpec((None, RK, D_IN), lambda j: (layer, j, 0)),
                _full_spec(b_gate), lspec(pool_t), lspec(poolw), _full_spec(pscale), _full_spec(slng),
                _full_spec(slnb),
                pl.BlockSpec((None, N_SGU_GROUPS, SUBLANES, CHUNK), lambda j: (layer, 0, 0, 0)),
                pl.BlockSpec((None, N_SGU_GROUPS, CHUNK), lambda j: (layer, 0, 0))]
    squeeze_mid = lambda s: pl.BlockSpec((s.shape[0], None, s.shape[2]), lambda j: (0, 0, 0))
    out_specs = (pl.BlockSpec((RK, D_IN), lambda j: (j, 0)),) + tuple(
        squeeze_mid(s) if len(s.shape) == 3 and s.shape[1] == 1 else _full_spec(s) for s in out_shape[1:])
    return pl.pallas_call(
        functools.partial(_sample_proj_kernel, layer),
        out_shape=out_shape,
        grid=(RK_STEPS,),
        in_specs=in_specs,
        out_specs=out_specs,
        scratch_shapes=[pltpu.VMEM((nb, D_IN), f32)],
        compiler_params=pltpu.CompilerParams(dimension_semantics=("arbitrary",),
                                             vmem_limit_bytes=VMEM_LIMIT),
        name="sample_proj",
    )(*args)


def _sample_attn_kernel(layer, sink_ref, qkv_ref, kc_ref, vc_ref, wpa_ref, wpb_ref, wpc_ref, wout_ref,
                        o_ref, wpa_b_ref, wpb_b_ref, wpc_b_ref, wout_b_ref):
    wpa_b_ref[...] = (0.5 * wpa_ref[...]).astype(bf16)
    wpb_b_ref[...] = (0.5 * wpb_ref[...]).astype(bf16)
    wpc_b_ref[...] = (0.5 * wpc_ref[...]).astype(bf16)
    wout_b_ref[...] = wout_ref[...].astype(bf16)
    head = lax.broadcasted_iota(jnp.int32, (1, Q_PER_KV, 1), 1)
    for kv in range(N_KV_HEADS):
        qb = qkv_ref[:, kv * Q_PER_KV:(kv + 1) * Q_PER_KV, :].astype(bf16)
        kn = qkv_ref[:, N_HEADS + kv:N_HEADS + kv + 1, :]
        vn = qkv_ref[:, N_HEADS + N_KV_HEADS + kv:N_HEADS + N_KV_HEADS + kv + 1, :]
        sink = jnp.zeros((1, Q_PER_KV, 1), f32)
        for g in range(Q_PER_KV):
            sink = jnp.where(head == g, sink_ref[layer, kv * Q_PER_KV + g], sink)
        s = jnp.einsum('bgd,bdw->bgw', qb, kc_ref[:, kv].astype(bf16), preferred_element_type=f32)
        s_new = jnp.sum(qb.astype(f32) * kn.astype(bf16).astype(f32), axis=-1, keepdims=True)
        m = jnp.maximum(jnp.maximum(jnp.max(s, axis=-1, keepdims=True), s_new), sink)
        p = jnp.exp(s - m)
        p_new = jnp.exp(s_new - m)
        den = jnp.sum(p, axis=-1, keepdims=True) + p_new + jnp.exp(sink - m)
        o = jnp.einsum('bgw,bdw->bgd', (p / den).astype(bf16), vc_ref[:, kv].astype(bf16),
                       preferred_element_type=f32)
        o_ref[:, kv] = o + (p_new / den) * vn


def _sample_attn(layer, sink, qkv3, kc, vc, wpa, wpb, wpc, wout):
    nb, n_rows, _ = qkv3.shape
    steps = nb // SB
    cache_spec = pl.BlockSpec((None, SB, N_KV_HEADS, HEAD_DIM, WINDOW), lambda b: (layer, b, 0, 0, 0))
    weights = (wpa, wpb, wpc, wout)
    w_in_specs = [pl.BlockSpec((None, w.shape[1] // steps, w.shape[2]), lambda b: (layer, b, 0)) for w in weights]
    w_out_specs = [pl.BlockSpec((w.shape[1] // steps, w.shape[2]), lambda b: (b, 0)) for w in weights]
    return pl.pallas_call(
        functools.partial(_sample_attn_kernel, layer),
        out_shape=(jax.ShapeDtypeStruct((nb, N_KV_HEADS, Q_PER_KV, HEAD_DIM), f32),
                   *(jax.ShapeDtypeStruct(w.shape[1:], bf16) for w in weights)),
        grid=(steps,),
        in_specs=[pl.BlockSpec(memory_space=pltpu.SMEM),
                  pl.BlockSpec((SB, n_rows, HEAD_DIM), lambda b: (b, 0, 0)), cache_spec, cache_spec,
                  *w_in_specs],
        out_specs=(pl.BlockSpec((SB, N_KV_HEADS, Q_PER_KV, HEAD_DIM), lambda b: (b, 0, 0, 0)), *w_out_specs),
        compiler_params=pltpu.CompilerParams(dimension_semantics=("arbitrary",),
                                             vmem_limit_bytes=VMEM_LIMIT),
        name="sample_attn",
    )(sink, qkv3, kc, vc, *weights)


def _sample_merge_kernel(layer, x_ref, ain_ref, bin_ref, yc_ref, szc_ref, g_ref, wpa_ref, wpb_ref, wpc_ref,
                         wout_ref, lng_ref, lnb_ref, y_ref):
    row = lambda ref: ref[layer:layer + 1, :]
    cin = (yc_ref[...] * szc_ref[...]).astype(bf16)
    merged = (g_ref[:, 0:D_MODEL] * _dot(ain_ref[...], wpa_ref[...])
              + g_ref[:, D_MODEL:2 * D_MODEL] * _dot(bin_ref[...], wpb_ref[...])
              + g_ref[:, 2 * D_MODEL:3 * D_MODEL] * _dot(cin, wpc_ref[...]))
    out = _dot(merged.astype(bf16), wout_ref[...])
    y_ref[...] = _layer_norm(ALPHA * x_ref[...] + out, row(lng_ref), row(lnb_ref))


def _sample_merge(layer, x, ain, bin_, yc, szc, g, wpa, wpb, wpc, wout, lng, lnb):
    nb = x.shape[0]
    x_spec = pl.BlockSpec((nb, None, D_MODEL), lambda i: (0, 0, 0))
    return pl.pallas_call(
        functools.partial(_sample_merge_kernel, layer),
        out_shape=jax.ShapeDtypeStruct(x.shape, f32),
        grid=(1,),
        in_specs=[x_spec, _full_spec(ain), _full_spec(bin_), _full_spec(yc), _full_spec(szc),
                  _full_spec(g), _full_spec(wpa), _full_spec(wpb), _full_spec(wpc), _full_spec(wout),
                  _full_spec(lng), _full_spec(lnb)],
        out_specs=x_spec,
        compiler_params=pltpu.CompilerParams(dimension_semantics=("arbitrary",),
                                             vmem_limit_bytes=VMEM_LIMIT),
        name="sample_merge",
    )(x, ain, bin_, yc, szc, g, wpa, wpb, wpc, wout, lng, lnb)


def _rope_tables(positions):
    halfd = HEAD_DIM // 2
    inv = ROPE_THETA ** (-np.arange(halfd, dtype=np.float64) / halfd)
    ang = np.asarray(positions, dtype=np.float64)[:, None] * inv[None, :]
    cos = np.tile(np.cos(ang), (1, LANES // halfd))
    sin = np.tile(np.concatenate([-np.sin(ang), np.sin(ang)], axis=1), (1, LANES // HEAD_DIM))
    return jnp.asarray(cos, f32), jnp.asarray(sin, f32)


def kernel(x_prompt, x_sample, state_pool, cache_k_win, cache_v_win, w_in, b_gate, pool_w, pool_scale, sgu_ln_g, sgu_ln_b, sgu_w, sgu_b, attn_sinks, w_proj_a, w_proj_b, w_proj_c, w_out, ln_g, ln_b):
    B, L, _ = x_prompt.shape
    nb = x_sample.shape[0]
    cos_p, sin_p = _rope_tables(np.arange(L))
    cos_s, sin_s = _rope_tables(np.array([PAST_LEN]))

    sguw = sgu_w.reshape(DEPTH, N_SGU_GROUPS * CHUNK, CHUNK)
    sgub = jnp.repeat(jnp.swapaxes(sgu_b, 1, 2), POOL_GC, axis=2)
    bg_t = jnp.transpose(b_gate, (1, 0, 2))
    kc_t = jnp.transpose(cache_k_win, (0, 1, 3, 4, 2))
    vc_t = jnp.transpose(cache_v_win, (0, 1, 3, 4, 2))
    pool_t = jnp.transpose(state_pool, (0, 2, 1, 3))

    y_s = x_sample
    pool_s, chunk_v, kts, vts, weights = ([] for _ in range(5))
    for l in range(DEPTH):
        w_in_b, qkv, kt, vt, ps, cv, ain, bin_, szc, g, pool_bd = _sample_proj(
            l, y_s, cos_s, sin_s, w_in, bg_t, pool_t, pool_w, pool_scale, sgu_ln_g, sgu_ln_b, sgu_w, sgu_b)
        o, wpa, wpb, wpc, wout = _sample_attn(
            l, attn_sinks, qkv.reshape(nb, N_HEADS + 2 * N_KV_HEADS, HEAD_DIM), kc_t, vc_t,
            w_proj_a, w_proj_b, w_proj_c, w_out)
        y_s = _sample_merge(l, y_s, ain, bin_, o.reshape(nb, D_ATTN), szc, g, wpa, wpb, wpc, wout, ln_g, ln_b)
        pool_s.append(ps); chunk_v.append(cv); kts.append(kt); vts.append(vt)
        weights.append((w_in_b, wpa, wpb, wpc, wout, pool_bd))

    y_p = x_prompt
    pool_p, k_p, v_p = ([] for _ in range(3))
    for l in range(DEPTH):
        w_in_b, wpa, wpb, wpc, wout, poolw = weights[l]
        slide_args = (kts, vts, kc_t, vc_t) if l == DEPTH - 1 else None
        y_p, pp, kp, vp, *slid = _prompt_layer(
            l, y_p, cos_p, sin_p, attn_sinks, w_in_b, bg_t, poolw, pool_scale, sgu_ln_g, sgu_ln_b, sguw,
            sgub, wpa, wpb, wpc, wout, ln_g, ln_b, slide_args=slide_args)
        pool_p.append(pp); k_p.append(kp); v_p.append(vp)
    k_s, v_s = slid

    to_cache = lambda a: jnp.transpose(a, (0, 1, 4, 2, 3))
    prompt_cache = lambda lst: to_cache(jnp.stack(lst).reshape(DEPTH, B, N_KV_HEADS, HEAD_DIM, WINDOW))
    return (y_p, y_s,
            jnp.stack(pool_p), prompt_cache(k_p), prompt_cache(v_p),
            jnp.transpose(jnp.stack(pool_s), (0, 2, 1, 3)), to_cache(k_s), to_cache(v_s),
            jnp.stack(chunk_v))
```

```python
import functools

import numpy as np
import jax
import jax.numpy as jnp
from jax import lax
from jax.experimental import pallas as pl
from jax.experimental.pallas import tpu as pltpu

D_MODEL = 1024
DEPTH = 2
PAST_LEN = 8192
D_POOL = 256
POOL_WINDOWS = (2, 4, 8, 16)
POOL_GC = 64
POOL_BUF = 15
D_SGU = 256
CHUNK = 128
N_SGU_GROUPS = 4
HEAD_DIM = 64
N_HEADS = 8
N_KV_HEADS = 2
Q_PER_KV = 4
D_ATTN = 512
D_KV = 128
WINDOW = 128
BLOCK = 128
ROPE_THETA = 10000.0
N_BRANCHES = 3
D_IN = 2 * D_POOL + 3 * D_SGU + 2 * D_ATTN + 2 * D_KV + N_BRANCHES * D_MODEL
ALPHA = (2.0 * DEPTH) ** 0.25
LN_EPS = 1e-5
NEG_INF = -1e30
SCALE = HEAD_DIM ** -0.5

OFF_XA, OFF_ZA, OFF_U, OFF_V, OFF_ZB = 0, 256, 512, 768, 1024
OFF_Q, OFF_K, OFF_VV, OFF_ZC, OFF_G = 1280, 1792, 1920, 2048, 2560

LANES = 128
SUBLANES = 8
TM = 512
SUB = 2
NBLK = TM // BLOCK
HIST = 32
NCHUNK = 256
OUT_CHUNKS = ((0, 256), (256, 512))
VMEM_LIMIT = 56 * 1024 * 1024
SB = 32
RK = 256
W_SLOTS = 3
RK_STEPS = D_MODEL // RK

bf16 = jnp.bfloat16
f32 = jnp.float32


def _dot(a, b):
    return jnp.dot(a, b, preferred_element_type=f32)


def _dot_nt(a, b):
    return lax.dot_general(a, b, (((1,), (1,)), ((), ())), preferred_element_type=f32)


def _sigmoid(z):
    return 0.5 * jnp.tanh(0.5 * z) + 0.5


def _silu(z):
    return z * _sigmoid(z)


def _gate2(half_pre, bias):
    return jnp.tanh(half_pre + 0.5 * bias) + 1.0


def _layer_norm(x, g, b):
    mu = jnp.mean(x, axis=-1, keepdims=True)
    xc = x - mu
    var = jnp.mean(xc * xc, axis=-1, keepdims=True)
    return xc * lax.rsqrt(var + LN_EPS) * g + b


def _rope128(x, cos, sin_signed):
    lane = lax.broadcasted_iota(jnp.int32, x.shape, 1)
    first_half = (lane % HEAD_DIM) < (HEAD_DIM // 2)
    partner = jnp.where(first_half,
                        pltpu.roll(x, LANES - HEAD_DIM // 2, 1),
                        pltpu.roll(x, HEAD_DIM // 2, 1))
    return x * cos + partner * sin_signed


def _group_select(lane, a0, a1, a2, a3):
    return jnp.where(lane < 64, a0, jnp.where(lane < 128, a1, jnp.where(lane < 192, a2, a3)))


def _layer_spec(arr, layer, single_buffer=False):
    block = (None,) + arr.shape[1:]
    zeros = (0,) * (arr.ndim - 1)
    index_map = lambda *_: (layer,) + zeros
    if single_buffer:
        return pl.BlockSpec(block, index_map, pipeline_mode=pl.Buffered(1))
    return pl.BlockSpec(block, index_map)


def _full_spec(arr, single_buffer=False):
    zeros = (0,) * len(arr.shape)
    if single_buffer:
        return pl.BlockSpec(arr.shape, lambda *_: zeros, pipeline_mode=pl.Buffered(1))
    return pl.BlockSpec(arr.shape, lambda *_: zeros)


def _slide_windows(step, per_step, new_refs, old_refs, out_refs):
    lane = lax.broadcasted_iota(jnp.int32, (D_KV, WINDOW), 1)
    shift = jnp.where(step == 0, 0, LANES - step * per_step)
    for new_ref, old_ref, out_ref in zip(new_refs, old_refs, out_refs):
        for l in range(DEPTH):
            new_cols = pltpu.roll(new_ref[l][...], shift, 1)
            for s in range(per_step):
                slid = pltpu.roll(old_ref[l, s].reshape(D_KV, WINDOW), WINDOW - 1, 1)
                newest = jnp.broadcast_to(new_cols[:, s:s + 1], (D_KV, WINDOW))
                out_ref[l, s] = jnp.where(lane == WINDOW - 1, newest, slid).reshape(
                    N_KV_HEADS, HEAD_DIM, WINDOW)


def _prompt_kernel(layer, slide, *refs):
    def sub_tile(s, carry):
        _prompt_tile(layer, slide, s, refs)
        return carry
    lax.fori_loop(0, SUB, sub_tile, 0)


def _prompt_tile(layer, slide, s, refs):
    n_in = 18 + (2 * DEPTH + 2 if slide else 0)
    n_out = 4 + (2 if slide else 0)
    (sinks_ref, x_ref, cos_ref, sin_ref, w_in_ref, bg_ref, poolw_ref, pscale_ref,
     slng_ref, slnb_ref, sguw_ref, sgub_ref, wpa_ref, wpb_ref, wpc_ref, wout_ref,
     lng_ref, lnb_ref) = refs[:18]
    y_ref, pool_out_ref, k_out_ref, v_out_ref = refs[n_in:n_in + 4]
    (h_ref, ext_ref, s2_ref, s4_ref, s8_ref, qs_ref,
     ka_ref, kb_ref, kc_ref, kd_ref, va_ref, vb_ref, vc_ref, vd_ref,
     ain_ref, bin_ref, cin_ref, mg_ref, gate_ref, klast_ref, vlast_ref) = refs[n_in + n_out:]
    tile_rows = pl.ds(pl.multiple_of(s * TM, TM), TM)
    x_ref, y_ref, cos_ref, sin_ref = (r.at[tile_rows] for r in (x_ref, y_ref, cos_ref, sin_ref))
    i = pl.program_id(1) * SUB + s
    last = pl.num_programs(1) * SUB - 1
    kv_refs = (ka_ref, kb_ref, kc_ref, kd_ref, va_ref, vb_ref, vc_ref, vd_ref)
    row = lambda ref: ref[layer:layer + 1, :]

    @pl.when(i == 0)
    def _():
        ext_ref[0:HIST, :] = jnp.zeros((HIST, D_POOL), f32)
        for r in kv_refs:
            r[0:BLOCK, :] = jnp.zeros((BLOCK, LANES), bf16)

    xb = x_ref[...].astype(bf16)
    half = OFF_G // 2
    h_ref[:, 0:half] = _dot(xb, w_in_ref[:, 0:half])
    h_ref[:, half:OFF_G] = _dot(xb, w_in_ref[:, half:OFF_G])

    if slide:
        step = pl.program_id(0) * (last + 1) + i
        new_refs = (refs[18:18 + DEPTH], refs[18 + DEPTH:18 + 2 * DEPTH])
        old_refs = refs[18 + 2 * DEPTH:n_in]
        per_step = old_refs[0].shape[1] // SUB
        mine = lambda r: r.at[:, pl.ds(s * per_step, per_step)]
        _slide_windows(step, per_step, new_refs, [mine(r) for r in old_refs],
                       [mine(r) for r in refs[n_in + 4:n_in + 6]])

    xa = h_ref[:, OFF_XA:OFF_XA + D_POOL]
    ext_ref[HIST:HIST + TM, :] = xa
    n = HIST + TM
    s2_ref[8:n, :] = ext_ref[8:n, :] + ext_ref[7:n - 1, :]
    s4_ref[16:n, :] = s2_ref[16:n, :] + s2_ref[14:n - 2, :]
    s8_ref[24:n, :] = s4_ref[24:n, :] + s4_ref[20:n - 4, :]
    w16 = s8_ref[HIST:n, :] + s8_ref[HIST - 8:n - 8, :]
    lane_p = lax.broadcasted_iota(jnp.int32, (TM, D_POOL), 1)
    row_p = lax.broadcasted_iota(jnp.int32, (TM, D_POOL), 0)
    win = _group_select(lane_p, s2_ref[HIST:n, :], s4_ref[HIST:n, :], s8_ref[HIST:n, :], w16)
    width = _group_select(lane_p, POOL_WINDOWS[0], POOL_WINDOWS[1], POOL_WINDOWS[2], POOL_WINDOWS[3])
    cnt = jnp.minimum(row_p + (i * TM + 1), width).astype(f32)
    pooled = win / cnt - xa
    ya = _dot(pooled.astype(bf16), poolw_ref[...]) * row(pscale_ref)
    za = h_ref[:, OFF_ZA:OFF_ZA + D_POOL]
    ain_ref[...] = (ya * _silu(za)).astype(bf16)
    ext_ref[HIST - 16:HIST, :] = ext_ref[n - 16:n, :]

    vn = _layer_norm(h_ref[:, OFF_V:OFF_V + D_SGU], row(slng_ref), row(slnb_ref)).astype(bf16)
    wr = lax.broadcasted_iota(jnp.int32, (N_SGU_GROUPS * CHUNK, CHUNK), 0) % CHUNK
    wc = lax.broadcasted_iota(jnp.int32, (N_SGU_GROUPS * CHUNK, CHUNK), 1)
    w_s = jnp.where(wc <= wr, sguw_ref[...], 0.0).astype(bf16)
    lane_c = lax.broadcasted_iota(jnp.int32, (CHUNK, D_SGU), 1)
    for j in range(NBLK):
        rows = slice(j * CHUNK, (j + 1) * CHUNK)
        r = _dot(w_s, vn[rows, :])
        mixed = _group_select(lane_c, r[0:CHUNK], r[CHUNK:2 * CHUNK], r[2 * CHUNK:3 * CHUNK],
                              r[3 * CHUNK:4 * CHUNK]) + sgub_ref[...]
        yb = h_ref[rows, OFF_U:OFF_U + D_SGU] * mixed
        bin_ref[rows, :] = (yb * _silu(h_ref[rows, OFF_ZB:OFF_ZB + D_SGU])).astype(bf16)

    cos = cos_ref[...]
    sin = sin_ref[...]
    for c in range(D_ATTN // LANES):
        qc = _rope128(h_ref[:, OFF_Q + c * LANES:OFF_Q + (c + 1) * LANES], cos, sin)
        qs_ref[:, c * LANES:(c + 1) * LANES] = (qc * SCALE).astype(bf16)
    kr = _rope128(h_ref[:, OFF_K:OFF_K + D_KV], cos, sin)
    vv = h_ref[:, OFF_VV:OFF_VV + D_KV]
    klast_ref[...] = kr[TM - WINDOW:TM, :]
    vlast_ref[...] = vv[TM - WINDOW:TM, :]
    lane_k = lax.broadcasted_iota(jnp.int32, (TM, LANES), 1)
    lo = lane_k < HEAD_DIM
    for src, (a_ref, b_ref, c_ref, d_ref) in ((kr, kv_refs[0:4]), (vv, kv_refs[4:8])):
        sw = pltpu.roll(src, HEAD_DIM, 1)
        a_ref[BLOCK:BLOCK + TM, :] = jnp.where(lo, src, 0.0).astype(bf16)
        b_ref[BLOCK:BLOCK + TM, :] = jnp.where(lo, 0.0, src).astype(bf16)
        c_ref[BLOCK:BLOCK + TM, :] = jnp.where(lo, sw, 0.0).astype(bf16)
        d_ref[BLOCK:BLOCK + TM, :] = jnp.where(lo, 0.0, sw).astype(bf16)

    qrow = lax.broadcasted_iota(jnp.int32, (2 * BLOCK, 2 * BLOCK), 0) % BLOCK
    kcol = lax.broadcasted_iota(jnp.int32, (2 * BLOCK, 2 * BLOCK), 1)
    band = (kcol >= qrow) & (kcol <= qrow + WINDOW)
    band_first = band & (kcol >= jnp.where(i > 0, 0, BLOCK))
    top = lax.broadcasted_iota(jnp.int32, (2 * BLOCK, 1), 0) < BLOCK
    n_gate = N_BRANCHES * D_MODEL // NCHUNK
    n_unit = NBLK * N_KV_HEADS
    gate_sched = [range(u * n_gate // n_unit, (u + 1) * n_gate // n_unit) for u in range(n_unit)]

    def scores(u):
        j, kv = divmod(u, N_KV_HEADS)
        rows = slice(j * BLOCK, (j + 1) * BLOCK)
        keys = slice(j * BLOCK, j * BLOCK + 2 * BLOCK)
        c0 = kv * Q_PER_KV * HEAD_DIM
        qst = jnp.concatenate([qs_ref[rows, c0:c0 + LANES], qs_ref[rows, c0 + LANES:c0 + 2 * LANES]], axis=0)
        k_even, k_odd = (ka_ref, kd_ref) if kv == 0 else (kc_ref, kb_ref)
        kcat = jnp.concatenate([k_even[keys, :], k_odd[keys, :]], axis=0)
        return _dot_nt(qst, kcat)

    def attend(u, sc):
        j, kv = divmod(u, N_KV_HEADS)
        rows = slice(j * BLOCK, (j + 1) * BLOCK)
        keys = slice(j * BLOCK, j * BLOCK + 2 * BLOCK)
        allowed = band_first if j == 0 else band
        c0 = kv * Q_PER_KV * HEAD_DIM
        h0 = kv * Q_PER_KV
        v_even, v_odd = (va_ref, vd_ref) if kv == 0 else (vc_ref, vb_ref)
        probs = []
        for par in range(2):
            sink = jnp.where(top, sinks_ref[layer, h0 + par], sinks_ref[layer, h0 + 2 + par])
            sm = jnp.where(allowed, sc[:, par * 2 * BLOCK:(par + 1) * 2 * BLOCK], NEG_INF)
            m = jnp.maximum(jnp.max(sm, axis=-1, keepdims=True), sink)
            p = jnp.exp(sm - m)
            den = jnp.sum(p, axis=-1, keepdims=True) + jnp.exp(sink - m)
            probs.append((p / den).astype(bf16))
        pcat = jnp.concatenate(probs, axis=1)
        vcat = jnp.concatenate([v_even[keys, :], v_odd[keys, :]], axis=0)
        o = _dot(pcat, vcat)
        for pr in range(2):
            cols = slice(c0 + pr * LANES, c0 + (pr + 1) * LANES)
            zc = h_ref[rows, OFF_ZC + c0 + pr * LANES:OFF_ZC + c0 + (pr + 1) * LANES]
            cin_ref[rows, cols] = (o[pr * BLOCK:(pr + 1) * BLOCK] * _silu(zc)).astype(bf16)

    sc_next = scores(0)
    for u in range(n_unit):
        sc = sc_next
        for gc in gate_sched[u]:
            gcols = slice(gc * NCHUNK, (gc + 1) * NCHUNK)
            gate_ref[:, gcols] = _dot(xb, w_in_ref[:, OFF_G + gc * NCHUNK:OFF_G + (gc + 1) * NCHUNK])
        if u + 1 < n_unit:
            sc_next = scores(u + 1)
        attend(u, sc)

    for r in kv_refs:
        r[0:BLOCK, :] = r[TM:TM + BLOCK, :]

    for c in range(D_MODEL // NCHUNK):
        cols = slice(c * NCHUNK, (c + 1) * NCHUNK)
        acc = None
        for br, (in_ref, wp_ref) in enumerate(((ain_ref, wpa_ref), (bin_ref, wpb_ref), (cin_ref, wpc_ref))):
            g0 = br * D_MODEL + c * NCHUNK
            term = (_gate2(gate_ref[:, g0:g0 + NCHUNK], bg_ref[br, layer:layer + 1, cols])
                    * _dot(in_ref[...], wp_ref[:, cols]))
            acc = term if acc is None else acc + term
        mg_ref[:, cols] = acc.astype(bf16)

    for r0, r1 in OUT_CHUNKS:
        rows = slice(r0, r1)
        out = _dot(mg_ref[rows, :], wout_ref[...])
        y_ref[rows, :] = _layer_norm(ALPHA * x_ref[rows, :] + out, row(lng_ref), row(lnb_ref))

    @pl.when(i == last)
    def _():
        pool_out_ref[...] = ext_ref[n - POOL_BUF:n, :]
        k_out_ref[...] = klast_ref[...].T
        v_out_ref[...] = vlast_ref[...].T


def _prompt_layer(layer, x, cos, sin, sinks, w_in, b_gate, poolw, pscale, slng, slnb, sguw, sgub,
                  wpa, wpb, wpc, wout, lng, lnb, slide_args=None):
    B, L, _ = x.shape
    rows = TM * SUB
    nt = L // rows
    grid = (B, nt)
    row_spec = lambda w: pl.BlockSpec((rows, w), lambda b, i: (i, 0))
    lspec = lambda a: _layer_spec(a, layer, single_buffer=True)
    wspec = lambda a: _full_spec(a, single_buffer=True)
    in_specs = [
        pl.BlockSpec(memory_space=pltpu.SMEM),
        pl.BlockSpec((None, rows, D_MODEL), lambda b, i: (b, i, 0)),
        row_spec(LANES), row_spec(LANES),
        wspec(w_in), _full_spec(b_gate), wspec(poolw), _full_spec(pscale), _full_spec(slng), _full_spec(slnb),
        lspec(sguw), lspec(sgub), wspec(wpa), wspec(wpb), wspec(wpc), wspec(wout),
        _full_spec(lng), _full_spec(lnb),
    ]
    out_shape = (
        jax.ShapeDtypeStruct((B, L, D_MODEL), f32),
        jax.ShapeDtypeStruct((B, POOL_BUF, D_POOL), f32),
        jax.ShapeDtypeStruct((B, D_KV, WINDOW), f32),
        jax.ShapeDtypeStruct((B, D_KV, WINDOW), f32),
    )
    out_specs = (
        pl.BlockSpec((None, rows, D_MODEL), lambda b, i: (b, i, 0)),
        pl.BlockSpec((None, POOL_BUF, D_POOL), lambda b, i: (b, 0, 0)),
        pl.BlockSpec((None, D_KV, WINDOW), lambda b, i: (b, 0, 0)),
        pl.BlockSpec((None, D_KV, WINDOW), lambda b, i: (b, 0, 0)),
    )
    args = [sinks, x, cos, sin, w_in, b_gate, poolw, pscale, slng, slnb, sguw, sgub,
            wpa, wpb, wpc, wout, lng, lnb]
    if slide_args is not None:
        new_k, new_v, cache_k, cache_v = slide_args
        depth, nb = cache_k.shape[:2]
        per_step = nb // (B * nt)
        assert per_step * B * nt == nb and depth == DEPTH
        cache_spec = pl.BlockSpec((depth, per_step) + cache_k.shape[2:], lambda b, i: (0, b * nt + i, 0, 0, 0))
        in_specs += [_full_spec(a) for a in (*new_k, *new_v)] + [cache_spec, cache_spec]
        args += [*new_k, *new_v, cache_k, cache_v]
        out_shape += (jax.ShapeDtypeStruct(cache_k.shape, f32), jax.ShapeDtypeStruct(cache_v.shape, f32))
        out_specs += (cache_spec, cache_spec)
    kv_scratch = [pltpu.VMEM((BLOCK + TM, LANES), bf16) for _ in range(8)]
    scratch = [
        pltpu.VMEM((TM, OFF_G), f32),
        pltpu.VMEM((HIST + TM, D_POOL), f32),
        pltpu.VMEM((HIST + TM, D_POOL), f32),
        pltpu.VMEM((HIST + TM, D_POOL), f32),
        pltpu.VMEM((HIST + TM, D_POOL), f32),
        pltpu.VMEM((TM, D_ATTN), bf16),
        *kv_scratch,
        pltpu.VMEM((TM, D_POOL), bf16),
        pltpu.VMEM((TM, D_SGU), bf16),
        pltpu.VMEM((TM, D_ATTN), bf16),
        pltpu.VMEM((TM, D_MODEL), bf16),
        pltpu.VMEM((TM, N_BRANCHES * D_MODEL), f32),
        pltpu.VMEM((WINDOW, D_KV), f32),
        pltpu.VMEM((WINDOW, D_KV), f32),
    ]
    return pl.pallas_call(
        functools.partial(_prompt_kernel, layer, slide_args is not None),
        out_shape=out_shape,
        grid=grid,
        in_specs=in_specs,
        out_specs=out_specs,
        scratch_shapes=scratch,
        compiler_params=pltpu.CompilerParams(
            dimension_semantics=("arbitrary", "arbitrary"),
            vmem_limit_bytes=VMEM_LIMIT),
        name="prompt_layer",
    )(*args)


def _sample_proj_kernel(layer, x_ref, cos_ref, sin_ref, w_in_ref, bg_ref, pb_ref, poolw_ref, pscale_ref,
                        slng_ref, slnb_ref, sw0_ref, sb0_ref,
                        wb_ref, qkv_ref, kt_ref, vt_ref, pool_ref, vn_ref, ain_ref, bin_ref,
                        szc_ref, g_ref, poolbd_ref, h_ref, wbuf_ref, wsem):
    j = pl.program_id(0)
    row = lambda ref: ref[layer:layer + 1, :]
    def chunk_copy(step):
        slot = step % W_SLOTS
        return pltpu.make_async_copy(w_in_ref.at[layer, pl.ds(step * RK, RK), :], wbuf_ref.at[slot], wsem.at[slot])

    @pl.when(j == 0)
    def _():
        for first in range(W_SLOTS - 1):
            chunk_copy(first).start()

    @pl.when(j + W_SLOTS - 1 < RK_STEPS)
    def _():
        chunk_copy(j + W_SLOTS - 1).start()

    chunk_copy(j).wait()
    w_chunk = wbuf_ref.at[j % W_SLOTS]
    wb_ref[:, 0:OFF_G] = w_chunk[:, 0:OFF_G].astype(bf16)
    wb_ref[:, OFF_G:D_IN] = (0.5 * w_chunk[:, OFF_G:D_IN]).astype(bf16)

    @pl.when(j == 0)
    def _():
        h_ref[...] = jnp.zeros(h_ref.shape, f32)

    h_ref[...] += _dot(x_ref[...].astype(bf16), wb_ref[...])

    def hcols(off, width):
        return h_ref[:, off:off + width]

    @pl.when(j == RK_STEPS - 1)
    def _():
        xa = hcols(OFF_XA, D_POOL)
        lane = lax.broadcasted_iota(jnp.int32, xa.shape, 1)
        first_row = _group_select(lane, *(POOL_BUF - (w - 1) for w in POOL_WINDOWS))
        win = xa
        for r in range(POOL_BUF):
            win = win + jnp.where(first_row <= r, pb_ref[r], 0.0)
        width = _group_select(lane, *POOL_WINDOWS).astype(f32)
        pooled = win / width - xa
        zero_blk = jnp.zeros((POOL_GC, POOL_GC), f32)
        n_grp = len(POOL_WINDOWS)
        pool_bd = jnp.concatenate(
            [jnp.concatenate([poolw_ref[g] if gg == g else zero_blk for gg in range(n_grp)], axis=1)
             for g in range(n_grp)], axis=0).astype(bf16)
        poolbd_ref[...] = pool_bd
        ya = _dot(pooled.astype(bf16), pool_bd) * row(pscale_ref)
        ain_ref[...] = (ya * _silu(hcols(OFF_ZA, D_POOL))).astype(bf16)
        for r in range(POOL_BUF - 1):
            pool_ref[r] = pb_ref[r + 1]
        pool_ref[POOL_BUF - 1] = xa
        vn = _layer_norm(hcols(OFF_V, D_SGU), row(slng_ref), row(slnb_ref))
        vn_ref[...] = vn
        lane_g = lax.broadcasted_iota(jnp.int32, (1, D_SGU), 1)
        sw0 = _group_select(lane_g, *(sw0_ref[g, 0:1, 0:1] for g in range(N_SGU_GROUPS)))
        sb0 = _group_select(lane_g, *(sb0_ref[g:g + 1, 0:1] for g in range(N_SGU_GROUPS)))
        yb = hcols(OFF_U, D_SGU) * (sw0 * vn + sb0)
        bin_ref[...] = (yb * _silu(hcols(OFF_ZB, D_SGU))).astype(bf16)
        cos = cos_ref[...]
        sin = sin_ref[...]
        for c in range(D_ATTN // LANES):
            qkv_ref[:, c * LANES:(c + 1) * LANES] = _rope128(hcols(OFF_Q + c * LANES, LANES), cos, sin) * SCALE
        kr = _rope128(hcols(OFF_K, D_KV), cos, sin)
        vv = hcols(OFF_VV, D_KV)
        qkv_ref[:, D_ATTN:D_ATTN + D_KV] = kr
        qkv_ref[:, D_ATTN + D_KV:D_ATTN + 2 * D_KV] = vv
        kt_ref[...] = kr.T
        vt_ref[...] = vv.T
        szc_ref[...] = _silu(hcols(OFF_ZC, D_ATTN))
        for br in range(N_BRANCHES):
            cols = slice(br * D_MODEL, (br + 1) * D_MODEL)
            g_ref[:, cols] = _gate2(hcols(OFF_G + br * D_MODEL, D_MODEL), bg_ref[br, layer:layer + 1, :])


def _sample_proj(layer, x, cos, sin, w_in, b_gate, pool_t, poolw, pscale, slng, slnb, sw0, sb0):
    nb = x.shape[0]
    out_shape = (
        jax.ShapeDtypeStruct((D_MODEL, D_IN), bf16),
        jax.ShapeDtypeStruct((nb, D_ATTN + 2 * D_KV), f32),
        jax.ShapeDtypeStruct((D_KV, nb), f32),
        jax.ShapeDtypeStruct((D_KV, nb), f32),
        jax.ShapeDtypeStruct((POOL_BUF, nb, D_POOL), f32),
        jax.ShapeDtypeStruct((nb, 1, D_SGU), f32),
        jax.ShapeDtypeStruct((nb, D_POOL), bf16),
        jax.ShapeDtypeStruct((nb, D_SGU), bf16),
        jax.ShapeDtypeStruct((nb, D_ATTN), f32),
        jax.ShapeDtypeStruct((nb, N_BRANCHES * D_MODEL), f32),
        jax.ShapeDtypeStruct((D_POOL, D_POOL), bf16),
    )
    lspec = lambda a: _layer_spec(a, layer)
    args = (x, cos, sin, w_in, b_gate, pool_t, poolw, pscale, slng, slnb, sw0, sb0)
    in_specs = [pl.BlockSpec((nb, None, RK), lambda j: (0, 0, j)), _full_spec(cos), _full_spec(sin),
                pl.BlockSpec(memory_space=pl.ANY),
                _full_spec(b_gate), lspec(pool_t), lspec(poolw), _full_spec(pscale), _full_spec(slng),
                _full_spec(slnb),
                pl.BlockSpec((None, N_SGU_GROUPS, SUBLANES, CHUNK), lambda j: (layer, 0, 0, 0)),
                pl.BlockSpec((None, N_SGU_GROUPS, CHUNK), lambda j: (layer, 0, 0))]
    squeeze_mid = lambda s: pl.BlockSpec((s.shape[0], None, s.shape[2]), lambda j: (0, 0, 0))
    out_specs = (pl.BlockSpec((RK, D_IN), lambda j: (j, 0)),) + tuple(
        squeeze_mid(s) if len(s.shape) == 3 and s.shape[1] == 1 else _full_spec(s) for s in out_shape[1:])
    return pl.pallas_call(
        functools.partial(_sample_proj_kernel, layer),
        out_shape=out_shape,
        grid=(RK_STEPS,),
        in_specs=in_specs,
        out_specs=out_specs,
        scratch_shapes=[pltpu.VMEM((nb, D_IN), f32), pltpu.VMEM((W_SLOTS, RK, D_IN), f32),
                        pltpu.SemaphoreType.DMA((W_SLOTS,))],
        compiler_params=pltpu.CompilerParams(dimension_semantics=("arbitrary",),
                                             vmem_limit_bytes=VMEM_LIMIT),
        name="sample_proj",
    )(*args)


def _sample_attn_kernel(layer, sink_ref, qkv_ref, kc_ref, vc_ref, o_ref):
    head = lax.broadcasted_iota(jnp.int32, (1, Q_PER_KV, 1), 1)
    for kv in range(N_KV_HEADS):
        qb = qkv_ref[:, kv * Q_PER_KV:(kv + 1) * Q_PER_KV, :].astype(bf16)
        kn = qkv_ref[:, N_HEADS + kv:N_HEADS + kv + 1, :]
        vn = qkv_ref[:, N_HEADS + N_KV_HEADS + kv:N_HEADS + N_KV_HEADS + kv + 1, :]
        sink = jnp.zeros((1, Q_PER_KV, 1), f32)
        for g in range(Q_PER_KV):
            sink = jnp.where(head == g, sink_ref[layer, kv * Q_PER_KV + g], sink)
        s = jnp.einsum('bgd,bdw->bgw', qb, kc_ref[:, kv].astype(bf16), preferred_element_type=f32)
        s_new = jnp.sum(qb.astype(f32) * kn.astype(bf16).astype(f32), axis=-1, keepdims=True)
        m = jnp.maximum(jnp.maximum(jnp.max(s, axis=-1, keepdims=True), s_new), sink)
        p = jnp.exp(s - m)
        p_new = jnp.exp(s_new - m)
        den = jnp.sum(p, axis=-1, keepdims=True) + p_new + jnp.exp(sink - m)
        o = jnp.einsum('bgw,bdw->bgd', (p / den).astype(bf16), vc_ref[:, kv].astype(bf16),
                       preferred_element_type=f32)
        o_ref[:, kv] = o + (p_new / den) * vn


def _sample_attn(layer, sink, qkv3, kc, vc):
    nb, n_rows, _ = qkv3.shape
    cache_spec = pl.BlockSpec((None, SB, N_KV_HEADS, HEAD_DIM, WINDOW), lambda b: (layer, b, 0, 0, 0))
    return pl.pallas_call(
        functools.partial(_sample_attn_kernel, layer),
        out_shape=jax.ShapeDtypeStruct((nb, N_KV_HEADS, Q_PER_KV, HEAD_DIM), f32),
        grid=(nb // SB,),
        in_specs=[pl.BlockSpec(memory_space=pltpu.SMEM),
                  pl.BlockSpec((SB, n_rows, HEAD_DIM), lambda b: (b, 0, 0)), cache_spec, cache_spec],
        out_specs=pl.BlockSpec((SB, N_KV_HEADS, Q_PER_KV, HEAD_DIM), lambda b: (b, 0, 0, 0)),
        compiler_params=pltpu.CompilerParams(dimension_semantics=("arbitrary",),
                                             vmem_limit_bytes=VMEM_LIMIT),
        name="sample_attn",
    )(sink, qkv3, kc, vc)


def _sample_merge_kernel(layer, x_ref, ain_ref, bin_ref, yc_ref, szc_ref, g_ref, wpa_ref, wpb_ref, wpc_ref,
                         wout_ref, lng_ref, lnb_ref, y_ref, wpa_b_ref, wpb_b_ref, wpc_b_ref, wout_b_ref):
    row = lambda ref: ref[layer:layer + 1, :]
    wpa = (0.5 * wpa_ref[...]).astype(bf16)
    wpb = (0.5 * wpb_ref[...]).astype(bf16)
    wpc = (0.5 * wpc_ref[...]).astype(bf16)
    wout = wout_ref[...].astype(bf16)
    wpa_b_ref[...] = wpa
    wpb_b_ref[...] = wpb
    wpc_b_ref[...] = wpc
    wout_b_ref[...] = wout
    cin = (yc_ref[...] * szc_ref[...]).astype(bf16)
    merged = (g_ref[:, 0:D_MODEL] * _dot(ain_ref[...], wpa)
              + g_ref[:, D_MODEL:2 * D_MODEL] * _dot(bin_ref[...], wpb)
              + g_ref[:, 2 * D_MODEL:3 * D_MODEL] * _dot(cin, wpc))
    out = _dot(merged.astype(bf16), wout)
    y_ref[...] = _layer_norm(ALPHA * x_ref[...] + out, row(lng_ref), row(lnb_ref))


def _sample_merge(layer, x, ain, bin_, yc, szc, g, wpa, wpb, wpc, wout, lng, lnb):
    lspec = lambda a: _layer_spec(a, layer)
    nb = x.shape[0]
    x_spec = pl.BlockSpec((nb, None, D_MODEL), lambda i: (0, 0, 0))
    out_shape = (jax.ShapeDtypeStruct(x.shape, f32),) + tuple(
        jax.ShapeDtypeStruct(w.shape[1:], bf16) for w in (wpa, wpb, wpc, wout))
    return pl.pallas_call(
        functools.partial(_sample_merge_kernel, layer),
        out_shape=out_shape,
        grid=(1,),
        in_specs=[x_spec, _full_spec(ain), _full_spec(bin_), _full_spec(yc), _full_spec(szc),
                  _full_spec(g), lspec(wpa), lspec(wpb), lspec(wpc), lspec(wout), _full_spec(lng),
                  _full_spec(lnb)],
        out_specs=(x_spec,) + tuple(_full_spec(s) for s in out_shape[1:]),
        compiler_params=pltpu.CompilerParams(dimension_semantics=("arbitrary",),
                                             vmem_limit_bytes=VMEM_LIMIT),
        name="sample_merge",
    )(x, ain, bin_, yc, szc, g, wpa, wpb, wpc, wout, lng, lnb)


def _rope_tables(positions):
    halfd = HEAD_DIM // 2
    inv = ROPE_THETA ** (-np.arange(halfd, dtype=np.float64) / halfd)
    ang = np.asarray(positions, dtype=np.float64)[:, None] * inv[None, :]
    cos = np.tile(np.cos(ang), (1, LANES // halfd))
    sin = np.tile(np.concatenate([-np.sin(ang), np.sin(ang)], axis=1), (1, LANES // HEAD_DIM))
    return jnp.asarray(cos, f32), jnp.asarray(sin, f32)


def kernel(x_prompt, x_sample, state_pool, cache_k_win, cache_v_win, w_in, b_gate, pool_w, pool_scale, sgu_ln_g, sgu_ln_b, sgu_w, sgu_b, attn_sinks, w_proj_a, w_proj_b, w_proj_c, w_out, ln_g, ln_b):
    B, L, _ = x_prompt.shape
    nb = x_sample.shape[0]
    cos_p, sin_p = _rope_tables(np.arange(L))
    cos_s, sin_s = _rope_tables(np.array([PAST_LEN]))

    sguw = sgu_w.reshape(DEPTH, N_SGU_GROUPS * CHUNK, CHUNK)
    sgub = jnp.repeat(jnp.swapaxes(sgu_b, 1, 2), POOL_GC, axis=2)
    bg_t = jnp.transpose(b_gate, (1, 0, 2))
    kc_t = jnp.transpose(cache_k_win, (0, 1, 3, 4, 2))
    vc_t = jnp.transpose(cache_v_win, (0, 1, 3, 4, 2))
    pool_t = jnp.transpose(state_pool, (0, 2, 1, 3))

    y_s = x_sample
    pool_s, chunk_v, kts, vts, weights = ([] for _ in range(5))
    for l in range(DEPTH):
        w_in_b, qkv, kt, vt, ps, cv, ain, bin_, szc, g, pool_bd = _sample_proj(
            l, y_s, cos_s, sin_s, w_in, bg_t, pool_t, pool_w, pool_scale, sgu_ln_g, sgu_ln_b, sgu_w, sgu_b)
        o = _sample_attn(l, attn_sinks, qkv.reshape(nb, N_HEADS + 2 * N_KV_HEADS, HEAD_DIM), kc_t, vc_t)
        y_s, wpa, wpb, wpc, wout = _sample_merge(
            l, y_s, ain, bin_, o.reshape(nb, D_ATTN), szc, g, w_proj_a, w_proj_b, w_proj_c, w_out, ln_g, ln_b)
        pool_s.append(ps); chunk_v.append(cv); kts.append(kt); vts.append(vt)
        weights.append((w_in_b, wpa, wpb, wpc, wout, pool_bd))

    y_p = x_prompt
    pool_p, k_p, v_p = ([] for _ in range(3))
    for l in range(DEPTH):
        w_in_b, wpa, wpb, wpc, wout, poolw = weights[l]
        slide_args = (kts, vts, kc_t, vc_t) if l == DEPTH - 1 else None
        y_p, pp, kp, vp, *slid = _prompt_layer(
            l, y_p, cos_p, sin_p, attn_sinks, w_in_b, bg_t, poolw, pool_scale, sgu_ln_g, sgu_ln_b, sguw,
            sgub, wpa, wpb, wpc, wout, ln_g, ln_b, slide_args=slide_args)
        pool_p.append(pp); k_p.append(kp); v_p.append(vp)
    k_s, v_s = slid

    to_cache = lambda a: jnp.transpose(a, (0, 1, 4, 2, 3))
    prompt_cache = lambda lst: to_cache(jnp.stack(lst).reshape(DEPTH, B, N_KV_HEADS, HEAD_DIM, WINDOW))
    return (y_p, y_s,
            jnp.stack(pool_p), prompt_cache(k_p), prompt_cache(v_p),
            jnp.transpose(jnp.stack(pool_s), (0, 2, 1, 3)), to_cache(k_s), to_cache(v_s),
            jnp.stack(chunk_v))
```
